```python
import math
import jax
import jax.numpy as jnp
from jax import lax
import numpy as np

D_MODEL = 1024
BATCH = 4
SEQ = 8192
DEPTH = 2

GRID_W = 64
CTX_LEN = 256
N_BRANCH = 3
N_MOD = 6
ATTN_HEADS = D_MODEL // 256
ATTN_HD = 64
ATTN_VD = 2 * ATTN_HD
ATTN_W = ATTN_HEADS * ATTN_VD
QBLOCK = 128
ROPE_BASE = 10000.0
ROPE_AXIS_DIM = ATTN_HD // 2
ROPE_FREQS = ROPE_AXIS_DIM // 2
LAMBDA_INIT_BASE = 0.8
LAMBDA_INIT_AMP = 0.6
LAMBDA_INIT_RATE = 0.3
POOL_WINDOWS = (2, 4, 8, 16)
POOL_GROUPS = len(POOL_WINDOWS)
POOL_W = D_MODEL // 2
POOL_GC = POOL_W // POOL_GROUPS
HYENA_W = D_MODEL // 2
HYENA_ORDER = 2
HYENA_SHORT = 3
HYENA_BANDS = 16
HYENA_EMB = 2 * HYENA_BANDS + 1
HYENA_FFN = 64
HYENA_FAST_DECAY = 0.3
HYENA_SLOW_DECAY = 1.5
HYENA_TARGET = 1e-2
FFN_HIDDEN = -(-(8 * D_MODEL) // (3 * 256)) * 256
Q0 = 0
K0 = Q0 + ATTN_W
V0 = K0 + ATTN_W
P0 = V0 + ATTN_W
H0 = P0 + POOL_W
G0 = H0 + (HYENA_ORDER + 1) * HYENA_W
IN_W = G0 + N_BRANCH * D_MODEL
EPS = 1e-6

kernel_name = 'hybrid_diffattn_pool_hyena_dit'


def rms_norm(x, g):
    xf = x.astype(jnp.float32)
    y = xf * lax.rsqrt(jnp.mean(xf * xf, axis=-1, keepdims=True) + EPS)
    return (y * g.astype(jnp.float32)).astype(x.dtype)


def modulate(h, shift, scale):
    return h * (1.0 + scale) + shift


def axial_rope_tables(rows):
    r = jnp.arange(rows, dtype=jnp.float32)
    cidx = jnp.arange(GRID_W, dtype=jnp.float32)
    row = jnp.broadcast_to(r[:, None], (rows, GRID_W)).reshape(-1)
    col = jnp.broadcast_to(cidx[None, :], (rows, GRID_W)).reshape(-1)
    inv = 1.0 / (ROPE_BASE ** (jnp.arange(ROPE_FREQS, dtype=jnp.float32) * 2.0 / ROPE_AXIS_DIM))
    ang = jnp.stack([row[:, None] * inv, col[:, None] * inv], axis=1)
    return jnp.cos(ang), jnp.sin(ang)


def apply_axial_rope(x, cos, sin):
    xr = x.astype(jnp.float32).reshape(x.shape[:-1] + (2, 2, ROPE_FREQS))
    c = cos[None, :, None, None]
    s = sin[None, :, None, None]
    a, b = xr[..., 0, :], xr[..., 1, :]
    out = jnp.stack([a * c - b * s, b * c + a * s], axis=-2)
    return out.reshape(x.shape).astype(x.dtype)


def qk_heads(t):
    return t.reshape(t.shape[:2] + (ATTN_HEADS, 2, ATTN_HD))


def v_heads(t):
    return t.reshape(t.shape[:2] + (ATTN_HEADS, ATTN_VD))


def diff_attend(q, k, v, lam):
    s = jnp.einsum('bqhmd,bkhmd->bhmqk', q, k, preferred_element_type=jnp.float32) * (ATTN_HD ** -0.5)
    p = jax.nn.softmax(s, axis=-1)
    a = p[:, :, 0] - lam * p[:, :, 1]
    return jnp.einsum('bhqk,bkhe->bqhe', a.astype(v.dtype), v)


def blocked_diff_attend(q, k, v, lam):
    b, l = q.shape[:2]
    nb = l // QBLOCK
    qb = jnp.moveaxis(q.reshape((b, nb, QBLOCK) + q.shape[2:]), 1, 0)
    ob = lax.map(lambda qq: diff_attend(qq, k, v, lam), qb)
    return jnp.moveaxis(ob, 0, 1).reshape((b, l) + ob.shape[3:])


def diff_attn_branch(o, lam_init, subln_g, w_o):
    o = rms_norm(o, subln_g) * (1.0 - lam_init)
    return o.reshape(o.shape[:2] + (ATTN_W,)) @ w_o


def pool_branch(u, w_pool, pool_scale, w_o):
    b, l, _ = u.shape
    uf = u.astype(jnp.float32)
    cs = jnp.concatenate([jnp.zeros((b, 1, POOL_W), jnp.float32), jnp.cumsum(uf, axis=1)], axis=1)
    t = jnp.arange(l)
    outs = []
    for g, w in enumerate(POOL_WINDOWS):
        hi = jnp.minimum(t + w // 2, l)
        lo = jnp.maximum(t - w // 2, 0)
        csg = cs[..., g * POOL_GC:(g + 1) * POOL_GC]
        mean = (csg[:, hi] - csg[:, lo]) / (hi - lo).astype(jnp.float32)[None, :, None]
        outs.append(mean - uf[..., g * POOL_GC:(g + 1) * POOL_GC])
    pooled = jnp.stack(outs, axis=2).astype(u.dtype)
    mixed = jnp.einsum('blgc,gcd->blgd', pooled, w_pool).reshape(b, l, POOL_W) * pool_scale
    return mixed @ w_o


def short_conv(u, w, bias):
    y = lax.conv_general_dilated(u, w[:, None, :].astype(u.dtype), window_strides=(1,),
                                 padding=[(HYENA_SHORT // 2, HYENA_SHORT // 2)],
                                 dimension_numbers=('NWC', 'WIO', 'NWC'),
                                 feature_group_count=u.shape[-1])
    return y + bias


def hyena_filters(l, w1, b1, f1, w2, b2, f2, w3):
    f32 = jnp.float32
    t01 = jnp.linspace(0.0, 1.0, l, dtype=f32)
    tr = jnp.arange(l, dtype=f32)
    bands = jnp.linspace(1e-4, HYENA_BANDS - 1, HYENA_BANDS, dtype=f32)
    ang = 2.0 * math.pi * tr[:, None] * bands[None, :] / l
    z = jnp.concatenate([t01[:, None], jnp.cos(ang), jnp.sin(ang)], axis=-1)
    h = jnp.sin(f1.astype(f32) * (z @ w1.astype(f32) + b1.astype(f32)))
    h = jnp.sin(f2.astype(f32) * (h @ w2.astype(f32) + b2.astype(f32)))
    h = (h @ w3.astype(f32)).reshape(l, 2, HYENA_ORDER, HYENA_W)
    max_decay = math.log(HYENA_TARGET) / HYENA_FAST_DECAY
    min_decay = math.log(HYENA_TARGET) / HYENA_SLOW_DECAY
    deltas = jnp.abs(jnp.linspace(min_decay, max_decay, HYENA_W, dtype=f32))
    h = h * jnp.exp(-t01[:, None] * deltas[None, :])[:, None, None, :]
    k = jnp.concatenate([h[:, 0], jnp.zeros((1, HYENA_ORDER, HYENA_W), f32), h[1:, 1][::-1]], axis=0)
    k = k / jnp.sum(jnp.abs(k), axis=0, keepdims=True)
    return jnp.fft.rfft(k, axis=0)


def fft_long_conv(u, kf, bias):
    l = u.shape[1]
    uf = u.astype(jnp.float32)
    y = jnp.fft.irfft(jnp.fft.rfft(uf, n=2 * l, axis=1) * kf[None], n=2 * l, axis=1)[:, :l]
    return (y + uf * bias.astype(jnp.float32)).astype(u.dtype)


def hyena_branch(u, w_short, b_short, w1, b1, f1, w2, b2, f2, w3, hy_bias, w_o):
    l = u.shape[1]
    u = short_conv(u, w_short, b_short)
    v, x1, x2 = jnp.split(u, HYENA_ORDER + 1, axis=-1)
    kf = hyena_filters(l, w1, b1, f1, w2, b2, f2, w3)
    z = x1 * fft_long_conv(v, kf[:, 0], hy_bias[0])
    y = x2 * fft_long_conv(z, kf[:, 1], hy_bias[1])
    return y @ w_o


def mix_stream(proj, attn_o, lam_init, subln_g, w_attn_o, w_pool, pool_scale, w_pool_o, w_short, b_short,
               hf_w1, hf_b1, hf_freq1, hf_w2, hf_b2, hf_freq2, hf_w3, hy_bias, w_hy_o, w_out):
    b, l, _ = proj.shape
    a = diff_attn_branch(attn_o, lam_init, subln_g, w_attn_o)
    p = pool_branch(proj[..., P0:H0], w_pool, pool_scale, w_pool_o)
    y = hyena_branch(proj[..., H0:G0], w_short, b_short, hf_w1, hf_b1, hf_freq1, hf_w2, hf_b2, hf_freq2,
                     hf_w3, hy_bias, w_hy_o)
    g = jax.nn.sigmoid(proj[..., G0:].reshape(b, l, N_BRANCH, D_MODEL))
    merged = g[:, :, 0] * a + g[:, :, 1] * p + g[:, :, 2] * y
    return merged @ w_out


def swiglu(h, w_in, w_out):
    gate, up = jnp.split(h @ w_in, 2, axis=-1)
    return (jax.nn.silu(gate) * up) @ w_out


def setup_inputs(seed: int = 0) -> dict:
    key = jax.random.key(seed)
    ks = iter(jax.random.split(key, 40))
    f32 = jnp.float32

    def nrm(shape, scale):
        return jax.random.normal(next(ks), shape, f32) * scale

    d = D_MODEL
    return {
        'x': nrm((BATCH, SEQ, d), 1.0),
        'c': nrm((BATCH, d), 1.0),
        'ctx': nrm((BATCH, CTX_LEN, d), 1.0),
        'c_ctx': nrm((d,), 1.0),
        'w_mod': nrm((DEPTH, d, N_MOD * d), 0.5 * d ** -0.5),
        'b_mod': nrm((DEPTH, N_MOD * d), 0.02),
        'norm1_g': 1.0 + nrm((DEPTH, d), 0.02),
        'norm2_g': 1.0 + nrm((DEPTH, d), 0.02),
        'w_in': nrm((DEPTH, d, IN_W), d ** -0.5),
        'lam_qk': nrm((DEPTH, 4, ATTN_HD), 0.1),
        'subln_g': 1.0 + nrm((DEPTH, ATTN_VD), 0.02),
        'w_attn_o': nrm((DEPTH, ATTN_W, d), ATTN_W ** -0.5),
        'w_pool': nrm((DEPTH, POOL_GROUPS, POOL_GC, POOL_GC), POOL_GC ** -0.5),
        'pool_scale': 1.0 + nrm((DEPTH, POOL_W), 0.02),
        'w_pool_o': nrm((DEPTH, POOL_W, d), POOL_W ** -0.5),
        'w_short': nrm((DEPTH, HYENA_SHORT, (HYENA_ORDER + 1) * HYENA_W), HYENA_SHORT ** -0.5),
        'b_short': nrm((DEPTH, (HYENA_ORDER + 1) * HYENA_W), 0.02),
        'hf_w1': nrm((DEPTH, HYENA_EMB, HYENA_FFN), HYENA_EMB ** -0.5),
        'hf_b1': nrm((DEPTH, HYENA_FFN), 0.02),
        'hf_freq1': 1.0 + nrm((DEPTH, HYENA_FFN), 0.02),
        'hf_w2': nrm((DEPTH, HYENA_FFN, HYENA_FFN), HYENA_FFN ** -0.5),
        'hf_b2': nrm((DEPTH, HYENA_FFN), 0.02),
        'hf_freq2': 1.0 + nrm((DEPTH, HYENA_FFN), 0.02),
        'hf_w3': nrm((DEPTH, HYENA_FFN, 2 * HYENA_ORDER * HYENA_W), HYENA_FFN ** -0.5),
        'hy_bias': nrm((DEPTH, HYENA_ORDER, HYENA_W), 0.5),
        'w_hy_o': nrm((DEPTH, HYENA_W, d), HYENA_W ** -0.5),
        'w_out': nrm((DEPTH, d, d), d ** -0.5),
        'w_ffn_in': nrm((DEPTH, d, 2 * FFN_HIDDEN), d ** -0.5),
        'w_ffn_out': nrm((DEPTH, FFN_HIDDEN, d), FFN_HIDDEN ** -0.5),
        'final_g': 1.0 + nrm((d,), 0.02),
    }


def reference(x, c, ctx, c_ctx, w_mod, b_mod, norm1_g, norm2_g, w_in, lam_qk, subln_g, w_attn_o,
              w_pool, pool_scale, w_pool_o, w_short, b_short, hf_w1, hf_b1, hf_freq1, hf_w2, hf_b2,
              hf_freq2, hf_w3, hy_bias, w_hy_o, w_out, w_ffn_in, w_ffn_out, final_g):
    rows = x.shape[1] // GRID_W
    cos, sin = axial_rope_tables(rows)
    c_act = jax.nn.silu(c)
    cc_act = jax.nn.silu(c_ctx)
    h_lat, h_ctx = x, ctx
    for l in range(DEPTH):
        last = l == DEPTH - 1
        lam_init = LAMBDA_INIT_BASE - LAMBDA_INIT_AMP * math.exp(-LAMBDA_INIT_RATE * l)
        lq = lam_qk[l].astype(jnp.float32)
        lam = jnp.exp(jnp.sum(lq[0] * lq[1])) - jnp.exp(jnp.sum(lq[2] * lq[3])) + lam_init
        mod = jnp.split((c_act @ w_mod[l] + b_mod[l])[:, None, :], N_MOD, axis=-1)
        mod_c = jnp.split((cc_act @ w_mod[l] + b_mod[l])[None, None, :], N_MOD, axis=-1)
        mix_params = (subln_g[l], w_attn_o[l], w_pool[l], pool_scale[l], w_pool_o[l], w_short[l], b_short[l],
                      hf_w1[l], hf_b1[l], hf_freq1[l], hf_w2[l], hf_b2[l], hf_freq2[l], hf_w3[l], hy_bias[l],
                      w_hy_o[l], w_out[l])

        a_lat = modulate(rms_norm(h_lat, norm1_g[l]), mod[0], mod[1])
        a_ctx = modulate(rms_norm(h_ctx, norm1_g[l]), mod_c[0], mod_c[1])
        proj = a_lat @ w_in[l]
        q = apply_axial_rope(qk_heads(proj[..., Q0:K0]), cos, sin)
        k = apply_axial_rope(qk_heads(proj[..., K0:V0]), cos, sin)
        v = v_heads(proj[..., V0:P0])
        if last:
            proj_kv = a_ctx @ w_in[l][:, K0:P0]
            k_c = qk_heads(proj_kv[..., :ATTN_W])
            v_c = v_heads(proj_kv[..., ATTN_W:])
        else:
            proj_c = a_ctx @ w_in[l]
            k_c = qk_heads(proj_c[..., K0:V0])
            v_c = v_heads(proj_c[..., V0:P0])
            o_c = diff_attend(qk_heads(proj_c[..., Q0:K0]), k_c, v_c, lam)
            h_ctx_mid = h_ctx + mod_c[2] * mix_stream(proj_c, o_c, lam_init, *mix_params)
            h_ctx = h_ctx_mid + mod_c[5] * swiglu(
                modulate(rms_norm(h_ctx_mid, norm2_g[l]), mod_c[3], mod_c[4]), w_ffn_in[l], w_ffn_out[l])
        o = blocked_diff_attend(q, jnp.concatenate([k_c, k], axis=1), jnp.concatenate([v_c, v], axis=1), lam)
        h_lat = h_lat + mod[2] * mix_stream(proj, o, lam_init, *mix_params)

        h_lat = h_lat + mod[5] * swiglu(
            modulate(rms_norm(h_lat, norm2_g[l]), mod[3], mod[4]), w_ffn_in[l], w_ffn_out[l])
    return rms_norm(h_lat, final_g)
```

```python
import functools
import math

import jax
import jax.numpy as jnp
from jax import lax
from jax.experimental import pallas as pl
from jax.experimental.pallas import tpu as pltpu

F32 = jnp.float32
BF16 = jnp.bfloat16
HIGHEST = lax.Precision.HIGHEST

GRID_W = 64
N_MOD = 6
ATTN_HD = 64
ATTN_VD = 128
QK_SCALE = ATTN_HD ** -0.5
ROPE_BASE = 10000.0
ROPE_FREQS = 16
LAMBDA_INIT_BASE = 0.8
LAMBDA_INIT_AMP = 0.6
LAMBDA_INIT_RATE = 0.3
POOL_WINDOWS = (2, 4, 8, 16)
POOL_GC = 128
HALO = 8
HYENA_BANDS = 16
HYENA_FAST_DECAY = 0.3
HYENA_SLOW_DECAY = 1.5
HYENA_TARGET = 1e-2
EPS = 1e-6
DFT_R = 128

VMEM_LIMIT = 56 * 1024 * 1024


def _params(n_grid):
    return pltpu.CompilerParams(dimension_semantics=("arbitrary",) * n_grid,
                                vmem_limit_bytes=VMEM_LIMIT)


def _const_spec(shape):
    zeros = (0,) * len(shape)
    return pl.BlockSpec(shape, lambda *_: zeros, pipeline_mode=pl.Buffered(1))


def _dot(a, b):
    return jnp.dot(a, b, preferred_element_type=F32)


def _split3(w):
    hi = w.astype(BF16)
    lo = (w - hi.astype(F32)).astype(BF16)
    return jnp.concatenate([hi, hi, lo], axis=-1)


def _mm3(w3, x):
    xh = x.astype(BF16)
    xl = (x - xh.astype(F32)).astype(BF16)
    return _dot(w3, jnp.concatenate([xh, xl, xh], axis=0))


def _mod_kernel(c_ref, w_ref, b_ref, o_ref):
    c = c_ref[...]
    act = c * jax.nn.sigmoid(c)
    o_ref[0] = jnp.dot(act, w_ref[0], precision=HIGHEST, preferred_element_type=F32) + b_ref[0]


def _modulation(c8, w_mod, b_mod):
    depth, d, n = w_mod.shape
    tn = 1536
    return pl.pallas_call(
        _mod_kernel,
        grid=(depth, n // tn),
        in_specs=[pl.BlockSpec((8, d), lambda l, j: (0, 0)),
                  pl.BlockSpec((1, d, tn), lambda l, j: (l, 0, j)),
                  pl.BlockSpec((1, 1, tn), lambda l, j: (l, 0, j))],
        out_specs=pl.BlockSpec((1, 8, tn), lambda l, j: (l, 0, j)),
        out_shape=jax.ShapeDtypeStruct((depth, 8, n), F32),
        compiler_params=_params(2),
        name="modulation",
    )(c8, w_mod, b_mod.reshape(depth, 1, n))


def _mod_spec(mod, d, col):
    if mod.shape[0] == 1:
        return pl.BlockSpec((1, 1, d), lambda b, i: (0, 0, col))
    return pl.BlockSpec((1, 1, d), lambda b, i: (b, 0, col))


def _norm_mod(x, g, shift, scale):
    ms = jnp.mean(x * x, axis=-1, keepdims=True)
    a = x * lax.rsqrt(ms + EPS) * g
    return a * (1.0 + scale) + shift


def _inproj_kernel(*refs, segs, rope):
    h_ref, sh_ref, sc_ref, g_ref, w_ref = refs[:5]
    n_in = 7 if rope else 5
    out_refs = refs[n_in:]
    a = _norm_mod(h_ref[0], g_ref[...], sh_ref[0], sc_ref[0]).astype(BF16)
    col = 0
    for (kind, width, scale), o_ref in zip(segs, out_refs):
        p = _dot(a, w_ref[:, col:col + width])
        col += width
        if kind == "rope":
            psw = _dot(a, w_ref[:, col:col + width])
            col += width
            p = p * refs[5][...] + psw * refs[6][...]
        elif kind == "sigmoid":
            p = jax.nn.sigmoid(p)
        if scale != 1.0:
            p = p * scale
        o_ref[0] = p.astype(o_ref.dtype)


def _inproj(h, mod, norm_g, w_cat, segs, out_dtypes, rope_tabs=None, tm=256):
    b, s, d = h.shape
    tm = min(tm, s)
    rope = rope_tabs is not None
    in_specs = [pl.BlockSpec((1, tm, d), lambda b_, i: (b_, i, 0)),
                _mod_spec(mod, d, 0), _mod_spec(mod, d, 1),
                _const_spec((1, d)), _const_spec(w_cat.shape)]
    args = [h, mod, mod, norm_g.reshape(1, d), w_cat]
    if rope:
        wr = rope_tabs[0].shape[1]
        in_specs += [pl.BlockSpec((tm, wr), lambda b_, i: (i, 0))] * 2
        args += list(rope_tabs)
    out_specs = [pl.BlockSpec((1, tm, w), lambda b_, i: (b_, i, 0)) for (_, w, _) in segs]
    out_shape = [jax.ShapeDtypeStruct((b, s, w), dt) for (_, w, _), dt in zip(segs, out_dtypes)]
    return pl.pallas_call(
        functools.partial(_inproj_kernel, segs=tuple(segs), rope=rope),
        grid=(b, s // tm), in_specs=in_specs, out_specs=out_specs, out_shape=out_shape,
        compiler_params=_params(2), name="inproj",
    )(*args)


def _attn_kernel(lq_ref, q_ref, k_ref, v_ref, g_ref, o_ref, *, lam_init, tq):
    lq = lq_ref[...]
    lam = (jnp.exp(jnp.sum(lq[0:1] * lq[1:2], axis=-1, keepdims=True))
           - jnp.exp(jnp.sum(lq[2:3] * lq[3:4], axis=-1, keepdims=True)) + lam_init)
    q = q_ref[0]
    lane = lax.broadcasted_iota(jnp.int32, q.shape, 1)
    zero = jnp.zeros_like(q)
    qq = jnp.concatenate([jnp.where(lane < ATTN_HD, q, zero),
                          jnp.where(lane >= ATTN_HD, q, zero)], axis=0)
    s = lax.dot_general(qq, k_ref[0], (((1,), (1,)), ((), ())), preferred_element_type=F32)
    m = jnp.max(s, axis=-1, keepdims=True)
    p = jnp.exp(s - m)
    l = jnp.sum(p, axis=-1, keepdims=True)
    r0 = 1.0 / l[:tq]
    r1 = lam / l[tq:]
    a = (p[:tq] * r0 - p[tq:] * r1).astype(BF16)
    o = _dot(a, v_ref[0])
    ms = jnp.mean(o * o, axis=-1, keepdims=True)
    o_ref[0] = (o * lax.rsqrt(ms + EPS) * g_ref[...] * (1.0 - lam_init)).astype(o_ref.dtype)


def _attention(q, k, v, lam_qk, subln_g, lam_init, tq=128):
    b, s, w = q.shape
    lk = k.shape[1]
    heads = w // ATTN_VD
    tq = min(tq, s)
    return pl.pallas_call(
        functools.partial(_attn_kernel, lam_init=lam_init, tq=tq),
        grid=(b, heads, s // tq),
        in_specs=[pl.BlockSpec(lam_qk.shape, lambda b_, h, i: (0, 0)),
                  pl.BlockSpec((1, tq, ATTN_VD), lambda b_, h, i: (b_, i, h)),
                  pl.BlockSpec((1, lk, ATTN_VD), lambda b_, h, i: (b_, 0, h)),
                  pl.BlockSpec((1, lk, ATTN_VD), lambda b_, h, i: (b_, 0, h)),
                  pl.BlockSpec((1, ATTN_VD), lambda b_, h, i: (0, 0))],
        out_specs=pl.BlockSpec((1, tq, ATTN_VD), lambda b_, h, i: (b_, i, h)),
        out_shape=jax.ShapeDtypeStruct((b, s, w), F32),
        compiler_params=_params(3), name="diff_attention",
    )(lam_qk, q, k, v, subln_g.reshape(1, ATTN_VD))


def _local_kernel(up_ref, upp_ref, upn_ref, uh_ref, uhp_ref, uhn_ref, wp_ref, ps_ref, ws_ref,
                  bs_ref, pm_ref, v_ref, x1_ref, x2_ref, pbuf, hbuf, *, ts, seq):
    i = pl.program_id(1)
    first = i == 0
    last = i == pl.num_programs(1) - 1

    def fill(buf, cur, prev, nxt):
        buf[0:HALO] = jnp.where(first, 0.0, prev[0])
        buf[HALO:HALO + ts] = cur[0]
        buf[HALO + ts:HALO + ts + HALO] = jnp.where(last, 0.0, nxt[0])

    fill(pbuf, up_ref, upp_ref, upn_ref)
    fill(hbuf, uh_ref, uhp_ref, uhn_ref)

    t = i * ts + lax.broadcasted_iota(jnp.int32, (ts, 1), 0)
    for g, w in enumerate(POOL_WINDOWS):
        cs = slice(g * POOL_GC, (g + 1) * POOL_GC)
        acc = pbuf[HALO - w // 2:HALO - w // 2 + ts, cs]
        for j in range(1 - w // 2, w // 2):
            acc = acc + pbuf[HALO + j:HALO + j + ts, cs]
        cnt = (jnp.minimum(t + w // 2, seq) - jnp.maximum(t - w // 2, 0)).astype(F32)
        pooled = acc / cnt - pbuf[HALO:HALO + ts, cs]
        mixed = _dot(pooled.astype(BF16), wp_ref[g])
        pm_ref[0, :, cs] = mixed * ps_ref[:, cs]

    ws = ws_ref[...]
    conv = (hbuf[HALO - 1:HALO - 1 + ts] * ws[0:1] + hbuf[HALO:HALO + ts] * ws[1:2]
            + hbuf[HALO + 1:HALO + 1 + ts] * ws[2:3] + bs_ref[...])
    c = conv.shape[1] // 3
    v_ref[0] = conv[:, :c]
    x1_ref[0] = conv[:, c:2 * c]
    x2_ref[0] = conv[:, 2 * c:]


def _local_mixers(up, uh, w_pool, pool_scale, w_short, b_short, ts=512):
    b, s, cp = up.shape
    ch = uh.shape[2]
    c = ch // 3
    ts = min(ts, s)
    nh = ts // HALO
    last_halo = s // HALO - 1

    def cur(w):
        return pl.BlockSpec((1, ts, w), lambda b_, i: (b_, i, 0))

    def prev(w):
        return pl.BlockSpec((1, HALO, w), lambda b_, i: (b_, jnp.maximum(i * nh - 1, 0), 0))

    def nxt(w):
        return pl.BlockSpec((1, HALO, w), lambda b_, i: (b_, jnp.minimum((i + 1) * nh, last_halo), 0))

    return pl.pallas_call(
        functools.partial(_local_kernel, ts=ts, seq=s),
        grid=(b, s // ts),
        in_specs=[cur(cp), prev(cp), nxt(cp), cur(ch), prev(ch), nxt(ch),
                  _const_spec(w_pool.shape), _const_spec((1, cp)),
                  _const_spec(w_short.shape), _const_spec((1, ch))],
        out_specs=[cur(cp), cur(c), cur(c), cur(c)],
        out_shape=[jax.ShapeDtypeStruct((b, s, cp), F32)] + [jax.ShapeDtypeStruct((b, s, c), F32)] * 3,
        scratch_shapes=[pltpu.VMEM((ts + 2 * HALO, cp), F32), pltpu.VMEM((ts + 2 * HALO, ch), F32)],
        compiler_params=_params(2), name="pool_shortconv",
    )(up, up, up, uh, uh, uh, w_pool.astype(BF16), pool_scale.reshape(1, cp), w_short,
      b_short.reshape(1, ch))


def _filter_kernel(bands_ref, w1t_ref, w1c_ref, w1s_ref, b1_ref, f1_ref, w2_ref, b2_ref, f2_ref,
                   w3_ref, dl_ref, h_ref, asum_ref, *, tl, l):
    i = pl.program_id(0)
    t = (i * tl + lax.broadcasted_iota(jnp.int32, (tl, 1), 0)).astype(F32)
    t01 = t / float(l - 1)
    ang = (2.0 * math.pi / l) * t * bands_ref[...]
    pre = (t01 * w1t_ref[...]
           + jnp.dot(jnp.cos(ang), w1c_ref[...], precision=HIGHEST, preferred_element_type=F32)
           + jnp.dot(jnp.sin(ang), w1s_ref[...], precision=HIGHEST, preferred_element_type=F32))
    h1 = jnp.sin(f1_ref[...] * (pre + b1_ref[...]))
    h2 = jnp.sin(f2_ref[...] * (jnp.dot(h1, w2_ref[...], precision=HIGHEST,
                                        preferred_element_type=F32) + b2_ref[...]))
    h3 = jnp.dot(h2, w3_ref[...], precision=HIGHEST, preferred_element_type=F32)
    h = h3 * jnp.exp(-t01 * dl_ref[...])
    h_ref[...] = h
    col = lax.broadcasted_iota(jnp.int32, h.shape, 1)
    keep = jnp.logical_or(col < h.shape[1] // 2, t > 0.5)
    part = jnp.sum(jnp.where(keep, jnp.abs(h), 0.0), axis=0, keepdims=True)

    @pl.when(i == 0)
    def _():
        asum_ref[...] = jnp.zeros_like(asum_ref)

    asum_ref[...] += part


def _hyena_time_kernel(l, w1, b1, f1, w2, b2, f2, w3):
    ffn = w1.shape[1]
    nc = w3.shape[1]
    c = nc // 4
    tl = min(l, 512)
    bands = jnp.zeros((1, 128), F32).at[0, :HYENA_BANDS].set(
        jnp.linspace(1e-4, HYENA_BANDS - 1, HYENA_BANDS, dtype=F32))
    w1c = jnp.zeros((128, ffn), F32).at[:HYENA_BANDS].set(w1[1:1 + HYENA_BANDS])
    w1s = jnp.zeros((128, ffn), F32).at[:HYENA_BANDS].set(w1[1 + HYENA_BANDS:])
    max_decay = math.log(HYENA_TARGET) / HYENA_FAST_DECAY
    min_decay = math.log(HYENA_TARGET) / HYENA_SLOW_DECAY
    deltas = jnp.abs(jnp.linspace(min_decay, max_decay, c, dtype=F32))
    dl = jnp.tile(deltas, 4).reshape(1, nc)
    row = lambda a: a.reshape(1, -1)
    h, asum = pl.pallas_call(
        functools.partial(_filter_kernel, tl=tl, l=l),
        grid=(l // tl,),
        in_specs=[_const_spec((1, 128)), _const_spec((1, ffn)), _const_spec((128, ffn)),
                  _const_spec((128, ffn)), _const_spec((1, ffn)), _const_spec((1, ffn)),
                  _const_spec((ffn, ffn)), _const_spec((1, ffn)), _const_spec((1, ffn)),
                  _const_spec((ffn, nc)), _const_spec((1, nc))],
        out_specs=[pl.BlockSpec((tl, nc), lambda i: (i, 0)), pl.BlockSpec((1, nc), lambda i: (0, 0))],
        out_shape=[jax.ShapeDtypeStruct((l, nc), F32), jax.ShapeDtypeStruct((1, nc), F32)],
        compiler_params=_params(1), name="hyena_filter_mlp",
    )(bands, w1[0:1], w1c, w1s, row(b1), row(f1), w2, row(b2), row(f2), w3, dl)
    half = nc // 2
    kt = jnp.concatenate([h[:, :half], jnp.zeros((1, half), F32), h[1:, half:][::-1]], axis=0)
    inv_norm = 1.0 / (asum[:, :half] + asum[:, half:])
    return kt, inv_norm


def _angles(prod, n):
    th = (2.0 * math.pi / n) * (prod % n).astype(F32)
    return jnp.cos(th), jnp.sin(th)


def _dft_tables():
    r = DFT_R
    n = r * r
    idx = jnp.arange(r, dtype=jnp.int32)
    c, s = _angles(idx[:, None] * idx[None, :], r)
    ch, sh = c[:, :r // 2], s[:, :r // 2]
    w1_data = jnp.concatenate([jnp.concatenate([ch, sh], 1), jnp.concatenate([-sh, ch], 1)], 0)
    w1_real = jnp.concatenate([c, -s], 0)
    ct, st = c[:r // 2], s[:r // 2]
    w2 = jnp.concatenate([jnp.concatenate([ct, -st], 1), jnp.concatenate([st, ct], 1)], 0)
    k = idx[:, None, None] + r * idx[None, :, None]
    cg, sg = _angles(k * idx[None, None, :], n)
    g = jnp.concatenate([jnp.concatenate([cg, sg], 2), jnp.concatenate([-sg, cg], 2)], 1)
    cgt, sgt = jnp.swapaxes(cg, 1, 2), jnp.swapaxes(sg, 1, 2)
    gi = jnp.concatenate([jnp.concatenate([cgt, -sgt], 2), jnp.concatenate([sgt, cgt], 2)], 1)
    return dict(w1_data=_split3(w1_data), w1_real=_split3(w1_real), w2=_split3(w2),
                g=_split3(g), gi=_split3(gi))


def _dft1_kernel(x_ref, w_ref, o_ref, *, parts):
    x = jnp.concatenate([x_ref[0, p] for p in range(parts)], axis=0)
    o_ref[0] = _mm3(w_ref[...], x)


def _dft_major(x, w3, tc=4096):
    p, parts, r, cols = x.shape
    return pl.pallas_call(
        functools.partial(_dft1_kernel, parts=parts),
        grid=(p, cols // tc),
        in_specs=[pl.BlockSpec((1, parts, r, tc), lambda p_, j: (p_, 0, 0, j)), _const_spec(w3.shape)],
        out_specs=pl.BlockSpec((1, 2 * DFT_R, tc), lambda p_, j: (p_, 0, j)),
        out_shape=jax.ShapeDtypeStruct((p, 2 * DFT_R, cols), F32),
        compiler_params=_params(2), name="dft_major",
    )(x, w3)


def _spectrum_kernel(a_ref, g_ref, sc_ref, o_ref):
    a = jnp.concatenate([a_ref[0, 0, 0], a_ref[0, 1, 0]], axis=0)
    x = _mm3(g_ref[0], a) * sc_ref[...]
    r = x.shape[0] // 2
    o_ref[0, 0] = x[:r]
    o_ref[1, 0] = x[r:]


def _filter_spectrum(kt, scale, tabs):
    r = DFT_R
    c = kt.shape[1]
    a = _dft_major(kt.reshape(1, 1, r, r * c), tabs["w1_real"])
    a = a.reshape(1, 2, r, r, c)
    tcs = min(512, c)
    return pl.pallas_call(
        _spectrum_kernel,
        grid=(r, c // tcs),
        in_specs=[pl.BlockSpec((1, 2, 1, r, tcs), lambda k, j: (0, 0, k, 0, j)),
                  pl.BlockSpec((1, 2 * r, 6 * r), lambda k, j: (k, 0, 0)),
                  pl.BlockSpec((1, tcs), lambda k, j: (0, j))],
        out_specs=pl.BlockSpec((2, 1, r, tcs), lambda k, j: (0, k, 0, j)),
        out_shape=jax.ShapeDtypeStruct((2, r, r, c), F32),
        compiler_params=_params(2), name="filter_spectrum",
    )(a, tabs["g"], scale)


def _dft2_kernel(a_ref, g_ref, gi_ref, h_ref, o_ref):
    a = jnp.concatenate([a_ref[0, 0, 0], a_ref[0, 1, 0]], axis=0)
    x = _mm3(g_ref[0], a)
    r = x.shape[0] // 2
    xr, xi = x[:r], x[r:]
    hr, hi = h_ref[0, 0], h_ref[1, 0]
    y = jnp.concatenate([xr * hr - xi * hi, xr * hi + xi * hr], axis=0)
    bb = _mm3(gi_ref[0], y)
    o_ref[0, 0, 0] = bb[:r]
    o_ref[0, 1, 0] = bb[r:]


def _dft3_kernel(b_ref, w_ref, u_ref, x_ref, bias_ref, o_ref):
    y = _mm3(w_ref[...], b_ref[0])
    half = y.shape[0] // 2
    bias = bias_ref[...]
    o_ref[0, 0] = x_ref[0, 0] * (y[:half] + u_ref[0, 0] * bias)
    o_ref[0, 1] = x_ref[0, 1] * (y[half:] + u_ref[0, 1] * bias)


def _long_conv_gate(u, x, hspec, order, bias, tabs, tc=4096):
    b, l, c = u.shape
    r = DFT_R
    p = b // 2
    cols = r * c
    a = _dft_major(u.reshape(p, 2, r // 2, cols), tabs["w1_data"], tc)
    a = a.reshape(p, 2, r, r, c)
    bb = pl.pallas_call(
        _dft2_kernel,
        grid=(r, p),
        in_specs=[pl.BlockSpec((1, 2, 1, r, c), lambda k, p_: (p_, 0, k, 0, 0)),
                  pl.BlockSpec((1, 2 * r, 6 * r), lambda k, p_: (k, 0, 0)),
                  pl.BlockSpec((1, 2 * r, 6 * r), lambda k, p_: (k, 0, 0)),
                  pl.BlockSpec((2, 1, r, c), lambda k, p_: (0, k, 0, order))],
        out_specs=pl.BlockSpec((1, 2, 1, r, c), lambda k, p_: (p_, 0, k, 0, 0)),
        out_shape=jax.ShapeDtypeStruct((p, 2, r, r, c), F32),
        compiler_params=_params(2), name="dft_minor_filter",
    )(a, tabs["g"], tabs["gi"], hspec)
    bb = bb.reshape(p, 2 * r, cols)
    bias_t = jnp.tile(bias.reshape(1, c), (1, tc // c))
    view = (p, 2, r // 2, cols)
    blk = pl.BlockSpec((1, 2, r // 2, tc), lambda p_, j: (p_, 0, 0, j))
    out = pl.pallas_call(
        _dft3_kernel,
        grid=(p, cols // tc),
        in_specs=[pl.BlockSpec((1, 2 * r, tc), lambda p_, j: (p_, 0, j)),
                  _const_spec(tabs["w2"].shape), blk, blk, _const_spec((1, tc))],
        out_specs=blk,
        out_shape=jax.ShapeDtypeStruct(view, F32),
        compiler_params=_params(2), name="idft_major_gate",
    )(bb, tabs["w2"], u.reshape(view), x.reshape(view), bias_t)
    return out.reshape(b, l, c)


def _dense_tables(l):
    n = 2 * l
    k = jnp.arange(n, dtype=jnp.int32)
    t = jnp.arange(l, dtype=jnp.int32)
    c, s = _angles(k[:, None] * t[None, :], n)
    wf = jnp.concatenate([jnp.concatenate([c, s], 1), jnp.concatenate([-s, c], 1)], 0)
    ct, st = c.T, s.T
    wi = jnp.concatenate([jnp.concatenate([ct, -st], 1), jnp.concatenate([st, ct], 1)], 0)
    ca, sa = _angles(k[:, None] * k[None, :], n)
    wk = jnp.concatenate([ca, -sa], 0)
    return dict(wf=_split3(wf), wi=_split3(wi), wk=_split3(wk))


def _dense_spectrum_kernel(k_ref, w_ref, sc_ref, o_ref):
    o_ref[...] = _mm3(w_ref[...], k_ref[...]) * sc_ref[...]


def _dense_spectrum(kt, scale, wk):
    n, c = kt.shape
    return pl.pallas_call(
        _dense_spectrum_kernel,
        grid=(1,),
        in_specs=[_const_spec((n, c)), _const_spec(wk.shape), _const_spec((1, c))],
        out_specs=pl.BlockSpec((2 * n, c), lambda i: (0, 0)),
        out_shape=jax.ShapeDtypeStruct((2 * n, c), F32),
        compiler_params=_params(1), name="dense_filter_spectrum",
    )(kt, wk, scale)


def _dense_conv_kernel(u_ref, x_ref, wf_ref, wi_ref, h_ref, bias_ref, o_ref):
    u0, u1 = u_ref[0], u_ref[1]
    z = _mm3(wf_ref[...], jnp.concatenate([u0, u1], axis=0))
    n = z.shape[0] // 2
    zr, zi = z[:n], z[n:]
    hr, hi = h_ref[:n], h_ref[n:]
    y = _mm3(wi_ref[...], jnp.concatenate([zr * hr - zi * hi, zr * hi + zi * hr], axis=0))
    l = y.shape[0] // 2
    bias = bias_ref[...]
    o_ref[0] = x_ref[0] * (y[:l] + u0 * bias)
    o_ref[1] = x_ref[1] * (y[l:] + u1 * bias)


def _dense_conv_gate(u, x, hspec, order, bias, tabs):
    b, l, c = u.shape
    blk = pl.BlockSpec((2, l, c), lambda p_: (p_, 0, 0))
    return pl.pallas_call(
        _dense_conv_kernel,
        grid=(b // 2,),
        in_specs=[blk, blk, _const_spec(tabs["wf"].shape), _const_spec(tabs["wi"].shape),
                  pl.BlockSpec((4 * l, c), lambda p_: (0, order)), _const_spec((1, c))],
        out_specs=blk,
        out_shape=jax.ShapeDtypeStruct((b, l, c), F32),
        compiler_params=_params(1), name="dense_conv_gate",
    )(u, x, tabs["wf"], tabs["wi"], hspec, bias.reshape(1, c))


def _mix_kernel(h_ref, o_ref, pm_ref, y_ref, g_ref, gate_ref, wa_ref, wp_ref, wh_ref, wo_ref, out_ref):
    d = h_ref.shape[2]
    a = _dot(o_ref[0].astype(BF16), wa_ref[...])
    p = _dot(pm_ref[0].astype(BF16), wp_ref[...])
    y = _dot(y_ref[0].astype(BF16), wh_ref[...])
    merged = g_ref[0, :, 0:d] * a + g_ref[0, :, d:2 * d] * p + g_ref[0, :, 2 * d:3 * d] * y
    out = _dot(merged.astype(BF16), wo_ref[...])
    out_ref[0] = h_ref[0] + gate_ref[0] * out


def _mix(h, o, pm, y, g, mod, w_attn_o, w_pool_o, w_hy_o, w_out, tm=512):
    b, s, d = h.shape
    tm = min(tm, s)

    def tok(w):
        return pl.BlockSpec((1, tm, w), lambda b_, i: (b_, i, 0))

    ws = [w.astype(BF16) for w in (w_attn_o, w_pool_o, w_hy_o, w_out)]
    return pl.pallas_call(
        _mix_kernel,
        grid=(b, s // tm),
        in_specs=[tok(d), tok(o.shape[2]), tok(pm.shape[2]), tok(y.shape[2]), tok(g.shape[2]),
                  _mod_spec(mod, d, 2)] + [_const_spec(w.shape) for w in ws],
        out_specs=tok(d),
        out_shape=jax.ShapeDtypeStruct((b, s, d), F32),
        compiler_params=_params(2), name="merge_outproj",
    )(h, o, pm, y, g, mod, *ws)


def _ffn_kernel(h_ref, sh_ref, sc_ref, gate_ref, g_ref, wi_ref, wo_ref, fg_ref, out_ref, *,
                hidden, chunk, final):
    x = h_ref[0]
    a = _norm_mod(x, g_ref[...], sh_ref[0], sc_ref[0]).astype(BF16)
    acc = jnp.zeros(x.shape, F32)
    for c0 in range(0, hidden, chunk):
        gt = _dot(a, wi_ref[:, c0:c0 + chunk])
        up = _dot(a, wi_ref[:, hidden + c0:hidden + c0 + chunk])
        act = (gt * jax.nn.sigmoid(gt) * up).astype(BF16)
        acc = acc + _dot(act, wo_ref[c0:c0 + chunk, :])
    y = x + gate_ref[0] * acc
    if final:
        ms = jnp.mean(y * y, axis=-1, keepdims=True)
        y = y * lax.rsqrt(ms + EPS) * fg_ref[...]
    out_ref[0] = y


def _ffn(h, mod, norm_g, w_in, w_out, final_g, final, tm=512):
    b, s, d = h.shape
    hidden = w_out.shape[0]
    tm = min(tm, s)
    tok = pl.BlockSpec((1, tm, d), lambda b_, i: (b_, i, 0))
    return pl.pallas_call(
        functools.partial(_ffn_kernel, hidden=hidden, chunk=hidden // 2, final=final),
        grid=(b, s // tm),
        in_specs=[tok, _mod_spec(mod, d, 3), _mod_spec(mod, d, 4), _mod_spec(mod, d, 5),
                  _const_spec((1, d)), _const_spec(w_in.shape), _const_spec(w_out.shape),
                  _const_spec((1, d))],
        out_specs=tok,
        out_shape=jax.ShapeDtypeStruct((b, s, d), F32),
        compiler_params=_params(2), name="swiglu_ffn",
    )(h, mod, mod, mod, norm_g.reshape(1, d), w_in.astype(BF16), w_out.astype(BF16),
      final_g.reshape(1, d))


def _rope_tables(seq, width):
    t = jnp.arange(seq, dtype=jnp.int32)
    row = (t // GRID_W).astype(F32)
    col = (t % GRID_W).astype(F32)
    inv = 1.0 / (ROPE_BASE ** (jnp.arange(ROPE_FREQS, dtype=F32) * 2.0 / (2 * ROPE_FREQS)))
    ar, ac = row[:, None] * inv, col[:, None] * inv
    cos = jnp.concatenate([jnp.cos(ar)] * 2 + [jnp.cos(ac)] * 2, axis=1)
    sin = jnp.concatenate([-jnp.sin(ar), jnp.sin(ar), -jnp.sin(ac), jnp.sin(ac)], axis=1)
    reps = width // cos.shape[1]
    return jnp.tile(cos, (1, reps)), jnp.tile(sin, (1, reps))


def _swap_halves(w):
    j = jnp.arange(w.shape[1])
    return w[:, jnp.where(j % (2 * ROPE_FREQS) < ROPE_FREQS, j + ROPE_FREQS, j - ROPE_FREQS)]


def kernel(x, c, ctx, c_ctx, w_mod, b_mod, norm1_g, norm2_g, w_in, lam_qk, subln_g, w_attn_o, w_pool, pool_scale, w_pool_o, w_short, b_short, hf_w1, hf_b1, hf_freq1, hf_w2, hf_b2, hf_freq2, hf_w3, hy_bias, w_hy_o, w_out, w_ffn_in, w_ffn_out, final_g):
    batch, seq, d = x.shape
    depth = w_mod.shape[0]
    ctx_len = ctx.shape[1]
    aw = w_attn_o.shape[1]
    pw = w_pool_o.shape[1]
    hw = w_hy_o.shape[1]
    q0, k0, v0, p0 = 0, aw, 2 * aw, 3 * aw
    h0 = p0 + pw
    g0 = h0 + 3 * hw
    assert seq * 2 == DFT_R * DFT_R and batch % 2 == 0

    c8 = jnp.zeros((8, d), F32).at[:batch].set(c).at[batch].set(c_ctx)
    mod_all = _modulation(c8, w_mod, b_mod)
    rope_tabs = _rope_tables(seq, aw)
    tabs = _dft_tables()
    ctx_tabs = _dense_tables(ctx_len)

    h_lat, h_ctx = x, ctx
    for l in range(depth):
        last = l == depth - 1
        lam_init = LAMBDA_INIT_BASE - LAMBDA_INIT_AMP * math.exp(-LAMBDA_INIT_RATE * l)
        mod = mod_all[l, :batch].reshape(batch, 1, N_MOD * d)
        mod_c = mod_all[l, batch:batch + 1].reshape(1, 1, N_MOD * d)
        wl = w_in[l]
        wq, wk = wl[:, q0:k0], wl[:, k0:v0]
        fparams = (hf_w1[l], hf_b1[l], hf_freq1[l], hf_w2[l], hf_b2[l], hf_freq2[l], hf_w3[l])

        w_cat = jnp.concatenate([wq, _swap_halves(wq), wk, _swap_halves(wk), wl[:, v0:]], axis=1)
        segs = [("rope", aw, QK_SCALE), ("rope", aw, 1.0), ("plain", aw, 1.0), ("plain", pw, 1.0),
                ("plain", 3 * hw, 1.0), ("sigmoid", 3 * d, 1.0)]
        q, k, v, up, uh, g = _inproj(h_lat, mod, norm1_g[l], w_cat.astype(BF16), segs,
                                     [BF16, BF16, BF16, F32, F32, F32], rope_tabs)

        if last:
            segs_c = [("plain", aw, 1.0), ("plain", aw, 1.0)]
            k_c, v_c = _inproj(h_ctx, mod_c, norm1_g[l], wl[:, k0:p0].astype(BF16), segs_c, [BF16, BF16])
        else:
            segs_c = [("plain", aw, QK_SCALE), ("plain", aw, 1.0), ("plain", aw, 1.0), ("plain", pw, 1.0),
                      ("plain", 3 * hw, 1.0), ("sigmoid", 3 * d, 1.0)]
            q_c, k_c, v_c, up_c, uh_c, g_c = _inproj(h_ctx, mod_c, norm1_g[l], wl.astype(BF16), segs_c,
                                                    [BF16, BF16, BF16, F32, F32, F32])
            o_c = _attention(q_c, k_c, v_c, lam_qk[l], subln_g[l], lam_init)
            pm_c, vv_c, x1_c, x2_c = _local_mixers(up_c, uh_c, w_pool[l], pool_scale[l], w_short[l], b_short[l])
            kt_c, inv_norm_c = _hyena_time_kernel(ctx_len, *fparams)
            hspec_c = _dense_spectrum(kt_c, inv_norm_c / (2 * ctx_len), ctx_tabs["wk"])
            z_c = _dense_conv_gate(vv_c, x1_c, hspec_c, 0, hy_bias[l, 0], ctx_tabs)
            y_c = _dense_conv_gate(z_c, x2_c, hspec_c, 1, hy_bias[l, 1], ctx_tabs)
            h_ctx_mid = _mix(h_ctx, o_c, pm_c, y_c, g_c, mod_c, w_attn_o[l], w_pool_o[l], w_hy_o[l], w_out[l])
            h_ctx_new = _ffn(h_ctx_mid, mod_c, norm2_g[l], w_ffn_in[l], w_ffn_out[l], final_g, False)

        o = _attention(q, jnp.concatenate([k_c, k], axis=1), jnp.concatenate([v_c, v], axis=1),
                       lam_qk[l], subln_g[l], lam_init)
        pm, vv, x1, x2 = _local_mixers(up, uh, w_pool[l], pool_scale[l], w_short[l], b_short[l])
        kt, inv_norm = _hyena_time_kernel(seq, *fparams)
        hspec = _filter_spectrum(kt, inv_norm / (2 * seq), tabs)
        z = _long_conv_gate(vv, x1, hspec, 0, hy_bias[l, 0], tabs)
        y = _long_conv_gate(z, x2, hspec, 1, hy_bias[l, 1], tabs)
        h_mid = _mix(h_lat, o, pm, y, g, mod, w_attn_o[l], w_pool_o[l], w_hy_o[l], w_out[l])
        h_lat = _ffn(h_mid, mod, norm2_g[l], w_ffn_in[l], w_ffn_out[l], final_g, last)
        if not last:
            h_ctx = h_ctx_new
    return h_lat
```

```python
import functools
import math

import jax
import jax.numpy as jnp
from jax import lax
from jax.experimental import pallas as pl
from jax.experimental.pallas import tpu as pltpu

F32 = jnp.float32
BF16 = jnp.bfloat16
HIGHEST = lax.Precision.HIGHEST

GRID_W = 64
N_MOD = 6
ATTN_HD = 64
ATTN_VD = 128
Q_SCALE = ATTN_HD ** -0.5 * math.log2(math.e)
ROPE_BASE = 10000.0
ROPE_FREQS = 16
LAMBDA_INIT_BASE = 0.8
LAMBDA_INIT_AMP = 0.6
LAMBDA_INIT_RATE = 0.3
POOL_WINDOWS = (2, 4, 8, 16)
POOL_GC = 128
HALO = 8
HYENA_BANDS = 16
HYENA_FAST_DECAY = 0.3
HYENA_SLOW_DECAY = 1.5
HYENA_TARGET = 1e-2
EPS = 1e-6
DFT_DATA_PASSES = 1
DFT_R = 128

VMEM_LIMIT = 56 * 1024 * 1024


def _params(n_grid, flags=None):
    return pltpu.CompilerParams(dimension_semantics=("arbitrary",) * n_grid,
                                vmem_limit_bytes=VMEM_LIMIT, flags=flags)


def _const_spec(shape):
    zeros = (0,) * len(shape)
    return pl.BlockSpec(shape, lambda *_: zeros, pipeline_mode=pl.Buffered(1))


def _dot(a, b):
    return jnp.dot(a, b, preferred_element_type=F32)


def _split3(w, passes=3):
    hi = w.astype(BF16)
    if passes == 1:
        return hi
    lo = (w - hi.astype(F32)).astype(BF16)
    return jnp.concatenate([hi, hi, lo], axis=-1)


def _mm3(w3, x):
    xh = x.astype(BF16)
    if w3.shape[-1] == x.shape[0]:
        return _dot(w3, xh)
    xl = (x - xh.astype(F32)).astype(BF16)
    return _dot(w3, jnp.concatenate([xh, xl, xh], axis=0))


def _mod_kernel(c_ref, w_ref, b_ref, o_ref):
    c = c_ref[...]
    act = c * jax.nn.sigmoid(c)
    o_ref[0] = jnp.dot(act, w_ref[0], precision=HIGHEST, preferred_element_type=F32) + b_ref[0]


def _modulation(c8, w_mod, b_mod):
    depth, d, n = w_mod.shape
    tn = 1536
    return pl.pallas_call(
        _mod_kernel,
        grid=(depth, n // tn),
        in_specs=[pl.BlockSpec((8, d), lambda l, j: (0, 0)),
                  pl.BlockSpec((1, d, tn), lambda l, j: (l, 0, j)),
                  pl.BlockSpec((1, 1, tn), lambda l, j: (l, 0, j))],
        out_specs=pl.BlockSpec((1, 8, tn), lambda l, j: (l, 0, j)),
        out_shape=jax.ShapeDtypeStruct((depth, 8, n), F32),
        compiler_params=_params(2),
        name="modulation",
    )(c8, w_mod, b_mod.reshape(depth, 1, n))


def _mod_spec(mod, d, col):
    if mod.shape[0] == 1:
        return pl.BlockSpec((1, 1, d), lambda b, i: (0, 0, col))
    return pl.BlockSpec((1, 1, d), lambda b, i: (b, 0, col))


def _norm_mod(x, g, shift, scale):
    ms = jnp.mean(x * x, axis=-1, keepdims=True)
    a = x * lax.rsqrt(ms + EPS) * g
    return a * (1.0 + scale) + shift


def _inproj_kernel(*refs, segs, rope):
    h_ref, sh_ref, sc_ref, g_ref, w_ref = refs[:5]
    n_in = 7 if rope else 5
    out_refs = refs[n_in:]
    a = _norm_mod(h_ref[0], g_ref[...], sh_ref[0], sc_ref[0]).astype(BF16)
    col = 0
    for (kind, width, scale), o_ref in zip(segs, out_refs):
        p = _dot(a, w_ref[:, col:col + width])
        col += width
        if kind == "rope":
            psw = _dot(a, w_ref[:, col:col + width])
            col += width
            p = p * refs[5][...] + psw * refs[6][...]
        elif kind == "sigmoid":
            p = jax.nn.sigmoid(p)
        if scale != 1.0:
            p = p * scale
        o_ref[0] = p.astype(o_ref.dtype)


def _inproj(h, mod, norm_g, w_cat, segs, out_dtypes, rope_tabs=None, tm=256):
    b, s, d = h.shape
    tm = min(tm, s)
    rope = rope_tabs is not None
    in_specs = [pl.BlockSpec((1, tm, d), lambda b_, i: (b_, i, 0)),
                _mod_spec(mod, d, 0), _mod_spec(mod, d, 1),
                _const_spec((1, d)), _const_spec(w_cat.shape)]
    args = [h, mod, mod, norm_g.reshape(1, d), w_cat]
    if rope:
        wr = rope_tabs[0].shape[1]
        in_specs += [pl.BlockSpec((tm, wr), lambda b_, i: (i, 0))] * 2
        args += list(rope_tabs)
    out_specs = [pl.BlockSpec((1, tm, w), lambda b_, i: (b_, i, 0)) for (_, w, _) in segs]
    out_shape = [jax.ShapeDtypeStruct((b, s, w), dt) for (_, w, _), dt in zip(segs, out_dtypes)]
    return pl.pallas_call(
        functools.partial(_inproj_kernel, segs=tuple(segs), rope=rope),
        grid=(b, s // tm), in_specs=in_specs, out_specs=out_specs, out_shape=out_shape,
        compiler_params=_params(2), name="inproj",
    )(*args)


def _lambda(lq_ref, lam_init):
    lq = lq_ref[...]
    return (jnp.exp(jnp.sum(lq[0:1] * lq[1:2], axis=-1, keepdims=True))
            - jnp.exp(jnp.sum(lq[2:3] * lq[3:4], axis=-1, keepdims=True)) + lam_init)


def _split_halves(q):
    lane = lax.broadcasted_iota(jnp.int32, q.shape, 1)
    zero = jnp.zeros_like(q)
    return jnp.concatenate([jnp.where(lane < ATTN_HD, q, zero),
                            jnp.where(lane >= ATTN_HD, q, zero)], axis=0)


def _qk(qq, k):
    return lax.dot_general(qq, k, (((1,), (1,)), ((), ())), preferred_element_type=F32)


def _sub_ln(o, g, lam_init):
    ms = jnp.mean(o * o, axis=-1, keepdims=True)
    return o * lax.rsqrt(ms + EPS) * g * (1.0 - lam_init)


def _attn_kernel(lq_ref, q_ref, k_ref, v_ref, g_ref, o_ref, *, lam_init, tq):
    lam = _lambda(lq_ref, lam_init)
    s = _qk(_split_halves(q_ref[0]), k_ref[0])
    m = jnp.max(s, axis=-1, keepdims=True)
    p = jnp.exp2(s - m)
    l = jnp.sum(p, axis=-1, keepdims=True)
    a = (p[:tq] - p[tq:] * (lam * l[:tq] / l[tq:])).astype(BF16)
    o = _dot(a, v_ref[0]) / l[:tq]
    o_ref[0] = _sub_ln(o, g_ref[...], lam_init).astype(o_ref.dtype)


def _attn_pipe_kernel(lq_ref, q_ref, k_ref, v_ref, g_ref, o_ref, s0, s1, s2, m_scr, l_scr, *,
                      lam_init, tq, lk, kc, nq):
    j = pl.program_id(2)
    bufs = (s0, s1, s2)
    lanes = ATTN_VD
    ngrp = kc // lanes

    @pl.when(j == 0)
    def _():
        s1[...] = jnp.zeros_like(s1)
        s2[...] = jnp.zeros_like(s2)
        m_scr[2] = jnp.zeros(m_scr.shape[1:], F32)
        l_scr[1] = jnp.ones(l_scr.shape[1:], F32)

    lam = _lambda(lq_ref, lam_init)
    for r in range(3):
        sig = 3 * j + r
        buf_a, buf_b, buf_c = bufs[r], bufs[(r + 2) % 3], bufs[(r + 1) % 3]
        t_a = jnp.minimum(sig, nq - 1)
        t_c = jnp.clip(sig - 2, 0, nq - 1)
        qq = _split_halves(q_ref[0, pl.ds(pl.multiple_of(t_a * tq, tq), tq), :])
        m_b = m_scr[(r + 2) % 3]
        l_c = l_scr[(r + 1) % 3]
        ratio = lam * l_c[:tq] / l_c[tq:]
        m_a = None
        l_b = None
        o_c = None
        for c0 in range(0, lk, kc):
            cols = slice(c0, c0 + kc)
            s = _qk(qq, k_ref[0, cols, :])
            buf_a[:, cols] = s
            for g in range(ngrp):
                sg = s[:, g * lanes:(g + 1) * lanes]
                m_a = sg if m_a is None else jnp.maximum(m_a, sg)
            for g in range(ngrp):
                gc = slice(c0 + g * lanes, c0 + (g + 1) * lanes)
                p = jnp.exp2(buf_b[:, gc] - m_b)
                buf_b[:, gc] = p
                l_b = p if l_b is None else l_b + p
            pc = buf_c[:, cols]
            a = jnp.concatenate([pc[:tq, g * lanes:(g + 1) * lanes] - pc[tq:, g * lanes:(g + 1) * lanes] * ratio
                                 for g in range(ngrp)], axis=1).astype(BF16)
            pv = _dot(a, v_ref[0, cols, :])
            o_c = pv if o_c is None else o_c + pv
        m_scr[r] = jnp.broadcast_to(jnp.max(m_a, axis=-1, keepdims=True), m_a.shape)
        l_scr[(r + 2) % 3] = jnp.broadcast_to(jnp.sum(l_b, axis=-1, keepdims=True), l_b.shape)
        o = o_c / l_c[:tq]
        o_ref[0, pl.ds(pl.multiple_of(t_c * tq, tq), tq), :] = _sub_ln(o, g_ref[...], lam_init).astype(o_ref.dtype)


def _attention_pipelined(q, k, v, lam_qk, subln_g, lam_init, tq=128, kc=768):
    b, s, w = q.shape
    lk = k.shape[1]
    heads = w // ATTN_VD
    nq = s // tq
    nsteps = -(-(nq + 2) // 3)
    return pl.pallas_call(
        functools.partial(_attn_pipe_kernel, lam_init=lam_init, tq=tq, lk=lk, kc=kc, nq=nq),
        grid=(b, heads, nsteps),
        in_specs=[pl.BlockSpec(lam_qk.shape, lambda b_, h, i: (0, 0)),
                  pl.BlockSpec((1, s, ATTN_VD), lambda b_, h, i: (b_, 0, h)),
                  pl.BlockSpec((1, lk, ATTN_VD), lambda b_, h, i: (b_, 0, h)),
                  pl.BlockSpec((1, lk, ATTN_VD), lambda b_, h, i: (b_, 0, h)),
                  pl.BlockSpec((1, ATTN_VD), lambda b_, h, i: (0, 0))],
        out_specs=pl.BlockSpec((1, s, ATTN_VD), lambda b_, h, i: (b_, 0, h)),
        out_shape=jax.ShapeDtypeStruct((b, s, w), F32),
        scratch_shapes=[pltpu.VMEM((2 * tq, lk), F32)] * 3
        + [pltpu.VMEM((3, 2 * tq, ATTN_VD), F32)] * 2,
        compiler_params=_params(3), name="diff_attention_pipelined",
    )(lam_qk, q, k, v, subln_g.reshape(1, ATTN_VD))


def _attention(q, k, v, lam_qk, subln_g, lam_init, tq=128):
    b, s, w = q.shape
    lk = k.shape[1]
    heads = w // ATTN_VD
    tq = min(tq, s)
    return pl.pallas_call(
        functools.partial(_attn_kernel, lam_init=lam_init, tq=tq),
        grid=(b, heads, s // tq),
        in_specs=[pl.BlockSpec(lam_qk.shape, lambda b_, h, i: (0, 0)),
                  pl.BlockSpec((1, tq, ATTN_VD), lambda b_, h, i: (b_, i, h)),
                  pl.BlockSpec((1, lk, ATTN_VD), lambda b_, h, i: (b_, 0, h)),
                  pl.BlockSpec((1, lk, ATTN_VD), lambda b_, h, i: (b_, 0, h)),
                  pl.BlockSpec((1, ATTN_VD), lambda b_, h, i: (0, 0))],
        out_specs=pl.BlockSpec((1, tq, ATTN_VD), lambda b_, h, i: (b_, i, h)),
        out_shape=jax.ShapeDtypeStruct((b, s, w), F32),
        compiler_params=_params(3), name="diff_attention",
    )(lam_qk, q, k, v, subln_g.reshape(1, ATTN_VD))


def _local_kernel(up_ref, upp_ref, upn_ref, uh_ref, uhp_ref, uhn_ref, wp_ref, ps_ref, ws_ref,
                  bs_ref, pm_ref, v_ref, x1_ref, x2_ref, pbuf, hbuf, *, ts, seq):
    i = pl.program_id(1)
    first = i == 0
    last = i == pl.num_programs(1) - 1

    def fill(buf, cur, prev, nxt):
        buf[0:HALO] = jnp.where(first, 0.0, prev[0])
        buf[HALO:HALO + ts] = cur[0]
        buf[HALO + ts:HALO + ts + HALO] = jnp.where(last, 0.0, nxt[0])

    fill(pbuf, up_ref, upp_ref, upn_ref)
    fill(hbuf, uh_ref, uhp_ref, uhn_ref)

    t = i * ts + lax.broadcasted_iota(jnp.int32, (ts, 1), 0)
    for g, w in enumerate(POOL_WINDOWS):
        cs = slice(g * POOL_GC, (g + 1) * POOL_GC)
        acc = pbuf[HALO - w // 2:HALO - w // 2 + ts, cs]
        for j in range(1 - w // 2, w // 2):
            acc = acc + pbuf[HALO + j:HALO + j + ts, cs]
        cnt = (jnp.minimum(t + w // 2, seq) - jnp.maximum(t - w // 2, 0)).astype(F32)
        pooled = acc / cnt - pbuf[HALO:HALO + ts, cs]
        mixed = _dot(pooled.astype(BF16), wp_ref[g])
        pm_ref[0, :, cs] = mixed * ps_ref[:, cs]

    ws = ws_ref[...]
    conv = (hbuf[HALO - 1:HALO - 1 + ts] * ws[0:1] + hbuf[HALO:HALO + ts] * ws[1:2]
            + hbuf[HALO + 1:HALO + 1 + ts] * ws[2:3] + bs_ref[...])
    c = conv.shape[1] // 3
    v_ref[0] = conv[:, :c]
    x1_ref[0] = conv[:, c:2 * c]
    x2_ref[0] = conv[:, 2 * c:]


def _local_mixers(up, uh, w_pool, pool_scale, w_short, b_short, ts=512):
    b, s, cp = up.shape
    ch = uh.shape[2]
    c = ch // 3
    ts = min(ts, s)
    nh = ts // HALO
    last_halo = s // HALO - 1

    def cur(w):
        return pl.BlockSpec((1, ts, w), lambda b_, i: (b_, i, 0))

    def prev(w):
        return pl.BlockSpec((1, HALO, w), lambda b_, i: (b_, jnp.maximum(i * nh - 1, 0), 0))

    def nxt(w):
        return pl.BlockSpec((1, HALO, w), lambda b_, i: (b_, jnp.minimum((i + 1) * nh, last_halo), 0))

    return pl.pallas_call(
        functools.partial(_local_kernel, ts=ts, seq=s),
        grid=(b, s // ts),
        in_specs=[cur(cp), prev(cp), nxt(cp), cur(ch), prev(ch), nxt(ch),
                  _const_spec(w_pool.shape), _const_spec((1, cp)),
                  _const_spec(w_short.shape), _const_spec((1, ch))],
        out_specs=[cur(cp), cur(c), cur(c), cur(c)],
        out_shape=[jax.ShapeDtypeStruct((b, s, cp), F32)] + [jax.ShapeDtypeStruct((b, s, c), F32)] * 3,
        scratch_shapes=[pltpu.VMEM((ts + 2 * HALO, cp), F32), pltpu.VMEM((ts + 2 * HALO, ch), F32)],
        compiler_params=_params(2), name="pool_shortconv",
    )(up, up, up, uh, uh, uh, w_pool.astype(BF16), pool_scale.reshape(1, cp), w_short,
      b_short.reshape(1, ch))


def _filter_kernel(bands_ref, w1t_ref, w1c_ref, w1s_ref, b1_ref, f1_ref, w2_ref, b2_ref, f2_ref,
                   w3_ref, dl_ref, h_ref, asum_ref, *, tl, l):
    i = pl.program_id(0)
    n = i * tl + lax.broadcasted_iota(jnp.int32, (tl, 1), 0)
    t = jnp.where(n < l, n, 2 * l - n).astype(F32)
    t01 = t / float(l - 1)
    ang = (2.0 * math.pi / l) * t * bands_ref[...]
    pre = (t01 * w1t_ref[...]
           + jnp.dot(jnp.cos(ang), w1c_ref[...], precision=HIGHEST, preferred_element_type=F32)
           + jnp.dot(jnp.sin(ang), w1s_ref[...], precision=HIGHEST, preferred_element_type=F32))
    h1 = jnp.sin(f1_ref[...] * (pre + b1_ref[...]))
    h2 = jnp.sin(f2_ref[...] * (jnp.dot(h1, w2_ref[...], precision=HIGHEST,
                                        preferred_element_type=F32) + b2_ref[...]))
    h3 = jnp.dot(h2, w3_ref[0], precision=HIGHEST, preferred_element_type=F32)
    h = jnp.where(n == l, 0.0, h3 * jnp.exp(-t01 * dl_ref[...]))
    h_ref[...] = h
    part = jnp.sum(jnp.abs(h), axis=0, keepdims=True)

    @pl.when(i == 0)
    def _():
        asum_ref[...] = jnp.zeros_like(asum_ref)

    asum_ref[...] += part


def _hyena_time_kernel(l, w1, b1, f1, w2, b2, f2, w3):
    ffn = w1.shape[1]
    nc = w3.shape[1] // 2
    c = nc // 2
    tl = min(l, 512)
    w3d = jnp.swapaxes(w3.reshape(ffn, 2, nc), 0, 1)
    bands = jnp.zeros((1, 128), F32).at[0, :HYENA_BANDS].set(
        jnp.linspace(1e-4, HYENA_BANDS - 1, HYENA_BANDS, dtype=F32))
    w1c = jnp.zeros((128, ffn), F32).at[:HYENA_BANDS].set(w1[1:1 + HYENA_BANDS])
    w1s = jnp.zeros((128, ffn), F32).at[:HYENA_BANDS].set(w1[1 + HYENA_BANDS:])
    max_decay = math.log(HYENA_TARGET) / HYENA_FAST_DECAY
    min_decay = math.log(HYENA_TARGET) / HYENA_SLOW_DECAY
    deltas = jnp.abs(jnp.linspace(min_decay, max_decay, c, dtype=F32))
    dl = jnp.tile(deltas, 2).reshape(1, nc)
    row = lambda a: a.reshape(1, -1)
    nfwd = l // tl
    kt, asum = pl.pallas_call(
        functools.partial(_filter_kernel, tl=tl, l=l),
        grid=(2 * nfwd,),
        in_specs=[_const_spec((1, 128)), _const_spec((1, ffn)), _const_spec((128, ffn)),
                  _const_spec((128, ffn)), _const_spec((1, ffn)), _const_spec((1, ffn)),
                  _const_spec((ffn, ffn)), _const_spec((1, ffn)), _const_spec((1, ffn)),
                  pl.BlockSpec((1, ffn, nc), lambda i: (i // nfwd, 0, 0)), _const_spec((1, nc))],
        out_specs=[pl.BlockSpec((tl, nc), lambda i: (i, 0)), pl.BlockSpec((1, nc), lambda i: (0, 0))],
        out_shape=[jax.ShapeDtypeStruct((2 * l, nc), F32), jax.ShapeDtypeStruct((1, nc), F32)],
        compiler_params=_params(1), name="hyena_filter_mlp",
    )(bands, w1[0:1], w1c, w1s, row(b1), row(f1), w2, row(b2), row(f2), w3d, dl)
    return kt, 1.0 / asum


def _angles(prod, n):
    th = (2.0 * math.pi / n) * (prod % n).astype(F32)
    return jnp.cos(th), jnp.sin(th)


def _dft_tables():
    r = DFT_R
    n = r * r
    idx = jnp.arange(r, dtype=jnp.int32)
    c, s = _angles(idx[:, None] * idx[None, :], r)
    ch, sh = c[:, :r // 2], s[:, :r // 2]
    w1_data = jnp.concatenate([jnp.concatenate([ch, sh], 1), jnp.concatenate([-sh, ch], 1)], 0)
    w1_real = jnp.concatenate([c, -s], 0)
    ct, st = c[:r // 2], s[:r // 2]
    w2 = jnp.concatenate([jnp.concatenate([ct, -st], 1), jnp.concatenate([st, ct], 1)], 0)
    k = idx[:, None, None] + r * idx[None, :, None]
    cg, sg = _angles(k * idx[None, None, :], n)
    g = jnp.concatenate([jnp.concatenate([cg, sg], 2), jnp.concatenate([-sg, cg], 2)], 1)
    cgt, sgt = jnp.swapaxes(cg, 1, 2), jnp.swapaxes(sg, 1, 2)
    gi = jnp.concatenate([jnp.concatenate([cgt, -sgt], 2), jnp.concatenate([sgt, cgt], 2)], 1)
    dp = DFT_DATA_PASSES
    return dict(w1_data=_split3(w1_data, dp), w1_real=_split3(w1_real), w2=_split3(w2, dp),
                g=_split3(g, dp), gi=_split3(gi, dp), g_filter=_split3(g))


def _dft1_kernel(x_ref, w_ref, o_ref, *, parts):
    x = jnp.concatenate([x_ref[0, p] for p in range(parts)], axis=0)
    o_ref[0] = _mm3(w_ref[...], x)


def _dft_major(x, w3, tc=4096):
    p, parts, r, cols = x.shape
    return pl.pallas_call(
        functools.partial(_dft1_kernel, parts=parts),
        grid=(p, cols // tc),
        in_specs=[pl.BlockSpec((1, parts, r, tc), lambda p_, j: (p_, 0, 0, j)), _const_spec(w3.shape)],
        out_specs=pl.BlockSpec((1, 2 * DFT_R, tc), lambda p_, j: (p_, 0, j)),
        out_shape=jax.ShapeDtypeStruct((p, 2 * DFT_R, cols), F32),
        compiler_params=_params(2), name="dft_major",
    )(x, w3)


def _spectrum_kernel(a_ref, g_ref, sc_ref, o_ref):
    a = jnp.concatenate([a_ref[0, 0, 0], a_ref[0, 1, 0]], axis=0)
    x = _mm3(g_ref[0], a) * sc_ref[...]
    r = x.shape[0] // 2
    o_ref[0, 0] = x[:r]
    o_ref[1, 0] = x[r:]


def _filter_spectrum(kt, scale, tabs):
    r = DFT_R
    c = kt.shape[1]
    a = _dft_major(kt.reshape(1, 1, r, r * c), tabs["w1_real"])
    a = a.reshape(1, 2, r, r, c)
    tcs = min(512, c)
    return pl.pallas_call(
        _spectrum_kernel,
        grid=(r, c // tcs),
        in_specs=[pl.BlockSpec((1, 2, 1, r, tcs), lambda k, j: (0, 0, k, 0, j)),
                  pl.BlockSpec((1,) + tabs["g_filter"].shape[1:], lambda k, j: (k, 0, 0)),
                  pl.BlockSpec((1, tcs), lambda k, j: (0, j))],
        out_specs=pl.BlockSpec((2, 1, r, tcs), lambda k, j: (0, k, 0, j)),
        out_shape=jax.ShapeDtypeStruct((2, r, r, c), F32),
        compiler_params=_params(2), name="filter_spectrum",
    )(a, tabs["g_filter"], scale)


def _dft2_kernel(a_ref, g_ref, gi_ref, h_ref, o_ref):
    a = jnp.concatenate([a_ref[0, 0, 0], a_ref[0, 1, 0]], axis=0)
    x = _mm3(g_ref[0], a)
    r = x.shape[0] // 2
    xr, xi = x[:r], x[r:]
    hr, hi = h_ref[0, 0], h_ref[1, 0]
    y = jnp.concatenate([xr * hr - xi * hi, xr * hi + xi * hr], axis=0)
    bb = _mm3(gi_ref[0], y)
    o_ref[0, 0, 0] = bb[:r]
    o_ref[0, 1, 0] = bb[r:]


def _dft3_kernel(b_ref, w_ref, u_ref, x_ref, bias_ref, o_ref):
    y = _mm3(w_ref[...], b_ref[0])
    half = y.shape[0] // 2
    bias = bias_ref[...]
    o_ref[0, 0] = x_ref[0, 0] * (y[:half] + u_ref[0, 0] * bias)
    o_ref[0, 1] = x_ref[0, 1] * (y[half:] + u_ref[0, 1] * bias)


def _long_conv_gate(u, x, hspec, order, bias, tabs, tc=4096):
    b, l, c = u.shape
    r = DFT_R
    p = b // 2
    cols = r * c
    a = _dft_major(u.reshape(p, 2, r // 2, cols), tabs["w1_data"], tc)
    a = a.reshape(p, 2, r, r, c)
    bb = pl.pallas_call(
        _dft2_kernel,
        grid=(r, p),
        in_specs=[pl.BlockSpec((1, 2, 1, r, c), lambda k, p_: (p_, 0, k, 0, 0)),
                  pl.BlockSpec((1,) + tabs["g"].shape[1:], lambda k, p_: (k, 0, 0)),
                  pl.BlockSpec((1,) + tabs["gi"].shape[1:], lambda k, p_: (k, 0, 0)),
                  pl.BlockSpec((2, 1, r, c), lambda k, p_: (0, k, 0, order))],
        out_specs=pl.BlockSpec((1, 2, 1, r, c), lambda k, p_: (p_, 0, k, 0, 0)),
        out_shape=jax.ShapeDtypeStruct((p, 2, r, r, c), F32),
        compiler_params=_params(2), name="dft_minor_filter",
    )(a, tabs["g"], tabs["gi"], hspec)
    bb = bb.reshape(p, 2 * r, cols)
    bias_t = jnp.tile(bias.reshape(1, c), (1, tc // c))
    view = (p, 2, r // 2, cols)
    blk = pl.BlockSpec((1, 2, r // 2, tc), lambda p_, j: (p_, 0, 0, j))
    out = pl.pallas_call(
        _dft3_kernel,
        grid=(p, cols // tc),
        in_specs=[pl.BlockSpec((1, 2 * r, tc), lambda p_, j: (p_, 0, j)),
                  _const_spec(tabs["w2"].shape), blk, blk, _const_spec((1, tc))],
        out_specs=blk,
        out_shape=jax.ShapeDtypeStruct(view, F32),
        compiler_params=_params(2), name="idft_major_gate",
    )(bb, tabs["w2"], u.reshape(view), x.reshape(view), bias_t)
    return out.reshape(b, l, c)


def _dense_tables(l):
    n = 2 * l
    k = jnp.arange(n, dtype=jnp.int32)
    t = jnp.arange(l, dtype=jnp.int32)
    c, s = _angles(k[:, None] * t[None, :], n)
    wf = jnp.concatenate([jnp.concatenate([c, s], 1), jnp.concatenate([-s, c], 1)], 0)
    ct, st = c.T, s.T
    wi = jnp.concatenate([jnp.concatenate([ct, -st], 1), jnp.concatenate([st, ct], 1)], 0)
    ca, sa = _angles(k[:, None] * k[None, :], n)
    wk = jnp.concatenate([ca, -sa], 0)
    dp = DFT_DATA_PASSES
    return dict(wf=_split3(wf, dp), wi=_split3(wi, dp), wk=_split3(wk))


def _dense_spectrum_kernel(k_ref, w_ref, sc_ref, o_ref):
    o_ref[...] = _mm3(w_ref[...], k_ref[...]) * sc_ref[...]


def _dense_spectrum(kt, scale, wk):
    n, c = kt.shape
    return pl.pallas_call(
        _dense_spectrum_kernel,
        grid=(1,),
        in_specs=[_const_spec((n, c)), _const_spec(wk.shape), _const_spec((1, c))],
        out_specs=pl.BlockSpec((2 * n, c), lambda i: (0, 0)),
        out_shape=jax.ShapeDtypeStruct((2 * n, c), F32),
        compiler_params=_params(1), name="dense_filter_spectrum",
    )(kt, wk, scale)


def _dense_conv_kernel(u_ref, x_ref, wf_ref, wi_ref, h_ref, bias_ref, o_ref):
    u0, u1 = u_ref[0], u_ref[1]
    z = _mm3(wf_ref[...], jnp.concatenate([u0, u1], axis=0))
    n = z.shape[0] // 2
    zr, zi = z[:n], z[n:]
    hr, hi = h_ref[:n], h_ref[n:]
    y = _mm3(wi_ref[...], jnp.concatenate([zr * hr - zi * hi, zr * hi + zi * hr], axis=0))
    l = y.shape[0] // 2
    bias = bias_ref[...]
    o_ref[0] = x_ref[0] * (y[:l] + u0 * bias)
    o_ref[1] = x_ref[1] * (y[l:] + u1 * bias)


def _dense_conv_gate(u, x, hspec, order, bias, tabs):
    b, l, c = u.shape
    blk = pl.BlockSpec((2, l, c), lambda p_: (p_, 0, 0))
    return pl.pallas_call(
        _dense_conv_kernel,
        grid=(b // 2,),
        in_specs=[blk, blk, _const_spec(tabs["wf"].shape), _const_spec(tabs["wi"].shape),
                  pl.BlockSpec((4 * l, c), lambda p_: (0, order)), _const_spec((1, c))],
        out_specs=blk,
        out_shape=jax.ShapeDtypeStruct((b, l, c), F32),
        compiler_params=_params(1), name="dense_conv_gate",
    )(u, x, tabs["wf"], tabs["wi"], hspec, bias.reshape(1, c))


def _mix_kernel(h_ref, o_ref, pm_ref, y_ref, g_ref, gate_ref, wa_ref, wp_ref, wh_ref, wo_ref, out_ref):
    d = h_ref.shape[2]
    a = _dot(o_ref[0].astype(BF16), wa_ref[...])
    p = _dot(pm_ref[0].astype(BF16), wp_ref[...])
    y = _dot(y_ref[0].astype(BF16), wh_ref[...])
    merged = g_ref[0, :, 0:d] * a + g_ref[0, :, d:2 * d] * p + g_ref[0, :, 2 * d:3 * d] * y
    out = _dot(merged.astype(BF16), wo_ref[...])
    out_ref[0] = h_ref[0] + gate_ref[0] * out


def _mix(h, o, pm, y, g, mod, w_attn_o, w_pool_o, w_hy_o, w_out, tm=512):
    b, s, d = h.shape
    tm = min(tm, s)

    def tok(w):
        return pl.BlockSpec((1, tm, w), lambda b_, i: (b_, i, 0))

    ws = [w.astype(BF16) for w in (w_attn_o, w_pool_o, w_hy_o, w_out)]
    return pl.pallas_call(
        _mix_kernel,
        grid=(b, s // tm),
        in_specs=[tok(d), tok(o.shape[2]), tok(pm.shape[2]), tok(y.shape[2]), tok(g.shape[2]),
                  _mod_spec(mod, d, 2)] + [_const_spec(w.shape) for w in ws],
        out_specs=tok(d),
        out_shape=jax.ShapeDtypeStruct((b, s, d), F32),
        compiler_params=_params(2), name="merge_outproj",
    )(h, o, pm, y, g, mod, *ws)


def _ffn_kernel(h_ref, sh_ref, sc_ref, gate_ref, g_ref, wi_ref, wo_ref, fg_ref, out_ref, *,
                hidden, chunk, final):
    x = h_ref[0]
    a = _norm_mod(x, g_ref[...], sh_ref[0], sc_ref[0]).astype(BF16)
    acc = jnp.zeros(x.shape, F32)
    for c0 in range(0, hidden, chunk):
        gt = _dot(a, wi_ref[:, c0:c0 + chunk])
        up = _dot(a, wi_ref[:, hidden + c0:hidden + c0 + chunk])
        act = (gt * jax.nn.sigmoid(gt) * up).astype(BF16)
        acc = acc + _dot(act, wo_ref[c0:c0 + chunk, :])
    y = x + gate_ref[0] * acc
    if final:
        ms = jnp.mean(y * y, axis=-1, keepdims=True)
        y = y * lax.rsqrt(ms + EPS) * fg_ref[...]
    out_ref[0] = y


def _ffn(h, mod, norm_g, w_in, w_out, final_g, final, tm=512):
    b, s, d = h.shape
    hidden = w_out.shape[0]
    tm = min(tm, s)
    tok = pl.BlockSpec((1, tm, d), lambda b_, i: (b_, i, 0))
    return pl.pallas_call(
        functools.partial(_ffn_kernel, hidden=hidden, chunk=hidden // 2, final=final),
        grid=(b, s // tm),
        in_specs=[tok, _mod_spec(mod, d, 3), _mod_spec(mod, d, 4), _mod_spec(mod, d, 5),
                  _const_spec((1, d)), _const_spec(w_in.shape), _const_spec(w_out.shape),
                  _const_spec((1, d))],
        out_specs=tok,
        out_shape=jax.ShapeDtypeStruct((b, s, d), F32),
        compiler_params=_params(2), name="swiglu_ffn",
    )(h, mod, mod, mod, norm_g.reshape(1, d), w_in.astype(BF16), w_out.astype(BF16),
      final_g.reshape(1, d))


def _rope_tables(seq, width):
    t = jnp.arange(seq, dtype=jnp.int32)
    row = (t // GRID_W).astype(F32)
    col = (t % GRID_W).astype(F32)
    inv = 1.0 / (ROPE_BASE ** (jnp.arange(ROPE_FREQS, dtype=F32) * 2.0 / (2 * ROPE_FREQS)))
    ar, ac = row[:, None] * inv, col[:, None] * inv
    cos = jnp.concatenate([jnp.cos(ar)] * 2 + [jnp.cos(ac)] * 2, axis=1)
    sin = jnp.concatenate([-jnp.sin(ar), jnp.sin(ar), -jnp.sin(ac), jnp.sin(ac)], axis=1)
    reps = width // cos.shape[1]
    return jnp.tile(cos, (1, reps)), jnp.tile(sin, (1, reps))


def _swap_halves(w):
    j = jnp.arange(w.shape[1])
    return w[:, jnp.where(j % (2 * ROPE_FREQS) < ROPE_FREQS, j + ROPE_FREQS, j - ROPE_FREQS)]


def kernel(x, c, ctx, c_ctx, w_mod, b_mod, norm1_g, norm2_g, w_in, lam_qk, subln_g, w_attn_o, w_pool, pool_scale, w_pool_o, w_short, b_short, hf_w1, hf_b1, hf_freq1, hf_w2, hf_b2, hf_freq2, hf_w3, hy_bias, w_hy_o, w_out, w_ffn_in, w_ffn_out, final_g):
    batch, seq, d = x.shape
    depth = w_mod.shape[0]
    ctx_len = ctx.shape[1]
    aw = w_attn_o.shape[1]
    pw = w_pool_o.shape[1]
    hw = w_hy_o.shape[1]
    q0, k0, v0, p0 = 0, aw, 2 * aw, 3 * aw
    h0 = p0 + pw
    g0 = h0 + 3 * hw
    assert seq * 2 == DFT_R * DFT_R and batch % 2 == 0

    c8 = jnp.zeros((8, d), F32).at[:batch].set(c).at[batch].set(c_ctx)
    mod_all = _modulation(c8, w_mod, b_mod)
    rope_tabs = _rope_tables(seq, aw)
    tabs = _dft_tables()
    ctx_tabs = _dense_tables(ctx_len)

    h_lat, h_ctx = x, ctx
    for l in range(depth):
        last = l == depth - 1
        lam_init = LAMBDA_INIT_BASE - LAMBDA_INIT_AMP * math.exp(-LAMBDA_INIT_RATE * l)
        mod = mod_all[l, :batch].reshape(batch, 1, N_MOD * d)
        mod_c = mod_all[l, batch:batch + 1].reshape(1, 1, N_MOD * d)
        wl = w_in[l]
        wq, wk = wl[:, q0:k0], wl[:, k0:v0]
        fparams = (hf_w1[l], hf_b1[l], hf_freq1[l], hf_w2[l], hf_b2[l], hf_freq2[l], hf_w3[l])

        w_cat = jnp.concatenate([wq, _swap_halves(wq), wk, _swap_halves(wk), wl[:, v0:]], axis=1)
        segs = [("rope", aw, Q_SCALE), ("rope", aw, 1.0), ("plain", aw, 1.0), ("plain", pw, 1.0),
                ("plain", 3 * hw, 1.0), ("sigmoid", 3 * d, 1.0)]
        q, k, v, up, uh, g = _inproj(h_lat, mod, norm1_g[l], w_cat.astype(BF16), segs,
                                     [BF16, BF16, BF16, F32, F32, F32], rope_tabs)

        if last:
            segs_c = [("plain", aw, 1.0), ("plain", aw, 1.0)]
            k_c, v_c = _inproj(h_ctx, mod_c, norm1_g[l], wl[:, k0:p0].astype(BF16), segs_c, [BF16, BF16])
        else:
            segs_c = [("plain", aw, Q_SCALE), ("plain", aw, 1.0), ("plain", aw, 1.0), ("plain", pw, 1.0),
                      ("plain", 3 * hw, 1.0), ("sigmoid", 3 * d, 1.0)]
            q_c, k_c, v_c, up_c, uh_c, g_c = _inproj(h_ctx, mod_c, norm1_g[l], wl.astype(BF16), segs_c,
                                                    [BF16, BF16, BF16, F32, F32, F32])
            o_c = _attention(q_c, k_c, v_c, lam_qk[l], subln_g[l], lam_init)
            pm_c, vv_c, x1_c, x2_c = _local_mixers(up_c, uh_c, w_pool[l], pool_scale[l], w_short[l], b_short[l])
            kt_c, inv_norm_c = _hyena_time_kernel(ctx_len, *fparams)
            hspec_c = _dense_spectrum(kt_c, inv_norm_c / (2 * ctx_len), ctx_tabs["wk"])
            z_c = _dense_conv_gate(vv_c, x1_c, hspec_c, 0, hy_bias[l, 0], ctx_tabs)
            y_c = _dense_conv_gate(z_c, x2_c, hspec_c, 1, hy_bias[l, 1], ctx_tabs)
            h_ctx_mid = _mix(h_ctx, o_c, pm_c, y_c, g_c, mod_c, w_attn_o[l], w_pool_o[l], w_hy_o[l], w_out[l])
            h_ctx_new = _ffn(h_ctx_mid, mod_c, norm2_g[l], w_ffn_in[l], w_ffn_out[l], final_g, False)

        o = _attention_pipelined(q, jnp.concatenate([k_c, k], axis=1), jnp.concatenate([v_c, v], axis=1),
                                 lam_qk[l], subln_g[l], lam_init)
        pm, vv, x1, x2 = _local_mixers(up, uh, w_pool[l], pool_scale[l], w_short[l], b_short[l])
        kt, inv_norm = _hyena_time_kernel(seq, *fparams)
        hspec = _filter_spectrum(kt, inv_norm / (2 * seq), tabs)
        z = _long_conv_gate(vv, x1, hspec, 0, hy_bias[l, 0], tabs)
        y = _long_conv_gate(z, x2, hspec, 1, hy_bias[l, 1], tabs)
        h_mid = _mix(h_lat, o, pm, y, g, mod, w_attn_o[l], w_pool_o[l], w_hy_o[l], w_out[l])
        h_lat = _ffn(h_mid, mod, norm2_g[l], w_ffn_in[l], w_ffn_out[l], final_g, last)
        if not last:
            h_ctx = h_ctx_new
    return h_lat
```

```python
import functools
import math

import jax
import jax.numpy as jnp
from jax import lax
from jax.experimental import pallas as pl
from jax.experimental.pallas import tpu as pltpu

F32 = jnp.float32
BF16 = jnp.bfloat16
HIGHEST = lax.Precision.HIGHEST

GRID_W = 64
N_MOD = 6
ATTN_HD = 64
ATTN_VD = 128
Q_SCALE = ATTN_HD ** -0.5 * math.log2(math.e)
ROPE_BASE = 10000.0
ROPE_FREQS = 16
LAMBDA_INIT_BASE = 0.8
LAMBDA_INIT_AMP = 0.6
LAMBDA_INIT_RATE = 0.3
POOL_WINDOWS = (2, 4, 8, 16)
POOL_GC = 128
HALO = 8
HYENA_BANDS = 16
HYENA_FAST_DECAY = 0.3
HYENA_SLOW_DECAY = 1.5
HYENA_TARGET = 1e-2
EPS = 1e-6
DFT_DATA_PASSES = 1
DFT_R = 128

VMEM_LIMIT = 56 * 1024 * 1024


def _params(n_grid, flags=None):
    return pltpu.CompilerParams(dimension_semantics=("arbitrary",) * n_grid,
                                vmem_limit_bytes=VMEM_LIMIT, flags=flags)


def _const_spec(shape):
    zeros = (0,) * len(shape)
    return pl.BlockSpec(shape, lambda *_: zeros, pipeline_mode=pl.Buffered(1))


def _dot(a, b):
    return jnp.dot(a, b, preferred_element_type=F32)


def _split3(w, passes=3):
    hi = w.astype(BF16)
    if passes == 1:
        return hi
    lo = (w - hi.astype(F32)).astype(BF16)
    return jnp.concatenate([hi, hi, lo], axis=-1)


def _mm3(w3, x):
    xh = x.astype(BF16)
    if w3.shape[-1] == x.shape[0]:
        return _dot(w3, xh)
    xl = (x - xh.astype(F32)).astype(BF16)
    return _dot(w3, jnp.concatenate([xh, xl, xh], axis=0))


def _mod_kernel(c_ref, w_ref, b_ref, o_ref):
    c = c_ref[...]
    act = c * jax.nn.sigmoid(c)
    o_ref[0] = jnp.dot(act, w_ref[0], precision=HIGHEST, preferred_element_type=F32) + b_ref[0]


def _modulation(c8, w_mod, b_mod):
    depth, d, n = w_mod.shape
    tn = 1536
    return pl.pallas_call(
        _mod_kernel,
        grid=(depth, n // tn),
        in_specs=[pl.BlockSpec((8, d), lambda l, j: (0, 0)),
                  pl.BlockSpec((1, d, tn), lambda l, j: (l, 0, j)),
                  pl.BlockSpec((1, 1, tn), lambda l, j: (l, 0, j))],
        out_specs=pl.BlockSpec((1, 8, tn), lambda l, j: (l, 0, j)),
        out_shape=jax.ShapeDtypeStruct((depth, 8, n), F32),
        compiler_params=_params(2),
        name="modulation",
    )(c8, w_mod, b_mod.reshape(depth, 1, n))


def _mod_spec(mod, d, col):
    if mod.shape[0] == 1:
        return pl.BlockSpec((1, 1, d), lambda b, i: (0, 0, col))
    return pl.BlockSpec((1, 1, d), lambda b, i: (b, 0, col))


def _norm_mod(x, g, shift, scale):
    ms = jnp.mean(x * x, axis=-1, keepdims=True)
    a = x * lax.rsqrt(ms + EPS) * g
    return a * (1.0 + scale) + shift


def _rope(p, cos, sin):
    lanes = cos.shape[1]
    lane = lax.broadcasted_iota(jnp.int32, cos.shape, 1)
    is_a = lane % (2 * ROPE_FREQS) < ROPE_FREQS
    out = []
    for g in range(p.shape[1] // lanes):
        x = p[:, g * lanes:(g + 1) * lanes]
        partner = jnp.where(is_a, pltpu.roll(x, lanes - ROPE_FREQS, 1), pltpu.roll(x, ROPE_FREQS, 1))
        out.append(x * cos + partner * sin)
    return jnp.concatenate(out, axis=1)


def _inproj_kernel(*refs, segs, rope):
    h_ref, sh_ref, sc_ref, g_ref, w_ref = refs[:5]
    n_in = 7 if rope else 5
    out_refs = refs[n_in:]
    a = _norm_mod(h_ref[0], g_ref[...], sh_ref[0], sc_ref[0]).astype(BF16)
    col = 0
    for (kind, width, scale), o_ref in zip(segs, out_refs):
        p = _dot(a, w_ref[:, col:col + width])
        col += width
        if kind == "rope":
            p = _rope(p, refs[5][...], refs[6][...])
        elif kind == "sigmoid":
            p = jax.nn.sigmoid(p)
        if scale != 1.0:
            p = p * scale
        o_ref[0] = p.astype(o_ref.dtype)


def _inproj(h, mod, norm_g, w_cat, segs, out_dtypes, rope_tabs=None, tm=256):
    b, s, d = h.shape
    tm = min(tm, s)
    rope = rope_tabs is not None
    in_specs = [pl.BlockSpec((1, tm, d), lambda b_, i: (b_, i, 0)),
                _mod_spec(mod, d, 0), _mod_spec(mod, d, 1),
                _const_spec((1, d)), _const_spec(w_cat.shape)]
    args = [h, mod, mod, norm_g.reshape(1, d), w_cat]
    if rope:
        wr = rope_tabs[0].shape[1]
        in_specs += [pl.BlockSpec((tm, wr), lambda b_, i: (i, 0))] * 2
        args += list(rope_tabs)
    out_specs = [pl.BlockSpec((1, tm, w), lambda b_, i: (b_, i, 0)) for (_, w, _) in segs]
    out_shape = [jax.ShapeDtypeStruct((b, s, w), dt) for (_, w, _), dt in zip(segs, out_dtypes)]
    return pl.pallas_call(
        functools.partial(_inproj_kernel, segs=tuple(segs), rope=rope),
        grid=(b, s // tm), in_specs=in_specs, out_specs=out_specs, out_shape=out_shape,
        compiler_params=_params(2), name="inproj",
    )(*args)


def _lambda(lq_ref, lam_init):
    lq = lq_ref[...]
    return (jnp.exp(jnp.sum(lq[0:1] * lq[1:2], axis=-1, keepdims=True))
            - jnp.exp(jnp.sum(lq[2:3] * lq[3:4], axis=-1, keepdims=True)) + lam_init)


def _split_halves(q):
    lane = lax.broadcasted_iota(jnp.int32, q.shape, 1)
    zero = jnp.zeros_like(q)
    return jnp.concatenate([jnp.where(lane < ATTN_HD, q, zero),
                            jnp.where(lane >= ATTN_HD, q, zero)], axis=0)


def _qk(qq, k):
    return lax.dot_general(qq, k, (((1,), (1,)), ((), ())), preferred_element_type=F32)


def _sub_ln(o, g, lam_init):
    ms = jnp.mean(o * o, axis=-1, keepdims=True)
    return o * lax.rsqrt(ms + EPS) * g * (1.0 - lam_init)


def _attn_kernel(lq_ref, q_ref, k_ref, v_ref, g_ref, o_ref, *, lam_init, tq):
    lam = _lambda(lq_ref, lam_init)
    s = _qk(_split_halves(q_ref[0]), k_ref[0])
    m = jnp.max(s, axis=-1, keepdims=True)
    p = jnp.exp2(s - m)
    l = jnp.sum(p, axis=-1, keepdims=True)
    a = (p[:tq] - p[tq:] * (lam * l[:tq] / l[tq:])).astype(BF16)
    o = _dot(a, v_ref[0]) / l[:tq]
    o_ref[0] = _sub_ln(o, g_ref[...], lam_init).astype(o_ref.dtype)


def _tree(op, xs):
    xs = list(xs)
    while len(xs) > 1:
        xs = [op(xs[i], xs[i + 1]) if i + 1 < len(xs) else xs[i] for i in range(0, len(xs), 2)]
    return xs[0]


def _attn_pipe_kernel(lq_ref, q_ref, k_ref, v_ref, g_ref, o_ref, s0, s1, s2, m_scr, l_scr, *,
                      lam_init, tq, lk, kc, nq):
    j = pl.program_id(2)
    bufs = (s0, s1, s2)
    lanes = ATTN_VD
    ngrp = kc // lanes

    @pl.when(j == 0)
    def _():
        s1[...] = jnp.zeros_like(s1)
        s2[...] = jnp.zeros_like(s2)
        m_scr[2] = jnp.zeros(m_scr.shape[1:], F32)
        l_scr[1] = jnp.ones(l_scr.shape[1:], F32)

    lam = _lambda(lq_ref, lam_init)
    for r in range(3):
        sig = 3 * j + r
        buf_a, buf_b, buf_c = bufs[r], bufs[(r + 2) % 3], bufs[(r + 1) % 3]
        t_a = jnp.minimum(sig, nq - 1)
        t_c = jnp.clip(sig - 2, 0, nq - 1)
        qq = _split_halves(q_ref[0, pl.ds(pl.multiple_of(t_a * tq, tq), tq), :])
        m_b = m_scr[(r + 2) % 3]
        l_c = l_scr[(r + 1) % 3]
        ratio = lam * l_c[:tq] / l_c[tq:]
        m_parts, l_parts, o_parts = [], [], []
        for c0 in range(0, lk, kc):
            cols = slice(c0, c0 + kc)
            s = _qk(qq, k_ref[0, cols, :])
            buf_a[:, cols] = s
            m_parts.append(_tree(jnp.maximum, [s[:, g * lanes:(g + 1) * lanes] for g in range(ngrp)]))
            ps = []
            for g in range(ngrp):
                gc = slice(c0 + g * lanes, c0 + (g + 1) * lanes)
                p = jnp.exp2(buf_b[:, gc] - m_b)
                buf_b[:, gc] = p
                ps.append(p)
            l_parts.append(_tree(jnp.add, ps))
            pc = buf_c[:, cols]
            a = jnp.concatenate([pc[:tq, g * lanes:(g + 1) * lanes] - pc[tq:, g * lanes:(g + 1) * lanes] * ratio
                                 for g in range(ngrp)], axis=1).astype(BF16)
            o_parts.append(_dot(a, v_ref[0, cols, :]))
        m_a = _tree(jnp.maximum, m_parts)
        l_b = _tree(jnp.add, l_parts)
        o_c = _tree(jnp.add, o_parts)
        m_scr[r] = jnp.broadcast_to(jnp.max(m_a, axis=-1, keepdims=True), m_a.shape)
        l_scr[(r + 2) % 3] = jnp.broadcast_to(jnp.sum(l_b, axis=-1, keepdims=True), l_b.shape)
        o = o_c / l_c[:tq]
        o_ref[0, pl.ds(pl.multiple_of(t_c * tq, tq), tq), :] = _sub_ln(o, g_ref[...], lam_init).astype(o_ref.dtype)


def _attention_pipelined(q, k, v, lam_qk, subln_g, lam_init, tq=128, kc=768):
    b, s, w = q.shape
    lk = k.shape[1]
    heads = w // ATTN_VD
    nq = s // tq
    nsteps = -(-(nq + 2) // 3)
    return pl.pallas_call(
        functools.partial(_attn_pipe_kernel, lam_init=lam_init, tq=tq, lk=lk, kc=kc, nq=nq),
        grid=(b, heads, nsteps),
        in_specs=[pl.BlockSpec(lam_qk.shape, lambda b_, h, i: (0, 0)),
                  pl.BlockSpec((1, s, ATTN_VD), lambda b_, h, i: (b_, 0, h)),
                  pl.BlockSpec((1, lk, ATTN_VD), lambda b_, h, i: (b_, 0, h)),
                  pl.BlockSpec((1, lk, ATTN_VD), lambda b_, h, i: (b_, 0, h)),
                  pl.BlockSpec((1, ATTN_VD), lambda b_, h, i: (0, 0))],
        out_specs=pl.BlockSpec((1, s, ATTN_VD), lambda b_, h, i: (b_, 0, h)),
        out_shape=jax.ShapeDtypeStruct((b, s, w), F32),
        scratch_shapes=[pltpu.VMEM((2 * tq, lk), F32)] * 3
        + [pltpu.VMEM((3, 2 * tq, ATTN_VD), F32)] * 2,
        compiler_params=_params(3), name="diff_attention_pipelined",
    )(lam_qk, q, k, v, subln_g.reshape(1, ATTN_VD))


def _attention(q, k, v, lam_qk, subln_g, lam_init, tq=128):
    b, s, w = q.shape
    lk = k.shape[1]
    heads = w // ATTN_VD
    tq = min(tq, s)
    return pl.pallas_call(
        functools.partial(_attn_kernel, lam_init=lam_init, tq=tq),
        grid=(b, heads, s // tq),
        in_specs=[pl.BlockSpec(lam_qk.shape, lambda b_, h, i: (0, 0)),
                  pl.BlockSpec((1, tq, ATTN_VD), lambda b_, h, i: (b_, i, h)),
                  pl.BlockSpec((1, lk, ATTN_VD), lambda b_, h, i: (b_, 0, h)),
                  pl.BlockSpec((1, lk, ATTN_VD), lambda b_, h, i: (b_, 0, h)),
                  pl.BlockSpec((1, ATTN_VD), lambda b_, h, i: (0, 0))],
        out_specs=pl.BlockSpec((1, tq, ATTN_VD), lambda b_, h, i: (b_, i, h)),
        out_shape=jax.ShapeDtypeStruct((b, s, w), F32),
        compiler_params=_params(3), name="diff_attention",
    )(lam_qk, q, k, v, subln_g.reshape(1, ATTN_VD))


def _local_kernel(up_ref, upp_ref, upn_ref, uh_ref, uhp_ref, uhn_ref, wp_ref, ps_ref, ws_ref,
                  bs_ref, pm_ref, v_ref, x1_ref, x2_ref, pbuf, hbuf, *, ts, seq):
    i = pl.program_id(1)
    first = i == 0
    last = i == pl.num_programs(1) - 1

    def fill(buf, cur, prev, nxt):
        buf[0:HALO] = jnp.where(first, 0.0, prev[0])
        buf[HALO:HALO + ts] = cur[0]
        buf[HALO + ts:HALO + ts + HALO] = jnp.where(last, 0.0, nxt[0])

    fill(pbuf, up_ref, upp_ref, upn_ref)
    fill(hbuf, uh_ref, uhp_ref, uhn_ref)

    t = i * ts + lax.broadcasted_iota(jnp.int32, (ts, 1), 0)
    for g, w in enumerate(POOL_WINDOWS):
        cs = slice(g * POOL_GC, (g + 1) * POOL_GC)
        acc = pbuf[HALO - w // 2:HALO - w // 2 + ts, cs]
        for j in range(1 - w // 2, w // 2):
            acc = acc + pbuf[HALO + j:HALO + j + ts, cs]
        cnt = (jnp.minimum(t + w // 2, seq) - jnp.maximum(t - w // 2, 0)).astype(F32)
        pooled = acc / cnt - pbuf[HALO:HALO + ts, cs]
        mixed = _dot(pooled.astype(BF16), wp_ref[g])
        pm_ref[0, :, cs] = mixed * ps_ref[:, cs]

    ws = ws_ref[...]
    conv = (hbuf[HALO - 1:HALO - 1 + ts] * ws[0:1] + hbuf[HALO:HALO + ts] * ws[1:2]
            + hbuf[HALO + 1:HALO + 1 + ts] * ws[2:3] + bs_ref[...])
    c = conv.shape[1] // 3
    v_ref[0] = conv[:, :c]
    x1_ref[0] = conv[:, c:2 * c]
    x2_ref[0] = conv[:, 2 * c:]


def _local_mixers(up, uh, w_pool, pool_scale, w_short, b_short, ts=512):
    b, s, cp = up.shape
    ch = uh.shape[2]
    c = ch // 3
    ts = min(ts, s)
    nh = ts // HALO
    last_halo = s // HALO - 1

    def cur(w):
        return pl.BlockSpec((1, ts, w), lambda b_, i: (b_, i, 0))

    def prev(w):
        return pl.BlockSpec((1, HALO, w), lambda b_, i: (b_, jnp.maximum(i * nh - 1, 0), 0))

    def nxt(w):
        return pl.BlockSpec((1, HALO, w), lambda b_, i: (b_, jnp.minimum((i + 1) * nh, last_halo), 0))

    return pl.pallas_call(
        functools.partial(_local_kernel, ts=ts, seq=s),
        grid=(b, s // ts),
        in_specs=[cur(cp), prev(cp), nxt(cp), cur(ch), prev(ch), nxt(ch),
                  _const_spec(w_pool.shape), _const_spec((1, cp)),
                  _const_spec(w_short.shape), _const_spec((1, ch))],
        out_specs=[cur(cp), cur(c), cur(c), cur(c)],
        out_shape=[jax.ShapeDtypeStruct((b, s, cp), F32)] + [jax.ShapeDtypeStruct((b, s, c), F32)] * 3,
        scratch_shapes=[pltpu.VMEM((ts + 2 * HALO, cp), F32), pltpu.VMEM((ts + 2 * HALO, ch), F32)],
        compiler_params=_params(2), name="pool_shortconv",
    )(up, up, up, uh, uh, uh, w_pool.astype(BF16), pool_scale.reshape(1, cp), w_short,
      b_short.reshape(1, ch))


def _filter_kernel(bands_ref, w1t_ref, w1c_ref, w1s_ref, b1_ref, f1_ref, w2_ref, b2_ref, f2_ref,
                   w3_ref, dl_ref, h_ref, asum_ref, *, tl, l):
    i = pl.program_id(0)
    n = i * tl + lax.broadcasted_iota(jnp.int32, (tl, 1), 0)
    t = jnp.where(n < l, n, 2 * l - n).astype(F32)
    t01 = t / float(l - 1)
    ang = (2.0 * math.pi / l) * t * bands_ref[...]
    pre = (t01 * w1t_ref[...]
           + jnp.dot(jnp.cos(ang), w1c_ref[...], precision=HIGHEST, preferred_element_type=F32)
           + jnp.dot(jnp.sin(ang), w1s_ref[...], precision=HIGHEST, preferred_element_type=F32))
    h1 = jnp.sin(f1_ref[...] * (pre + b1_ref[...]))
    h2 = jnp.sin(f2_ref[...] * (jnp.dot(h1, w2_ref[...], precision=HIGHEST,
                                        preferred_element_type=F32) + b2_ref[...]))
    h3 = jnp.dot(h2, w3_ref[0], precision=HIGHEST, preferred_element_type=F32)
    h = jnp.where(n == l, 0.0, h3 * jnp.exp(-t01 * dl_ref[...]))
    h_ref[...] = h
    part = jnp.sum(jnp.abs(h), axis=0, keepdims=True)

    @pl.when(i == 0)
    def _():
        asum_ref[...] = jnp.zeros_like(asum_ref)

    asum_ref[...] += part


def _hyena_time_kernel(l, w1, b1, f1, w2, b2, f2, w3):
    ffn = w1.shape[1]
    nc = w3.shape[1] // 2
    c = nc // 2
    tl = min(l, 512)
    w3d = jnp.swapaxes(w3.reshape(ffn, 2, nc), 0, 1)
    bands = jnp.zeros((1, 128), F32).at[0, :HYENA_BANDS].set(
        jnp.linspace(1e-4, HYENA_BANDS - 1, HYENA_BANDS, dtype=F32))
    w1c = jnp.zeros((128, ffn), F32).at[:HYENA_BANDS].set(w1[1:1 + HYENA_BANDS])
    w1s = jnp.zeros((128, ffn), F32).at[:HYENA_BANDS].set(w1[1 + HYENA_BANDS:])
    max_decay = math.log(HYENA_TARGET) / HYENA_FAST_DECAY
    min_decay = math.log(HYENA_TARGET) / HYENA_SLOW_DECAY
    deltas = jnp.abs(jnp.linspace(min_decay, max_decay, c, dtype=F32))
    dl = jnp.tile(deltas, 2).reshape(1, nc)
    row = lambda a: a.reshape(1, -1)
    nfwd = l // tl
    kt, asum = pl.pallas_call(
        functools.partial(_filter_kernel, tl=tl, l=l),
        grid=(2 * nfwd,),
        in_specs=[_const_spec((1, 128)), _const_spec((1, ffn)), _const_spec((128, ffn)),
                  _const_spec((128, ffn)), _const_spec((1, ffn)), _const_spec((1, ffn)),
                  _const_spec((ffn, ffn)), _const_spec((1, ffn)), _const_spec((1, ffn)),
                  pl.BlockSpec((1, ffn, nc), lambda i: (i // nfwd, 0, 0)), _const_spec((1, nc))],
        out_specs=[pl.BlockSpec((tl, nc), lambda i: (i, 0)), pl.BlockSpec((1, nc), lambda i: (0, 0))],
        out_shape=[jax.ShapeDtypeStruct((2 * l, nc), F32), jax.ShapeDtypeStruct((1, nc), F32)],
        compiler_params=_params(1), name="hyena_filter_mlp",
    )(bands, w1[0:1], w1c, w1s, row(b1), row(f1), w2, row(b2), row(f2), w3d, dl)
    return kt, 1.0 / asum


def _angles(prod, n):
    th = (2.0 * math.pi / n) * (prod % n).astype(F32)
    return jnp.cos(th), jnp.sin(th)


def _dft_tables():
    r = DFT_R
    n = r * r
    idx = jnp.arange(r, dtype=jnp.int32)
    c, s = _angles(idx[:, None] * idx[None, :], r)
    ch, sh = c[:, :r // 2], s[:, :r // 2]
    w1_data = jnp.concatenate([jnp.concatenate([ch, sh], 1), jnp.concatenate([-sh, ch], 1)], 0)
    w1_real = jnp.concatenate([c, -s], 0)
    ct, st = c[:r // 2], s[:r // 2]
    w2 = jnp.concatenate([jnp.concatenate([ct, -st], 1), jnp.concatenate([st, ct], 1)], 0)
    k = idx[:, None, None] + r * idx[None, :, None]
    cg, sg = _angles(k * idx[None, None, :], n)
    g = jnp.concatenate([jnp.concatenate([cg, sg], 2), jnp.concatenate([-sg, cg], 2)], 1)
    cgt, sgt = jnp.swapaxes(cg, 1, 2), jnp.swapaxes(sg, 1, 2)
    gi = jnp.concatenate([jnp.concatenate([cgt, -sgt], 2), jnp.concatenate([sgt, cgt], 2)], 1)
    dp = DFT_DATA_PASSES
    return dict(w1_data=_split3(w1_data, dp), w1_real=_split3(w1_real), w2=_split3(w2, dp),
                g=_split3(g, dp), gi=_split3(gi, dp), g_filter=_split3(g))


N2_TILE = 8


def _dft1_kernel(x_ref, w_ref, o_ref, *, parts):
    r = o_ref.shape[2]
    for i in range(N2_TILE):
        x = jnp.concatenate([x_ref[p, :, i, :] for p in range(parts)], axis=0)
        y = _mm3(w_ref[...], x)
        o_ref[0, 0, :, i, :] = y[:r]
        o_ref[0, 1, :, i, :] = y[r:]


def _dft_major(x, w3, parts):
    b, n1, r, c = x.shape
    return pl.pallas_call(
        functools.partial(_dft1_kernel, parts=parts),
        grid=(b // parts, r // N2_TILE),
        in_specs=[pl.BlockSpec((parts, n1, N2_TILE, c), lambda p_, j: (p_, 0, j, 0)), _const_spec(w3.shape)],
        out_specs=pl.BlockSpec((1, 2, r, N2_TILE, c), lambda p_, j: (p_, 0, 0, j, 0)),
        out_shape=jax.ShapeDtypeStruct((b // parts, 2, r, r, c), F32),
        compiler_params=_params(2), name="dft_major",
    )(x, w3)


def _spectrum_kernel(a_ref, g_ref, sc_ref, o_ref):
    a = jnp.concatenate([a_ref[0, 0, 0], a_ref[0, 1, 0]], axis=0)
    x = _mm3(g_ref[0], a) * sc_ref[...]
    r = x.shape[0] // 2
    o_ref[0, 0] = x[:r]
    o_ref[1, 0] = x[r:]


def _filter_spectrum(kt, scale, tabs):
    r = DFT_R
    c = kt.shape[1]
    a = _dft_major(kt.reshape(1, r, r, c), tabs["w1_real"], 1)
    tcs = c
    return pl.pallas_call(
        _spectrum_kernel,
        grid=(r, c // tcs),
        in_specs=[pl.BlockSpec((1, 2, 1, r, tcs), lambda k, j: (0, 0, k, 0, j)),
                  pl.BlockSpec((1,) + tabs["g_filter"].shape[1:], lambda k, j: (k, 0, 0)),
                  pl.BlockSpec((1, tcs), lambda k, j: (0, j))],
        out_specs=pl.BlockSpec((2, 1, r, tcs), lambda k, j: (0, k, 0, j)),
        out_shape=jax.ShapeDtypeStruct((2, r, r, c), F32),
        compiler_params=_params(2), name="filter_spectrum",
    )(a, tabs["g_filter"], scale)


def _dft2_kernel(a_ref, g_ref, gi_ref, h_ref, o_ref):
    hr, hi = h_ref[0, 0], h_ref[1, 0]
    for p in range(a_ref.shape[0]):
        a = jnp.concatenate([a_ref[p, 0, 0], a_ref[p, 1, 0]], axis=0)
        x = _mm3(g_ref[0], a)
        r = x.shape[0] // 2
        xr, xi = x[:r], x[r:]
        y = jnp.concatenate([xr * hr - xi * hi, xr * hi + xi * hr], axis=0)
        bb = _mm3(gi_ref[0], y)
        o_ref[p, 0, 0] = bb[:r]
        o_ref[p, 1, 0] = bb[r:]


def _dft3_kernel(b_ref, w_ref, u_ref, x_ref, bias_ref, o_ref):
    bias = bias_ref[...]
    for i in range(N2_TILE):
        bb = jnp.concatenate([b_ref[0, 0, :, i, :], b_ref[0, 1, :, i, :]], axis=0)
        y = _mm3(w_ref[...], bb)
        half = y.shape[0] // 2
        for h in range(2):
            o_ref[h, :, i, :] = x_ref[h, :, i, :] * (y[h * half:(h + 1) * half] + u_ref[h, :, i, :] * bias)


def _long_conv_gate(u, x, hspec, order, bias, tabs):
    b, l, c = u.shape
    r = DFT_R
    p = b // 2
    nat = (b, r // 2, r, c)
    a = _dft_major(u.reshape(nat), tabs["w1_data"], 2)
    spec5 = pl.BlockSpec((p, 2, 1, r, c), lambda k: (0, 0, k, 0, 0))
    bb = pl.pallas_call(
        _dft2_kernel,
        grid=(r,),
        in_specs=[spec5,
                  pl.BlockSpec((1,) + tabs["g"].shape[1:], lambda k: (k, 0, 0)),
                  pl.BlockSpec((1,) + tabs["gi"].shape[1:], lambda k: (k, 0, 0)),
                  pl.BlockSpec((2, 1, r, c), lambda k: (0, k, 0, order))],
        out_specs=spec5,
        out_shape=jax.ShapeDtypeStruct((p, 2, r, r, c), F32),
        compiler_params=_params(1), name="dft_minor_filter",
    )(a, tabs["g"], tabs["gi"], hspec)
    blk = pl.BlockSpec((2, r // 2, N2_TILE, c), lambda p_, j: (p_, 0, j, 0))
    out = pl.pallas_call(
        _dft3_kernel,
        grid=(p, r // N2_TILE),
        in_specs=[pl.BlockSpec((1, 2, r, N2_TILE, c), lambda p_, j: (p_, 0, 0, j, 0)),
                  _const_spec(tabs["w2"].shape), blk, blk, _const_spec((1, c))],
        out_specs=blk,
        out_shape=jax.ShapeDtypeStruct(nat, F32),
        compiler_params=_params(2), name="idft_major_gate",
    )(bb, tabs["w2"], u.reshape(nat), x.reshape(nat), bias.reshape(1, c))
    return out.reshape(b, l, c)


def _dense_tables(l):
    n = 2 * l
    k = jnp.arange(n, dtype=jnp.int32)
    t = jnp.arange(l, dtype=jnp.int32)
    c, s = _angles(k[:, None] * t[None, :], n)
    wf = jnp.concatenate([jnp.concatenate([c, s], 1), jnp.concatenate([-s, c], 1)], 0)
    ct, st = c.T, s.T
    wi = jnp.concatenate([jnp.concatenate([ct, -st], 1), jnp.concatenate([st, ct], 1)], 0)
    ca, sa = _angles(k[:, None] * k[None, :], n)
    wk = jnp.concatenate([ca, -sa], 0)
    dp = DFT_DATA_PASSES
    return dict(wf=_split3(wf, dp), wi=_split3(wi, dp), wk=_split3(wk))


def _dense_spectrum_kernel(k_ref, w_ref, sc_ref, o_ref):
    o_ref[...] = _mm3(w_ref[...], k_ref[...]) * sc_ref[...]


def _dense_spectrum(kt, scale, wk):
    n, c = kt.shape
    return pl.pallas_call(
        _dense_spectrum_kernel,
        grid=(1,),
        in_specs=[_const_spec((n, c)), _const_spec(wk.shape), _const_spec((1, c))],
        out_specs=pl.BlockSpec((2 * n, c), lambda i: (0, 0)),
        out_shape=jax.ShapeDtypeStruct((2 * n, c), F32),
        compiler_params=_params(1), name="dense_filter_spectrum",
    )(kt, wk, scale)


def _dense_conv_kernel(u_ref, x_ref, wf_ref, wi_ref, h_ref, bias_ref, o_ref):
    u0, u1 = u_ref[0], u_ref[1]
    z = _mm3(wf_ref[...], jnp.concatenate([u0, u1], axis=0))
    n = z.shape[0] // 2
    zr, zi = z[:n], z[n:]
    hr, hi = h_ref[:n], h_ref[n:]
    y = _mm3(wi_ref[...], jnp.concatenate([zr * hr - zi * hi, zr * hi + zi * hr], axis=0))
    l = y.shape[0] // 2
    bias = bias_ref[...]
    o_ref[0] = x_ref[0] * (y[:l] + u0 * bias)
    o_ref[1] = x_ref[1] * (y[l:] + u1 * bias)


def _dense_conv_gate(u, x, hspec, order, bias, tabs):
    b, l, c = u.shape
    blk = pl.BlockSpec((2, l, c), lambda p_: (p_, 0, 0))
    return pl.pallas_call(
        _dense_conv_kernel,
        grid=(b // 2,),
        in_specs=[blk, blk, _const_spec(tabs["wf"].shape), _const_spec(tabs["wi"].shape),
                  pl.BlockSpec((4 * l, c), lambda p_: (0, order)), _const_spec((1, c))],
        out_specs=blk,
        out_shape=jax.ShapeDtypeStruct((b, l, c), F32),
        compiler_params=_params(1), name="dense_conv_gate",
    )(u, x, tabs["wf"], tabs["wi"], hspec, bias.reshape(1, c))


def _mix_kernel(h_ref, o_ref, pm_ref, y_ref, g_ref, gate_ref, wa_ref, wp_ref, wh_ref, wo_ref, out_ref):
    d = h_ref.shape[2]
    a = _dot(o_ref[0].astype(BF16), wa_ref[...])
    p = _dot(pm_ref[0].astype(BF16), wp_ref[...])
    y = _dot(y_ref[0].astype(BF16), wh_ref[...])
    merged = g_ref[0, :, 0:d] * a + g_ref[0, :, d:2 * d] * p + g_ref[0, :, 2 * d:3 * d] * y
    out = _dot(merged.astype(BF16), wo_ref[...])
    out_ref[0] = h_ref[0] + gate_ref[0] * out


def _mix(h, o, pm, y, g, mod, w_attn_o, w_pool_o, w_hy_o, w_out, tm=512):
    b, s, d = h.shape
    tm = min(tm, s)

    def tok(w):
        return pl.BlockSpec((1, tm, w), lambda b_, i: (b_, i, 0))

    ws = [w.astype(BF16) for w in (w_attn_o, w_pool_o, w_hy_o, w_out)]
    return pl.pallas_call(
        _mix_kernel,
        grid=(b, s // tm),
        in_specs=[tok(d), tok(o.shape[2]), tok(pm.shape[2]), tok(y.shape[2]), tok(g.shape[2]),
                  _mod_spec(mod, d, 2)] + [_const_spec(w.shape) for w in ws],
        out_specs=tok(d),
        out_shape=jax.ShapeDtypeStruct((b, s, d), F32),
        compiler_params=_params(2), name="merge_outproj",
    )(h, o, pm, y, g, mod, *ws)


def _ffn_kernel(h_ref, sh_ref, sc_ref, gate_ref, g_ref, wi_ref, wo_ref, fg_ref, out_ref, *,
                hidden, chunk, final):
    x = h_ref[0]
    a = _norm_mod(x, g_ref[...], sh_ref[0], sc_ref[0]).astype(BF16)
    acc = jnp.zeros(x.shape, F32)
    for c0 in range(0, hidden, chunk):
        gt = _dot(a, wi_ref[:, c0:c0 + chunk])
        up = _dot(a, wi_ref[:, hidden + c0:hidden + c0 + chunk])
        act = (gt * jax.nn.sigmoid(gt) * up).astype(BF16)
        acc = acc + _dot(act, wo_ref[c0:c0 + chunk, :])
    y = x + gate_ref[0] * acc
    if final:
        ms = jnp.mean(y * y, axis=-1, keepdims=True)
        y = y * lax.rsqrt(ms + EPS) * fg_ref[...]
    out_ref[0] = y


def _ffn(h, mod, norm_g, w_in, w_out, final_g, final, tm=512):
    b, s, d = h.shape
    hidden = w_out.shape[0]
    tm = min(tm, s)
    tok = pl.BlockSpec((1, tm, d), lambda b_, i: (b_, i, 0))
    return pl.pallas_call(
        functools.partial(_ffn_kernel, hidden=hidden, chunk=hidden // 2, final=final),
        grid=(b, s // tm),
        in_specs=[tok, _mod_spec(mod, d, 3), _mod_spec(mod, d, 4), _mod_spec(mod, d, 5),
                  _const_spec((1, d)), _const_spec(w_in.shape), _const_spec(w_out.shape),
                  _const_spec((1, d))],
        out_specs=tok,
        out_shape=jax.ShapeDtypeStruct((b, s, d), F32),
        compiler_params=_params(2), name="swiglu_ffn",
    )(h, mod, mod, mod, norm_g.reshape(1, d), w_in.astype(BF16), w_out.astype(BF16),
      final_g.reshape(1, d))


def _rope_tables(seq, width):
    t = jnp.arange(seq, dtype=jnp.int32)
    row = (t // GRID_W).astype(F32)
    col = (t % GRID_W).astype(F32)
    inv = 1.0 / (ROPE_BASE ** (jnp.arange(ROPE_FREQS, dtype=F32) * 2.0 / (2 * ROPE_FREQS)))
    ar, ac = row[:, None] * inv, col[:, None] * inv
    cos = jnp.concatenate([jnp.cos(ar)] * 2 + [jnp.cos(ac)] * 2, axis=1)
    sin = jnp.concatenate([-jnp.sin(ar), jnp.sin(ar), -jnp.sin(ac), jnp.sin(ac)], axis=1)
    reps = width // cos.shape[1]
    return jnp.tile(cos, (1, reps)), jnp.tile(sin, (1, reps))


def kernel(x, c, ctx, c_ctx, w_mod, b_mod, norm1_g, norm2_g, w_in, lam_qk, subln_g, w_attn_o, w_pool, pool_scale, w_pool_o, w_short, b_short, hf_w1, hf_b1, hf_freq1, hf_w2, hf_b2, hf_freq2, hf_w3, hy_bias, w_hy_o, w_out, w_ffn_in, w_ffn_out, final_g):
    batch, seq, d = x.shape
    depth = w_mod.shape[0]
    ctx_len = ctx.shape[1]
    aw = w_attn_o.shape[1]
    pw = w_pool_o.shape[1]
    hw = w_hy_o.shape[1]
    q0, k0, v0, p0 = 0, aw, 2 * aw, 3 * aw
    h0 = p0 + pw
    g0 = h0 + 3 * hw
    assert seq * 2 == DFT_R * DFT_R and batch % 2 == 0

    c8 = jnp.zeros((8, d), F32).at[:batch].set(c).at[batch].set(c_ctx)
    mod_all = _modulation(c8, w_mod, b_mod)
    rope_tabs = _rope_tables(seq, ATTN_VD)
    tabs = _dft_tables()
    ctx_tabs = _dense_tables(ctx_len)

    h_lat, h_ctx = x, ctx
    for l in range(depth):
        last = l == depth - 1
        lam_init = LAMBDA_INIT_BASE - LAMBDA_INIT_AMP * math.exp(-LAMBDA_INIT_RATE * l)
        mod = mod_all[l, :batch].reshape(batch, 1, N_MOD * d)
        mod_c = mod_all[l, batch:batch + 1].reshape(1, 1, N_MOD * d)
        wl = w_in[l]
        fparams = (hf_w1[l], hf_b1[l], hf_freq1[l], hf_w2[l], hf_b2[l], hf_freq2[l], hf_w3[l])

        wl_bf = wl.astype(BF16)
        segs = [("rope", aw, Q_SCALE), ("rope", aw, 1.0), ("plain", aw, 1.0), ("plain", pw, 1.0),
                ("plain", 3 * hw, 1.0), ("sigmoid", 3 * d, 1.0)]
        q, k, v, up, uh, g = _inproj(h_lat, mod, norm1_g[l], wl_bf, segs,
                                     [BF16, BF16, BF16, F32, F32, F32], rope_tabs)

        if last:
            segs_c = [("plain", aw, 1.0), ("plain", aw, 1.0)]
            k_c, v_c = _inproj(h_ctx, mod_c, norm1_g[l], wl_bf[:, k0:p0], segs_c, [BF16, BF16])
        else:
            segs_c = [("plain", aw, Q_SCALE), ("plain", aw, 1.0), ("plain", aw, 1.0), ("plain", pw, 1.0),
                      ("plain", 3 * hw, 1.0), ("sigmoid", 3 * d, 1.0)]
            q_c, k_c, v_c, up_c, uh_c, g_c = _inproj(h_ctx, mod_c, norm1_g[l], wl_bf, segs_c,
                                                    [BF16, BF16, BF16, F32, F32, F32])
            o_c = _attention(q_c, k_c, v_c, lam_qk[l], subln_g[l], lam_init)
            pm_c, vv_c, x1_c, x2_c = _local_mixers(up_c, uh_c, w_pool[l], pool_scale[l], w_short[l], b_short[l])
            kt_c, inv_norm_c = _hyena_time_kernel(ctx_len, *fparams)
            hspec_c = _dense_spectrum(kt_c, inv_norm_c / (2 * ctx_len), ctx_tabs["wk"])
            z_c = _dense_conv_gate(vv_c, x1_c, hspec_c, 0, hy_bias[l, 0], ctx_tabs)
            y_c = _dense_conv_gate(z_c, x2_c, hspec_c, 1, hy_bias[l, 1], ctx_tabs)
            h_ctx_mid = _mix(h_ctx, o_c, pm_c, y_c, g_c, mod_c, w_attn_o[l], w_pool_o[l], w_hy_o[l], w_out[l])
            h_ctx_new = _ffn(h_ctx_mid, mod_c, norm2_g[l], w_ffn_in[l], w_ffn_out[l], final_g, False)

        o = _attention_pipelined(q, jnp.concatenate([k_c, k], axis=1), jnp.concatenate([v_c, v], axis=1),
                                 lam_qk[l], subln_g[l], lam_init)
        pm, vv, x1, x2 = _local_mixers(up, uh, w_pool[l], pool_scale[l], w_short[l], b_short[l])
        kt, inv_norm = _hyena_time_kernel(seq, *fparams)
        hspec = _filter_spectrum(kt, inv_norm / (2 * seq), tabs)
        z = _long_conv_gate(vv, x1, hspec, 0, hy_bias[l, 0], tabs)
        y = _long_conv_gate(z, x2, hspec, 1, hy_bias[l, 1], tabs)
        h_mid = _mix(h_lat, o, pm, y, g, mod, w_attn_o[l], w_pool_o[l], w_hy_o[l], w_out[l])
        h_lat = _ffn(h_mid, mod, norm2_g[l], w_ffn_in[l], w_ffn_out[l], final_g, last)
        if not last:
            h_ctx = h_ctx_new
    return h_lat
```

```python
import functools
import math

import jax
import jax.numpy as jnp
from jax import lax
from jax.experimental import pallas as pl
from jax.experimental.pallas import tpu as pltpu

F32 = jnp.float32
BF16 = jnp.bfloat16
HIGHEST = lax.Precision.HIGHEST

GRID_W = 64
N_MOD = 6
ATTN_HD = 64
ATTN_VD = 128
Q_SCALE = ATTN_HD ** -0.5 * math.log2(math.e)
ROPE_BASE = 10000.0
ROPE_FREQS = 16
LAMBDA_INIT_BASE = 0.8
LAMBDA_INIT_AMP = 0.6
LAMBDA_INIT_RATE = 0.3
POOL_WINDOWS = (2, 4, 8, 16)
POOL_GC = 128
HALO = 8
HYENA_BANDS = 16
HYENA_FAST_DECAY = 0.3
HYENA_SLOW_DECAY = 1.5
HYENA_TARGET = 1e-2
EPS = 1e-6
DFT_DATA_PASSES = 1
DFT_R = 128

VMEM_LIMIT = 56 * 1024 * 1024


def _params(n_grid, flags=None):
    return pltpu.CompilerParams(dimension_semantics=("arbitrary",) * n_grid,
                                vmem_limit_bytes=VMEM_LIMIT, flags=flags)


def _const_spec(shape):
    zeros = (0,) * len(shape)
    return pl.BlockSpec(shape, lambda *_: zeros, pipeline_mode=pl.Buffered(1))


def _dot(a, b):
    return jnp.dot(a, b, preferred_element_type=F32)


def _split3(w, passes=3):
    hi = w.astype(BF16)
    if passes == 1:
        return hi
    lo = (w - hi.astype(F32)).astype(BF16)
    return jnp.concatenate([hi, hi, lo], axis=-1)


def _mm3(w3, x):
    xh = x.astype(BF16)
    if w3.shape[-1] == x.shape[0]:
        return _dot(w3, xh)
    xl = (x - xh.astype(F32)).astype(BF16)
    return _dot(w3, jnp.concatenate([xh, xl, xh], axis=0))


def _mod_kernel(c_ref, w_ref, b_ref, o_ref):
    c = c_ref[...]
    act = c * jax.nn.sigmoid(c)
    o_ref[0] = jnp.dot(act, w_ref[0], precision=HIGHEST, preferred_element_type=F32) + b_ref[0]


def _modulation(c8, w_mod, b_mod):
    depth, d, n = w_mod.shape
    tn = 1536
    return pl.pallas_call(
        _mod_kernel,
        grid=(depth, n // tn),
        in_specs=[pl.BlockSpec((8, d), lambda l, j: (0, 0)),
                  pl.BlockSpec((1, d, tn), lambda l, j: (l, 0, j)),
                  pl.BlockSpec((1, 1, tn), lambda l, j: (l, 0, j))],
        out_specs=pl.BlockSpec((1, 8, tn), lambda l, j: (l, 0, j)),
        out_shape=jax.ShapeDtypeStruct((depth, 8, n), F32),
        compiler_params=_params(2),
        name="modulation",
    )(c8, w_mod, b_mod.reshape(depth, 1, n))


def _mod_spec(mod, d, col):
    if mod.shape[0] == 1:
        return pl.BlockSpec((1, 1, d), lambda b, i: (0, 0, col))
    return pl.BlockSpec((1, 1, d), lambda b, i: (b, 0, col))


def _norm_mod(x, g, shift, scale):
    ms = jnp.mean(x * x, axis=-1, keepdims=True)
    a = x * lax.rsqrt(ms + EPS) * g
    return a * (1.0 + scale) + shift


def _rope(p, cos, sin):
    lanes = cos.shape[1]
    lane = lax.broadcasted_iota(jnp.int32, cos.shape, 1)
    is_a = lane % (2 * ROPE_FREQS) < ROPE_FREQS
    out = []
    for g in range(p.shape[1] // lanes):
        x = p[:, g * lanes:(g + 1) * lanes]
        partner = jnp.where(is_a, pltpu.roll(x, lanes - ROPE_FREQS, 1), pltpu.roll(x, ROPE_FREQS, 1))
        out.append(x * cos + partner * sin)
    return jnp.concatenate(out, axis=1)


def _inproj_kernel(*refs, segs, rope):
    h_ref, sh_ref, sc_ref, g_ref, w_ref = refs[:5]
    n_in = 7 if rope else 5
    out_refs = refs[n_in:]
    a = _norm_mod(h_ref[0], g_ref[...], sh_ref[0], sc_ref[0]).astype(BF16)
    col = 0
    for (kind, width, scale), o_ref in zip(segs, out_refs):
        p = _dot(a, w_ref[:, col:col + width])
        col += width
        if kind == "rope":
            p = _rope(p, refs[5][...], refs[6][...])
        elif kind == "sigmoid":
            p = jax.nn.sigmoid(p)
        if scale != 1.0:
            p = p * scale
        o_ref[0] = p.astype(o_ref.dtype)


def _inproj(h, mod, norm_g, w_cat, segs, out_dtypes, rope_tabs=None, tm=256):
    b, s, d = h.shape
    tm = min(tm, s)
    rope = rope_tabs is not None
    in_specs = [pl.BlockSpec((1, tm, d), lambda b_, i: (b_, i, 0)),
                _mod_spec(mod, d, 0), _mod_spec(mod, d, 1),
                _const_spec((1, d)), _const_spec(w_cat.shape)]
    args = [h, mod, mod, norm_g.reshape(1, d), w_cat]
    if rope:
        wr = rope_tabs[0].shape[1]
        in_specs += [pl.BlockSpec((tm, wr), lambda b_, i: (i, 0))] * 2
        args += list(rope_tabs)
    out_specs = [pl.BlockSpec((1, tm, w), lambda b_, i: (b_, i, 0)) for (_, w, _) in segs]
    out_shape = [jax.ShapeDtypeStruct((b, s, w), dt) for (_, w, _), dt in zip(segs, out_dtypes)]
    return pl.pallas_call(
        functools.partial(_inproj_kernel, segs=tuple(segs), rope=rope),
        grid=(b, s // tm), in_specs=in_specs, out_specs=out_specs, out_shape=out_shape,
        compiler_params=_params(2), name="inproj",
    )(*args)


def _lambda(lq_ref, lam_init):
    lq = lq_ref[...]
    return (jnp.exp(jnp.sum(lq[0:1] * lq[1:2], axis=-1, keepdims=True))
            - jnp.exp(jnp.sum(lq[2:3] * lq[3:4], axis=-1, keepdims=True)) + lam_init)


def _split_halves(q):
    lane = lax.broadcasted_iota(jnp.int32, q.shape, 1)
    zero = jnp.zeros_like(q)
    return jnp.concatenate([jnp.where(lane < ATTN_HD, q, zero),
                            jnp.where(lane >= ATTN_HD, q, zero)], axis=0)


def _qk(qq, k):
    return lax.dot_general(qq, k, (((1,), (1,)), ((), ())), preferred_element_type=F32)


def _sub_ln(o, g, lam_init):
    ms = jnp.mean(o * o, axis=-1, keepdims=True)
    return o * lax.rsqrt(ms + EPS) * g * (1.0 - lam_init)


def _attn_kernel(lq_ref, q_ref, k_ref, v_ref, g_ref, o_ref, *, lam_init, tq):
    lam = _lambda(lq_ref, lam_init)
    s = _qk(_split_halves(q_ref[0]), k_ref[0])
    m = jnp.max(s, axis=-1, keepdims=True)
    p = jnp.exp2(s - m)
    l = jnp.sum(p, axis=-1, keepdims=True)
    a = (p[:tq] - p[tq:] * (lam * l[:tq] / l[tq:])).astype(BF16)
    o = _dot(a, v_ref[0]) / l[:tq]
    o_ref[0] = _sub_ln(o, g_ref[...], lam_init).astype(o_ref.dtype)


def _attn_t_kernel(lq_ref, q_ref, k_ref, vt_ref, g_ref, o_ref, s0, s1, m_scr, *, lam_init, tq, lk, kc, nq):
    j = pl.program_id(2)
    bufs = (s0, s1)

    @pl.when(j == 0)
    def _():
        s1[...] = jnp.zeros_like(s1)
        m_scr[1] = jnp.zeros(m_scr.shape[1:], F32)

    lam = _lambda(lq_ref, lam_init)
    for r in range(2):
        sig = 2 * j + r
        buf_a, buf_b = bufs[r], bufs[1 - r]
        t_a = jnp.minimum(sig, nq - 1)
        t_b = jnp.clip(sig - 1, 0, nq - 1)
        qq = _split_halves(q_ref[0, pl.ds(pl.multiple_of(t_a * tq, tq), tq), :])
        m_b = m_scr[1 - r]
        m8 = l8 = acc = None
        for c0 in range(0, lk, kc):
            rows = slice(c0, c0 + kc)
            s = _qk(k_ref[0, rows, :], qq)
            buf_a[rows, :] = s
            mc = jnp.max(s.reshape(kc // 8, 8, 2 * tq), axis=0)
            m8 = mc if m8 is None else jnp.maximum(m8, mc)
            p = jnp.exp2(buf_b[rows, :].reshape(kc // 8, 8, 2 * tq) - m_b)
            lc = jnp.sum(p, axis=0)
            l8 = lc if l8 is None else l8 + lc
            pv = _dot(vt_ref[0, :, rows], p.reshape(kc, 2 * tq).astype(BF16))
            acc = pv if acc is None else acc + pv
        m_scr[r] = jnp.broadcast_to(jnp.max(m8, axis=0, keepdims=True), m8.shape)
        l = jnp.sum(l8, axis=0, keepdims=True)
        ot = acc[:, :tq] / l[:, :tq] - acc[:, tq:] * (lam / l[:, tq:])
        ms = jnp.mean(ot * ot, axis=0, keepdims=True)
        ot = ot * lax.rsqrt(ms + EPS) * g_ref[...] * (1.0 - lam_init)
        o_ref[0, pl.ds(pl.multiple_of(t_b * tq, tq), tq), :] = ot.T.astype(o_ref.dtype)


def _attention_t(q, k, vt, lam_qk, subln_g, lam_init, tq=128, kc=768):
    b, s, w = q.shape
    lk = k.shape[1]
    heads = w // ATTN_VD
    nq = s // tq
    nsteps = -(-(nq + 1) // 2)
    return pl.pallas_call(
        functools.partial(_attn_t_kernel, lam_init=lam_init, tq=tq, lk=lk, kc=kc, nq=nq),
        grid=(b, heads, nsteps),
        in_specs=[pl.BlockSpec(lam_qk.shape, lambda b_, h, i: (0, 0)),
                  pl.BlockSpec((1, s, ATTN_VD), lambda b_, h, i: (b_, 0, h)),
                  pl.BlockSpec((1, lk, ATTN_VD), lambda b_, h, i: (b_, 0, h)),
                  pl.BlockSpec((1, ATTN_VD, lk), lambda b_, h, i: (b_, h, 0)),
                  pl.BlockSpec((ATTN_VD, 1), lambda b_, h, i: (0, 0))],
        out_specs=pl.BlockSpec((1, s, ATTN_VD), lambda b_, h, i: (b_, 0, h)),
        out_shape=jax.ShapeDtypeStruct((b, s, w), F32),
        scratch_shapes=[pltpu.VMEM((lk, 2 * tq), F32)] * 2 + [pltpu.VMEM((2, 8, 2 * tq), F32)],
        compiler_params=_params(3), name="diff_attention_t",
    )(lam_qk, q, k, vt, subln_g.reshape(ATTN_VD, 1))


def _attention(q, k, v, lam_qk, subln_g, lam_init, tq=128):
    b, s, w = q.shape
    lk = k.shape[1]
    heads = w // ATTN_VD
    tq = min(tq, s)
    return pl.pallas_call(
        functools.partial(_attn_kernel, lam_init=lam_init, tq=tq),
        grid=(b, heads, s // tq),
        in_specs=[pl.BlockSpec(lam_qk.shape, lambda b_, h, i: (0, 0)),
                  pl.BlockSpec((1, tq, ATTN_VD), lambda b_, h, i: (b_, i, h)),
                  pl.BlockSpec((1, lk, ATTN_VD), lambda b_, h, i: (b_, 0, h)),
                  pl.BlockSpec((1, lk, ATTN_VD), lambda b_, h, i: (b_, 0, h)),
                  pl.BlockSpec((1, ATTN_VD), lambda b_, h, i: (0, 0))],
        out_specs=pl.BlockSpec((1, tq, ATTN_VD), lambda b_, h, i: (b_, i, h)),
        out_shape=jax.ShapeDtypeStruct((b, s, w), F32),
        compiler_params=_params(3), name="diff_attention",
    )(lam_qk, q, k, v, subln_g.reshape(1, ATTN_VD))


def _local_kernel(up_ref, upp_ref, upn_ref, uh_ref, uhp_ref, uhn_ref, wp_ref, ps_ref, ws_ref,
                  bs_ref, pm_ref, v_ref, x1_ref, x2_ref, pbuf, hbuf, *, ts, seq):
    i = pl.program_id(1)
    first = i == 0
    last = i == pl.num_programs(1) - 1

    def fill(buf, cur, prev, nxt):
        buf[0:HALO] = jnp.where(first, 0.0, prev[0])
        buf[HALO:HALO + ts] = cur[0]
        buf[HALO + ts:HALO + ts + HALO] = jnp.where(last, 0.0, nxt[0])

    fill(pbuf, up_ref, upp_ref, upn_ref)
    fill(hbuf, uh_ref, uhp_ref, uhn_ref)

    t = i * ts + lax.broadcasted_iota(jnp.int32, (ts, 1), 0)
    for g, w in enumerate(POOL_WINDOWS):
        cs = slice(g * POOL_GC, (g + 1) * POOL_GC)
        acc = pbuf[HALO - w // 2:HALO - w // 2 + ts, cs]
        for j in range(1 - w // 2, w // 2):
            acc = acc + pbuf[HALO + j:HALO + j + ts, cs]
        cnt = (jnp.minimum(t + w // 2, seq) - jnp.maximum(t - w // 2, 0)).astype(F32)
        pooled = acc / cnt - pbuf[HALO:HALO + ts, cs]
        mixed = _dot(pooled.astype(BF16), wp_ref[g])
        pm_ref[0, :, cs] = mixed * ps_ref[:, cs]

    ws = ws_ref[...]
    conv = (hbuf[HALO - 1:HALO - 1 + ts] * ws[0:1] + hbuf[HALO:HALO + ts] * ws[1:2]
            + hbuf[HALO + 1:HALO + 1 + ts] * ws[2:3] + bs_ref[...])
    c = conv.shape[1] // 3
    v_ref[0] = conv[:, :c]
    x1_ref[0] = conv[:, c:2 * c]
    x2_ref[0] = conv[:, 2 * c:]


def _local_mixers(up, uh, w_pool, pool_scale, w_short, b_short, ts=512):
    b, s, cp = up.shape
    ch = uh.shape[2]
    c = ch // 3
    ts = min(ts, s)
    nh = ts // HALO
    last_halo = s // HALO - 1

    def cur(w):
        return pl.BlockSpec((1, ts, w), lambda b_, i: (b_, i, 0))

    def prev(w):
        return pl.BlockSpec((1, HALO, w), lambda b_, i: (b_, jnp.maximum(i * nh - 1, 0), 0))

    def nxt(w):
        return pl.BlockSpec((1, HALO, w), lambda b_, i: (b_, jnp.minimum((i + 1) * nh, last_halo), 0))

    return pl.pallas_call(
        functools.partial(_local_kernel, ts=ts, seq=s),
        grid=(b, s // ts),
        in_specs=[cur(cp), prev(cp), nxt(cp), cur(ch), prev(ch), nxt(ch),
                  _const_spec(w_pool.shape), _const_spec((1, cp)),
                  _const_spec(w_short.shape), _const_spec((1, ch))],
        out_specs=[cur(cp), cur(c), cur(c), cur(c)],
        out_shape=[jax.ShapeDtypeStruct((b, s, cp), F32)] + [jax.ShapeDtypeStruct((b, s, c), F32)] * 3,
        scratch_shapes=[pltpu.VMEM((ts + 2 * HALO, cp), F32), pltpu.VMEM((ts + 2 * HALO, ch), F32)],
        compiler_params=_params(2), name="pool_shortconv",
    )(up, up, up, uh, uh, uh, w_pool.astype(BF16), pool_scale.reshape(1, cp), w_short,
      b_short.reshape(1, ch))


def _filter_kernel(bands_ref, w1t_ref, w1c_ref, w1s_ref, b1_ref, f1_ref, w2_ref, b2_ref, f2_ref,
                   w3_ref, dl_ref, h_ref, asum_ref, *, tl, l):
    i = pl.program_id(0)
    n = i * tl + lax.broadcasted_iota(jnp.int32, (tl, 1), 0)
    t = jnp.where(n < l, n, 2 * l - n).astype(F32)
    t01 = t / float(l - 1)
    ang = (2.0 * math.pi / l) * t * bands_ref[...]
    pre = (t01 * w1t_ref[...]
           + jnp.dot(jnp.cos(ang), w1c_ref[...], precision=HIGHEST, preferred_element_type=F32)
           + jnp.dot(jnp.sin(ang), w1s_ref[...], precision=HIGHEST, preferred_element_type=F32))
    h1 = jnp.sin(f1_ref[...] * (pre + b1_ref[...]))
    h2 = jnp.sin(f2_ref[...] * (jnp.dot(h1, w2_ref[...], precision=HIGHEST,
                                        preferred_element_type=F32) + b2_ref[...]))
    h3 = jnp.dot(h2, w3_ref[0], precision=HIGHEST, preferred_element_type=F32)
    h = jnp.where(n == l, 0.0, h3 * jnp.exp(-t01 * dl_ref[...]))
    h_ref[...] = h
    part = jnp.sum(jnp.abs(h), axis=0, keepdims=True)

    @pl.when(i == 0)
    def _():
        asum_ref[...] = jnp.zeros_like(asum_ref)

    asum_ref[...] += part


def _hyena_time_kernel(l, w1, b1, f1, w2, b2, f2, w3):
    ffn = w1.shape[1]
    nc = w3.shape[1] // 2
    c = nc // 2
    tl = min(l, 512)
    w3d = jnp.swapaxes(w3.reshape(ffn, 2, nc), 0, 1)
    bands = jnp.zeros((1, 128), F32).at[0, :HYENA_BANDS].set(
        jnp.linspace(1e-4, HYENA_BANDS - 1, HYENA_BANDS, dtype=F32))
    w1c = jnp.zeros((128, ffn), F32).at[:HYENA_BANDS].set(w1[1:1 + HYENA_BANDS])
    w1s = jnp.zeros((128, ffn), F32).at[:HYENA_BANDS].set(w1[1 + HYENA_BANDS:])
    max_decay = math.log(HYENA_TARGET) / HYENA_FAST_DECAY
    min_decay = math.log(HYENA_TARGET) / HYENA_SLOW_DECAY
    deltas = jnp.abs(jnp.linspace(min_decay, max_decay, c, dtype=F32))
    dl = jnp.tile(deltas, 2).reshape(1, nc)
    row = lambda a: a.reshape(1, -1)
    nfwd = l // tl
    kt, asum = pl.pallas_call(
        functools.partial(_filter_kernel, tl=tl, l=l),
        grid=(2 * nfwd,),
        in_specs=[_const_spec((1, 128)), _const_spec((1, ffn)), _const_spec((128, ffn)),
                  _const_spec((128, ffn)), _const_spec((1, ffn)), _const_spec((1, ffn)),
                  _const_spec((ffn, ffn)), _const_spec((1, ffn)), _const_spec((1, ffn)),
                  pl.BlockSpec((1, ffn, nc), lambda i: (i // nfwd, 0, 0)), _const_spec((1, nc))],
        out_specs=[pl.BlockSpec((tl, nc), lambda i: (i, 0)), pl.BlockSpec((1, nc), lambda i: (0, 0))],
        out_shape=[jax.ShapeDtypeStruct((2 * l, nc), F32), jax.ShapeDtypeStruct((1, nc), F32)],
        compiler_params=_params(1), name="hyena_filter_mlp",
    )(bands, w1[0:1], w1c, w1s, row(b1), row(f1), w2, row(b2), row(f2), w3d, dl)
    return kt, 1.0 / asum


def _angles(prod, n):
    th = (2.0 * math.pi / n) * (prod % n).astype(F32)
    return jnp.cos(th), jnp.sin(th)


def _dft_tables():
    r = DFT_R
    n = r * r
    idx = jnp.arange(r, dtype=jnp.int32)
    c, s = _angles(idx[:, None] * idx[None, :], r)
    ch, sh = c[:, :r // 2], s[:, :r // 2]
    w1_data = jnp.concatenate([jnp.concatenate([ch, sh], 1), jnp.concatenate([-sh, ch], 1)], 0)
    w1_real = jnp.concatenate([c, -s], 0)
    ct, st = c[:r // 2], s[:r // 2]
    w2 = jnp.concatenate([jnp.concatenate([ct, -st], 1), jnp.concatenate([st, ct], 1)], 0)
    k = idx[:, None, None] + r * idx[None, :, None]
    cg, sg = _angles(k * idx[None, None, :], n)
    g = jnp.concatenate([jnp.concatenate([cg, sg], 2), jnp.concatenate([-sg, cg], 2)], 1)
    cgt, sgt = jnp.swapaxes(cg, 1, 2), jnp.swapaxes(sg, 1, 2)
    gi = jnp.concatenate([jnp.concatenate([cgt, -sgt], 2), jnp.concatenate([sgt, cgt], 2)], 1)
    dp = DFT_DATA_PASSES
    return dict(w1_data=_split3(w1_data, dp), w1_real=_split3(w1_real), w2=_split3(w2, dp),
                g=_split3(g, dp), gi=_split3(gi, dp), g_filter=_split3(g))


N2_TILE = 8


def _dft1_kernel(x_ref, w_ref, o_ref, *, parts):
    r = o_ref.shape[2]
    for i in range(N2_TILE):
        x = jnp.concatenate([x_ref[p, :, i, :] for p in range(parts)], axis=0)
        y = _mm3(w_ref[...], x)
        o_ref[0, 0, :, i, :] = y[:r]
        o_ref[0, 1, :, i, :] = y[r:]


def _dft_major(x, w3, parts):
    b, n1, r, c = x.shape
    return pl.pallas_call(
        functools.partial(_dft1_kernel, parts=parts),
        grid=(b // parts, r // N2_TILE),
        in_specs=[pl.BlockSpec((parts, n1, N2_TILE, c), lambda p_, j: (p_, 0, j, 0)), _const_spec(w3.shape)],
        out_specs=pl.BlockSpec((1, 2, r, N2_TILE, c), lambda p_, j: (p_, 0, 0, j, 0)),
        out_shape=jax.ShapeDtypeStruct((b // parts, 2, r, r, c), F32),
        compiler_params=_params(2), name="dft_major",
    )(x, w3)


def _spectrum_kernel(a_ref, g_ref, sc_ref, o_ref):
    a = jnp.concatenate([a_ref[0, 0, 0], a_ref[0, 1, 0]], axis=0)
    x = _mm3(g_ref[0], a) * sc_ref[...]
    r = x.shape[0] // 2
    o_ref[0, 0] = x[:r]
    o_ref[1, 0] = x[r:]


def _filter_spectrum(kt, scale, tabs):
    r = DFT_R
    c = kt.shape[1]
    a = _dft_major(kt.reshape(1, r, r, c), tabs["w1_real"], 1)
    tcs = c
    return pl.pallas_call(
        _spectrum_kernel,
        grid=(r, c // tcs),
        in_specs=[pl.BlockSpec((1, 2, 1, r, tcs), lambda k, j: (0, 0, k, 0, j)),
                  pl.BlockSpec((1,) + tabs["g_filter"].shape[1:], lambda k, j: (k, 0, 0)),
                  pl.BlockSpec((1, tcs), lambda k, j: (0, j))],
        out_specs=pl.BlockSpec((2, 1, r, tcs), lambda k, j: (0, k, 0, j)),
        out_shape=jax.ShapeDtypeStruct((2, r, r, c), F32),
        compiler_params=_params(2), name="filter_spectrum",
    )(a, tabs["g_filter"], scale)


def _dft2_kernel(a_ref, g_ref, gi_ref, h_ref, o_ref):
    hr, hi = h_ref[0, 0], h_ref[1, 0]
    for p in range(a_ref.shape[0]):
        a = jnp.concatenate([a_ref[p, 0, 0], a_ref[p, 1, 0]], axis=0)
        x = _mm3(g_ref[0], a)
        r = x.shape[0] // 2
        xr, xi = x[:r], x[r:]
        y = jnp.concatenate([xr * hr - xi * hi, xr * hi + xi * hr], axis=0)
        bb = _mm3(gi_ref[0], y)
        o_ref[p, 0, 0] = bb[:r]
        o_ref[p, 1, 0] = bb[r:]


def _dft3_kernel(b_ref, w_ref, u_ref, x_ref, bias_ref, o_ref):
    bias = bias_ref[...]
    for i in range(N2_TILE):
        bb = jnp.concatenate([b_ref[0, 0, :, i, :], b_ref[0, 1, :, i, :]], axis=0)
        y = _mm3(w_ref[...], bb)
        half = y.shape[0] // 2
        for h in range(2):
            o_ref[h, :, i, :] = x_ref[h, :, i, :] * (y[h * half:(h + 1) * half] + u_ref[h, :, i, :] * bias)


def _long_conv_gate(u, x, hspec, order, bias, tabs):
    b, l, c = u.shape
    r = DFT_R
    p = b // 2
    nat = (b, r // 2, r, c)
    a = _dft_major(u.reshape(nat), tabs["w1_data"], 2)
    spec5 = pl.BlockSpec((p, 2, 1, r, c), lambda k: (0, 0, k, 0, 0))
    bb = pl.pallas_call(
        _dft2_kernel,
        grid=(r,),
        in_specs=[spec5,
                  pl.BlockSpec((1,) + tabs["g"].shape[1:], lambda k: (k, 0, 0)),
                  pl.BlockSpec((1,) + tabs["gi"].shape[1:], lambda k: (k, 0, 0)),
                  pl.BlockSpec((2, 1, r, c), lambda k: (0, k, 0, order))],
        out_specs=spec5,
        out_shape=jax.ShapeDtypeStruct((p, 2, r, r, c), F32),
        compiler_params=_params(1), name="dft_minor_filter",
    )(a, tabs["g"], tabs["gi"], hspec)
    blk = pl.BlockSpec((2, r // 2, N2_TILE, c), lambda p_, j: (p_, 0, j, 0))
    out = pl.pallas_call(
        _dft3_kernel,
        grid=(p, r // N2_TILE),
        in_specs=[pl.BlockSpec((1, 2, r, N2_TILE, c), lambda p_, j: (p_, 0, 0, j, 0)),
                  _const_spec(tabs["w2"].shape), blk, blk, _const_spec((1, c))],
        out_specs=blk,
        out_shape=jax.ShapeDtypeStruct(nat, F32),
        compiler_params=_params(2), name="idft_major_gate",
    )(bb, tabs["w2"], u.reshape(nat), x.reshape(nat), bias.reshape(1, c))
    return out.reshape(b, l, c)


def _dense_tables(l):
    n = 2 * l
    k = jnp.arange(n, dtype=jnp.int32)
    t = jnp.arange(l, dtype=jnp.int32)
    c, s = _angles(k[:, None] * t[None, :], n)
    wf = jnp.concatenate([jnp.concatenate([c, s], 1), jnp.concatenate([-s, c], 1)], 0)
    ct, st = c.T, s.T
    wi = jnp.concatenate([jnp.concatenate([ct, -st], 1), jnp.concatenate([st, ct], 1)], 0)
    ca, sa = _angles(k[:, None] * k[None, :], n)
    wk = jnp.concatenate([ca, -sa], 0)
    dp = DFT_DATA_PASSES
    return dict(wf=_split3(wf, dp), wi=_split3(wi, dp), wk=_split3(wk))


def _dense_spectrum_kernel(k_ref, w_ref, sc_ref, o_ref):
    o_ref[...] = _mm3(w_ref[...], k_ref[...]) * sc_ref[...]


def _dense_spectrum(kt, scale, wk):
    n, c = kt.shape
    return pl.pallas_call(
        _dense_spectrum_kernel,
        grid=(1,),
        in_specs=[_const_spec((n, c)), _const_spec(wk.shape), _const_spec((1, c))],
        out_specs=pl.BlockSpec((2 * n, c), lambda i: (0, 0)),
        out_shape=jax.ShapeDtypeStruct((2 * n, c), F32),
        compiler_params=_params(1), name="dense_filter_spectrum",
    )(kt, wk, scale)


def _dense_conv_kernel(u_ref, x_ref, wf_ref, wi_ref, h_ref, bias_ref, o_ref):
    u0, u1 = u_ref[0], u_ref[1]
    z = _mm3(wf_ref[...], jnp.concatenate([u0, u1], axis=0))
    n = z.shape[0] // 2
    zr, zi = z[:n], z[n:]
    hr, hi = h_ref[:n], h_ref[n:]
    y = _mm3(wi_ref[...], jnp.concatenate([zr * hr - zi * hi, zr * hi + zi * hr], axis=0))
    l = y.shape[0] // 2
    bias = bias_ref[...]
    o_ref[0] = x_ref[0] * (y[:l] + u0 * bias)
    o_ref[1] = x_ref[1] * (y[l:] + u1 * bias)


def _dense_conv_gate(u, x, hspec, order, bias, tabs):
    b, l, c = u.shape
    blk = pl.BlockSpec((2, l, c), lambda p_: (p_, 0, 0))
    return pl.pallas_call(
        _dense_conv_kernel,
        grid=(b // 2,),
        in_specs=[blk, blk, _const_spec(tabs["wf"].shape), _const_spec(tabs["wi"].shape),
                  pl.BlockSpec((4 * l, c), lambda p_: (0, order)), _const_spec((1, c))],
        out_specs=blk,
        out_shape=jax.ShapeDtypeStruct((b, l, c), F32),
        compiler_params=_params(1), name="dense_conv_gate",
    )(u, x, tabs["wf"], tabs["wi"], hspec, bias.reshape(1, c))


def _mix_kernel(h_ref, o_ref, pm_ref, y_ref, g_ref, gate_ref, wa_ref, wp_ref, wh_ref, wo_ref, out_ref):
    d = h_ref.shape[2]
    a = _dot(o_ref[0].astype(BF16), wa_ref[...])
    p = _dot(pm_ref[0].astype(BF16), wp_ref[...])
    y = _dot(y_ref[0].astype(BF16), wh_ref[...])
    merged = g_ref[0, :, 0:d] * a + g_ref[0, :, d:2 * d] * p + g_ref[0, :, 2 * d:3 * d] * y
    out = _dot(merged.astype(BF16), wo_ref[...])
    out_ref[0] = h_ref[0] + gate_ref[0] * out


def _mix(h, o, pm, y, g, mod, w_attn_o, w_pool_o, w_hy_o, w_out, tm=512):
    b, s, d = h.shape
    tm = min(tm, s)

    def tok(w):
        return pl.BlockSpec((1, tm, w), lambda b_, i: (b_, i, 0))

    ws = [w.astype(BF16) for w in (w_attn_o, w_pool_o, w_hy_o, w_out)]
    return pl.pallas_call(
        _mix_kernel,
        grid=(b, s // tm),
        in_specs=[tok(d), tok(o.shape[2]), tok(pm.shape[2]), tok(y.shape[2]), tok(g.shape[2]),
                  _mod_spec(mod, d, 2)] + [_const_spec(w.shape) for w in ws],
        out_specs=tok(d),
        out_shape=jax.ShapeDtypeStruct((b, s, d), F32),
        compiler_params=_params(2), name="merge_outproj",
    )(h, o, pm, y, g, mod, *ws)


def _ffn_kernel(h_ref, sh_ref, sc_ref, gate_ref, g_ref, wi_ref, wo_ref, fg_ref, out_ref, *,
                hidden, chunk, final):
    x = h_ref[0]
    a = _norm_mod(x, g_ref[...], sh_ref[0], sc_ref[0]).astype(BF16)
    acc = jnp.zeros(x.shape, F32)
    for c0 in range(0, hidden, chunk):
        gt = _dot(a, wi_ref[:, c0:c0 + chunk])
        up = _dot(a, wi_ref[:, hidden + c0:hidden + c0 + chunk])
        act = (gt * jax.nn.sigmoid(gt) * up).astype(BF16)
        acc = acc + _dot(act, wo_ref[c0:c0 + chunk, :])
    y = x + gate_ref[0] * acc
    if final:
        ms = jnp.mean(y * y, axis=-1, keepdims=True)
        y = y * lax.rsqrt(ms + EPS) * fg_ref[...]
    out_ref[0] = y


def _ffn(h, mod, norm_g, w_in, w_out, final_g, final, tm=512):
    b, s, d = h.shape
    hidden = w_out.shape[0]
    tm = min(tm, s)
    tok = pl.BlockSpec((1, tm, d), lambda b_, i: (b_, i, 0))
    return pl.pallas_call(
        functools.partial(_ffn_kernel, hidden=hidden, chunk=hidden // 2, final=final),
        grid=(b, s // tm),
        in_specs=[tok, _mod_spec(mod, d, 3), _mod_spec(mod, d, 4), _mod_spec(mod, d, 5),
                  _const_spec((1, d)), _const_spec(w_in.shape), _const_spec(w_out.shape),
                  _const_spec((1, d))],
        out_specs=tok,
        out_shape=jax.ShapeDtypeStruct((b, s, d), F32),
        compiler_params=_params(2), name="swiglu_ffn",
    )(h, mod, mod, mod, norm_g.reshape(1, d), w_in.astype(BF16), w_out.astype(BF16),
      final_g.reshape(1, d))


def _rope_tables(seq, width):
    t = jnp.arange(seq, dtype=jnp.int32)
    row = (t // GRID_W).astype(F32)
    col = (t % GRID_W).astype(F32)
    inv = 1.0 / (ROPE_BASE ** (jnp.arange(ROPE_FREQS, dtype=F32) * 2.0 / (2 * ROPE_FREQS)))
    ar, ac = row[:, None] * inv, col[:, None] * inv
    cos = jnp.concatenate([jnp.cos(ar)] * 2 + [jnp.cos(ac)] * 2, axis=1)
    sin = jnp.concatenate([-jnp.sin(ar), jnp.sin(ar), -jnp.sin(ac), jnp.sin(ac)], axis=1)
    reps = width // cos.shape[1]
    return jnp.tile(cos, (1, reps)), jnp.tile(sin, (1, reps))


def kernel(x, c, ctx, c_ctx, w_mod, b_mod, norm1_g, norm2_g, w_in, lam_qk, subln_g, w_attn_o, w_pool, pool_scale, w_pool_o, w_short, b_short, hf_w1, hf_b1, hf_freq1, hf_w2, hf_b2, hf_freq2, hf_w3, hy_bias, w_hy_o, w_out, w_ffn_in, w_ffn_out, final_g):
    batch, seq, d = x.shape
    depth = w_mod.shape[0]
    ctx_len = ctx.shape[1]
    aw = w_attn_o.shape[1]
    pw = w_pool_o.shape[1]
    hw = w_hy_o.shape[1]
    q0, k0, v0, p0 = 0, aw, 2 * aw, 3 * aw
    h0 = p0 + pw
    g0 = h0 + 3 * hw
    assert seq * 2 == DFT_R * DFT_R and batch % 2 == 0

    c8 = jnp.zeros((8, d), F32).at[:batch].set(c).at[batch].set(c_ctx)
    mod_all = _modulation(c8, w_mod, b_mod)
    rope_tabs = _rope_tables(seq, ATTN_VD)
    tabs = _dft_tables()
    ctx_tabs = _dense_tables(ctx_len)

    h_lat, h_ctx = x, ctx
    for l in range(depth):
        last = l == depth - 1
        lam_init = LAMBDA_INIT_BASE - LAMBDA_INIT_AMP * math.exp(-LAMBDA_INIT_RATE * l)
        mod = mod_all[l, :batch].reshape(batch, 1, N_MOD * d)
        mod_c = mod_all[l, batch:batch + 1].reshape(1, 1, N_MOD * d)
        wl = w_in[l]
        fparams = (hf_w1[l], hf_b1[l], hf_freq1[l], hf_w2[l], hf_b2[l], hf_freq2[l], hf_w3[l])

        wl_bf = wl.astype(BF16)
        segs = [("rope", aw, Q_SCALE), ("rope", aw, 1.0), ("plain", aw, 1.0), ("plain", pw, 1.0),
                ("plain", 3 * hw, 1.0), ("sigmoid", 3 * d, 1.0)]
        q, k, v, up, uh, g = _inproj(h_lat, mod, norm1_g[l], wl_bf, segs,
                                     [BF16, BF16, BF16, F32, F32, F32], rope_tabs)

        if last:
            segs_c = [("plain", aw, 1.0), ("plain", aw, 1.0)]
            k_c, v_c = _inproj(h_ctx, mod_c, norm1_g[l], wl_bf[:, k0:p0], segs_c, [BF16, BF16])
        else:
            segs_c = [("plain", aw, Q_SCALE), ("plain", aw, 1.0), ("plain", aw, 1.0), ("plain", pw, 1.0),
                      ("plain", 3 * hw, 1.0), ("sigmoid", 3 * d, 1.0)]
            q_c, k_c, v_c, up_c, uh_c, g_c = _inproj(h_ctx, mod_c, norm1_g[l], wl_bf, segs_c,
                                                    [BF16, BF16, BF16, F32, F32, F32])
            o_c = _attention(q_c, k_c, v_c, lam_qk[l], subln_g[l], lam_init)
            pm_c, vv_c, x1_c, x2_c = _local_mixers(up_c, uh_c, w_pool[l], pool_scale[l], w_short[l], b_short[l])
            kt_c, inv_norm_c = _hyena_time_kernel(ctx_len, *fparams)
            hspec_c = _dense_spectrum(kt_c, inv_norm_c / (2 * ctx_len), ctx_tabs["wk"])
            z_c = _dense_conv_gate(vv_c, x1_c, hspec_c, 0, hy_bias[l, 0], ctx_tabs)
            y_c = _dense_conv_gate(z_c, x2_c, hspec_c, 1, hy_bias[l, 1], ctx_tabs)
            h_ctx_mid = _mix(h_ctx, o_c, pm_c, y_c, g_c, mod_c, w_attn_o[l], w_pool_o[l], w_hy_o[l], w_out[l])
            h_ctx_new = _ffn(h_ctx_mid, mod_c, norm2_g[l], w_ffn_in[l], w_ffn_out[l], final_g, False)

        vt = jnp.swapaxes(jnp.concatenate([v_c, v], axis=1), 1, 2)
        o = _attention_t(q, jnp.concatenate([k_c, k], axis=1), vt, lam_qk[l], subln_g[l], lam_init)
        pm, vv, x1, x2 = _local_mixers(up, uh, w_pool[l], pool_scale[l], w_short[l], b_short[l])
        kt, inv_norm = _hyena_time_kernel(seq, *fparams)
        hspec = _filter_spectrum(kt, inv_norm / (2 * seq), tabs)
        z = _long_conv_gate(vv, x1, hspec, 0, hy_bias[l, 0], tabs)
        y = _long_conv_gate(z, x2, hspec, 1, hy_bias[l, 1], tabs)
        h_mid = _mix(h_lat, o, pm, y, g, mod, w_attn_o[l], w_pool_o[l], w_hy_o[l], w_out[l])
        h_lat = _ffn(h_mid, mod, norm2_g[l], w_ffn_in[l], w_ffn_out[l], final_g, last)
        if not last:
            h_ctx = h_ctx_new
    return h_lat
```

```python
import functools
import math

import jax
import jax.numpy as jnp
from jax import lax
from jax.experimental import pallas as pl
from jax.experimental.pallas import tpu as pltpu

F32 = jnp.float32
BF16 = jnp.bfloat16
HIGHEST = lax.Precision.HIGHEST

GRID_W = 64
N_MOD = 6
ATTN_HD = 64
ATTN_VD = 128
Q_SCALE = ATTN_HD ** -0.5 * math.log2(math.e)
ROPE_BASE = 10000.0
ROPE_FREQS = 16
LAMBDA_INIT_BASE = 0.8
LAMBDA_INIT_AMP = 0.6
LAMBDA_INIT_RATE = 0.3
POOL_WINDOWS = (2, 4, 8, 16)
POOL_GC = 128
HALO = 8
HYENA_BANDS = 16
HYENA_FAST_DECAY = 0.3
HYENA_SLOW_DECAY = 1.5
HYENA_TARGET = 1e-2
EPS = 1e-6
DFT_DATA_PASSES = 1
DFT_FILTER_PASSES = 1
DFT_R = 128
HALF_ROWS = 72

VMEM_LIMIT = 56 * 1024 * 1024


def _params(n_grid, flags=None):
    return pltpu.CompilerParams(dimension_semantics=("arbitrary",) * n_grid,
                                vmem_limit_bytes=VMEM_LIMIT, flags=flags)


def _const_spec(shape):
    zeros = (0,) * len(shape)
    return pl.BlockSpec(shape, lambda *_: zeros, pipeline_mode=pl.Buffered(1))


def _dot(a, b):
    return jnp.dot(a, b, preferred_element_type=F32)


def _split3(w, passes=3):
    hi = w.astype(BF16)
    if passes == 1:
        return hi
    lo = (w - hi.astype(F32)).astype(BF16)
    return jnp.concatenate([hi, hi, lo], axis=-1)


def _mm3(w3, x):
    xh = x.astype(BF16)
    if w3.shape[-1] == x.shape[0]:
        return _dot(w3, xh)
    xl = (x - xh.astype(F32)).astype(BF16)
    return _dot(w3, jnp.concatenate([xh, xl, xh], axis=0))


def _mod_kernel(c_ref, w_ref, b_ref, o_ref):
    c = c_ref[...]
    act = c * jax.nn.sigmoid(c)
    o_ref[0] = jnp.dot(act, w_ref[0], precision=HIGHEST, preferred_element_type=F32) + b_ref[0]


def _modulation(c8, w_mod, b_mod):
    depth, d, n = w_mod.shape
    tn = 1536
    return pl.pallas_call(
        _mod_kernel,
        grid=(depth, n // tn),
        in_specs=[pl.BlockSpec((8, d), lambda l, j: (0, 0)),
                  pl.BlockSpec((1, d, tn), lambda l, j: (l, 0, j)),
                  pl.BlockSpec((1, 1, tn), lambda l, j: (l, 0, j))],
        out_specs=pl.BlockSpec((1, 8, tn), lambda l, j: (l, 0, j)),
        out_shape=jax.ShapeDtypeStruct((depth, 8, n), F32),
        compiler_params=_params(2),
        name="modulation",
    )(c8, w_mod, b_mod.reshape(depth, 1, n))


def _mod_spec(mod, d, col):
    if mod.shape[0] == 1:
        return pl.BlockSpec((1, 1, d), lambda b, i: (0, 0, col))
    return pl.BlockSpec((1, 1, d), lambda b, i: (b, 0, col))


def _norm_mod(x, g, shift, scale):
    ms = jnp.mean(x * x, axis=-1, keepdims=True)
    a = x * lax.rsqrt(ms + EPS) * g
    return a * (1.0 + scale) + shift


def _rope(p, cos, sin):
    lanes = cos.shape[1]
    lane = lax.broadcasted_iota(jnp.int32, cos.shape, 1)
    is_a = lane % (2 * ROPE_FREQS) < ROPE_FREQS
    out = []
    for g in range(p.shape[1] // lanes):
        x = p[:, g * lanes:(g + 1) * lanes]
        partner = jnp.where(is_a, pltpu.roll(x, lanes - ROPE_FREQS, 1), pltpu.roll(x, ROPE_FREQS, 1))
        out.append(x * cos + partner * sin)
    return jnp.concatenate(out, axis=1)


def _inproj_kernel(*refs, segs, rope):
    h_ref, sh_ref, sc_ref, g_ref, w_ref = refs[:5]
    n_in = 7 if rope else 5
    out_refs = refs[n_in:]
    a = _norm_mod(h_ref[0], g_ref[...], sh_ref[0], sc_ref[0]).astype(BF16)
    col = 0
    for (kind, width, scale), o_ref in zip(segs, out_refs):
        p = _dot(a, w_ref[:, col:col + width])
        col += width
        if kind == "rope":
            p = _rope(p, refs[5][...], refs[6][...])
        elif kind == "sigmoid":
            p = jax.nn.sigmoid(p)
        if scale != 1.0:
            p = p * scale
        o_ref[0] = p.astype(o_ref.dtype)


def _inproj(h, mod, norm_g, w_cat, segs, out_dtypes, rope_tabs=None, tm=256):
    b, s, d = h.shape
    tm = min(tm, s)
    rope = rope_tabs is not None
    in_specs = [pl.BlockSpec((1, tm, d), lambda b_, i: (b_, i, 0)),
                _mod_spec(mod, d, 0), _mod_spec(mod, d, 1),
                _const_spec((1, d)), _const_spec(w_cat.shape)]
    args = [h, mod, mod, norm_g.reshape(1, d), w_cat]
    if rope:
        wr = rope_tabs[0].shape[1]
        in_specs += [pl.BlockSpec((tm, wr), lambda b_, i: (i, 0))] * 2
        args += list(rope_tabs)
    out_specs = [pl.BlockSpec((1, tm, w), lambda b_, i: (b_, i, 0)) for (_, w, _) in segs]
    out_shape = [jax.ShapeDtypeStruct((b, s, w), dt) for (_, w, _), dt in zip(segs, out_dtypes)]
    return pl.pallas_call(
        functools.partial(_inproj_kernel, segs=tuple(segs), rope=rope),
        grid=(b, s // tm), in_specs=in_specs, out_specs=out_specs, out_shape=out_shape,
        compiler_params=_params(2), name="inproj",
    )(*args)


def _lambda(lq_ref, lam_init):
    lq = lq_ref[...]
    return (jnp.exp(jnp.sum(lq[0:1] * lq[1:2], axis=-1, keepdims=True))
            - jnp.exp(jnp.sum(lq[2:3] * lq[3:4], axis=-1, keepdims=True)) + lam_init)


def _split_halves(q):
    lane = lax.broadcasted_iota(jnp.int32, q.shape, 1)
    zero = jnp.zeros_like(q)
    return jnp.concatenate([jnp.where(lane < ATTN_HD, q, zero),
                            jnp.where(lane >= ATTN_HD, q, zero)], axis=0)


def _qk(qq, k):
    return lax.dot_general(qq, k, (((1,), (1,)), ((), ())), preferred_element_type=F32)


def _sub_ln(o, g, lam_init):
    ms = jnp.mean(o * o, axis=-1, keepdims=True)
    return o * lax.rsqrt(ms + EPS) * g * (1.0 - lam_init)


def _attn_kernel(lq_ref, q_ref, k_ref, v_ref, g_ref, o_ref, *, lam_init, tq):
    lam = _lambda(lq_ref, lam_init)
    s = _qk(_split_halves(q_ref[0]), k_ref[0])
    m = jnp.max(s, axis=-1, keepdims=True)
    p = jnp.exp2(s - m)
    l = jnp.sum(p, axis=-1, keepdims=True)
    a = (p[:tq] - p[tq:] * (lam * l[:tq] / l[tq:])).astype(BF16)
    o = _dot(a, v_ref[0]) / l[:tq]
    o_ref[0] = _sub_ln(o, g_ref[...], lam_init).astype(o_ref.dtype)


def _attn_t_kernel(lq_ref, q_ref, k_ref, vt_ref, g_ref, o_ref, s0, s1, m_scr, *, lam_init, tq, lk, kc, nq):
    j = pl.program_id(2)
    bufs = (s0, s1)

    @pl.when(j == 0)
    def _():
        s1[...] = jnp.zeros_like(s1)
        m_scr[1] = jnp.zeros(m_scr.shape[1:], F32)

    lam = _lambda(lq_ref, lam_init)
    for r in range(2):
        sig = 2 * j + r
        buf_a, buf_b = bufs[r], bufs[1 - r]
        t_a = jnp.minimum(sig, nq - 1)
        t_b = jnp.clip(sig - 1, 0, nq - 1)
        qq = _split_halves(q_ref[0, pl.ds(pl.multiple_of(t_a * tq, tq), tq), :])
        m_b = m_scr[1 - r]
        m8 = l8 = acc = None
        for c0 in range(0, lk, kc):
            rows = slice(c0, c0 + kc)
            s = _qk(k_ref[0, rows, :], qq)
            buf_a[rows, :] = s
            for i in range(0, kc, 8):
                m8 = s[i:i + 8] if m8 is None else jnp.maximum(m8, s[i:i + 8])
            p = jnp.exp2(buf_b[rows, :].reshape(kc // 8, 8, 2 * tq) - m_b)
            lc = jnp.sum(p, axis=0)
            l8 = lc if l8 is None else l8 + lc
            pv = _dot(vt_ref[0, :, rows], p.reshape(kc, 2 * tq).astype(BF16))
            acc = pv if acc is None else acc + pv
        m_scr[r] = jnp.broadcast_to(jnp.max(m8, axis=0, keepdims=True), m8.shape)
        l = jnp.sum(l8, axis=0, keepdims=True)
        ot = acc[:, :tq] / l[:, :tq] - acc[:, tq:] * (lam / l[:, tq:])
        ms = jnp.mean(ot * ot, axis=0, keepdims=True)
        ot = ot * lax.rsqrt(ms + EPS) * g_ref[...] * (1.0 - lam_init)
        o_ref[0, pl.ds(pl.multiple_of(t_b * tq, tq), tq), :] = ot.T.astype(o_ref.dtype)


def _attention_t(q, k, vt, lam_qk, subln_g, lam_init, tq=128, kc=768):
    b, s, w = q.shape
    lk = k.shape[1]
    heads = w // ATTN_VD
    nq = s // tq
    nsteps = -(-(nq + 1) // 2)
    return pl.pallas_call(
        functools.partial(_attn_t_kernel, lam_init=lam_init, tq=tq, lk=lk, kc=kc, nq=nq),
        grid=(b, heads, nsteps),
        in_specs=[pl.BlockSpec(lam_qk.shape, lambda b_, h, i: (0, 0)),
                  pl.BlockSpec((1, s, ATTN_VD), lambda b_, h, i: (b_, 0, h)),
                  pl.BlockSpec((1, lk, ATTN_VD), lambda b_, h, i: (b_, 0, h)),
                  pl.BlockSpec((1, ATTN_VD, lk), lambda b_, h, i: (b_, h, 0)),
                  pl.BlockSpec((ATTN_VD, 1), lambda b_, h, i: (0, 0))],
        out_specs=pl.BlockSpec((1, s, ATTN_VD), lambda b_, h, i: (b_, 0, h)),
        out_shape=jax.ShapeDtypeStruct((b, s, w), F32),
        scratch_shapes=[pltpu.VMEM((lk, 2 * tq), F32)] * 2 + [pltpu.VMEM((2, 8, 2 * tq), F32)],
        compiler_params=_params(3), name="diff_attention_t",
    )(lam_qk, q, k, vt, subln_g.reshape(ATTN_VD, 1))


def _attention(q, k, v, lam_qk, subln_g, lam_init, tq=128):
    b, s, w = q.shape
    lk = k.shape[1]
    heads = w // ATTN_VD
    tq = min(tq, s)
    return pl.pallas_call(
        functools.partial(_attn_kernel, lam_init=lam_init, tq=tq),
        grid=(b, heads, s // tq),
        in_specs=[pl.BlockSpec(lam_qk.shape, lambda b_, h, i: (0, 0)),
                  pl.BlockSpec((1, tq, ATTN_VD), lambda b_, h, i: (b_, i, h)),
                  pl.BlockSpec((1, lk, ATTN_VD), lambda b_, h, i: (b_, 0, h)),
                  pl.BlockSpec((1, lk, ATTN_VD), lambda b_, h, i: (b_, 0, h)),
                  pl.BlockSpec((1, ATTN_VD), lambda b_, h, i: (0, 0))],
        out_specs=pl.BlockSpec((1, tq, ATTN_VD), lambda b_, h, i: (b_, i, h)),
        out_shape=jax.ShapeDtypeStruct((b, s, w), F32),
        compiler_params=_params(3), name="diff_attention",
    )(lam_qk, q, k, v, subln_g.reshape(1, ATTN_VD))


def _local_kernel(up_ref, upp_ref, upn_ref, uh_ref, uhp_ref, uhn_ref, wp_ref, ps_ref, ws_ref,
                  bs_ref, pm_ref, v_ref, x1_ref, x2_ref, pbuf, hbuf, *, ts, seq):
    i = pl.program_id(1)
    first = i == 0
    last = i == pl.num_programs(1) - 1

    def fill(buf, cur, prev, nxt):
        buf[0:HALO] = jnp.where(first, 0.0, prev[0])
        buf[HALO:HALO + ts] = cur[0]
        buf[HALO + ts:HALO + ts + HALO] = jnp.where(last, 0.0, nxt[0])

    fill(pbuf, up_ref, upp_ref, upn_ref)
    fill(hbuf, uh_ref, uhp_ref, uhn_ref)

    t = i * ts + lax.broadcasted_iota(jnp.int32, (ts, 1), 0)
    for g, w in enumerate(POOL_WINDOWS):
        cs = slice(g * POOL_GC, (g + 1) * POOL_GC)
        acc = pbuf[HALO - w // 2:HALO - w // 2 + ts, cs]
        for j in range(1 - w // 2, w // 2):
            acc = acc + pbuf[HALO + j:HALO + j + ts, cs]
        cnt = (jnp.minimum(t + w // 2, seq) - jnp.maximum(t - w // 2, 0)).astype(F32)
        pooled = acc / cnt - pbuf[HALO:HALO + ts, cs]
        mixed = _dot(pooled.astype(BF16), wp_ref[g])
        pm_ref[0, :, cs] = mixed * ps_ref[:, cs]

    ws = ws_ref[...]
    conv = (hbuf[HALO - 1:HALO - 1 + ts] * ws[0:1] + hbuf[HALO:HALO + ts] * ws[1:2]
            + hbuf[HALO + 1:HALO + 1 + ts] * ws[2:3] + bs_ref[...])
    c = conv.shape[1] // 3
    v_ref[0] = conv[:, :c]
    x1_ref[0] = conv[:, c:2 * c]
    x2_ref[0] = conv[:, 2 * c:]


def _local_mixers(up, uh, w_pool, pool_scale, w_short, b_short, ts=512):
    b, s, cp = up.shape
    ch = uh.shape[2]
    c = ch // 3
    ts = min(ts, s)
    nh = ts // HALO
    last_halo = s // HALO - 1

    def cur(w):
        return pl.BlockSpec((1, ts, w), lambda b_, i: (b_, i, 0))

    def prev(w):
        return pl.BlockSpec((1, HALO, w), lambda b_, i: (b_, jnp.maximum(i * nh - 1, 0), 0))

    def nxt(w):
        return pl.BlockSpec((1, HALO, w), lambda b_, i: (b_, jnp.minimum((i + 1) * nh, last_halo), 0))

    return pl.pallas_call(
        functools.partial(_local_kernel, ts=ts, seq=s),
        grid=(b, s // ts),
        in_specs=[cur(cp), prev(cp), nxt(cp), cur(ch), prev(ch), nxt(ch),
                  _const_spec(w_pool.shape), _const_spec((1, cp)),
                  _const_spec(w_short.shape), _const_spec((1, ch))],
        out_specs=[cur(cp), cur(c), cur(c), cur(c)],
        out_shape=[jax.ShapeDtypeStruct((b, s, cp), F32)] + [jax.ShapeDtypeStruct((b, s, c), F32)] * 3,
        scratch_shapes=[pltpu.VMEM((ts + 2 * HALO, cp), F32), pltpu.VMEM((ts + 2 * HALO, ch), F32)],
        compiler_params=_params(2), name="pool_shortconv",
    )(up, up, up, uh, uh, uh, w_pool.astype(BF16), pool_scale.reshape(1, cp), w_short,
      b_short.reshape(1, ch))


def _filter_kernel(z_ref, w1_ref, b1_ref, f1_ref, w2_ref, b2_ref, f2_ref, w3_ref, dl_ref, h_ref, asum_ref, *,
                   tl, l):
    i = pl.program_id(0)
    n = i * tl + lax.broadcasted_iota(jnp.int32, (tl, 1), 0)
    z = z_ref[...]
    t01 = z[:, 0:1]
    pre = jnp.dot(z, w1_ref[...], precision=HIGHEST, preferred_element_type=F32)
    h1 = jnp.sin(f1_ref[...] * (pre + b1_ref[...]))
    h2 = jnp.sin(f2_ref[...] * (jnp.dot(h1, w2_ref[...], precision=HIGHEST,
                                        preferred_element_type=F32) + b2_ref[...]))
    h3 = jnp.dot(h2, w3_ref[0], precision=HIGHEST, preferred_element_type=F32)
    h = jnp.where(n == l, 0.0, h3 * jnp.exp(-t01 * dl_ref[...]))
    h_ref[...] = h
    part = jnp.sum(jnp.abs(h), axis=0, keepdims=True)

    @pl.when(i == 0)
    def _():
        asum_ref[...] = jnp.zeros_like(asum_ref)

    asum_ref[...] += part


def _hyena_time_kernel(l, w1, b1, f1, w2, b2, f2, w3):
    ffn = w1.shape[1]
    nc = w3.shape[1] // 2
    c = nc // 2
    tl = min(l, 512)
    w3d = jnp.swapaxes(w3.reshape(ffn, 2, nc), 0, 1)
    n = jnp.arange(2 * l, dtype=jnp.int32)
    t = jnp.where(n < l, n, 2 * l - n).astype(F32)
    bands = jnp.linspace(1e-4, HYENA_BANDS - 1, HYENA_BANDS, dtype=F32)
    ang = 2.0 * math.pi * t[:, None] * bands[None, :] / l
    emb = 2 * HYENA_BANDS + 1
    z = jnp.concatenate([(t / (l - 1))[:, None], jnp.cos(ang), jnp.sin(ang),
                         jnp.zeros((2 * l, 128 - emb), F32)], axis=1)
    w1p = jnp.zeros((128, ffn), F32).at[:emb].set(w1)
    max_decay = math.log(HYENA_TARGET) / HYENA_FAST_DECAY
    min_decay = math.log(HYENA_TARGET) / HYENA_SLOW_DECAY
    deltas = jnp.abs(jnp.linspace(min_decay, max_decay, c, dtype=F32))
    dl = jnp.tile(deltas, 2).reshape(1, nc)
    row = lambda a: a.reshape(1, -1)
    nfwd = l // tl
    kt, asum = pl.pallas_call(
        functools.partial(_filter_kernel, tl=tl, l=l),
        grid=(2 * nfwd,),
        in_specs=[pl.BlockSpec((tl, 128), lambda i: (i, 0)), _const_spec((128, ffn)),
                  _const_spec((1, ffn)), _const_spec((1, ffn)),
                  _const_spec((ffn, ffn)), _const_spec((1, ffn)), _const_spec((1, ffn)),
                  pl.BlockSpec((1, ffn, nc), lambda i: (i // nfwd, 0, 0)), _const_spec((1, nc))],
        out_specs=[pl.BlockSpec((tl, nc), lambda i: (i, 0)), pl.BlockSpec((1, nc), lambda i: (0, 0))],
        out_shape=[jax.ShapeDtypeStruct((2 * l, nc), F32), jax.ShapeDtypeStruct((1, nc), F32)],
        compiler_params=_params(1), name="hyena_filter_mlp",
    )(z, w1p, row(b1), row(f1), w2, row(b2), row(f2), w3d, dl)
    return kt, 1.0 / asum


def _angles(prod, n):
    th = (2.0 * math.pi / n) * (prod % n).astype(F32)
    return jnp.cos(th), jnp.sin(th)


def _dft_tables():
    r = DFT_R
    n = r * r
    idx = jnp.arange(r, dtype=jnp.int32)
    c, s = _angles(idx[:, None] * idx[None, :], r)
    ch, sh = c[:, :r // 2], s[:, :r // 2]
    w1_data = jnp.concatenate([jnp.concatenate([ch, sh], 1), jnp.concatenate([-sh, ch], 1)], 0)
    pad = jnp.zeros((HALF_ROWS - (r // 2 + 1), r), F32)
    w1_real = jnp.concatenate([c[:r // 2 + 1], pad, -s[:r // 2 + 1], pad], 0)
    ct, st = c[:r // 2], s[:r // 2]
    w2 = jnp.concatenate([jnp.concatenate([ct, -st], 1), jnp.concatenate([st, ct], 1)], 0)
    k = idx[:, None, None] + r * idx[None, :, None]
    cg, sg = _angles(k * idx[None, None, :], n)
    g = jnp.concatenate([jnp.concatenate([cg, sg], 2), jnp.concatenate([-sg, cg], 2)], 1)
    cgt, sgt = jnp.swapaxes(cg, 1, 2), jnp.swapaxes(sg, 1, 2)
    gi = jnp.concatenate([jnp.concatenate([cgt, -sgt], 2), jnp.concatenate([sgt, cgt], 2)], 1)
    dp = DFT_DATA_PASSES
    fp = DFT_FILTER_PASSES
    return dict(w1_data=_split3(w1_data, dp), w1_real=_split3(w1_real, fp), w2=_split3(w2, dp),
                g=_split3(g, dp), gi=_split3(gi, dp), g_filter=_split3(g, fp))


N2_TILE = 8
K1_TILE = 2


def _dft1_kernel(x_ref, w_ref, o_ref, *, parts):
    r = o_ref.shape[2]
    for i in range(N2_TILE):
        x = jnp.concatenate([x_ref[p, :, i, :] for p in range(parts)], axis=0)
        y = _mm3(w_ref[...], x)
        o_ref[0, 0, :, i, :] = y[:r]
        o_ref[0, 1, :, i, :] = y[r:]


def _dft_major(x, w3, parts):
    b, n1, r, c = x.shape
    k1 = w3.shape[0] // 2
    return pl.pallas_call(
        functools.partial(_dft1_kernel, parts=parts),
        grid=(b // parts, r // N2_TILE),
        in_specs=[pl.BlockSpec((parts, n1, N2_TILE, c), lambda p_, j: (p_, 0, j, 0)), _const_spec(w3.shape)],
        out_specs=pl.BlockSpec((1, 2, k1, N2_TILE, c), lambda p_, j: (p_, 0, 0, j, 0)),
        out_shape=jax.ShapeDtypeStruct((b // parts, 2, k1, r, c), F32),
        compiler_params=_params(2), name="dft_major",
    )(x, w3)


def _spectrum_kernel(a_ref, g_ref, sc_ref, o_ref, *, nyquist):
    sign = jnp.where(pl.program_id(0) <= nyquist, 1.0, -1.0)
    a = jnp.concatenate([a_ref[0, 0, 0], a_ref[0, 1, 0] * sign], axis=0)
    x = _mm3(g_ref[0], a) * sc_ref[...]
    r = x.shape[0] // 2
    o_ref[0, 0] = x[:r]
    o_ref[1, 0] = x[r:]


def _filter_spectrum(kt, scale, tabs):
    r = DFT_R
    c = kt.shape[1]
    a = _dft_major(kt.reshape(1, r, r, c), tabs["w1_real"], 1)
    tcs = c
    return pl.pallas_call(
        functools.partial(_spectrum_kernel, nyquist=r // 2),
        grid=(r, c // tcs),
        in_specs=[pl.BlockSpec((1, 2, 1, r, tcs), lambda k, j: (0, 0, jnp.where(k <= r // 2, k, r - k), 0, j)),
                  pl.BlockSpec((1,) + tabs["g_filter"].shape[1:], lambda k, j: (k, 0, 0)),
                  pl.BlockSpec((1, tcs), lambda k, j: (0, j))],
        out_specs=pl.BlockSpec((2, 1, r, tcs), lambda k, j: (0, k, 0, j)),
        out_shape=jax.ShapeDtypeStruct((2, r, r, c), F32),
        compiler_params=_params(2), name="filter_spectrum",
    )(a, tabs["g_filter"], scale)


def _dft2_kernel(a_ref, g_ref, gi_ref, h_ref, o_ref):
    for kk in range(K1_TILE):
        hr, hi = h_ref[0, kk], h_ref[1, kk]
        for p in range(a_ref.shape[0]):
            a = jnp.concatenate([a_ref[p, 0, kk], a_ref[p, 1, kk]], axis=0)
            x = _mm3(g_ref[kk], a)
            r = x.shape[0] // 2
            xr, xi = x[:r], x[r:]
            y = jnp.concatenate([xr * hr - xi * hi, xr * hi + xi * hr], axis=0)
            bb = _mm3(gi_ref[kk], y)
            o_ref[p, 0, kk] = bb[:r]
            o_ref[p, 1, kk] = bb[r:]


def _dft3_kernel(b_ref, w_ref, u_ref, x_ref, bias_ref, o_ref):
    bias = bias_ref[...]
    for i in range(N2_TILE):
        bb = jnp.concatenate([b_ref[0, 0, :, i, :], b_ref[0, 1, :, i, :]], axis=0)
        y = _mm3(w_ref[...], bb)
        half = y.shape[0] // 2
        for h in range(2):
            o_ref[h, :, i, :] = x_ref[h, :, i, :] * (y[h * half:(h + 1) * half] + u_ref[h, :, i, :] * bias)


def _long_conv_gate(u, x, hspec, order, bias, tabs):
    b, l, c = u.shape
    r = DFT_R
    p = b // 2
    nat = (b, r // 2, r, c)
    a = _dft_major(u.reshape(nat), tabs["w1_data"], 2)
    spec5 = pl.BlockSpec((p, 2, K1_TILE, r, c), lambda k: (0, 0, k, 0, 0))
    bb = pl.pallas_call(
        _dft2_kernel,
        grid=(r // K1_TILE,),
        in_specs=[spec5,
                  pl.BlockSpec((K1_TILE,) + tabs["g"].shape[1:], lambda k: (k, 0, 0)),
                  pl.BlockSpec((K1_TILE,) + tabs["gi"].shape[1:], lambda k: (k, 0, 0)),
                  pl.BlockSpec((2, K1_TILE, r, c), lambda k: (0, k, 0, order))],
        out_specs=spec5,
        out_shape=jax.ShapeDtypeStruct((p, 2, r, r, c), F32),
        compiler_params=_params(1), name="dft_minor_filter",
    )(a, tabs["g"], tabs["gi"], hspec)
    blk = pl.BlockSpec((2, r // 2, N2_TILE, c), lambda p_, j: (p_, 0, j, 0))
    out = pl.pallas_call(
        _dft3_kernel,
        grid=(p, r // N2_TILE),
        in_specs=[pl.BlockSpec((1, 2, r, N2_TILE, c), lambda p_, j: (p_, 0, 0, j, 0)),
                  _const_spec(tabs["w2"].shape), blk, blk, _const_spec((1, c))],
        out_specs=blk,
        out_shape=jax.ShapeDtypeStruct(nat, F32),
        compiler_params=_params(2), name="idft_major_gate",
    )(bb, tabs["w2"], u.reshape(nat), x.reshape(nat), bias.reshape(1, c))
    return out.reshape(b, l, c)


def _dense_tables(l):
    n = 2 * l
    k = jnp.arange(n, dtype=jnp.int32)
    t = jnp.arange(l, dtype=jnp.int32)
    c, s = _angles(k[:, None] * t[None, :], n)
    wf = jnp.concatenate([jnp.concatenate([c, s], 1), jnp.concatenate([-s, c], 1)], 0)
    ct, st = c.T, s.T
    wi = jnp.concatenate([jnp.concatenate([ct, -st], 1), jnp.concatenate([st, ct], 1)], 0)
    ca, sa = _angles(k[:, None] * k[None, :], n)
    wk = jnp.concatenate([ca, -sa], 0)
    dp = DFT_DATA_PASSES
    return dict(wf=_split3(wf, dp), wi=_split3(wi, dp), wk=_split3(wk, DFT_FILTER_PASSES))


def _dense_spectrum_kernel(k_ref, w_ref, sc_ref, o_ref):
    o_ref[...] = _mm3(w_ref[...], k_ref[...]) * sc_ref[...]


def _dense_spectrum(kt, scale, wk):
    n, c = kt.shape
    return pl.pallas_call(
        _dense_spectrum_kernel,
        grid=(1,),
        in_specs=[_const_spec((n, c)), _const_spec(wk.shape), _const_spec((1, c))],
        out_specs=pl.BlockSpec((2 * n, c), lambda i: (0, 0)),
        out_shape=jax.ShapeDtypeStruct((2 * n, c), F32),
        compiler_params=_params(1), name="dense_filter_spectrum",
    )(kt, wk, scale)


def _dense_conv_kernel(u_ref, x_ref, wf_ref, wi_ref, h_ref, bias_ref, o_ref):
    u0, u1 = u_ref[0], u_ref[1]
    z = _mm3(wf_ref[...], jnp.concatenate([u0, u1], axis=0))
    n = z.shape[0] // 2
    zr, zi = z[:n], z[n:]
    hr, hi = h_ref[:n], h_ref[n:]
    y = _mm3(wi_ref[...], jnp.concatenate([zr * hr - zi * hi, zr * hi + zi * hr], axis=0))
    l = y.shape[0] // 2
    bias = bias_ref[...]
    o_ref[0] = x_ref[0] * (y[:l] + u0 * bias)
    o_ref[1] = x_ref[1] * (y[l:] + u1 * bias)


def _dense_conv_gate(u, x, hspec, order, bias, tabs):
    b, l, c = u.shape
    blk = pl.BlockSpec((2, l, c), lambda p_: (p_, 0, 0))
    return pl.pallas_call(
        _dense_conv_kernel,
        grid=(b // 2,),
        in_specs=[blk, blk, _const_spec(tabs["wf"].shape), _const_spec(tabs["wi"].shape),
                  pl.BlockSpec((4 * l, c), lambda p_: (0, order)), _const_spec((1, c))],
        out_specs=blk,
        out_shape=jax.ShapeDtypeStruct((b, l, c), F32),
        compiler_params=_params(1), name="dense_conv_gate",
    )(u, x, tabs["wf"], tabs["wi"], hspec, bias.reshape(1, c))


def _mix_kernel(h_ref, o_ref, pm_ref, y_ref, g_ref, gate_ref, wa_ref, wp_ref, wh_ref, wo_ref, out_ref):
    d = h_ref.shape[2]
    a = _dot(o_ref[0].astype(BF16), wa_ref[...])
    p = _dot(pm_ref[0].astype(BF16), wp_ref[...])
    y = _dot(y_ref[0].astype(BF16), wh_ref[...])
    merged = g_ref[0, :, 0:d] * a + g_ref[0, :, d:2 * d] * p + g_ref[0, :, 2 * d:3 * d] * y
    out = _dot(merged.astype(BF16), wo_ref[...])
    out_ref[0] = h_ref[0] + gate_ref[0] * out


def _mix(h, o, pm, y, g, mod, w_attn_o, w_pool_o, w_hy_o, w_out, tm=512):
    b, s, d = h.shape
    tm = min(tm, s)

    def tok(w):
        return pl.BlockSpec((1, tm, w), lambda b_, i: (b_, i, 0))

    ws = [w.astype(BF16) for w in (w_attn_o, w_pool_o, w_hy_o, w_out)]
    return pl.pallas_call(
        _mix_kernel,
        grid=(b, s // tm),
        in_specs=[tok(d), tok(o.shape[2]), tok(pm.shape[2]), tok(y.shape[2]), tok(g.shape[2]),
                  _mod_spec(mod, d, 2)] + [_const_spec(w.shape) for w in ws],
        out_specs=tok(d),
        out_shape=jax.ShapeDtypeStruct((b, s, d), F32),
        compiler_params=_params(2), name="merge_outproj",
    )(h, o, pm, y, g, mod, *ws)


def _ffn_kernel(h_ref, sh_ref, sc_ref, gate_ref, g_ref, wi_ref, wo_ref, fg_ref, out_ref, *,
                hidden, chunk, final):
    x = h_ref[0]
    a = _norm_mod(x, g_ref[...], sh_ref[0], sc_ref[0]).astype(BF16)
    acc = jnp.zeros(x.shape, F32)
    for c0 in range(0, hidden, chunk):
        gt = _dot(a, wi_ref[:, c0:c0 + chunk])
        up = _dot(a, wi_ref[:, hidden + c0:hidden + c0 + chunk])
        act = (gt * jax.nn.sigmoid(gt) * up).astype(BF16)
        acc = acc + _dot(act, wo_ref[c0:c0 + chunk, :])
    y = x + gate_ref[0] * acc
    if final:
        ms = jnp.mean(y * y, axis=-1, keepdims=True)
        y = y * lax.rsqrt(ms + EPS) * fg_ref[...]
    out_ref[0] = y


def _ffn(h, mod, norm_g, w_in, w_out, final_g, final, tm=512):
    b, s, d = h.shape
    hidden = w_out.shape[0]
    tm = min(tm, s)
    tok = pl.BlockSpec((1, tm, d), lambda b_, i: (b_, i, 0))
    return pl.pallas_call(
        functools.partial(_ffn_kernel, hidden=hidden, chunk=hidden // 2, final=final),
        grid=(b, s // tm),
        in_specs=[tok, _mod_spec(mod, d, 3), _mod_spec(mod, d, 4), _mod_spec(mod, d, 5),
                  _const_spec((1, d)), _const_spec(w_in.shape), _const_spec(w_out.shape),
                  _const_spec((1, d))],
        out_specs=tok,
        out_shape=jax.ShapeDtypeStruct((b, s, d), F32),
        compiler_params=_params(2), name="swiglu_ffn",
    )(h, mod, mod, mod, norm_g.reshape(1, d), w_in.astype(BF16), w_out.astype(BF16),
      final_g.reshape(1, d))


def _rope_tables(seq, width):
    t = jnp.arange(seq, dtype=jnp.int32)
    row = (t // GRID_W).astype(F32)
    col = (t % GRID_W).astype(F32)
    inv = 1.0 / (ROPE_BASE ** (jnp.arange(ROPE_FREQS, dtype=F32) * 2.0 / (2 * ROPE_FREQS)))
    ar, ac = row[:, None] * inv, col[:, None] * inv
    cos = jnp.concatenate([jnp.cos(ar)] * 2 + [jnp.cos(ac)] * 2, axis=1)
    sin = jnp.concatenate([-jnp.sin(ar), jnp.sin(ar), -jnp.sin(ac), jnp.sin(ac)], axis=1)
    reps = width // cos.shape[1]
    return jnp.tile(cos, (1, reps)), jnp.tile(sin, (1, reps))


def kernel(x, c, ctx, c_ctx, w_mod, b_mod, norm1_g, norm2_g, w_in, lam_qk, subln_g, w_attn_o, w_pool, pool_scale, w_pool_o, w_short, b_short, hf_w1, hf_b1, hf_freq1, hf_w2, hf_b2, hf_freq2, hf_w3, hy_bias, w_hy_o, w_out, w_ffn_in, w_ffn_out, final_g):
    batch, seq, d = x.shape
    depth = w_mod.shape[0]
    ctx_len = ctx.shape[1]
    aw = w_attn_o.shape[1]
    pw = w_pool_o.shape[1]
    hw = w_hy_o.shape[1]
    q0, k0, v0, p0 = 0, aw, 2 * aw, 3 * aw
    h0 = p0 + pw
    g0 = h0 + 3 * hw
    assert seq * 2 == DFT_R * DFT_R and batch % 2 == 0

    c8 = jnp.zeros((8, d), F32).at[:batch].set(c).at[batch].set(c_ctx)
    mod_all = _modulation(c8, w_mod, b_mod)
    rope_tabs = _rope_tables(seq, ATTN_VD)
    tabs = _dft_tables()
    ctx_tabs = _dense_tables(ctx_len)

    h_lat, h_ctx = x, ctx
    for l in range(depth):
        last = l == depth - 1
        lam_init = LAMBDA_INIT_BASE - LAMBDA_INIT_AMP * math.exp(-LAMBDA_INIT_RATE * l)
        mod = mod_all[l, :batch].reshape(batch, 1, N_MOD * d)
        mod_c = mod_all[l, batch:batch + 1].reshape(1, 1, N_MOD * d)
        wl = w_in[l]
        fparams = (hf_w1[l], hf_b1[l], hf_freq1[l], hf_w2[l], hf_b2[l], hf_freq2[l], hf_w3[l])

        wl_bf = wl.astype(BF16)
        segs = [("rope", aw, Q_SCALE), ("rope", aw, 1.0), ("plain", aw, 1.0), ("plain", pw, 1.0),
                ("plain", 3 * hw, 1.0), ("sigmoid", 3 * d, 1.0)]
        q, k, v, up, uh, g = _inproj(h_lat, mod, norm1_g[l], wl_bf, segs,
                                     [BF16, BF16, BF16, F32, F32, F32], rope_tabs)

        if last:
            segs_c = [("plain", aw, 1.0), ("plain", aw, 1.0)]
            k_c, v_c = _inproj(h_ctx, mod_c, norm1_g[l], wl_bf[:, k0:p0], segs_c, [BF16, BF16])
        else:
            segs_c = [("plain", aw, Q_SCALE), ("plain", aw, 1.0), ("plain", aw, 1.0), ("plain", pw, 1.0),
                      ("plain", 3 * hw, 1.0), ("sigmoid", 3 * d, 1.0)]
            q_c, k_c, v_c, up_c, uh_c, g_c = _inproj(h_ctx, mod_c, norm1_g[l], wl_bf, segs_c,
                                                    [BF16, BF16, BF16, F32, F32, F32])
            o_c = _attention(q_c, k_c, v_c, lam_qk[l], subln_g[l], lam_init)
            pm_c, vv_c, x1_c, x2_c = _local_mixers(up_c, uh_c, w_pool[l], pool_scale[l], w_short[l], b_short[l])
            kt_c, inv_norm_c = _hyena_time_kernel(ctx_len, *fparams)
            hspec_c = _dense_spectrum(kt_c, inv_norm_c / (2 * ctx_len), ctx_tabs["wk"])
            z_c = _dense_conv_gate(vv_c, x1_c, hspec_c, 0, hy_bias[l, 0], ctx_tabs)
            y_c = _dense_conv_gate(z_c, x2_c, hspec_c, 1, hy_bias[l, 1], ctx_tabs)
            h_ctx_mid = _mix(h_ctx, o_c, pm_c, y_c, g_c, mod_c, w_attn_o[l], w_pool_o[l], w_hy_o[l], w_out[l])
            h_ctx_new = _ffn(h_ctx_mid, mod_c, norm2_g[l], w_ffn_in[l], w_ffn_out[l], final_g, False)

        vt = jnp.swapaxes(jnp.concatenate([v_c, v], axis=1), 1, 2)
        o = _attention_t(q, jnp.concatenate([k_c, k], axis=1), vt, lam_qk[l], subln_g[l], lam_init)
        pm, vv, x1, x2 = _local_mixers(up, uh, w_pool[l], pool_scale[l], w_short[l], b_short[l])
        kt, inv_norm = _hyena_time_kernel(seq, *fparams)
        hspec = _filter_spectrum(kt, inv_norm / (2 * seq), tabs)
        z = _long_conv_gate(vv, x1, hspec, 0, hy_bias[l, 0], tabs)
        y = _long_conv_gate(z, x2, hspec, 1, hy_bias[l, 1], tabs)
        h_mid = _mix(h_lat, o, pm, y, g, mod, w_attn_o[l], w_pool_o[l], w_hy_o[l], w_out[l])
        h_lat = _ffn(h_mid, mod, norm2_g[l], w_ffn_in[l], w_ffn_out[l], final_g, last)
        if not last:
            h_ctx = h_ctx_new
    return h_lat
```

```python
import functools
import math

import jax
import jax.numpy as jnp
from jax import lax
from jax.experimental import pallas as pl
from jax.experimental.pallas import tpu as pltpu

F32 = jnp.float32
BF16 = jnp.bfloat16
HIGHEST = lax.Precision.HIGHEST

GRID_W = 64
N_MOD = 6
ATTN_HD = 64
ATTN_VD = 128
Q_SCALE = ATTN_HD ** -0.5 * math.log2(math.e)
ROPE_BASE = 10000.0
ROPE_FREQS = 16
LAMBDA_INIT_BASE = 0.8
LAMBDA_INIT_AMP = 0.6
LAMBDA_INIT_RATE = 0.3
POOL_WINDOWS = (2, 4, 8, 16)
POOL_GC = 128
HALO = 8
HYENA_BANDS = 16
HYENA_FAST_DECAY = 0.3
HYENA_SLOW_DECAY = 1.5
HYENA_TARGET = 1e-2
EPS = 1e-6
DFT_DATA_PASSES = 1
DFT_FILTER_PASSES = 1
DFT_R = 128
HALF_ROWS = 72

VMEM_LIMIT = 56 * 1024 * 1024


def _params(n_grid, flags=None):
    return pltpu.CompilerParams(dimension_semantics=("arbitrary",) * n_grid,
                                vmem_limit_bytes=VMEM_LIMIT, flags=flags)


def _const_spec(shape):
    zeros = (0,) * len(shape)
    return pl.BlockSpec(shape, lambda *_: zeros, pipeline_mode=pl.Buffered(1))


def _dot(a, b):
    return jnp.dot(a, b, preferred_element_type=F32)


def _split3(w, passes=3):
    hi = w.astype(BF16)
    if passes == 1:
        return hi
    lo = (w - hi.astype(F32)).astype(BF16)
    return jnp.concatenate([hi, hi, lo], axis=-1)


def _mm3(w3, x):
    xh = x.astype(BF16)
    if w3.shape[-1] == x.shape[0]:
        return _dot(w3, xh)
    xl = (x - xh.astype(F32)).astype(BF16)
    return _dot(w3, jnp.concatenate([xh, xl, xh], axis=0))


def _mod_kernel(c_ref, w_ref, b_ref, o_ref):
    c = c_ref[...]
    act = c * jax.nn.sigmoid(c)
    o_ref[0] = jnp.dot(act, w_ref[0], precision=HIGHEST, preferred_element_type=F32) + b_ref[0]


def _modulation(c8, w_mod, b_mod):
    depth, d, n = w_mod.shape
    tn = 1536
    return pl.pallas_call(
        _mod_kernel,
        grid=(depth, n // tn),
        in_specs=[pl.BlockSpec((8, d), lambda l, j: (0, 0)),
                  pl.BlockSpec((1, d, tn), lambda l, j: (l, 0, j)),
                  pl.BlockSpec((1, 1, tn), lambda l, j: (l, 0, j))],
        out_specs=pl.BlockSpec((1, 8, tn), lambda l, j: (l, 0, j)),
        out_shape=jax.ShapeDtypeStruct((depth, 8, n), F32),
        compiler_params=_params(2),
        name="modulation",
    )(c8, w_mod, b_mod.reshape(depth, 1, n))


def _mod_spec(mod, d, col):
    if mod.shape[0] == 1:
        return pl.BlockSpec((1, 1, d), lambda b, i: (0, 0, col))
    return pl.BlockSpec((1, 1, d), lambda b, i: (b, 0, col))


def _norm_mod(x, g, shift, scale):
    ms = jnp.mean(x * x, axis=-1, keepdims=True)
    a = x * lax.rsqrt(ms + EPS) * g
    return a * (1.0 + scale) + shift


def _rope(p, cos, sin):
    lanes = cos.shape[1]
    lane = lax.broadcasted_iota(jnp.int32, cos.shape, 1)
    is_a = lane % (2 * ROPE_FREQS) < ROPE_FREQS
    out = []
    for g in range(p.shape[1] // lanes):
        x = p[:, g * lanes:(g + 1) * lanes]
        partner = jnp.where(is_a, pltpu.roll(x, lanes - ROPE_FREQS, 1), pltpu.roll(x, ROPE_FREQS, 1))
        out.append(x * cos + partner * sin)
    return jnp.concatenate(out, axis=1)


def _inproj_kernel(*refs, segs, rope):
    h_ref, sh_ref, sc_ref, g_ref, w_ref = refs[:5]
    n_in = 7 if rope else 5
    out_refs = refs[n_in:]
    a = _norm_mod(h_ref[0], g_ref[...], sh_ref[0], sc_ref[0]).astype(BF16)
    col = 0
    for (kind, width, scale), o_ref in zip(segs, out_refs):
        p = _dot(a, w_ref[:, col:col + width])
        col += width
        if kind == "rope":
            p = _rope(p, refs[5][...], refs[6][...])
        elif kind == "sigmoid":
            p = jax.nn.sigmoid(p)
        if scale != 1.0:
            p = p * scale
        o_ref[0] = p.astype(o_ref.dtype)


def _inproj(h, mod, norm_g, w_cat, segs, out_dtypes, rope_tabs=None, tm=256):
    b, s, d = h.shape
    tm = min(tm, s)
    rope = rope_tabs is not None
    in_specs = [pl.BlockSpec((1, tm, d), lambda b_, i: (b_, i, 0)),
                _mod_spec(mod, d, 0), _mod_spec(mod, d, 1),
                _const_spec((1, d)), _const_spec(w_cat.shape)]
    args = [h, mod, mod, norm_g.reshape(1, d), w_cat]
    if rope:
        wr = rope_tabs[0].shape[1]
        in_specs += [pl.BlockSpec((tm, wr), lambda b_, i: (i, 0))] * 2
        args += list(rope_tabs)
    out_specs = [pl.BlockSpec((1, tm, w), lambda b_, i: (b_, i, 0)) for (_, w, _) in segs]
    out_shape = [jax.ShapeDtypeStruct((b, s, w), dt) for (_, w, _), dt in zip(segs, out_dtypes)]
    return pl.pallas_call(
        functools.partial(_inproj_kernel, segs=tuple(segs), rope=rope),
        grid=(b, s // tm), in_specs=in_specs, out_specs=out_specs, out_shape=out_shape,
        compiler_params=_params(2), name="inproj",
    )(*args)


def _lambda(lq_ref, lam_init):
    lq = lq_ref[...]
    return (jnp.exp(jnp.sum(lq[0:1] * lq[1:2], axis=-1, keepdims=True))
            - jnp.exp(jnp.sum(lq[2:3] * lq[3:4], axis=-1, keepdims=True)) + lam_init)


def _split_halves(q):
    lane = lax.broadcasted_iota(jnp.int32, q.shape, 1)
    zero = jnp.zeros_like(q)
    return jnp.concatenate([jnp.where(lane < ATTN_HD, q, zero),
                            jnp.where(lane >= ATTN_HD, q, zero)], axis=0)


def _qk(qq, k):
    return lax.dot_general(qq, k, (((1,), (1,)), ((), ())), preferred_element_type=F32)


def _sub_ln(o, g, lam_init):
    ms = jnp.mean(o * o, axis=-1, keepdims=True)
    return o * lax.rsqrt(ms + EPS) * g * (1.0 - lam_init)


def _attn_kernel(lq_ref, q_ref, k_ref, v_ref, g_ref, o_ref, *, lam_init, tq):
    lam = _lambda(lq_ref, lam_init)
    s = _qk(_split_halves(q_ref[0]), k_ref[0])
    m = jnp.max(s, axis=-1, keepdims=True)
    p = jnp.exp2(s - m)
    l = jnp.sum(p, axis=-1, keepdims=True)
    a = (p[:tq] - p[tq:] * (lam * l[:tq] / l[tq:])).astype(BF16)
    o = _dot(a, v_ref[0]) / l[:tq]
    o_ref[0] = _sub_ln(o, g_ref[...], lam_init).astype(o_ref.dtype)


def _attn_t_kernel(lq_ref, q_ref, k_ref, vt_ref, g_ref, o_ref, s0, s1, m_scr, *, lam_init, tq, lk, kc, nq):
    j = pl.program_id(2)
    bufs = (s0, s1)

    @pl.when(j == 0)
    def _():
        s1[...] = jnp.zeros_like(s1)
        m_scr[1] = jnp.zeros(m_scr.shape[1:], F32)

    lam = _lambda(lq_ref, lam_init)
    for r in range(2):
        sig = 2 * j + r
        buf_a, buf_b = bufs[r], bufs[1 - r]
        t_a = jnp.minimum(sig, nq - 1)
        t_b = jnp.clip(sig - 1, 0, nq - 1)
        qq = _split_halves(q_ref[0, pl.ds(pl.multiple_of(t_a * tq, tq), tq), :])
        m_b = m_scr[1 - r]
        m8 = l8 = acc = None
        for c0 in range(0, lk, kc):
            rows = slice(c0, c0 + kc)
            s = _qk(k_ref[0, rows, :], qq)
            buf_a[rows, :] = s
            for i in range(0, kc, 8):
                m8 = s[i:i + 8] if m8 is None else jnp.maximum(m8, s[i:i + 8])
            p = jnp.exp2(buf_b[rows, :].reshape(kc // 8, 8, 2 * tq) - m_b)
            lc = jnp.sum(p, axis=0)
            l8 = lc if l8 is None else l8 + lc
            pv = _dot(vt_ref[0, :, rows], p.reshape(kc, 2 * tq).astype(BF16))
            acc = pv if acc is None else acc + pv
        m_scr[r] = jnp.broadcast_to(jnp.max(m8, axis=0, keepdims=True), m8.shape)
        l = jnp.sum(l8, axis=0, keepdims=True)
        ot = acc[:, :tq] / l[:, :tq] - acc[:, tq:] * (lam / l[:, tq:])
        ms = jnp.mean(ot * ot, axis=0, keepdims=True)
        ot = ot * lax.rsqrt(ms + EPS) * g_ref[...] * (1.0 - lam_init)
        o_ref[0, pl.ds(pl.multiple_of(t_b * tq, tq), tq), :] = ot.T.astype(o_ref.dtype)


def _attention_t(q, k, vt, lam_qk, subln_g, lam_init, tq=128, kc=768):
    b, s, w = q.shape
    lk = k.shape[1]
    heads = w // ATTN_VD
    nq = s // tq
    nsteps = -(-(nq + 1) // 2)
    return pl.pallas_call(
        functools.partial(_attn_t_kernel, lam_init=lam_init, tq=tq, lk=lk, kc=kc, nq=nq),
        grid=(b, heads, nsteps),
        in_specs=[pl.BlockSpec(lam_qk.shape, lambda b_, h, i: (0, 0)),
                  pl.BlockSpec((1, s, ATTN_VD), lambda b_, h, i: (b_, 0, h)),
                  pl.BlockSpec((1, lk, ATTN_VD), lambda b_, h, i: (b_, 0, h)),
                  pl.BlockSpec((1, ATTN_VD, lk), lambda b_, h, i: (b_, h, 0)),
                  pl.BlockSpec((ATTN_VD, 1), lambda b_, h, i: (0, 0))],
        out_specs=pl.BlockSpec((1, s, ATTN_VD), lambda b_, h, i: (b_, 0, h)),
        out_shape=jax.ShapeDtypeStruct((b, s, w), F32),
        scratch_shapes=[pltpu.VMEM((lk, 2 * tq), F32)] * 2 + [pltpu.VMEM((2, 8, 2 * tq), F32)],
        compiler_params=_params(3), name="diff_attention_t",
    )(lam_qk, q, k, vt, subln_g.reshape(ATTN_VD, 1))


def _attention(q, k, v, lam_qk, subln_g, lam_init, tq=128):
    b, s, w = q.shape
    lk = k.shape[1]
    heads = w // ATTN_VD
    tq = min(tq, s)
    return pl.pallas_call(
        functools.partial(_attn_kernel, lam_init=lam_init, tq=tq),
        grid=(b, heads, s // tq),
        in_specs=[pl.BlockSpec(lam_qk.shape, lambda b_, h, i: (0, 0)),
                  pl.BlockSpec((1, tq, ATTN_VD), lambda b_, h, i: (b_, i, h)),
                  pl.BlockSpec((1, lk, ATTN_VD), lambda b_, h, i: (b_, 0, h)),
                  pl.BlockSpec((1, lk, ATTN_VD), lambda b_, h, i: (b_, 0, h)),
                  pl.BlockSpec((1, ATTN_VD), lambda b_, h, i: (0, 0))],
        out_specs=pl.BlockSpec((1, tq, ATTN_VD), lambda b_, h, i: (b_, i, h)),
        out_shape=jax.ShapeDtypeStruct((b, s, w), F32),
        compiler_params=_params(3), name="diff_attention",
    )(lam_qk, q, k, v, subln_g.reshape(1, ATTN_VD))


def _local_kernel(up_ref, upp_ref, upn_ref, uh_ref, uhp_ref, uhn_ref, wp_ref, ps_ref, ws_ref,
                  bs_ref, pm_ref, v_ref, x1_ref, x2_ref, pbuf, hbuf, *, ts, seq):
    i = pl.program_id(1)
    first = i == 0
    last = i == pl.num_programs(1) - 1

    def fill(buf, cur, prev, nxt):
        buf[0:HALO] = jnp.where(first, 0.0, prev[0])
        buf[HALO:HALO + ts] = cur[0]
        buf[HALO + ts:HALO + ts + HALO] = jnp.where(last, 0.0, nxt[0])

    fill(pbuf, up_ref, upp_ref, upn_ref)
    fill(hbuf, uh_ref, uhp_ref, uhn_ref)

    t = i * ts + lax.broadcasted_iota(jnp.int32, (ts, 1), 0)
    for g, w in enumerate(POOL_WINDOWS):
        cs = slice(g * POOL_GC, (g + 1) * POOL_GC)
        acc = pbuf[HALO - w // 2:HALO - w // 2 + ts, cs]
        for j in range(1 - w // 2, w // 2):
            acc = acc + pbuf[HALO + j:HALO + j + ts, cs]
        cnt = (jnp.minimum(t + w // 2, seq) - jnp.maximum(t - w // 2, 0)).astype(F32)
        pooled = acc / cnt - pbuf[HALO:HALO + ts, cs]
        mixed = _dot(pooled.astype(BF16), wp_ref[g])
        pm_ref[0, :, cs] = mixed * ps_ref[:, cs]

    ws = ws_ref[...]
    conv = (hbuf[HALO - 1:HALO - 1 + ts] * ws[0:1] + hbuf[HALO:HALO + ts] * ws[1:2]
            + hbuf[HALO + 1:HALO + 1 + ts] * ws[2:3] + bs_ref[...])
    c = conv.shape[1] // 3
    v_ref[0] = conv[:, :c]
    x1_ref[0] = conv[:, c:2 * c]
    x2_ref[0] = conv[:, 2 * c:]


def _local_mixers(up, uh, w_pool, pool_scale, w_short, b_short, ts=512):
    b, s, cp = up.shape
    ch = uh.shape[2]
    c = ch // 3
    ts = min(ts, s)
    nh = ts // HALO
    last_halo = s // HALO - 1

    def cur(w):
        return pl.BlockSpec((1, ts, w), lambda b_, i: (b_, i, 0))

    def prev(w):
        return pl.BlockSpec((1, HALO, w), lambda b_, i: (b_, jnp.maximum(i * nh - 1, 0), 0))

    def nxt(w):
        return pl.BlockSpec((1, HALO, w), lambda b_, i: (b_, jnp.minimum((i + 1) * nh, last_halo), 0))

    return pl.pallas_call(
        functools.partial(_local_kernel, ts=ts, seq=s),
        grid=(b, s // ts),
        in_specs=[cur(cp), prev(cp), nxt(cp), cur(ch), prev(ch), nxt(ch),
                  _const_spec(w_pool.shape), _const_spec((1, cp)),
                  _const_spec(w_short.shape), _const_spec((1, ch))],
        out_specs=[cur(cp), cur(c), cur(c), cur(c)],
        out_shape=[jax.ShapeDtypeStruct((b, s, cp), F32)] + [jax.ShapeDtypeStruct((b, s, c), F32)] * 3,
        scratch_shapes=[pltpu.VMEM((ts + 2 * HALO, cp), F32), pltpu.VMEM((ts + 2 * HALO, ch), F32)],
        compiler_params=_params(2), name="pool_shortconv",
    )(up, up, up, uh, uh, uh, w_pool.astype(BF16), pool_scale.reshape(1, cp), w_short,
      b_short.reshape(1, ch))


def _filter_kernel(z_ref, w1_ref, b1_ref, f1_ref, w2_ref, b2_ref, f2_ref, w3_ref, dl_ref, h_ref, asum_ref, *,
                   tl, l):
    i = pl.program_id(0)
    n = i * tl + lax.broadcasted_iota(jnp.int32, (tl, 1), 0)
    z = z_ref[...]
    t01 = z[:, 0:1]
    pre = jnp.dot(z, w1_ref[...], precision=HIGHEST, preferred_element_type=F32)
    h1 = jnp.sin(f1_ref[...] * (pre + b1_ref[...]))
    h2 = jnp.sin(f2_ref[...] * (jnp.dot(h1, w2_ref[...], precision=HIGHEST,
                                        preferred_element_type=F32) + b2_ref[...]))
    h3 = jnp.dot(h2, w3_ref[0], precision=HIGHEST, preferred_element_type=F32)
    h = jnp.where(n == l, 0.0, h3 * jnp.exp(-t01 * dl_ref[...]))
    h_ref[...] = h
    part = jnp.sum(jnp.abs(h), axis=0, keepdims=True)

    @pl.when(i == 0)
    def _():
        asum_ref[...] = jnp.zeros_like(asum_ref)

    asum_ref[...] += part


def _hyena_time_kernel(l, w1, b1, f1, w2, b2, f2, w3):
    ffn = w1.shape[1]
    nc = w3.shape[1] // 2
    c = nc // 2
    tl = min(l, 512)
    w3d = jnp.swapaxes(w3.reshape(ffn, 2, nc), 0, 1)
    n = jnp.arange(2 * l, dtype=jnp.int32)
    t = jnp.where(n < l, n, 2 * l - n).astype(F32)
    bands = jnp.linspace(1e-4, HYENA_BANDS - 1, HYENA_BANDS, dtype=F32)
    ang = 2.0 * math.pi * t[:, None] * bands[None, :] / l
    emb = 2 * HYENA_BANDS + 1
    z = jnp.concatenate([(t / (l - 1))[:, None], jnp.cos(ang), jnp.sin(ang),
                         jnp.zeros((2 * l, 128 - emb), F32)], axis=1)
    w1p = jnp.zeros((128, ffn), F32).at[:emb].set(w1)
    max_decay = math.log(HYENA_TARGET) / HYENA_FAST_DECAY
    min_decay = math.log(HYENA_TARGET) / HYENA_SLOW_DECAY
    deltas = jnp.abs(jnp.linspace(min_decay, max_decay, c, dtype=F32))
    dl = jnp.tile(deltas, 2).reshape(1, nc)
    row = lambda a: a.reshape(1, -1)
    nfwd = l // tl
    kt, asum = pl.pallas_call(
        functools.partial(_filter_kernel, tl=tl, l=l),
        grid=(2 * nfwd,),
        in_specs=[pl.BlockSpec((tl, 128), lambda i: (i, 0)), _const_spec((128, ffn)),
                  _const_spec((1, ffn)), _const_spec((1, ffn)),
                  _const_spec((ffn, ffn)), _const_spec((1, ffn)), _const_spec((1, ffn)),
                  pl.BlockSpec((1, ffn, nc), lambda i: (i // nfwd, 0, 0)), _const_spec((1, nc))],
        out_specs=[pl.BlockSpec((tl, nc), lambda i: (i, 0)), pl.BlockSpec((1, nc), lambda i: (0, 0))],
        out_shape=[jax.ShapeDtypeStruct((2 * l, nc), F32), jax.ShapeDtypeStruct((1, nc), F32)],
        compiler_params=_params(1), name="hyena_filter_mlp",
    )(z, w1p, row(b1), row(f1), w2, row(b2), row(f2), w3d, dl)
    return kt, 1.0 / asum


def _angles(prod, n):
    th = (2.0 * math.pi / n) * (prod % n).astype(F32)
    return jnp.cos(th), jnp.sin(th)


def _dft_tables():
    r = DFT_R
    n = r * r
    idx = jnp.arange(r, dtype=jnp.int32)
    c, s = _angles(idx[:, None] * idx[None, :], r)
    ch, sh = c[:, :r // 2], s[:, :r // 2]
    w1_data = jnp.concatenate([jnp.concatenate([ch, sh], 1), jnp.concatenate([-sh, ch], 1)], 0)
    pad = jnp.zeros((HALF_ROWS - (r // 2 + 1), r), F32)
    w1_real = jnp.concatenate([c[:r // 2 + 1], pad, -s[:r // 2 + 1], pad], 0)
    ct, st = c[:r // 2], s[:r // 2]
    w2 = jnp.concatenate([jnp.concatenate([ct, -st], 1), jnp.concatenate([st, ct], 1)], 0)
    k = idx[:, None, None] + r * idx[None, :, None]
    cg, sg = _angles(k * idx[None, None, :], n)
    g = jnp.concatenate([jnp.concatenate([cg, sg], 2), jnp.concatenate([-sg, cg], 2)], 1)
    dp = DFT_DATA_PASSES
    fp = DFT_FILTER_PASSES
    return dict(w1_data=_split3(w1_data, dp), w1_real=_split3(w1_real, fp), w2=_split3(w2, dp),
                g=_split3(g, dp), g_filter=_split3(g, fp))


N2_TILE = 16
K1_TILE = 4


def _dft1_kernel(x_ref, w_ref, o_ref, *, parts):
    r = o_ref.shape[2]
    for i in range(N2_TILE):
        x = jnp.concatenate([x_ref[p, :, i, :] for p in range(parts)], axis=0)
        y = _mm3(w_ref[...], x)
        o_ref[0, 0, :, i, :] = y[:r]
        o_ref[0, 1, :, i, :] = y[r:]


def _dft_major(x, w3, parts):
    b, n1, r, c = x.shape
    k1 = w3.shape[0] // 2
    return pl.pallas_call(
        functools.partial(_dft1_kernel, parts=parts),
        grid=(b // parts, r // N2_TILE),
        in_specs=[pl.BlockSpec((parts, n1, N2_TILE, c), lambda p_, j: (p_, 0, j, 0)), _const_spec(w3.shape)],
        out_specs=pl.BlockSpec((1, 2, k1, N2_TILE, c), lambda p_, j: (p_, 0, 0, j, 0)),
        out_shape=jax.ShapeDtypeStruct((b // parts, 2, k1, r, c), F32),
        compiler_params=_params(2), name="dft_major",
    )(x, w3)


def _spectrum_kernel(a_ref, g_ref, sc_ref, o_ref, *, nyquist):
    sign = jnp.where(pl.program_id(0) <= nyquist, 1.0, -1.0)
    a = jnp.concatenate([a_ref[0, 0, 0], a_ref[0, 1, 0] * sign], axis=0)
    x = _mm3(g_ref[0], a) * sc_ref[...]
    r = x.shape[0] // 2
    o_ref[0, 0] = x[:r]
    o_ref[1, 0] = x[r:]


def _filter_spectrum(kt, scale, tabs):
    r = DFT_R
    c = kt.shape[1]
    a = _dft_major(kt.reshape(1, r, r, c), tabs["w1_real"], 1)
    tcs = c
    return pl.pallas_call(
        functools.partial(_spectrum_kernel, nyquist=r // 2),
        grid=(r, c // tcs),
        in_specs=[pl.BlockSpec((1, 2, 1, r, tcs), lambda k, j: (0, 0, jnp.where(k <= r // 2, k, r - k), 0, j)),
                  pl.BlockSpec((1,) + tabs["g_filter"].shape[1:], lambda k, j: (k, 0, 0)),
                  pl.BlockSpec((1, tcs), lambda k, j: (0, j))],
        out_specs=pl.BlockSpec((2, 1, r, tcs), lambda k, j: (0, k, 0, j)),
        out_shape=jax.ShapeDtypeStruct((2, r, r, c), F32),
        compiler_params=_params(2), name="filter_spectrum",
    )(a, tabs["g_filter"], scale)


def _mm3t(w3, x):
    assert w3.shape[0] == x.shape[0], "transposed product is only used in single-pass mode"
    return lax.dot_general(w3, x.astype(BF16), (((0,), (0,)), ((), ())), preferred_element_type=F32)


def _dft2_kernel(a_ref, g_ref, h_ref, o_ref):
    for kk in range(K1_TILE):
        hr, hi = h_ref[0, kk], h_ref[1, kk]
        for p in range(a_ref.shape[0]):
            a = jnp.concatenate([a_ref[p, 0, kk], a_ref[p, 1, kk]], axis=0)
            x = _mm3(g_ref[kk], a)
            r = x.shape[0] // 2
            xr, xi = x[:r], x[r:]
            y = jnp.concatenate([xr * hr - xi * hi, xr * hi + xi * hr], axis=0)
            bb = _mm3t(g_ref[kk], y)
            o_ref[p, 0, kk] = bb[:r]
            o_ref[p, 1, kk] = bb[r:]


def _dft3_kernel(b_ref, w_ref, u_ref, x_ref, bias_ref, o_ref):
    bias = bias_ref[...]
    for i in range(N2_TILE):
        bb = jnp.concatenate([b_ref[0, 0, :, i, :], b_ref[0, 1, :, i, :]], axis=0)
        y = _mm3(w_ref[...], bb)
        half = y.shape[0] // 2
        for h in range(2):
            o_ref[h, :, i, :] = x_ref[h, :, i, :] * (y[h * half:(h + 1) * half] + u_ref[h, :, i, :] * bias)


def _long_conv_gate(u, x, hspec, order, bias, tabs):
    b, l, c = u.shape
    r = DFT_R
    p = b // 2
    nat = (b, r // 2, r, c)
    a = _dft_major(u.reshape(nat), tabs["w1_data"], 2)
    spec5 = pl.BlockSpec((p, 2, K1_TILE, r, c), lambda k: (0, 0, k, 0, 0))
    bb = pl.pallas_call(
        _dft2_kernel,
        grid=(r // K1_TILE,),
        in_specs=[spec5,
                  pl.BlockSpec((K1_TILE,) + tabs["g"].shape[1:], lambda k: (k, 0, 0)),
                  pl.BlockSpec((2, K1_TILE, r, c), lambda k: (0, k, 0, order))],
        out_specs=spec5,
        out_shape=jax.ShapeDtypeStruct((p, 2, r, r, c), F32),
        compiler_params=_params(1), name="dft_minor_filter",
    )(a, tabs["g"], hspec)
    blk = pl.BlockSpec((2, r // 2, N2_TILE, c), lambda p_, j: (p_, 0, j, 0))
    out = pl.pallas_call(
        _dft3_kernel,
        grid=(p, r // N2_TILE),
        in_specs=[pl.BlockSpec((1, 2, r, N2_TILE, c), lambda p_, j: (p_, 0, 0, j, 0)),
                  _const_spec(tabs["w2"].shape), blk, blk, _const_spec((1, c))],
        out_specs=blk,
        out_shape=jax.ShapeDtypeStruct(nat, F32),
        compiler_params=_params(2), name="idft_major_gate",
    )(bb, tabs["w2"], u.reshape(nat), x.reshape(nat), bias.reshape(1, c))
    return out.reshape(b, l, c)


def _dense_tables(l):
    n = 2 * l
    k = jnp.arange(n, dtype=jnp.int32)
    t = jnp.arange(l, dtype=jnp.int32)
    c, s = _angles(k[:, None] * t[None, :], n)
    wf = jnp.concatenate([jnp.concatenate([c, s], 1), jnp.concatenate([-s, c], 1)], 0)
    ct, st = c.T, s.T
    wi = jnp.concatenate([jnp.concatenate([ct, -st], 1), jnp.concatenate([st, ct], 1)], 0)
    ca, sa = _angles(k[:, None] * k[None, :], n)
    wk = jnp.concatenate([ca, -sa], 0)
    dp = DFT_DATA_PASSES
    return dict(wf=_split3(wf, dp), wi=_split3(wi, dp), wk=_split3(wk, DFT_FILTER_PASSES))


def _dense_spectrum_kernel(k_ref, w_ref, sc_ref, o_ref):
    o_ref[...] = _mm3(w_ref[...], k_ref[...]) * sc_ref[...]


def _dense_spectrum(kt, scale, wk):
    n, c = kt.shape
    return pl.pallas_call(
        _dense_spectrum_kernel,
        grid=(1,),
        in_specs=[_const_spec((n, c)), _const_spec(wk.shape), _const_spec((1, c))],
        out_specs=pl.BlockSpec((2 * n, c), lambda i: (0, 0)),
        out_shape=jax.ShapeDtypeStruct((2 * n, c), F32),
        compiler_params=_params(1), name="dense_filter_spectrum",
    )(kt, wk, scale)


def _dense_conv_kernel(u_ref, x_ref, wf_ref, wi_ref, h_ref, bias_ref, o_ref):
    u0, u1 = u_ref[0], u_ref[1]
    z = _mm3(wf_ref[...], jnp.concatenate([u0, u1], axis=0))
    n = z.shape[0] // 2
    zr, zi = z[:n], z[n:]
    hr, hi = h_ref[:n], h_ref[n:]
    y = _mm3(wi_ref[...], jnp.concatenate([zr * hr - zi * hi, zr * hi + zi * hr], axis=0))
    l = y.shape[0] // 2
    bias = bias_ref[...]
    o_ref[0] = x_ref[0] * (y[:l] + u0 * bias)
    o_ref[1] = x_ref[1] * (y[l:] + u1 * bias)


def _dense_conv_gate(u, x, hspec, order, bias, tabs):
    b, l, c = u.shape
    blk = pl.BlockSpec((2, l, c), lambda p_: (p_, 0, 0))
    return pl.pallas_call(
        _dense_conv_kernel,
        grid=(b // 2,),
        in_specs=[blk, blk, _const_spec(tabs["wf"].shape), _const_spec(tabs["wi"].shape),
                  pl.BlockSpec((4 * l, c), lambda p_: (0, order)), _const_spec((1, c))],
        out_specs=blk,
        out_shape=jax.ShapeDtypeStruct((b, l, c), F32),
        compiler_params=_params(1), name="dense_conv_gate",
    )(u, x, tabs["wf"], tabs["wi"], hspec, bias.reshape(1, c))


def _mix_kernel(h_ref, o_ref, pm_ref, y_ref, g_ref, gate_ref, wa_ref, wp_ref, wh_ref, wo_ref, out_ref):
    d = h_ref.shape[2]
    a = _dot(o_ref[0].astype(BF16), wa_ref[...])
    p = _dot(pm_ref[0].astype(BF16), wp_ref[...])
    y = _dot(y_ref[0].astype(BF16), wh_ref[...])
    merged = g_ref[0, :, 0:d] * a + g_ref[0, :, d:2 * d] * p + g_ref[0, :, 2 * d:3 * d] * y
    out = _dot(merged.astype(BF16), wo_ref[...])
    out_ref[0] = h_ref[0] + gate_ref[0] * out


def _mix(h, o, pm, y, g, mod, w_attn_o, w_pool_o, w_hy_o, w_out, tm=512):
    b, s, d = h.shape
    tm = min(tm, s)

    def tok(w):
        return pl.BlockSpec((1, tm, w), lambda b_, i: (b_, i, 0))

    ws = [w.astype(BF16) for w in (w_attn_o, w_pool_o, w_hy_o, w_out)]
    return pl.pallas_call(
        _mix_kernel,
        grid=(b, s // tm),
        in_specs=[tok(d), tok(o.shape[2]), tok(pm.shape[2]), tok(y.shape[2]), tok(g.shape[2]),
                  _mod_spec(mod, d, 2)] + [_const_spec(w.shape) for w in ws],
        out_specs=tok(d),
        out_shape=jax.ShapeDtypeStruct((b, s, d), F32),
        compiler_params=_params(2), name="merge_outproj",
    )(h, o, pm, y, g, mod, *ws)


def _ffn_kernel(h_ref, sh_ref, sc_ref, gate_ref, g_ref, wi_ref, wo_ref, fg_ref, out_ref, *,
                hidden, chunk, final):
    x = h_ref[0]
    a = _norm_mod(x, g_ref[...], sh_ref[0], sc_ref[0]).astype(BF16)
    acc = jnp.zeros(x.shape, F32)
    for c0 in range(0, hidden, chunk):
        gt = _dot(a, wi_ref[:, c0:c0 + chunk])
        up = _dot(a, wi_ref[:, hidden + c0:hidden + c0 + chunk])
        act = (gt * jax.nn.sigmoid(gt) * up).astype(BF16)
        acc = acc + _dot(act, wo_ref[c0:c0 + chunk, :])
    y = x + gate_ref[0] * acc
    if final:
        ms = jnp.mean(y * y, axis=-1, keepdims=True)
        y = y * lax.rsqrt(ms + EPS) * fg_ref[...]
    out_ref[0] = y


def _ffn(h, mod, norm_g, w_in, w_out, final_g, final, tm=512):
    b, s, d = h.shape
    hidden = w_out.shape[0]
    tm = min(tm, s)
    tok = pl.BlockSpec((1, tm, d), lambda b_, i: (b_, i, 0))
    return pl.pallas_call(
        functools.partial(_ffn_kernel, hidden=hidden, chunk=hidden // 2, final=final),
        grid=(b, s // tm),
        in_specs=[tok, _mod_spec(mod, d, 3), _mod_spec(mod, d, 4), _mod_spec(mod, d, 5),
                  _const_spec((1, d)), _const_spec(w_in.shape), _const_spec(w_out.shape),
                  _const_spec((1, d))],
        out_specs=tok,
        out_shape=jax.ShapeDtypeStruct((b, s, d), F32),
        compiler_params=_params(2), name="swiglu_ffn",
    )(h, mod, mod, mod, norm_g.reshape(1, d), w_in.astype(BF16), w_out.astype(BF16),
      final_g.reshape(1, d))


def _rope_tables(seq, width):
    t = jnp.arange(seq, dtype=jnp.int32)
    row = (t // GRID_W).astype(F32)
    col = (t % GRID_W).astype(F32)
    inv = 1.0 / (ROPE_BASE ** (jnp.arange(ROPE_FREQS, dtype=F32) * 2.0 / (2 * ROPE_FREQS)))
    ar, ac = row[:, None] * inv, col[:, None] * inv
    cos = jnp.concatenate([jnp.cos(ar)] * 2 + [jnp.cos(ac)] * 2, axis=1)
    sin = jnp.concatenate([-jnp.sin(ar), jnp.sin(ar), -jnp.sin(ac), jnp.sin(ac)], axis=1)
    reps = width // cos.shape[1]
    return jnp.tile(cos, (1, reps)), jnp.tile(sin, (1, reps))


def kernel(x, c, ctx, c_ctx, w_mod, b_mod, norm1_g, norm2_g, w_in, lam_qk, subln_g, w_attn_o, w_pool, pool_scale, w_pool_o, w_short, b_short, hf_w1, hf_b1, hf_freq1, hf_w2, hf_b2, hf_freq2, hf_w3, hy_bias, w_hy_o, w_out, w_ffn_in, w_ffn_out, final_g):
    batch, seq, d = x.shape
    depth = w_mod.shape[0]
    ctx_len = ctx.shape[1]
    aw = w_attn_o.shape[1]
    pw = w_pool_o.shape[1]
    hw = w_hy_o.shape[1]
    q0, k0, v0, p0 = 0, aw, 2 * aw, 3 * aw
    h0 = p0 + pw
    g0 = h0 + 3 * hw
    assert seq * 2 == DFT_R * DFT_R and batch % 2 == 0

    c8 = jnp.zeros((8, d), F32).at[:batch].set(c).at[batch].set(c_ctx)
    mod_all = _modulation(c8, w_mod, b_mod)
    rope_tabs = _rope_tables(seq, ATTN_VD)
    tabs = _dft_tables()
    ctx_tabs = _dense_tables(ctx_len)

    h_lat, h_ctx = x, ctx
    for l in range(depth):
        last = l == depth - 1
        lam_init = LAMBDA_INIT_BASE - LAMBDA_INIT_AMP * math.exp(-LAMBDA_INIT_RATE * l)
        mod = mod_all[l, :batch].reshape(batch, 1, N_MOD * d)
        mod_c = mod_all[l, batch:batch + 1].reshape(1, 1, N_MOD * d)
        wl = w_in[l]
        fparams = (hf_w1[l], hf_b1[l], hf_freq1[l], hf_w2[l], hf_b2[l], hf_freq2[l], hf_w3[l])

        wl_bf = wl.astype(BF16)
        segs = [("rope", aw, Q_SCALE), ("rope", aw, 1.0), ("plain", aw, 1.0), ("plain", pw, 1.0),
                ("plain", 3 * hw, 1.0), ("sigmoid", 3 * d, 1.0)]
        q, k, v, up, uh, g = _inproj(h_lat, mod, norm1_g[l], wl_bf, segs,
                                     [BF16, BF16, BF16, F32, F32, F32], rope_tabs)

        if last:
            segs_c = [("plain", aw, 1.0), ("plain", aw, 1.0)]
            k_c, v_c = _inproj(h_ctx, mod_c, norm1_g[l], wl_bf[:, k0:p0], segs_c, [BF16, BF16])
        else:
            segs_c = [("plain", aw, Q_SCALE), ("plain", aw, 1.0), ("plain", aw, 1.0), ("plain", pw, 1.0),
                      ("plain", 3 * hw, 1.0), ("sigmoid", 3 * d, 1.0)]
            q_c, k_c, v_c, up_c, uh_c, g_c = _inproj(h_ctx, mod_c, norm1_g[l], wl_bf, segs_c,
                                                    [BF16, BF16, BF16, F32, F32, F32])
            o_c = _attention(q_c, k_c, v_c, lam_qk[l], subln_g[l], lam_init)
            pm_c, vv_c, x1_c, x2_c = _local_mixers(up_c, uh_c, w_pool[l], pool_scale[l], w_short[l], b_short[l])
            kt_c, inv_norm_c = _hyena_time_kernel(ctx_len, *fparams)
            hspec_c = _dense_spectrum(kt_c, inv_norm_c / (2 * ctx_len), ctx_tabs["wk"])
            z_c = _dense_conv_gate(vv_c, x1_c, hspec_c, 0, hy_bias[l, 0], ctx_tabs)
            y_c = _dense_conv_gate(z_c, x2_c, hspec_c, 1, hy_bias[l, 1], ctx_tabs)
            h_ctx_mid = _mix(h_ctx, o_c, pm_c, y_c, g_c, mod_c, w_attn_o[l], w_pool_o[l], w_hy_o[l], w_out[l])
            h_ctx_new = _ffn(h_ctx_mid, mod_c, norm2_g[l], w_ffn_in[l], w_ffn_out[l], final_g, False)

        vt = jnp.swapaxes(jnp.concatenate([v_c, v], axis=1), 1, 2)
        o = _attention_t(q, jnp.concatenate([k_c, k], axis=1), vt, lam_qk[l], subln_g[l], lam_init)
        pm, vv, x1, x2 = _local_mixers(up, uh, w_pool[l], pool_scale[l], w_short[l], b_short[l])
        kt, inv_norm = _hyena_time_kernel(seq, *fparams)
        hspec = _filter_spectrum(kt, inv_norm / (2 * seq), tabs)
        z = _long_conv_gate(vv, x1, hspec, 0, hy_bias[l, 0], tabs)
        y = _long_conv_gate(z, x2, hspec, 1, hy_bias[l, 1], tabs)
        h_mid = _mix(h_lat, o, pm, y, g, mod, w_attn_o[l], w_pool_o[l], w_hy_o[l], w_out[l])
        h_lat = _ffn(h_mid, mod, norm2_g[l], w_ffn_in[l], w_ffn_out[l], final_g, last)
        if not last:
            h_ctx = h_ctx_new
    return h_lat
```

```python
import functools
import math

import jax
import jax.numpy as jnp
from jax import lax
from jax.experimental import pallas as pl
from jax.experimental.pallas import tpu as pltpu

F32 = jnp.float32
BF16 = jnp.bfloat16
HIGHEST = lax.Precision.HIGHEST

GRID_W = 64
N_MOD = 6
ATTN_HD = 64
ATTN_VD = 128
Q_SCALE = ATTN_HD ** -0.5 * math.log2(math.e)
ROPE_BASE = 10000.0
ROPE_FREQS = 16
LAMBDA_INIT_BASE = 0.8
LAMBDA_INIT_AMP = 0.6
LAMBDA_INIT_RATE = 0.3
POOL_WINDOWS = (2, 4, 8, 16)
POOL_GC = 128
HALO = 8
HYENA_BANDS = 16
HYENA_FAST_DECAY = 0.3
HYENA_SLOW_DECAY = 1.5
HYENA_TARGET = 1e-2
EPS = 1e-6
DFT_DATA_PASSES = 1
DFT_FILTER_PASSES = 1
DFT_R = 128
HALF_ROWS = 72

VMEM_LIMIT = 56 * 1024 * 1024


def _params(n_grid, flags=None):
    return pltpu.CompilerParams(dimension_semantics=("arbitrary",) * n_grid,
                                vmem_limit_bytes=VMEM_LIMIT, flags=flags)


def _const_spec(shape):
    zeros = (0,) * len(shape)
    return pl.BlockSpec(shape, lambda *_: zeros, pipeline_mode=pl.Buffered(1))


def _dot(a, b):
    return jnp.dot(a, b, preferred_element_type=F32)


def _split3(w, passes=3):
    hi = w.astype(BF16)
    if passes == 1:
        return hi
    lo = (w - hi.astype(F32)).astype(BF16)
    return jnp.concatenate([hi, hi, lo], axis=-1)


def _mm3(w3, x):
    xh = x.astype(BF16)
    if w3.shape[-1] == x.shape[0]:
        return _dot(w3, xh)
    xl = (x - xh.astype(F32)).astype(BF16)
    return _dot(w3, jnp.concatenate([xh, xl, xh], axis=0))


def _mod_kernel(c_ref, w_ref, b_ref, o_ref):
    c = c_ref[...]
    act = c * jax.nn.sigmoid(c)
    o_ref[0] = jnp.dot(act, w_ref[0], precision=HIGHEST, preferred_element_type=F32) + b_ref[0]


def _modulation(c8, w_mod, b_mod):
    depth, d, n = w_mod.shape
    tn = 1536
    return pl.pallas_call(
        _mod_kernel,
        grid=(depth, n // tn),
        in_specs=[pl.BlockSpec((8, d), lambda l, j: (0, 0)),
                  pl.BlockSpec((1, d, tn), lambda l, j: (l, 0, j)),
                  pl.BlockSpec((1, 1, tn), lambda l, j: (l, 0, j))],
        out_specs=pl.BlockSpec((1, 8, tn), lambda l, j: (l, 0, j)),
        out_shape=jax.ShapeDtypeStruct((depth, 8, n), F32),
        compiler_params=_params(2),
        name="modulation",
    )(c8, w_mod, b_mod.reshape(depth, 1, n))


def _mod_spec(mod, d, col):
    if mod.shape[0] == 1:
        return pl.BlockSpec((1, 1, d), lambda b, i: (0, 0, col))
    return pl.BlockSpec((1, 1, d), lambda b, i: (b, 0, col))


def _norm_mod(x, g, shift, scale):
    ms = jnp.mean(x * x, axis=-1, keepdims=True)
    a = x * lax.rsqrt(ms + EPS) * g
    return a * (1.0 + scale) + shift


def _rope(p, cos, sin):
    lanes = cos.shape[1]
    lane = lax.broadcasted_iota(jnp.int32, cos.shape, 1)
    is_a = lane % (2 * ROPE_FREQS) < ROPE_FREQS
    out = []
    for g in range(p.shape[1] // lanes):
        x = p[:, g * lanes:(g + 1) * lanes]
        partner = jnp.where(is_a, pltpu.roll(x, lanes - ROPE_FREQS, 1), pltpu.roll(x, ROPE_FREQS, 1))
        out.append(x * cos + partner * sin)
    return jnp.concatenate(out, axis=1)


def _local_mix(pbuf, hbuf, wp_ref, ps_ref, ws_ref, bs_ref, pm_ref, v_ref, x1_ref, x2_ref, i, ts, seq):
    t = i * ts + lax.broadcasted_iota(jnp.int32, (ts, 1), 0)
    for g, w in enumerate(POOL_WINDOWS):
        cs = slice(g * POOL_GC, (g + 1) * POOL_GC)
        acc = pbuf[HALO - w // 2:HALO - w // 2 + ts, cs]
        for j in range(1 - w // 2, w // 2):
            acc = acc + pbuf[HALO + j:HALO + j + ts, cs]
        cnt = (jnp.minimum(t + w // 2, seq) - jnp.maximum(t - w // 2, 0)).astype(F32)
        pooled = acc / cnt - pbuf[HALO:HALO + ts, cs]
        mixed = _dot(pooled.astype(BF16), wp_ref[g])
        pm_ref[0, :, cs] = mixed * ps_ref[:, cs]

    ws = ws_ref[...]
    conv = (hbuf[HALO - 1:HALO - 1 + ts] * ws[0:1] + hbuf[HALO:HALO + ts] * ws[1:2]
            + hbuf[HALO + 1:HALO + 1 + ts] * ws[2:3] + bs_ref[...])
    c = conv.shape[1] // 3
    v_ref[0] = conv[:, :c]
    x1_ref[0] = conv[:, c:2 * c]
    x2_ref[0] = conv[:, 2 * c:]


def _inproj_kernel(*refs, segs, rope, local, tm, seq):
    it = iter(refs)
    h_ref = next(it)
    hp_ref, hn_ref = (next(it), next(it)) if local else (None, None)
    sh_ref, sc_ref, g_ref, w_ref = next(it), next(it), next(it), next(it)
    cos_ref, sin_ref = (next(it), next(it)) if rope else (None, None)
    loc_refs = [next(it) for _ in range(4)] if local else None
    rest = list(it)
    n_out = sum(4 if kind == "local" else 1 for kind, _, _, _ in segs)
    out_refs, scratch = rest[:n_out], rest[n_out:]

    g, sh, sc = g_ref[...], sh_ref[0], sc_ref[0]
    a32 = _norm_mod(h_ref[0], g, sh, sc)
    a = a32.astype(BF16)
    oi = 0
    for kind, col, width, scale in segs:
        wseg = w_ref[:, col:col + width]
        if kind == "local":
            i = pl.program_id(1)
            ext = jnp.concatenate([_norm_mod(hp_ref[0], g, sh, sc), a32, _norm_mod(hn_ref[0], g, sh, sc)],
                                  axis=0).astype(BF16)
            u = _dot(ext, wseg)
            row = lax.broadcasted_iota(jnp.int32, (tm + 2 * HALO, 1), 0)
            inside = jnp.logical_and(jnp.logical_or(i > 0, row >= HALO),
                                     jnp.logical_or(i < pl.num_programs(1) - 1, row < HALO + tm))
            u = jnp.where(inside, u, 0.0)
            pbuf, hbuf = scratch
            cp = pbuf.shape[1]
            pbuf[...] = u[:, :cp]
            hbuf[...] = u[:, cp:]
            _local_mix(pbuf, hbuf, *loc_refs, *out_refs[oi:oi + 4], i, tm, seq)
            oi += 4
            continue
        p = _dot(a, wseg)
        if kind == "rope":
            p = _rope(p, cos_ref[...], sin_ref[...])
        elif kind == "sigmoid":
            p = jax.nn.sigmoid(p)
        if scale != 1.0:
            p = p * scale
        out_refs[oi][0] = p.astype(out_refs[oi].dtype)
        oi += 1


def _inproj(h, mod, norm_g, w_cat, segs, out_dtypes, rope_tabs=None, local_params=None, tm=256):
    b, s, d = h.shape
    tm = min(tm, s)
    rope = rope_tabs is not None
    local = local_params is not None
    nh = tm // HALO
    tok = lambda w: pl.BlockSpec((1, tm, w), lambda b_, i: (b_, i, 0))
    in_specs = [tok(d)]
    args = [h]
    if local:
        in_specs += [pl.BlockSpec((1, HALO, d), lambda b_, i: (b_, jnp.maximum(i * nh - 1, 0), 0)),
                     pl.BlockSpec((1, HALO, d), lambda b_, i: (b_, jnp.minimum((i + 1) * nh, s // HALO - 1), 0))]
        args += [h, h]
    in_specs += [_mod_spec(mod, d, 0), _mod_spec(mod, d, 1), _const_spec((1, d)), _const_spec(w_cat.shape)]
    args += [mod, mod, norm_g.reshape(1, d), w_cat]
    if rope:
        wr = rope_tabs[0].shape[1]
        in_specs += [pl.BlockSpec((tm, wr), lambda b_, i: (i, 0))] * 2
        args += list(rope_tabs)
    scratch = []
    if local:
        w_pool, pool_scale, w_short, b_short = local_params
        cp, ch = pool_scale.shape[0], b_short.shape[0]
        in_specs += [_const_spec(w_pool.shape), _const_spec((1, cp)), _const_spec(w_short.shape),
                     _const_spec((1, ch))]
        args += [w_pool.astype(BF16), pool_scale.reshape(1, cp), w_short, b_short.reshape(1, ch)]
        scratch = [pltpu.VMEM((tm + 2 * HALO, cp), F32), pltpu.VMEM((tm + 2 * HALO, ch), F32)]
    widths = []
    for kind, _, w, _ in segs:
        widths += [cp, ch // 3, ch // 3, ch // 3] if kind == "local" else [w]
    out_specs = [tok(w) for w in widths]
    out_shape = [jax.ShapeDtypeStruct((b, s, w), dt) for w, dt in zip(widths, out_dtypes)]
    return pl.pallas_call(
        functools.partial(_inproj_kernel, segs=tuple(segs), rope=rope, local=local, tm=tm, seq=s),
        grid=(b, s // tm), in_specs=in_specs, out_specs=out_specs, out_shape=out_shape,
        scratch_shapes=scratch, compiler_params=_params(2), name="inproj",
    )(*args)


def _lambda(lq_ref, lam_init):
    lq = lq_ref[...]
    return (jnp.exp(jnp.sum(lq[0:1] * lq[1:2], axis=-1, keepdims=True))
            - jnp.exp(jnp.sum(lq[2:3] * lq[3:4], axis=-1, keepdims=True)) + lam_init)


def _split_halves(q):
    lane = lax.broadcasted_iota(jnp.int32, q.shape, 1)
    zero = jnp.zeros_like(q)
    return jnp.concatenate([jnp.where(lane < ATTN_HD, q, zero),
                            jnp.where(lane >= ATTN_HD, q, zero)], axis=0)


def _qk(qq, k):
    return lax.dot_general(qq, k, (((1,), (1,)), ((), ())), preferred_element_type=F32)


def _sub_ln(o, g, lam_init):
    ms = jnp.mean(o * o, axis=-1, keepdims=True)
    return o * lax.rsqrt(ms + EPS) * g * (1.0 - lam_init)


def _attn_kernel(lq_ref, q_ref, k_ref, v_ref, g_ref, o_ref, *, lam_init, tq):
    lam = _lambda(lq_ref, lam_init)
    s = _qk(_split_halves(q_ref[0]), k_ref[0])
    m = jnp.max(s, axis=-1, keepdims=True)
    p = jnp.exp2(s - m)
    l = jnp.sum(p, axis=-1, keepdims=True)
    a = (p[:tq] - p[tq:] * (lam * l[:tq] / l[tq:])).astype(BF16)
    o = _dot(a, v_ref[0]) / l[:tq]
    o_ref[0] = _sub_ln(o, g_ref[...], lam_init).astype(o_ref.dtype)


def _attn_t_kernel(lq_ref, q_ref, k_ref, vt_ref, g_ref, o_ref, s0, s1, m_scr, *, lam_init, tq, lk, kc, nq):
    j = pl.program_id(2)
    bufs = (s0, s1)

    @pl.when(j == 0)
    def _():
        s1[...] = jnp.zeros_like(s1)
        m_scr[1] = jnp.zeros(m_scr.shape[1:], F32)

    lam = _lambda(lq_ref, lam_init)
    for r in range(2):
        sig = 2 * j + r
        buf_a, buf_b = bufs[r], bufs[1 - r]
        t_a = jnp.minimum(sig, nq - 1)
        t_b = jnp.clip(sig - 1, 0, nq - 1)
        qq = _split_halves(q_ref[0, pl.ds(pl.multiple_of(t_a * tq, tq), tq), :])
        m_b = m_scr[1 - r]
        m8 = l8 = acc = None
        for c0 in range(0, lk, kc):
            rows = slice(c0, c0 + kc)
            s = _qk(k_ref[0, rows, :], qq)
            buf_a[rows, :] = s
            for i in range(0, kc, 8):
                m8 = s[i:i + 8] if m8 is None else jnp.maximum(m8, s[i:i + 8])
            p = jnp.exp2(buf_b[rows, :].reshape(kc // 8, 8, 2 * tq) - m_b)
            lc = jnp.sum(p, axis=0)
            l8 = lc if l8 is None else l8 + lc
            pv = _dot(vt_ref[0, :, rows], p.reshape(kc, 2 * tq).astype(BF16))
            acc = pv if acc is None else acc + pv
        m_scr[r] = jnp.broadcast_to(jnp.max(m8, axis=0, keepdims=True), m8.shape)
        l = jnp.sum(l8, axis=0, keepdims=True)
        ot = acc[:, :tq] / l[:, :tq] - acc[:, tq:] * (lam / l[:, tq:])
        ms = jnp.mean(ot * ot, axis=0, keepdims=True)
        ot = ot * lax.rsqrt(ms + EPS) * g_ref[...] * (1.0 - lam_init)
        o_ref[0, pl.ds(pl.multiple_of(t_b * tq, tq), tq), :] = ot.T.astype(o_ref.dtype)


def _attention_t(q, k, vt, lam_qk, subln_g, lam_init, tq=128, kc=768):
    b, s, w = q.shape
    lk = k.shape[1]
    heads = w // ATTN_VD
    nq = s // tq
    nsteps = -(-(nq + 1) // 2)
    return pl.pallas_call(
        functools.partial(_attn_t_kernel, lam_init=lam_init, tq=tq, lk=lk, kc=kc, nq=nq),
        grid=(b, heads, nsteps),
        in_specs=[pl.BlockSpec(lam_qk.shape, lambda b_, h, i: (0, 0)),
                  pl.BlockSpec((1, s, ATTN_VD), lambda b_, h, i: (b_, 0, h)),
                  pl.BlockSpec((1, lk, ATTN_VD), lambda b_, h, i: (b_, 0, h)),
                  pl.BlockSpec((1, ATTN_VD, lk), lambda b_, h, i: (b_, h, 0)),
                  pl.BlockSpec((ATTN_VD, 1), lambda b_, h, i: (0, 0))],
        out_specs=pl.BlockSpec((1, s, ATTN_VD), lambda b_, h, i: (b_, 0, h)),
        out_shape=jax.ShapeDtypeStruct((b, s, w), F32),
        scratch_shapes=[pltpu.VMEM((lk, 2 * tq), F32)] * 2 + [pltpu.VMEM((2, 8, 2 * tq), F32)],
        compiler_params=_params(3), name="diff_attention_t",
    )(lam_qk, q, k, vt, subln_g.reshape(ATTN_VD, 1))


def _attention(q, k, v, lam_qk, subln_g, lam_init, tq=128):
    b, s, w = q.shape
    lk = k.shape[1]
    heads = w // ATTN_VD
    tq = min(tq, s)
    return pl.pallas_call(
        functools.partial(_attn_kernel, lam_init=lam_init, tq=tq),
        grid=(b, heads, s // tq),
        in_specs=[pl.BlockSpec(lam_qk.shape, lambda b_, h, i: (0, 0)),
                  pl.BlockSpec((1, tq, ATTN_VD), lambda b_, h, i: (b_, i, h)),
                  pl.BlockSpec((1, lk, ATTN_VD), lambda b_, h, i: (b_, 0, h)),
                  pl.BlockSpec((1, lk, ATTN_VD), lambda b_, h, i: (b_, 0, h)),
                  pl.BlockSpec((1, ATTN_VD), lambda b_, h, i: (0, 0))],
        out_specs=pl.BlockSpec((1, tq, ATTN_VD), lambda b_, h, i: (b_, i, h)),
        out_shape=jax.ShapeDtypeStruct((b, s, w), F32),
        compiler_params=_params(3), name="diff_attention",
    )(lam_qk, q, k, v, subln_g.reshape(1, ATTN_VD))


def _filter_kernel(z_ref, w1_ref, b1_ref, f1_ref, w2_ref, b2_ref, f2_ref, w3_ref, dl_ref, h_ref, asum_ref, *,
                   tl, l):
    i = pl.program_id(0)
    n = i * tl + lax.broadcasted_iota(jnp.int32, (tl, 1), 0)
    z = z_ref[...]
    t01 = z[:, 0:1]
    pre = jnp.dot(z, w1_ref[...], precision=HIGHEST, preferred_element_type=F32)
    h1 = jnp.sin(f1_ref[...] * (pre + b1_ref[...]))
    h2 = jnp.sin(f2_ref[...] * (jnp.dot(h1, w2_ref[...], precision=HIGHEST,
                                        preferred_element_type=F32) + b2_ref[...]))
    h3 = jnp.dot(h2, w3_ref[0], precision=HIGHEST, preferred_element_type=F32)
    h = jnp.where(n == l, 0.0, h3 * jnp.exp(-t01 * dl_ref[...]))
    h_ref[...] = h
    part = jnp.sum(jnp.abs(h), axis=0, keepdims=True)

    @pl.when(i == 0)
    def _():
        asum_ref[...] = jnp.zeros_like(asum_ref)

    asum_ref[...] += part


def _hyena_time_kernel(l, w1, b1, f1, w2, b2, f2, w3):
    ffn = w1.shape[1]
    nc = w3.shape[1] // 2
    c = nc // 2
    tl = min(l, 512)
    w3d = jnp.swapaxes(w3.reshape(ffn, 2, nc), 0, 1)
    n = jnp.arange(2 * l, dtype=jnp.int32)
    t = jnp.where(n < l, n, 2 * l - n).astype(F32)
    bands = jnp.linspace(1e-4, HYENA_BANDS - 1, HYENA_BANDS, dtype=F32)
    ang = 2.0 * math.pi * t[:, None] * bands[None, :] / l
    emb = 2 * HYENA_BANDS + 1
    z = jnp.concatenate([(t / (l - 1))[:, None], jnp.cos(ang), jnp.sin(ang),
                         jnp.zeros((2 * l, 128 - emb), F32)], axis=1)
    w1p = jnp.zeros((128, ffn), F32).at[:emb].set(w1)
    max_decay = math.log(HYENA_TARGET) / HYENA_FAST_DECAY
    min_decay = math.log(HYENA_TARGET) / HYENA_SLOW_DECAY
    deltas = jnp.abs(jnp.linspace(min_decay, max_decay, c, dtype=F32))
    dl = jnp.tile(deltas, 2).reshape(1, nc)
    row = lambda a: a.reshape(1, -1)
    nfwd = l // tl
    kt, asum = pl.pallas_call(
        functools.partial(_filter_kernel, tl=tl, l=l),
        grid=(2 * nfwd,),
        in_specs=[pl.BlockSpec((tl, 128), lambda i: (i, 0)), _const_spec((128, ffn)),
                  _const_spec((1, ffn)), _const_spec((1, ffn)),
                  _const_spec((ffn, ffn)), _const_spec((1, ffn)), _const_spec((1, ffn)),
                  pl.BlockSpec((1, ffn, nc), lambda i: (i // nfwd, 0, 0)), _const_spec((1, nc))],
        out_specs=[pl.BlockSpec((tl, nc), lambda i: (i, 0)), pl.BlockSpec((1, nc), lambda i: (0, 0))],
        out_shape=[jax.ShapeDtypeStruct((2 * l, nc), F32), jax.ShapeDtypeStruct((1, nc), F32)],
        compiler_params=_params(1), name="hyena_filter_mlp",
    )(z, w1p, row(b1), row(f1), w2, row(b2), row(f2), w3d, dl)
    return kt, 1.0 / asum


def _angles(prod, n):
    th = (2.0 * math.pi / n) * (prod % n).astype(F32)
    return jnp.cos(th), jnp.sin(th)


def _dft_tables():
    r = DFT_R
    n = r * r
    idx = jnp.arange(r, dtype=jnp.int32)
    c, s = _angles(idx[:, None] * idx[None, :], r)
    ch, sh = c[:, :r // 2], s[:, :r // 2]
    w1_data = jnp.concatenate([jnp.concatenate([ch, sh], 1), jnp.concatenate([-sh, ch], 1)], 0)
    pad = jnp.zeros((HALF_ROWS - (r // 2 + 1), r), F32)
    w1_real = jnp.concatenate([c[:r // 2 + 1], pad, -s[:r // 2 + 1], pad], 0)
    ct, st = c[:r // 2], s[:r // 2]
    w2 = jnp.concatenate([jnp.concatenate([ct, -st], 1), jnp.concatenate([st, ct], 1)], 0)
    k = idx[:, None, None] + r * idx[None, :, None]
    cg, sg = _angles(k * idx[None, None, :], n)
    g = jnp.concatenate([jnp.concatenate([cg, sg], 2), jnp.concatenate([-sg, cg], 2)], 1)
    dp = DFT_DATA_PASSES
    fp = DFT_FILTER_PASSES
    return dict(w1_data=_split3(w1_data, dp), w1_real=_split3(w1_real, fp), w2=_split3(w2, dp),
                g=_split3(g, dp), g_filter=_split3(g, fp))


N2_TILE = 16
K1_TILE = 4


def _dft1_kernel(x_ref, w_ref, o_ref, *, parts):
    r = o_ref.shape[2]
    for i in range(N2_TILE):
        x = jnp.concatenate([x_ref[p, :, i, :] for p in range(parts)], axis=0)
        y = _mm3(w_ref[...], x)
        o_ref[0, 0, :, i, :] = y[:r]
        o_ref[0, 1, :, i, :] = y[r:]


def _dft_major(x, w3, parts):
    b, n1, r, c = x.shape
    k1 = w3.shape[0] // 2
    return pl.pallas_call(
        functools.partial(_dft1_kernel, parts=parts),
        grid=(b // parts, r // N2_TILE),
        in_specs=[pl.BlockSpec((parts, n1, N2_TILE, c), lambda p_, j: (p_, 0, j, 0)), _const_spec(w3.shape)],
        out_specs=pl.BlockSpec((1, 2, k1, N2_TILE, c), lambda p_, j: (p_, 0, 0, j, 0)),
        out_shape=jax.ShapeDtypeStruct((b // parts, 2, k1, r, c), F32),
        compiler_params=_params(2), name="dft_major",
    )(x, w3)


def _spectrum_kernel(a_ref, g_ref, sc_ref, o_ref, *, nyquist):
    sign = jnp.where(pl.program_id(0) <= nyquist, 1.0, -1.0)
    a = jnp.concatenate([a_ref[0, 0, 0], a_ref[0, 1, 0] * sign], axis=0)
    x = _mm3(g_ref[0], a) * sc_ref[...]
    r = x.shape[0] // 2
    o_ref[0, 0] = x[:r]
    o_ref[1, 0] = x[r:]


def _filter_spectrum(kt, scale, tabs):
    r = DFT_R
    c = kt.shape[1]
    a = _dft_major(kt.reshape(1, r, r, c), tabs["w1_real"], 1)
    tcs = c
    return pl.pallas_call(
        functools.partial(_spectrum_kernel, nyquist=r // 2),
        grid=(r, c // tcs),
        in_specs=[pl.BlockSpec((1, 2, 1, r, tcs), lambda k, j: (0, 0, jnp.where(k <= r // 2, k, r - k), 0, j)),
                  pl.BlockSpec((1,) + tabs["g_filter"].shape[1:], lambda k, j: (k, 0, 0)),
                  pl.BlockSpec((1, tcs), lambda k, j: (0, j))],
        out_specs=pl.BlockSpec((2, 1, r, tcs), lambda k, j: (0, k, 0, j)),
        out_shape=jax.ShapeDtypeStruct((2, r, r, c), F32),
        compiler_params=_params(2), name="filter_spectrum",
    )(a, tabs["g_filter"], scale)


def _mm3t(w3, x):
    assert w3.shape[0] == x.shape[0], "transposed product is only used in single-pass mode"
    return lax.dot_general(w3, x.astype(BF16), (((0,), (0,)), ((), ())), preferred_element_type=F32)


def _dft2_kernel(a_ref, g_ref, h_ref, o_ref):
    for kk in range(K1_TILE):
        hr, hi = h_ref[0, kk], h_ref[1, kk]
        for p in range(a_ref.shape[0]):
            a = jnp.concatenate([a_ref[p, 0, kk], a_ref[p, 1, kk]], axis=0)
            x = _mm3(g_ref[kk], a)
            r = x.shape[0] // 2
            xr, xi = x[:r], x[r:]
            y = jnp.concatenate([xr * hr - xi * hi, xr * hi + xi * hr], axis=0)
            bb = _mm3t(g_ref[kk], y)
            o_ref[p, 0, kk] = bb[:r]
            o_ref[p, 1, kk] = bb[r:]


def _dft3_kernel(b_ref, w_ref, u_ref, x_ref, bias_ref, o_ref):
    bias = bias_ref[...]
    for i in range(N2_TILE):
        bb = jnp.concatenate([b_ref[0, 0, :, i, :], b_ref[0, 1, :, i, :]], axis=0)
        y = _mm3(w_ref[...], bb)
        half = y.shape[0] // 2
        for h in range(2):
            o_ref[h, :, i, :] = x_ref[h, :, i, :] * (y[h * half:(h + 1) * half] + u_ref[h, :, i, :] * bias)


def _long_conv_gate(u, x, hspec, order, bias, tabs):
    b, l, c = u.shape
    r = DFT_R
    p = b // 2
    nat = (b, r // 2, r, c)
    a = _dft_major(u.reshape(nat), tabs["w1_data"], 2)
    spec5 = pl.BlockSpec((p, 2, K1_TILE, r, c), lambda k: (0, 0, k, 0, 0))
    bb = pl.pallas_call(
        _dft2_kernel,
        grid=(r // K1_TILE,),
        in_specs=[spec5,
                  pl.BlockSpec((K1_TILE,) + tabs["g"].shape[1:], lambda k: (k, 0, 0)),
                  pl.BlockSpec((2, K1_TILE, r, c), lambda k: (0, k, 0, order))],
        out_specs=spec5,
        out_shape=jax.ShapeDtypeStruct((p, 2, r, r, c), F32),
        compiler_params=_params(1), name="dft_minor_filter",
    )(a, tabs["g"], hspec)
    blk = pl.BlockSpec((2, r // 2, N2_TILE, c), lambda p_, j: (p_, 0, j, 0))
    out = pl.pallas_call(
        _dft3_kernel,
        grid=(p, r // N2_TILE),
        in_specs=[pl.BlockSpec((1, 2, r, N2_TILE, c), lambda p_, j: (p_, 0, 0, j, 0)),
                  _const_spec(tabs["w2"].shape), blk, blk, _const_spec((1, c))],
        out_specs=blk,
        out_shape=jax.ShapeDtypeStruct(nat, F32),
        compiler_params=_params(2), name="idft_major_gate",
    )(bb, tabs["w2"], u.reshape(nat), x.reshape(nat), bias.reshape(1, c))
    return out.reshape(b, l, c)


def _dense_tables(l):
    n = 2 * l
    k = jnp.arange(n, dtype=jnp.int32)
    t = jnp.arange(l, dtype=jnp.int32)
    c, s = _angles(k[:, None] * t[None, :], n)
    wf = jnp.concatenate([jnp.concatenate([c, s], 1), jnp.concatenate([-s, c], 1)], 0)
    ct, st = c.T, s.T
    wi = jnp.concatenate([jnp.concatenate([ct, -st], 1), jnp.concatenate([st, ct], 1)], 0)
    ca, sa = _angles(k[:, None] * k[None, :], n)
    wk = jnp.concatenate([ca, -sa], 0)
    dp = DFT_DATA_PASSES
    return dict(wf=_split3(wf, dp), wi=_split3(wi, dp), wk=_split3(wk, DFT_FILTER_PASSES))


def _dense_spectrum_kernel(k_ref, w_ref, sc_ref, o_ref):
    o_ref[...] = _mm3(w_ref[...], k_ref[...]) * sc_ref[...]


def _dense_spectrum(kt, scale, wk):
    n, c = kt.shape
    return pl.pallas_call(
        _dense_spectrum_kernel,
        grid=(1,),
        in_specs=[_const_spec((n, c)), _const_spec(wk.shape), _const_spec((1, c))],
        out_specs=pl.BlockSpec((2 * n, c), lambda i: (0, 0)),
        out_shape=jax.ShapeDtypeStruct((2 * n, c), F32),
        compiler_params=_params(1), name="dense_filter_spectrum",
    )(kt, wk, scale)


def _dense_conv_kernel(u_ref, x_ref, wf_ref, wi_ref, h_ref, bias_ref, o_ref):
    u0, u1 = u_ref[0], u_ref[1]
    z = _mm3(wf_ref[...], jnp.concatenate([u0, u1], axis=0))
    n = z.shape[0] // 2
    zr, zi = z[:n], z[n:]
    hr, hi = h_ref[:n], h_ref[n:]
    y = _mm3(wi_ref[...], jnp.concatenate([zr * hr - zi * hi, zr * hi + zi * hr], axis=0))
    l = y.shape[0] // 2
    bias = bias_ref[...]
    o_ref[0] = x_ref[0] * (y[:l] + u0 * bias)
    o_ref[1] = x_ref[1] * (y[l:] + u1 * bias)


def _dense_conv_gate(u, x, hspec, order, bias, tabs):
    b, l, c = u.shape
    blk = pl.BlockSpec((2, l, c), lambda p_: (p_, 0, 0))
    return pl.pallas_call(
        _dense_conv_kernel,
        grid=(b // 2,),
        in_specs=[blk, blk, _const_spec(tabs["wf"].shape), _const_spec(tabs["wi"].shape),
                  pl.BlockSpec((4 * l, c), lambda p_: (0, order)), _const_spec((1, c))],
        out_specs=blk,
        out_shape=jax.ShapeDtypeStruct((b, l, c), F32),
        compiler_params=_params(1), name="dense_conv_gate",
    )(u, x, tabs["wf"], tabs["wi"], hspec, bias.reshape(1, c))


def _mix_kernel(h_ref, o_ref, pm_ref, y_ref, g_ref, gate_ref, wa_ref, wp_ref, wh_ref, wo_ref, out_ref):
    d = h_ref.shape[2]
    a = _dot(o_ref[0].astype(BF16), wa_ref[...])
    p = _dot(pm_ref[0].astype(BF16), wp_ref[...])
    y = _dot(y_ref[0].astype(BF16), wh_ref[...])
    merged = g_ref[0, :, 0:d] * a + g_ref[0, :, d:2 * d] * p + g_ref[0, :, 2 * d:3 * d] * y
    out = _dot(merged.astype(BF16), wo_ref[...])
    out_ref[0] = h_ref[0] + gate_ref[0] * out


def _mix(h, o, pm, y, g, mod, w_attn_o, w_pool_o, w_hy_o, w_out, tm=512):
    b, s, d = h.shape
    tm = min(tm, s)

    def tok(w):
        return pl.BlockSpec((1, tm, w), lambda b_, i: (b_, i, 0))

    ws = [w.astype(BF16) for w in (w_attn_o, w_pool_o, w_hy_o, w_out)]
    return pl.pallas_call(
        _mix_kernel,
        grid=(b, s // tm),
        in_specs=[tok(d), tok(o.shape[2]), tok(pm.shape[2]), tok(y.shape[2]), tok(g.shape[2]),
                  _mod_spec(mod, d, 2)] + [_const_spec(w.shape) for w in ws],
        out_specs=tok(d),
        out_shape=jax.ShapeDtypeStruct((b, s, d), F32),
        compiler_params=_params(2), name="merge_outproj",
    )(h, o, pm, y, g, mod, *ws)


def _ffn_kernel(h_ref, sh_ref, sc_ref, gate_ref, g_ref, wi_ref, wo_ref, fg_ref, out_ref, *,
                hidden, chunk, final):
    x = h_ref[0]
    a = _norm_mod(x, g_ref[...], sh_ref[0], sc_ref[0]).astype(BF16)
    acc = jnp.zeros(x.shape, F32)
    for c0 in range(0, hidden, chunk):
        gt = _dot(a, wi_ref[:, c0:c0 + chunk])
        up = _dot(a, wi_ref[:, hidden + c0:hidden + c0 + chunk])
        act = (gt * jax.nn.sigmoid(gt) * up).astype(BF16)
        acc = acc + _dot(act, wo_ref[c0:c0 + chunk, :])
    y = x + gate_ref[0] * acc
    if final:
        ms = jnp.mean(y * y, axis=-1, keepdims=True)
        y = y * lax.rsqrt(ms + EPS) * fg_ref[...]
    out_ref[0] = y


def _ffn(h, mod, norm_g, w_in, w_out, final_g, final, tm=512):
    b, s, d = h.shape
    hidden = w_out.shape[0]
    tm = min(tm, s)
    tok = pl.BlockSpec((1, tm, d), lambda b_, i: (b_, i, 0))
    return pl.pallas_call(
        functools.partial(_ffn_kernel, hidden=hidden, chunk=hidden // 2, final=final),
        grid=(b, s // tm),
        in_specs=[tok, _mod_spec(mod, d, 3), _mod_spec(mod, d, 4), _mod_spec(mod, d, 5),
                  _const_spec((1, d)), _const_spec(w_in.shape), _const_spec(w_out.shape),
                  _const_spec((1, d))],
        out_specs=tok,
        out_shape=jax.ShapeDtypeStruct((b, s, d), F32),
        compiler_params=_params(2), name="swiglu_ffn",
    )(h, mod, mod, mod, norm_g.reshape(1, d), w_in.astype(BF16), w_out.astype(BF16),
      final_g.reshape(1, d))


def _rope_tables(seq, width):
    t = jnp.arange(seq, dtype=jnp.int32)
    row = (t // GRID_W).astype(F32)
    col = (t % GRID_W).astype(F32)
    inv = 1.0 / (ROPE_BASE ** (jnp.arange(ROPE_FREQS, dtype=F32) * 2.0 / (2 * ROPE_FREQS)))
    ar, ac = row[:, None] * inv, col[:, None] * inv
    cos = jnp.concatenate([jnp.cos(ar)] * 2 + [jnp.cos(ac)] * 2, axis=1)
    sin = jnp.concatenate([-jnp.sin(ar), jnp.sin(ar), -jnp.sin(ac), jnp.sin(ac)], axis=1)
    reps = width // cos.shape[1]
    return jnp.tile(cos, (1, reps)), jnp.tile(sin, (1, reps))


def kernel(x, c, ctx, c_ctx, w_mod, b_mod, norm1_g, norm2_g, w_in, lam_qk, subln_g, w_attn_o, w_pool, pool_scale, w_pool_o, w_short, b_short, hf_w1, hf_b1, hf_freq1, hf_w2, hf_b2, hf_freq2, hf_w3, hy_bias, w_hy_o, w_out, w_ffn_in, w_ffn_out, final_g):
    batch, seq, d = x.shape
    depth = w_mod.shape[0]
    ctx_len = ctx.shape[1]
    aw = w_attn_o.shape[1]
    pw = w_pool_o.shape[1]
    hw = w_hy_o.shape[1]
    q0, k0, v0, p0 = 0, aw, 2 * aw, 3 * aw
    h0 = p0 + pw
    g0 = h0 + 3 * hw
    assert seq * 2 == DFT_R * DFT_R and batch % 2 == 0

    c8 = jnp.zeros((8, d), F32).at[:batch].set(c).at[batch].set(c_ctx)
    mod_all = _modulation(c8, w_mod, b_mod)
    rope_tabs = _rope_tables(seq, ATTN_VD)
    tabs = _dft_tables()
    ctx_tabs = _dense_tables(ctx_len)

    h_lat, h_ctx = x, ctx
    for l in range(depth):
        last = l == depth - 1
        lam_init = LAMBDA_INIT_BASE - LAMBDA_INIT_AMP * math.exp(-LAMBDA_INIT_RATE * l)
        mod = mod_all[l, :batch].reshape(batch, 1, N_MOD * d)
        mod_c = mod_all[l, batch:batch + 1].reshape(1, 1, N_MOD * d)
        wl = w_in[l]
        fparams = (hf_w1[l], hf_b1[l], hf_freq1[l], hf_w2[l], hf_b2[l], hf_freq2[l], hf_w3[l])

        wl_bf = wl.astype(BF16)
        local_params = (w_pool[l], pool_scale[l], w_short[l], b_short[l])
        out_dt = [F32, F32, F32, F32, BF16, BF16, BF16, F32]
        local_seg = ("local", p0, pw + 3 * hw, 1.0)
        segs = [local_seg, ("rope", q0, aw, Q_SCALE), ("rope", k0, aw, 1.0), ("plain", v0, aw, 1.0),
                ("sigmoid", g0, 3 * d, 1.0)]
        pm, vv, x1, x2, q, k, v, g = _inproj(h_lat, mod, norm1_g[l], wl_bf, segs, out_dt, rope_tabs, local_params)

        if last:
            segs_c = [("plain", k0, aw, 1.0), ("plain", v0, aw, 1.0)]
            k_c, v_c = _inproj(h_ctx, mod_c, norm1_g[l], wl_bf, segs_c, [BF16, BF16])
        else:
            segs_c = [local_seg, ("plain", q0, aw, Q_SCALE), ("plain", k0, aw, 1.0), ("plain", v0, aw, 1.0),
                      ("sigmoid", g0, 3 * d, 1.0)]
            pm_c, vv_c, x1_c, x2_c, q_c, k_c, v_c, g_c = _inproj(h_ctx, mod_c, norm1_g[l], wl_bf, segs_c, out_dt,
                                                                 None, local_params)
            o_c = _attention(q_c, k_c, v_c, lam_qk[l], subln_g[l], lam_init)
            kt_c, inv_norm_c = _hyena_time_kernel(ctx_len, *fparams)
            hspec_c = _dense_spectrum(kt_c, inv_norm_c / (2 * ctx_len), ctx_tabs["wk"])
            z_c = _dense_conv_gate(vv_c, x1_c, hspec_c, 0, hy_bias[l, 0], ctx_tabs)
            y_c = _dense_conv_gate(z_c, x2_c, hspec_c, 1, hy_bias[l, 1], ctx_tabs)
            h_ctx_mid = _mix(h_ctx, o_c, pm_c, y_c, g_c, mod_c, w_attn_o[l], w_pool_o[l], w_hy_o[l], w_out[l])
            h_ctx_new = _ffn(h_ctx_mid, mod_c, norm2_g[l], w_ffn_in[l], w_ffn_out[l], final_g, False)

        vt = jnp.swapaxes(jnp.concatenate([v_c, v], axis=1), 1, 2)
        o = _attention_t(q, jnp.concatenate([k_c, k], axis=1), vt, lam_qk[l], subln_g[l], lam_init)
        kt, inv_norm = _hyena_time_kernel(seq, *fparams)
        hspec = _filter_spectrum(kt, inv_norm / (2 * seq), tabs)
        z = _long_conv_gate(vv, x1, hspec, 0, hy_bias[l, 0], tabs)
        y = _long_conv_gate(z, x2, hspec, 1, hy_bias[l, 1], tabs)
        h_mid = _mix(h_lat, o, pm, y, g, mod, w_attn_o[l], w_pool_o[l], w_hy_o[l], w_out[l])
        h_lat = _ffn(h_mid, mod, norm2_g[l], w_ffn_in[l], w_ffn_out[l], final_g, last)
        if not last:
            h_ctx = h_ctx_new
    return h_lat
```

```python
import functools
import math

import jax
import jax.numpy as jnp
from jax import lax
from jax.experimental import pallas as pl
from jax.experimental.pallas import tpu as pltpu

F32 = jnp.float32
BF16 = jnp.bfloat16
HIGHEST = lax.Precision.HIGHEST

GRID_W = 64
N_MOD = 6
ATTN_HD = 64
ATTN_VD = 128
Q_SCALE = ATTN_HD ** -0.5 * math.log2(math.e)
ROPE_BASE = 10000.0
ROPE_FREQS = 16
LAMBDA_INIT_BASE = 0.8
LAMBDA_INIT_AMP = 0.6
LAMBDA_INIT_RATE = 0.3
POOL_WINDOWS = (2, 4, 8, 16)
POOL_GC = 128
HALO = 8
PROJ_PIECE = 1024
HYENA_BANDS = 16
HYENA_FAST_DECAY = 0.3
HYENA_SLOW_DECAY = 1.5
HYENA_TARGET = 1e-2
EPS = 1e-6
DFT_DATA_PASSES = 1
DFT_FILTER_PASSES = 1
DFT_R = 128
HALF_ROWS = 72

VMEM_LIMIT = 56 * 1024 * 1024


def _params(n_grid, flags=None):
    return pltpu.CompilerParams(dimension_semantics=("arbitrary",) * n_grid,
                                vmem_limit_bytes=VMEM_LIMIT, flags=flags)


def _const_spec(shape):
    zeros = (0,) * len(shape)
    return pl.BlockSpec(shape, lambda *_: zeros, pipeline_mode=pl.Buffered(1))


def _dot(a, b):
    return jnp.dot(a, b, preferred_element_type=F32)


def _split3(w, passes=3):
    hi = w.astype(BF16)
    if passes == 1:
        return hi
    lo = (w - hi.astype(F32)).astype(BF16)
    return jnp.concatenate([hi, hi, lo], axis=-1)


def _mm3(w3, x):
    xh = x.astype(BF16)
    if w3.shape[-1] == x.shape[0]:
        return _dot(w3, xh)
    xl = (x - xh.astype(F32)).astype(BF16)
    return _dot(w3, jnp.concatenate([xh, xl, xh], axis=0))


def _mod_kernel(c_ref, w_ref, b_ref, o_ref):
    c = c_ref[...]
    act = c * jax.nn.sigmoid(c)
    o_ref[0] = jnp.dot(act, w_ref[0], precision=HIGHEST, preferred_element_type=F32) + b_ref[0]


def _modulation(c8, w_mod, b_mod):
    depth, d, n = w_mod.shape
    tn = 1536
    return pl.pallas_call(
        _mod_kernel,
        grid=(depth, n // tn),
        in_specs=[pl.BlockSpec((8, d), lambda l, j: (0, 0)),
                  pl.BlockSpec((1, d, tn), lambda l, j: (l, 0, j)),
                  pl.BlockSpec((1, 1, tn), lambda l, j: (l, 0, j))],
        out_specs=pl.BlockSpec((1, 8, tn), lambda l, j: (l, 0, j)),
        out_shape=jax.ShapeDtypeStruct((depth, 8, n), F32),
        compiler_params=_params(2),
        name="modulation",
    )(c8, w_mod, b_mod.reshape(depth, 1, n))


def _mod_spec(mod, d, col):
    if mod.shape[0] == 1:
        return pl.BlockSpec((1, 1, d), lambda b, i: (0, 0, col))
    return pl.BlockSpec((1, 1, d), lambda b, i: (b, 0, col))


def _norm_mod(x, g, shift, scale):
    ms = jnp.mean(x * x, axis=-1, keepdims=True)
    a = x * lax.rsqrt(ms + EPS) * g
    return a * (1.0 + scale) + shift


def _rope(p, cos, sin):
    lanes = cos.shape[1]
    lane = lax.broadcasted_iota(jnp.int32, cos.shape, 1)
    is_a = lane % (2 * ROPE_FREQS) < ROPE_FREQS
    out = []
    for g in range(p.shape[1] // lanes):
        x = p[:, g * lanes:(g + 1) * lanes]
        partner = jnp.where(is_a, pltpu.roll(x, lanes - ROPE_FREQS, 1), pltpu.roll(x, ROPE_FREQS, 1))
        out.append(x * cos + partner * sin)
    return jnp.concatenate(out, axis=1)


def _local_steps(pbuf, hbuf, wp_ref, ps_ref, ws_ref, bs_ref, pm_ref, v_ref, x1_ref, x2_ref, i, ts, seq):
    t = i * ts + lax.broadcasted_iota(jnp.int32, (ts, 1), 0)

    def pool(g, w):
        cs = slice(g * POOL_GC, (g + 1) * POOL_GC)
        acc = pbuf[HALO - w // 2:HALO - w // 2 + ts, cs]
        for j in range(1 - w // 2, w // 2):
            acc = acc + pbuf[HALO + j:HALO + j + ts, cs]
        cnt = (jnp.minimum(t + w // 2, seq) - jnp.maximum(t - w // 2, 0)).astype(F32)
        pooled = acc / cnt - pbuf[HALO:HALO + ts, cs]
        mixed = _dot(pooled.astype(BF16), wp_ref[g])
        pm_ref[0, :, cs] = mixed * ps_ref[:, cs]

    def conv(o_ref, k):
        c = o_ref.shape[2]
        cs = slice(k * c, (k + 1) * c)
        o_ref[0] = (hbuf[HALO - 1:HALO - 1 + ts, cs] * ws_ref[0:1, cs] + hbuf[HALO:HALO + ts, cs] * ws_ref[1:2, cs]
                    + hbuf[HALO + 1:HALO + 1 + ts, cs] * ws_ref[2:3, cs] + bs_ref[:, cs])

    return ([functools.partial(pool, g, w) for g, w in enumerate(POOL_WINDOWS)]
            + [functools.partial(conv, o_ref, k) for k, o_ref in enumerate((v_ref, x1_ref, x2_ref))])


def _inproj_kernel(*refs, segs, rope, local, tm, seq):
    it = iter(refs)
    h_ref = next(it)
    hp_ref, hn_ref = (next(it), next(it)) if local else (None, None)
    sh_ref, sc_ref, g_ref, w_ref = next(it), next(it), next(it), next(it)
    cos_ref, sin_ref = (next(it), next(it)) if rope else (None, None)
    loc_refs = [next(it) for _ in range(4)] if local else None
    rest = list(it)
    n_out = sum(4 if kind == "local" else 1 for kind, _, _, _ in segs)
    out_refs, scratch = rest[:n_out], rest[n_out:]

    g, sh, sc = g_ref[...], sh_ref[0], sc_ref[0]
    a32 = _norm_mod(h_ref[0], g, sh, sc)
    a = a32.astype(BF16)

    def project(kind, col, width, scale, o_ref, c0):
        p = _dot(a, w_ref[:, col + c0:col + c0 + width])
        if kind == "rope":
            p = _rope(p, cos_ref[...], sin_ref[...])
        elif kind == "sigmoid":
            p = jax.nn.sigmoid(p)
        if scale != 1.0:
            p = p * scale
        o_ref[0, :, c0:c0 + width] = p.astype(o_ref.dtype)

    mxu_work, vpu_work = [], []
    oi = 0
    for kind, col, width, scale in segs:
        if kind == "local":
            i = pl.program_id(1)
            ext = jnp.concatenate([_norm_mod(hp_ref[0], g, sh, sc), a32, _norm_mod(hn_ref[0], g, sh, sc)],
                                  axis=0).astype(BF16)
            u = _dot(ext, w_ref[:, col:col + width])
            row = lax.broadcasted_iota(jnp.int32, (tm + 2 * HALO, 1), 0)
            inside = jnp.logical_and(jnp.logical_or(i > 0, row >= HALO),
                                     jnp.logical_or(i < pl.num_programs(1) - 1, row < HALO + tm))
            u = jnp.where(inside, u, 0.0)
            pbuf, hbuf = scratch
            cp = pbuf.shape[1]
            pbuf[...] = u[:, :cp]
            hbuf[...] = u[:, cp:]
            vpu_work = _local_steps(pbuf, hbuf, *loc_refs, *out_refs[oi:oi + 4], i, tm, seq)
            oi += 4
            continue
        piece = min(width, PROJ_PIECE)
        for c0 in range(0, width, piece):
            mxu_work.append(functools.partial(project, kind, col, piece, scale, out_refs[oi], c0))
        oi += 1
    for k in range(max(len(mxu_work), len(vpu_work))):
        if k < len(vpu_work):
            vpu_work[k]()
        if k < len(mxu_work):
            mxu_work[k]()


def _inproj(h, mod, norm_g, w_cat, segs, out_dtypes, rope_tabs=None, local_params=None, tm=256):
    b, s, d = h.shape
    tm = min(tm, s)
    rope = rope_tabs is not None
    local = local_params is not None
    nh = tm // HALO
    tok = lambda w: pl.BlockSpec((1, tm, w), lambda b_, i: (b_, i, 0))
    in_specs = [tok(d)]
    args = [h]
    if local:
        in_specs += [pl.BlockSpec((1, HALO, d), lambda b_, i: (b_, jnp.maximum(i * nh - 1, 0), 0)),
                     pl.BlockSpec((1, HALO, d), lambda b_, i: (b_, jnp.minimum((i + 1) * nh, s // HALO - 1), 0))]
        args += [h, h]
    in_specs += [_mod_spec(mod, d, 0), _mod_spec(mod, d, 1), _const_spec((1, d)), _const_spec(w_cat.shape)]
    args += [mod, mod, norm_g.reshape(1, d), w_cat]
    if rope:
        wr = rope_tabs[0].shape[1]
        in_specs += [pl.BlockSpec((tm, wr), lambda b_, i: (i, 0))] * 2
        args += list(rope_tabs)
    scratch = []
    if local:
        w_pool, pool_scale, w_short, b_short = local_params
        cp, ch = pool_scale.shape[0], b_short.shape[0]
        in_specs += [_const_spec(w_pool.shape), _const_spec((1, cp)), _const_spec(w_short.shape),
                     _const_spec((1, ch))]
        args += [w_pool.astype(BF16), pool_scale.reshape(1, cp), w_short, b_short.reshape(1, ch)]
        scratch = [pltpu.VMEM((tm + 2 * HALO, cp), F32), pltpu.VMEM((tm + 2 * HALO, ch), F32)]
    widths = []
    for kind, _, w, _ in segs:
        widths += [cp, ch // 3, ch // 3, ch // 3] if kind == "local" else [w]
    out_specs = [tok(w) for w in widths]
    out_shape = [jax.ShapeDtypeStruct((b, s, w), dt) for w, dt in zip(widths, out_dtypes)]
    return pl.pallas_call(
        functools.partial(_inproj_kernel, segs=tuple(segs), rope=rope, local=local, tm=tm, seq=s),
        grid=(b, s // tm), in_specs=in_specs, out_specs=out_specs, out_shape=out_shape,
        scratch_shapes=scratch, compiler_params=_params(2), name="inproj",
    )(*args)


def _lambda(lq_ref, lam_init):
    lq = lq_ref[...]
    return (jnp.exp(jnp.sum(lq[0:1] * lq[1:2], axis=-1, keepdims=True))
            - jnp.exp(jnp.sum(lq[2:3] * lq[3:4], axis=-1, keepdims=True)) + lam_init)


def _split_halves(q):
    lane = lax.broadcasted_iota(jnp.int32, q.shape, 1)
    zero = jnp.zeros_like(q)
    return jnp.concatenate([jnp.where(lane < ATTN_HD, q, zero),
                            jnp.where(lane >= ATTN_HD, q, zero)], axis=0)


def _qk(qq, k):
    return lax.dot_general(qq, k, (((1,), (1,)), ((), ())), preferred_element_type=F32)


def _sub_ln(o, g, lam_init):
    ms = jnp.mean(o * o, axis=-1, keepdims=True)
    return o * lax.rsqrt(ms + EPS) * g * (1.0 - lam_init)


def _attn_kernel(lq_ref, q_ref, k_ref, v_ref, g_ref, o_ref, *, lam_init, tq):
    lam = _lambda(lq_ref, lam_init)
    s = _qk(_split_halves(q_ref[0]), k_ref[0])
    m = jnp.max(s, axis=-1, keepdims=True)
    p = jnp.exp2(s - m)
    l = jnp.sum(p, axis=-1, keepdims=True)
    a = (p[:tq] - p[tq:] * (lam * l[:tq] / l[tq:])).astype(BF16)
    o = _dot(a, v_ref[0]) / l[:tq]
    o_ref[0] = _sub_ln(o, g_ref[...], lam_init).astype(o_ref.dtype)


def _attn_t_kernel(lq_ref, q_ref, k_ref, vt_ref, g_ref, o_ref, s0, s1, m_scr, *, lam_init, tq, lk, kc, nq):
    j = pl.program_id(2)
    bufs = (s0, s1)

    @pl.when(j == 0)
    def _():
        s1[...] = jnp.zeros_like(s1)
        m_scr[1] = jnp.zeros(m_scr.shape[1:], F32)

    lam = _lambda(lq_ref, lam_init)
    for r in range(2):
        sig = 2 * j + r
        buf_a, buf_b = bufs[r], bufs[1 - r]
        t_a = jnp.minimum(sig, nq - 1)
        t_b = jnp.clip(sig - 1, 0, nq - 1)
        qq = _split_halves(q_ref[0, pl.ds(pl.multiple_of(t_a * tq, tq), tq), :])
        m_b = m_scr[1 - r]
        m8 = l8 = acc = None
        for c0 in range(0, lk, kc):
            rows = slice(c0, c0 + kc)
            s = _qk(k_ref[0, rows, :], qq)
            buf_a[rows, :] = s
            for i in range(0, kc, 8):
                m8 = s[i:i + 8] if m8 is None else jnp.maximum(m8, s[i:i + 8])
            p = jnp.exp2(buf_b[rows, :].reshape(kc // 8, 8, 2 * tq) - m_b)
            lc = jnp.sum(p, axis=0)
            l8 = lc if l8 is None else l8 + lc
            pv = _dot(vt_ref[0, :, rows], p.reshape(kc, 2 * tq).astype(BF16))
            acc = pv if acc is None else acc + pv
        m_scr[r] = jnp.broadcast_to(jnp.max(m8, axis=0, keepdims=True), m8.shape)
        l = jnp.sum(l8, axis=0, keepdims=True)
        ot = acc[:, :tq] / l[:, :tq] - acc[:, tq:] * (lam / l[:, tq:])
        ms = jnp.mean(ot * ot, axis=0, keepdims=True)
        ot = ot * lax.rsqrt(ms + EPS) * g_ref[...] * (1.0 - lam_init)
        o_ref[0, pl.ds(pl.multiple_of(t_b * tq, tq), tq), :] = ot.T.astype(o_ref.dtype)


def _attention_t(q, k, vt, lam_qk, subln_g, lam_init, tq=128, kc=768):
    b, s, w = q.shape
    lk = k.shape[1]
    heads = w // ATTN_VD
    nq = s // tq
    nsteps = -(-(nq + 1) // 2)
    return pl.pallas_call(
        functools.partial(_attn_t_kernel, lam_init=lam_init, tq=tq, lk=lk, kc=kc, nq=nq),
        grid=(b, heads, nsteps),
        in_specs=[pl.BlockSpec(lam_qk.shape, lambda b_, h, i: (0, 0)),
                  pl.BlockSpec((1, s, ATTN_VD), lambda b_, h, i: (b_, 0, h)),
                  pl.BlockSpec((1, lk, ATTN_VD), lambda b_, h, i: (b_, 0, h)),
                  pl.BlockSpec((1, ATTN_VD, lk), lambda b_, h, i: (b_, h, 0)),
                  pl.BlockSpec((ATTN_VD, 1), lambda b_, h, i: (0, 0))],
        out_specs=pl.BlockSpec((1, s, ATTN_VD), lambda b_, h, i: (b_, 0, h)),
        out_shape=jax.ShapeDtypeStruct((b, s, w), F32),
        scratch_shapes=[pltpu.VMEM((lk, 2 * tq), F32)] * 2 + [pltpu.VMEM((2, 8, 2 * tq), F32)],
        compiler_params=_params(3), name="diff_attention_t",
    )(lam_qk, q, k, vt, subln_g.reshape(ATTN_VD, 1))


def _attention(q, k, v, lam_qk, subln_g, lam_init, tq=128):
    b, s, w = q.shape
    lk = k.shape[1]
    heads = w // ATTN_VD
    tq = min(tq, s)
    return pl.pallas_call(
        functools.partial(_attn_kernel, lam_init=lam_init, tq=tq),
        grid=(b, heads, s // tq),
        in_specs=[pl.BlockSpec(lam_qk.shape, lambda b_, h, i: (0, 0)),
                  pl.BlockSpec((1, tq, ATTN_VD), lambda b_, h, i: (b_, i, h)),
                  pl.BlockSpec((1, lk, ATTN_VD), lambda b_, h, i: (b_, 0, h)),
                  pl.BlockSpec((1, lk, ATTN_VD), lambda b_, h, i: (b_, 0, h)),
                  pl.BlockSpec((1, ATTN_VD), lambda b_, h, i: (0, 0))],
        out_specs=pl.BlockSpec((1, tq, ATTN_VD), lambda b_, h, i: (b_, i, h)),
        out_shape=jax.ShapeDtypeStruct((b, s, w), F32),
        compiler_params=_params(3), name="diff_attention",
    )(lam_qk, q, k, v, subln_g.reshape(1, ATTN_VD))


def _filter_kernel(z_ref, w1_ref, b1_ref, f1_ref, w2_ref, b2_ref, f2_ref, w3_ref, dl_ref, h_ref, asum_ref, *,
                   tl, l):
    i = pl.program_id(0)
    half = tl // 2
    z = z_ref[...]
    z2 = jnp.concatenate([z[:half], z[half:]], axis=1)
    pre = jnp.dot(z2, w1_ref[...], precision=HIGHEST, preferred_element_type=F32)
    h1 = jnp.sin(f1_ref[...] * (pre + b1_ref[...]))
    h2 = jnp.sin(f2_ref[...] * (jnp.dot(h1, w2_ref[...], precision=HIGHEST,
                                        preferred_element_type=F32) + b2_ref[...]))
    hh = h2.astype(BF16)
    hl = (h2 - hh.astype(F32)).astype(BF16)
    h2s = jnp.concatenate([hh, hl, hh], axis=1)
    part = None
    for j in range(2):
        rows = slice(j * half, (j + 1) * half)
        h3 = _dot(h2s, w3_ref[0, j])
        n = i * tl + j * half + lax.broadcasted_iota(jnp.int32, (half, 1), 0)
        h = jnp.where(n == l, 0.0, h3 * jnp.exp(-z[rows, 0:1] * dl_ref[...]))
        h_ref[rows, :] = h
        pj = jnp.sum(jnp.abs(h), axis=0, keepdims=True)
        part = pj if part is None else part + pj

    @pl.when(i == 0)
    def _():
        asum_ref[...] = jnp.zeros_like(asum_ref)

    asum_ref[...] += part


def _hyena_time_kernel(l, w1, b1, f1, w2, b2, f2, w3):
    ffn = w1.shape[1]
    nc = w3.shape[1] // 2
    c = nc // 2
    tl = min(l, 512)
    w3d = jnp.swapaxes(w3.reshape(ffn, 2, nc), 0, 1)
    n = jnp.arange(2 * l, dtype=jnp.int32)
    t = jnp.where(n < l, n, 2 * l - n).astype(F32)
    bands = jnp.linspace(1e-4, HYENA_BANDS - 1, HYENA_BANDS, dtype=F32)
    ang = 2.0 * math.pi * t[:, None] * bands[None, :] / l
    emb = 2 * HYENA_BANDS + 1
    z = jnp.concatenate([(t / (l - 1))[:, None], jnp.cos(ang), jnp.sin(ang),
                         jnp.zeros((2 * l, 128 - emb), F32)], axis=1)
    w1p = jnp.zeros((128, ffn), F32).at[:emb].set(w1)
    zf = jnp.zeros((ffn, ffn), F32)
    w1b = jnp.concatenate([jnp.concatenate([w1p, jnp.zeros_like(w1p)], 1),
                           jnp.concatenate([jnp.zeros_like(w1p), w1p], 1)], 0)
    w2b = jnp.concatenate([jnp.concatenate([w2, zf], 1), jnp.concatenate([zf, w2], 1)], 0)
    w3hi = w3d.astype(BF16)
    w3lo = (w3d - w3hi.astype(F32)).astype(BF16)
    zw = jnp.zeros_like(w3hi)
    first = jnp.concatenate([w3hi, zw, w3hi, zw, w3lo, zw], axis=1)
    second = jnp.concatenate([zw, w3hi, zw, w3hi, zw, w3lo], axis=1)
    w3s = jnp.stack([first, second], axis=1)
    max_decay = math.log(HYENA_TARGET) / HYENA_FAST_DECAY
    min_decay = math.log(HYENA_TARGET) / HYENA_SLOW_DECAY
    deltas = jnp.abs(jnp.linspace(min_decay, max_decay, c, dtype=F32))
    dl = jnp.tile(deltas, 2).reshape(1, nc)
    row2 = lambda a: jnp.tile(a.reshape(1, -1), (1, 2))
    nfwd = l // tl
    kt, asum = pl.pallas_call(
        functools.partial(_filter_kernel, tl=tl, l=l),
        grid=(2 * nfwd,),
        in_specs=[pl.BlockSpec((tl, 128), lambda i: (i, 0)), _const_spec(w1b.shape),
                  _const_spec((1, 2 * ffn)), _const_spec((1, 2 * ffn)),
                  _const_spec(w2b.shape), _const_spec((1, 2 * ffn)), _const_spec((1, 2 * ffn)),
                  pl.BlockSpec((1,) + w3s.shape[1:], lambda i: (i // nfwd, 0, 0, 0)), _const_spec((1, nc))],
        out_specs=[pl.BlockSpec((tl, nc), lambda i: (i, 0)), pl.BlockSpec((1, nc), lambda i: (0, 0))],
        out_shape=[jax.ShapeDtypeStruct((2 * l, nc), F32), jax.ShapeDtypeStruct((1, nc), F32)],
        compiler_params=_params(1), name="hyena_filter_mlp",
    )(z, w1b, row2(b1), row2(f1), w2b, row2(b2), row2(f2), w3s, dl)
    return kt, 1.0 / asum


def _angles(prod, n):
    th = (2.0 * math.pi / n) * (prod % n).astype(F32)
    return jnp.cos(th), jnp.sin(th)


def _dft_tables():
    r = DFT_R
    n = r * r
    idx = jnp.arange(r, dtype=jnp.int32)
    c, s = _angles(idx[:, None] * idx[None, :], r)
    ch, sh = c[:, :r // 2], s[:, :r // 2]
    w1_data = jnp.concatenate([jnp.concatenate([ch, sh], 1), jnp.concatenate([-sh, ch], 1)], 0)
    pad = jnp.zeros((HALF_ROWS - (r // 2 + 1), r), F32)
    w1_real = jnp.concatenate([c[:r // 2 + 1], pad, -s[:r // 2 + 1], pad], 0)
    ct, st = c[:r // 2], s[:r // 2]
    w2 = jnp.concatenate([jnp.concatenate([ct, -st], 1), jnp.concatenate([st, ct], 1)], 0)
    k = idx[:, None, None] + r * idx[None, :, None]
    cg, sg = _angles(k * idx[None, None, :], n)
    g = jnp.concatenate([jnp.concatenate([cg, sg], 2), jnp.concatenate([-sg, cg], 2)], 1)
    dp = DFT_DATA_PASSES
    fp = DFT_FILTER_PASSES
    return dict(w1_data=_split3(w1_data, dp), w1_real=_split3(w1_real, fp), w2=_split3(w2, dp),
                g=_split3(g, dp), g_filter=_split3(g, fp))


N2_TILE = 16
K1_TILE = 4


def _dft1_kernel(x_ref, w_ref, o_ref, *, parts):
    r = o_ref.shape[2]
    for i in range(N2_TILE):
        x = jnp.concatenate([x_ref[p, :, i, :] for p in range(parts)], axis=0)
        y = _mm3(w_ref[...], x)
        o_ref[0, 0, :, i, :] = y[:r]
        o_ref[0, 1, :, i, :] = y[r:]


def _dft_major(x, w3, parts):
    b, n1, r, c = x.shape
    k1 = w3.shape[0] // 2
    return pl.pallas_call(
        functools.partial(_dft1_kernel, parts=parts),
        grid=(b // parts, r // N2_TILE),
        in_specs=[pl.BlockSpec((parts, n1, N2_TILE, c), lambda p_, j: (p_, 0, j, 0)), _const_spec(w3.shape)],
        out_specs=pl.BlockSpec((1, 2, k1, N2_TILE, c), lambda p_, j: (p_, 0, 0, j, 0)),
        out_shape=jax.ShapeDtypeStruct((b // parts, 2, k1, r, c), F32),
        compiler_params=_params(2), name="dft_major",
    )(x, w3)


def _spectrum_kernel(a_ref, g_ref, sc_ref, o_ref, *, nyquist):
    sign = jnp.where(pl.program_id(0) <= nyquist, 1.0, -1.0)
    a = jnp.concatenate([a_ref[0, 0, 0], a_ref[0, 1, 0] * sign], axis=0)
    x = _mm3(g_ref[0], a) * sc_ref[...]
    r = x.shape[0] // 2
    o_ref[0, 0] = x[:r]
    o_ref[1, 0] = x[r:]


def _filter_spectrum(kt, scale, tabs):
    r = DFT_R
    c = kt.shape[1]
    a = _dft_major(kt.reshape(1, r, r, c), tabs["w1_real"], 1)
    tcs = c
    return pl.pallas_call(
        functools.partial(_spectrum_kernel, nyquist=r // 2),
        grid=(r, c // tcs),
        in_specs=[pl.BlockSpec((1, 2, 1, r, tcs), lambda k, j: (0, 0, jnp.where(k <= r // 2, k, r - k), 0, j)),
                  pl.BlockSpec((1,) + tabs["g_filter"].shape[1:], lambda k, j: (k, 0, 0)),
                  pl.BlockSpec((1, tcs), lambda k, j: (0, j))],
        out_specs=pl.BlockSpec((2, 1, r, tcs), lambda k, j: (0, k, 0, j)),
        out_shape=jax.ShapeDtypeStruct((2, r, r, c), F32),
        compiler_params=_params(2), name="filter_spectrum",
    )(a, tabs["g_filter"], scale)


def _mm3t(w3, x):
    assert w3.shape[0] == x.shape[0], "transposed product is only used in single-pass mode"
    return lax.dot_general(w3, x.astype(BF16), (((0,), (0,)), ((), ())), preferred_element_type=F32)


def _dft2_kernel(a_ref, g_ref, h_ref, o_ref):
    for kk in range(K1_TILE):
        hr, hi = h_ref[0, kk], h_ref[1, kk]
        for p in range(a_ref.shape[0]):
            a = jnp.concatenate([a_ref[p, 0, kk], a_ref[p, 1, kk]], axis=0)
            x = _mm3(g_ref[kk], a)
            r = x.shape[0] // 2
            xr, xi = x[:r], x[r:]
            y = jnp.concatenate([xr * hr - xi * hi, xr * hi + xi * hr], axis=0)
            bb = _mm3t(g_ref[kk], y)
            o_ref[p, 0, kk] = bb[:r]
            o_ref[p, 1, kk] = bb[r:]


def _dft3_kernel(b_ref, w_ref, u_ref, x_ref, bias_ref, o_ref):
    bias = bias_ref[...]
    for i in range(N2_TILE):
        bb = jnp.concatenate([b_ref[0, 0, :, i, :], b_ref[0, 1, :, i, :]], axis=0)
        y = _mm3(w_ref[...], bb)
        half = y.shape[0] // 2
        for h in range(2):
            o_ref[h, :, i, :] = x_ref[h, :, i, :] * (y[h * half:(h + 1) * half] + u_ref[h, :, i, :] * bias)


def _long_conv_gate(u, x, hspec, order, bias, tabs):
    b, l, c = u.shape
    r = DFT_R
    p = b // 2
    nat = (b, r // 2, r, c)
    a = _dft_major(u.reshape(nat), tabs["w1_data"], 2)
    spec5 = pl.BlockSpec((p, 2, K1_TILE, r, c), lambda k: (0, 0, k, 0, 0))
    bb = pl.pallas_call(
        _dft2_kernel,
        grid=(r // K1_TILE,),
        in_specs=[spec5,
                  pl.BlockSpec((K1_TILE,) + tabs["g"].shape[1:], lambda k: (k, 0, 0)),
                  pl.BlockSpec((2, K1_TILE, r, c), lambda k: (0, k, 0, order))],
        out_specs=spec5,
        out_shape=jax.ShapeDtypeStruct((p, 2, r, r, c), F32),
        compiler_params=_params(1), name="dft_minor_filter",
    )(a, tabs["g"], hspec)
    blk = pl.BlockSpec((2, r // 2, N2_TILE, c), lambda p_, j: (p_, 0, j, 0))
    out = pl.pallas_call(
        _dft3_kernel,
        grid=(p, r // N2_TILE),
        in_specs=[pl.BlockSpec((1, 2, r, N2_TILE, c), lambda p_, j: (p_, 0, 0, j, 0)),
                  _const_spec(tabs["w2"].shape), blk, blk, _const_spec((1, c))],
        out_specs=blk,
        out_shape=jax.ShapeDtypeStruct(nat, F32),
        compiler_params=_params(2), name="idft_major_gate",
    )(bb, tabs["w2"], u.reshape(nat), x.reshape(nat), bias.reshape(1, c))
    return out.reshape(b, l, c)


def _dense_tables(l):
    n = 2 * l
    k = jnp.arange(n, dtype=jnp.int32)
    t = jnp.arange(l, dtype=jnp.int32)
    c, s = _angles(k[:, None] * t[None, :], n)
    wf = jnp.concatenate([jnp.concatenate([c, s], 1), jnp.concatenate([-s, c], 1)], 0)
    ct, st = c.T, s.T
    wi = jnp.concatenate([jnp.concatenate([ct, -st], 1), jnp.concatenate([st, ct], 1)], 0)
    ca, sa = _angles(k[:, None] * k[None, :], n)
    wk = jnp.concatenate([ca, -sa], 0)
    dp = DFT_DATA_PASSES
    return dict(wf=_split3(wf, dp), wi=_split3(wi, dp), wk=_split3(wk, DFT_FILTER_PASSES))


def _dense_spectrum_kernel(k_ref, w_ref, sc_ref, o_ref):
    o_ref[...] = _mm3(w_ref[...], k_ref[...]) * sc_ref[...]


def _dense_spectrum(kt, scale, wk):
    n, c = kt.shape
    return pl.pallas_call(
        _dense_spectrum_kernel,
        grid=(1,),
        in_specs=[_const_spec((n, c)), _const_spec(wk.shape), _const_spec((1, c))],
        out_specs=pl.BlockSpec((2 * n, c), lambda i: (0, 0)),
        out_shape=jax.ShapeDtypeStruct((2 * n, c), F32),
        compiler_params=_params(1), name="dense_filter_spectrum",
    )(kt, wk, scale)


def _dense_conv_kernel(u_ref, x_ref, wf_ref, wi_ref, h_ref, bias_ref, o_ref):
    u0, u1 = u_ref[0], u_ref[1]
    z = _mm3(wf_ref[...], jnp.concatenate([u0, u1], axis=0))
    n = z.shape[0] // 2
    zr, zi = z[:n], z[n:]
    hr, hi = h_ref[:n], h_ref[n:]
    y = _mm3(wi_ref[...], jnp.concatenate([zr * hr - zi * hi, zr * hi + zi * hr], axis=0))
    l = y.shape[0] // 2
    bias = bias_ref[...]
    o_ref[0] = x_ref[0] * (y[:l] + u0 * bias)
    o_ref[1] = x_ref[1] * (y[l:] + u1 * bias)


def _dense_conv_gate(u, x, hspec, order, bias, tabs):
    b, l, c = u.shape
    blk = pl.BlockSpec((2, l, c), lambda p_: (p_, 0, 0))
    return pl.pallas_call(
        _dense_conv_kernel,
        grid=(b // 2,),
        in_specs=[blk, blk, _const_spec(tabs["wf"].shape), _const_spec(tabs["wi"].shape),
                  pl.BlockSpec((4 * l, c), lambda p_: (0, order)), _const_spec((1, c))],
        out_specs=blk,
        out_shape=jax.ShapeDtypeStruct((b, l, c), F32),
        compiler_params=_params(1), name="dense_conv_gate",
    )(u, x, tabs["wf"], tabs["wi"], hspec, bias.reshape(1, c))


def _mix_kernel(h_ref, o_ref, pm_ref, y_ref, g_ref, gate_ref, wa_ref, wp_ref, wh_ref, wo_ref, out_ref):
    d = h_ref.shape[2]
    a = _dot(o_ref[0].astype(BF16), wa_ref[...])
    p = _dot(pm_ref[0].astype(BF16), wp_ref[...])
    y = _dot(y_ref[0].astype(BF16), wh_ref[...])
    merged = g_ref[0, :, 0:d] * a + g_ref[0, :, d:2 * d] * p + g_ref[0, :, 2 * d:3 * d] * y
    out = _dot(merged.astype(BF16), wo_ref[...])
    out_ref[0] = h_ref[0] + gate_ref[0] * out


def _mix(h, o, pm, y, g, mod, w_attn_o, w_pool_o, w_hy_o, w_out, tm=512):
    b, s, d = h.shape
    tm = min(tm, s)

    def tok(w):
        return pl.BlockSpec((1, tm, w), lambda b_, i: (b_, i, 0))

    ws = [w.astype(BF16) for w in (w_attn_o, w_pool_o, w_hy_o, w_out)]
    return pl.pallas_call(
        _mix_kernel,
        grid=(b, s // tm),
        in_specs=[tok(d), tok(o.shape[2]), tok(pm.shape[2]), tok(y.shape[2]), tok(g.shape[2]),
                  _mod_spec(mod, d, 2)] + [_const_spec(w.shape) for w in ws],
        out_specs=tok(d),
        out_shape=jax.ShapeDtypeStruct((b, s, d), F32),
        compiler_params=_params(2), name="merge_outproj",
    )(h, o, pm, y, g, mod, *ws)


def _ffn_kernel(h_ref, sh_ref, sc_ref, gate_ref, g_ref, wi_ref, wo_ref, fg_ref, out_ref, *,
                hidden, chunk, final):
    x = h_ref[0]
    a = _norm_mod(x, g_ref[...], sh_ref[0], sc_ref[0]).astype(BF16)
    acc = jnp.zeros(x.shape, F32)
    for c0 in range(0, hidden, chunk):
        gt = _dot(a, wi_ref[:, c0:c0 + chunk])
        up = _dot(a, wi_ref[:, hidden + c0:hidden + c0 + chunk])
        act = (gt * jax.nn.sigmoid(gt) * up).astype(BF16)
        acc = acc + _dot(act, wo_ref[c0:c0 + chunk, :])
    y = x + gate_ref[0] * acc
    if final:
        ms = jnp.mean(y * y, axis=-1, keepdims=True)
        y = y * lax.rsqrt(ms + EPS) * fg_ref[...]
    out_ref[0] = y


def _ffn(h, mod, norm_g, w_in, w_out, final_g, final, tm=512):
    b, s, d = h.shape
    hidden = w_out.shape[0]
    tm = min(tm, s)
    tok = pl.BlockSpec((1, tm, d), lambda b_, i: (b_, i, 0))
    return pl.pallas_call(
        functools.partial(_ffn_kernel, hidden=hidden, chunk=hidden // 2, final=final),
        grid=(b, s // tm),
        in_specs=[tok, _mod_spec(mod, d, 3), _mod_spec(mod, d, 4), _mod_spec(mod, d, 5),
                  _const_spec((1, d)), _const_spec(w_in.shape), _const_spec(w_out.shape),
                  _const_spec((1, d))],
        out_specs=tok,
        out_shape=jax.ShapeDtypeStruct((b, s, d), F32),
        compiler_params=_params(2), name="swiglu_ffn",
    )(h, mod, mod, mod, norm_g.reshape(1, d), w_in.astype(BF16), w_out.astype(BF16),
      final_g.reshape(1, d))


def _rope_tables(seq, width):
    t = jnp.arange(seq, dtype=jnp.int32)
    row = (t // GRID_W).astype(F32)
    col = (t % GRID_W).astype(F32)
    inv = 1.0 / (ROPE_BASE ** (jnp.arange(ROPE_FREQS, dtype=F32) * 2.0 / (2 * ROPE_FREQS)))
    ar, ac = row[:, None] * inv, col[:, None] * inv
    cos = jnp.concatenate([jnp.cos(ar)] * 2 + [jnp.cos(ac)] * 2, axis=1)
    sin = jnp.concatenate([-jnp.sin(ar), jnp.sin(ar), -jnp.sin(ac), jnp.sin(ac)], axis=1)
    reps = width // cos.shape[1]
    return jnp.tile(cos, (1, reps)), jnp.tile(sin, (1, reps))


def kernel(x, c, ctx, c_ctx, w_mod, b_mod, norm1_g, norm2_g, w_in, lam_qk, subln_g, w_attn_o, w_pool, pool_scale, w_pool_o, w_short, b_short, hf_w1, hf_b1, hf_freq1, hf_w2, hf_b2, hf_freq2, hf_w3, hy_bias, w_hy_o, w_out, w_ffn_in, w_ffn_out, final_g):
    batch, seq, d = x.shape
    depth = w_mod.shape[0]
    ctx_len = ctx.shape[1]
    aw = w_attn_o.shape[1]
    pw = w_pool_o.shape[1]
    hw = w_hy_o.shape[1]
    q0, k0, v0, p0 = 0, aw, 2 * aw, 3 * aw
    h0 = p0 + pw
    g0 = h0 + 3 * hw
    assert seq * 2 == DFT_R * DFT_R and batch % 2 == 0

    c8 = jnp.zeros((8, d), F32).at[:batch].set(c).at[batch].set(c_ctx)
    mod_all = _modulation(c8, w_mod, b_mod)
    rope_tabs = _rope_tables(seq, ATTN_VD)
    tabs = _dft_tables()
    ctx_tabs = _dense_tables(ctx_len)

    h_lat, h_ctx = x, ctx
    for l in range(depth):
        last = l == depth - 1
        lam_init = LAMBDA_INIT_BASE - LAMBDA_INIT_AMP * math.exp(-LAMBDA_INIT_RATE * l)
        mod = mod_all[l, :batch].reshape(batch, 1, N_MOD * d)
        mod_c = mod_all[l, batch:batch + 1].reshape(1, 1, N_MOD * d)
        wl = w_in[l]
        fparams = (hf_w1[l], hf_b1[l], hf_freq1[l], hf_w2[l], hf_b2[l], hf_freq2[l], hf_w3[l])

        wl_bf = wl.astype(BF16)
        local_params = (w_pool[l], pool_scale[l], w_short[l], b_short[l])
        out_dt = [F32, F32, F32, F32, F32, BF16, BF16, BF16]
        local_seg = ("local", p0, pw + 3 * hw, 1.0)
        gate_seg = ("sigmoid", g0, 3 * d, 1.0)
        segs = [local_seg, gate_seg, ("rope", q0, aw, Q_SCALE), ("rope", k0, aw, 1.0), ("plain", v0, aw, 1.0)]
        pm, vv, x1, x2, g, q, k, v = _inproj(h_lat, mod, norm1_g[l], wl_bf, segs, out_dt, rope_tabs, local_params)

        if last:
            segs_c = [("plain", k0, aw, 1.0), ("plain", v0, aw, 1.0)]
            k_c, v_c = _inproj(h_ctx, mod_c, norm1_g[l], wl_bf, segs_c, [BF16, BF16])
        else:
            segs_c = [local_seg, gate_seg, ("plain", q0, aw, Q_SCALE), ("plain", k0, aw, 1.0), ("plain", v0, aw, 1.0)]
            pm_c, vv_c, x1_c, x2_c, g_c, q_c, k_c, v_c = _inproj(h_ctx, mod_c, norm1_g[l], wl_bf, segs_c, out_dt,
                                                                 None, local_params)
            o_c = _attention(q_c, k_c, v_c, lam_qk[l], subln_g[l], lam_init)
            kt_c, inv_norm_c = _hyena_time_kernel(ctx_len, *fparams)
            hspec_c = _dense_spectrum(kt_c, inv_norm_c / (2 * ctx_len), ctx_tabs["wk"])
            z_c = _dense_conv_gate(vv_c, x1_c, hspec_c, 0, hy_bias[l, 0], ctx_tabs)
            y_c = _dense_conv_gate(z_c, x2_c, hspec_c, 1, hy_bias[l, 1], ctx_tabs)
            h_ctx_mid = _mix(h_ctx, o_c, pm_c, y_c, g_c, mod_c, w_attn_o[l], w_pool_o[l], w_hy_o[l], w_out[l])
            h_ctx_new = _ffn(h_ctx_mid, mod_c, norm2_g[l], w_ffn_in[l], w_ffn_out[l], final_g, False)

        vt = jnp.swapaxes(jnp.concatenate([v_c, v], axis=1), 1, 2)
        o = _attention_t(q, jnp.concatenate([k_c, k], axis=1), vt, lam_qk[l], subln_g[l], lam_init)
        kt, inv_norm = _hyena_time_kernel(seq, *fparams)
        hspec = _filter_spectrum(kt, inv_norm / (2 * seq), tabs)
        z = _long_conv_gate(vv, x1, hspec, 0, hy_bias[l, 0], tabs)
        y = _long_conv_gate(z, x2, hspec, 1, hy_bias[l, 1], tabs)
        h_mid = _mix(h_lat, o, pm, y, g, mod, w_attn_o[l], w_pool_o[l], w_hy_o[l], w_out[l])
        h_lat = _ffn(h_mid, mod, norm2_g[l], w_ffn_in[l], w_ffn_out[l], final_g, last)
        if not last:
            h_ctx = h_ctx_new
    return h_lat
```

```python
import functools
import math

import jax
import jax.numpy as jnp
from jax import lax
from jax.experimental import pallas as pl
from jax.experimental.pallas import tpu as pltpu

F32 = jnp.float32
BF16 = jnp.bfloat16
HIGHEST = lax.Precision.HIGHEST

GRID_W = 64
N_MOD = 6
ATTN_HD = 64
ATTN_VD = 128
Q_SCALE = ATTN_HD ** -0.5 * math.log2(math.e)
ROPE_BASE = 10000.0
ROPE_FREQS = 16
LAMBDA_INIT_BASE = 0.8
LAMBDA_INIT_AMP = 0.6
LAMBDA_INIT_RATE = 0.3
POOL_WINDOWS = (2, 4, 8, 16)
POOL_GC = 128
HALO = 8
PROJ_PIECE = 1024
HYENA_BANDS = 16
HYENA_FAST_DECAY = 0.3
HYENA_SLOW_DECAY = 1.5
HYENA_TARGET = 1e-2
EPS = 1e-6
DFT_DATA_PASSES = 1
DFT_FILTER_PASSES = 1
DFT_R = 128
HALF_ROWS = 72

VMEM_LIMIT = 56 * 1024 * 1024


def _params(n_grid, flags=None):
    return pltpu.CompilerParams(dimension_semantics=("arbitrary",) * n_grid,
                                vmem_limit_bytes=VMEM_LIMIT, flags=flags)


def _const_spec(shape):
    zeros = (0,) * len(shape)
    return pl.BlockSpec(shape, lambda *_: zeros, pipeline_mode=pl.Buffered(1))


def _dot(a, b):
    return jnp.dot(a, b, preferred_element_type=F32)


def _split3(w, passes=3):
    hi = w.astype(BF16)
    if passes == 1:
        return hi
    lo = (w - hi.astype(F32)).astype(BF16)
    return jnp.concatenate([hi, hi, lo], axis=-1)


def _mm3(w3, x):
    xh = x.astype(BF16)
    if w3.shape[-1] == x.shape[0]:
        return _dot(w3, xh)
    xl = (x - xh.astype(F32)).astype(BF16)
    return _dot(w3, jnp.concatenate([xh, xl, xh], axis=0))


def _mod_kernel(c_ref, w_ref, b_ref, o_ref):
    c = c_ref[...]
    act = c * jax.nn.sigmoid(c)
    o_ref[0] = jnp.dot(act, w_ref[0], precision=HIGHEST, preferred_element_type=F32) + b_ref[0]


def _modulation(c8, w_mod, b_mod):
    depth, d, n = w_mod.shape
    tn = 1536
    return pl.pallas_call(
        _mod_kernel,
        grid=(depth, n // tn),
        in_specs=[pl.BlockSpec((8, d), lambda l, j: (0, 0)),
                  pl.BlockSpec((1, d, tn), lambda l, j: (l, 0, j)),
                  pl.BlockSpec((1, 1, tn), lambda l, j: (l, 0, j))],
        out_specs=pl.BlockSpec((1, 8, tn), lambda l, j: (l, 0, j)),
        out_shape=jax.ShapeDtypeStruct((depth, 8, n), F32),
        compiler_params=_params(2),
        name="modulation",
    )(c8, w_mod, b_mod.reshape(depth, 1, n))


def _mod_spec(mod, d, col):
    if mod.shape[0] == 1:
        return pl.BlockSpec((1, 1, d), lambda b, i: (0, 0, col))
    return pl.BlockSpec((1, 1, d), lambda b, i: (b, 0, col))


def _norm_mod(x, g, shift, scale):
    ms = jnp.mean(x * x, axis=-1, keepdims=True)
    a = x * lax.rsqrt(ms + EPS) * g
    return a * (1.0 + scale) + shift


def _rope(p, cos, sin):
    lanes = cos.shape[1]
    lane = lax.broadcasted_iota(jnp.int32, cos.shape, 1)
    is_a = lane % (2 * ROPE_FREQS) < ROPE_FREQS
    out = []
    for g in range(p.shape[1] // lanes):
        x = p[:, g * lanes:(g + 1) * lanes]
        partner = jnp.where(is_a, pltpu.roll(x, lanes - ROPE_FREQS, 1), pltpu.roll(x, ROPE_FREQS, 1))
        out.append(x * cos + partner * sin)
    return jnp.concatenate(out, axis=1)


def _local_steps(pbuf, hbuf, wp_ref, ps_ref, ws_ref, bs_ref, pm_ref, v_ref, x1_ref, x2_ref, i, ts, seq):
    t = i * ts + lax.broadcasted_iota(jnp.int32, (ts, 1), 0)

    def pool(g, w):
        cs = slice(g * POOL_GC, (g + 1) * POOL_GC)
        acc = pbuf[HALO - w // 2:HALO - w // 2 + ts, cs]
        for j in range(1 - w // 2, w // 2):
            acc = acc + pbuf[HALO + j:HALO + j + ts, cs]
        cnt = (jnp.minimum(t + w // 2, seq) - jnp.maximum(t - w // 2, 0)).astype(F32)
        pooled = acc / cnt - pbuf[HALO:HALO + ts, cs]
        mixed = _dot(pooled.astype(BF16), wp_ref[g])
        pm_ref[0, :, cs] = mixed * ps_ref[:, cs]

    def conv(o_ref, k):
        c = o_ref.shape[2]
        cs = slice(k * c, (k + 1) * c)
        o_ref[0] = (hbuf[HALO - 1:HALO - 1 + ts, cs] * ws_ref[0:1, cs] + hbuf[HALO:HALO + ts, cs] * ws_ref[1:2, cs]
                    + hbuf[HALO + 1:HALO + 1 + ts, cs] * ws_ref[2:3, cs] + bs_ref[:, cs])

    return ([functools.partial(pool, g, w) for g, w in enumerate(POOL_WINDOWS)]
            + [functools.partial(conv, o_ref, k) for k, o_ref in enumerate((v_ref, x1_ref, x2_ref))])


def _inproj_kernel(*refs, segs, rope, local, tm, seq):
    it = iter(refs)
    h_ref = next(it)
    hp_ref, hn_ref = (next(it), next(it)) if local else (None, None)
    sh_ref, sc_ref, g_ref, w_ref = next(it), next(it), next(it), next(it)
    cos_ref, sin_ref = (next(it), next(it)) if rope else (None, None)
    loc_refs = [next(it) for _ in range(4)] if local else None
    rest = list(it)
    n_out = sum(4 if kind == "local" else 1 for kind, _, _, _ in segs)
    out_refs, scratch = rest[:n_out], rest[n_out:]

    g, sh, sc = g_ref[...], sh_ref[0], sc_ref[0]
    a32 = _norm_mod(h_ref[0], g, sh, sc)
    a = a32.astype(BF16)

    def project(kind, col, width, scale, o_ref, c0):
        p = _dot(a, w_ref[:, col + c0:col + c0 + width])
        if kind == "rope":
            p = _rope(p, cos_ref[...], sin_ref[...])
        elif kind == "sigmoid":
            p = jax.nn.sigmoid(p)
        if scale != 1.0:
            p = p * scale
        o_ref[0, :, c0:c0 + width] = p.astype(o_ref.dtype)

    mxu_work, vpu_work = [], []
    oi = 0
    for kind, col, width, scale in segs:
        if kind == "local":
            i = pl.program_id(1)
            ext = jnp.concatenate([_norm_mod(hp_ref[0], g, sh, sc), a32, _norm_mod(hn_ref[0], g, sh, sc)],
                                  axis=0).astype(BF16)
            u = _dot(ext, w_ref[:, col:col + width])
            row = lax.broadcasted_iota(jnp.int32, (tm + 2 * HALO, 1), 0)
            inside = jnp.logical_and(jnp.logical_or(i > 0, row >= HALO),
                                     jnp.logical_or(i < pl.num_programs(1) - 1, row < HALO + tm))
            u = jnp.where(inside, u, 0.0)
            pbuf, hbuf = scratch
            cp = pbuf.shape[1]
            pbuf[...] = u[:, :cp]
            hbuf[...] = u[:, cp:]
            vpu_work = _local_steps(pbuf, hbuf, *loc_refs, *out_refs[oi:oi + 4], i, tm, seq)
            oi += 4
            continue
        piece = min(width, PROJ_PIECE)
        for c0 in range(0, width, piece):
            mxu_work.append(functools.partial(project, kind, col, piece, scale, out_refs[oi], c0))
        oi += 1
    for k in range(max(len(mxu_work), len(vpu_work))):
        if k < len(vpu_work):
            vpu_work[k]()
        if k < len(mxu_work):
            mxu_work[k]()


def _inproj(h, mod, norm_g, w_cat, segs, out_dtypes, rope_tabs=None, local_params=None, tm=256):
    b, s, d = h.shape
    tm = min(tm, s)
    rope = rope_tabs is not None
    local = local_params is not None
    nh = tm // HALO
    tok = lambda w: pl.BlockSpec((1, tm, w), lambda b_, i: (b_, i, 0))
    in_specs = [tok(d)]
    args = [h]
    if local:
        in_specs += [pl.BlockSpec((1, HALO, d), lambda b_, i: (b_, jnp.maximum(i * nh - 1, 0), 0)),
                     pl.BlockSpec((1, HALO, d), lambda b_, i: (b_, jnp.minimum((i + 1) * nh, s // HALO - 1), 0))]
        args += [h, h]
    in_specs += [_mod_spec(mod, d, 0), _mod_spec(mod, d, 1), _const_spec((1, d)), _const_spec(w_cat.shape)]
    args += [mod, mod, norm_g.reshape(1, d), w_cat]
    if rope:
        wr = rope_tabs[0].shape[1]
        in_specs += [pl.BlockSpec((tm, wr), lambda b_, i: (i, 0))] * 2
        args += list(rope_tabs)
    scratch = []
    if local:
        w_pool, pool_scale, w_short, b_short = local_params
        cp, ch = pool_scale.shape[0], b_short.shape[0]
        in_specs += [_const_spec(w_pool.shape), _const_spec((1, cp)), _const_spec(w_short.shape),
                     _const_spec((1, ch))]
        args += [w_pool.astype(BF16), pool_scale.reshape(1, cp), w_short, b_short.reshape(1, ch)]
        scratch = [pltpu.VMEM((tm + 2 * HALO, cp), F32), pltpu.VMEM((tm + 2 * HALO, ch), F32)]
    widths = []
    for kind, _, w, _ in segs:
        widths += [cp, ch // 3, ch // 3, ch // 3] if kind == "local" else [w]
    out_specs = [tok(w) for w in widths]
    out_shape = [jax.ShapeDtypeStruct((b, s, w), dt) for w, dt in zip(widths, out_dtypes)]
    return pl.pallas_call(
        functools.partial(_inproj_kernel, segs=tuple(segs), rope=rope, local=local, tm=tm, seq=s),
        grid=(b, s // tm), in_specs=in_specs, out_specs=out_specs, out_shape=out_shape,
        scratch_shapes=scratch, compiler_params=_params(2), name="inproj",
    )(*args)


def _lambda(lq_ref, lam_init):
    lq = lq_ref[...]
    return (jnp.exp(jnp.sum(lq[0:1] * lq[1:2], axis=-1, keepdims=True))
            - jnp.exp(jnp.sum(lq[2:3] * lq[3:4], axis=-1, keepdims=True)) + lam_init)


def _split_halves(q):
    lane = lax.broadcasted_iota(jnp.int32, q.shape, 1)
    zero = jnp.zeros_like(q)
    return jnp.concatenate([jnp.where(lane < ATTN_HD, q, zero),
                            jnp.where(lane >= ATTN_HD, q, zero)], axis=0)


def _qk(qq, k):
    return lax.dot_general(qq, k, (((1,), (1,)), ((), ())), preferred_element_type=F32)


def _sub_ln(o, g, lam_init):
    ms = jnp.mean(o * o, axis=-1, keepdims=True)
    return o * lax.rsqrt(ms + EPS) * g * (1.0 - lam_init)


def _attn_kernel(lq_ref, q_ref, k_ref, v_ref, g_ref, o_ref, *, lam_init, tq):
    lam = _lambda(lq_ref, lam_init)
    s = _qk(_split_halves(q_ref[0]), k_ref[0])
    m = jnp.max(s, axis=-1, keepdims=True)
    p = jnp.exp2(s - m)
    l = jnp.sum(p, axis=-1, keepdims=True)
    a = (p[:tq] - p[tq:] * (lam * l[:tq] / l[tq:])).astype(BF16)
    o = _dot(a, v_ref[0]) / l[:tq]
    o_ref[0] = _sub_ln(o, g_ref[...], lam_init).astype(o_ref.dtype)


def _attn_t_kernel(lq_ref, q_ref, k_ref, vt_ref, g_ref, o_ref, s0, s1, m_scr, *, lam_init, tq, lk, kc, nq):
    j = pl.program_id(2)
    bufs = (s0, s1)

    @pl.when(j == 0)
    def _():
        s1[...] = jnp.zeros_like(s1)
        m_scr[1] = jnp.zeros(m_scr.shape[1:], F32)

    lam = _lambda(lq_ref, lam_init)
    for r in range(2):
        sig = 2 * j + r
        buf_a, buf_b = bufs[r], bufs[1 - r]
        t_a = jnp.minimum(sig, nq - 1)
        t_b = jnp.clip(sig - 1, 0, nq - 1)
        qq = _split_halves(q_ref[0, pl.ds(pl.multiple_of(t_a * tq, tq), tq), :])
        m_b = m_scr[1 - r]
        m8 = l8 = acc = None
        for c0 in range(0, lk, kc):
            rows = slice(c0, c0 + kc)
            s = _qk(k_ref[0, rows, :], qq)
            buf_a[rows, :] = s
            for i in range(0, kc, 8):
                m8 = s[i:i + 8] if m8 is None else jnp.maximum(m8, s[i:i + 8])
            p = jnp.exp2(buf_b[rows, :].reshape(kc // 8, 8, 2 * tq) - m_b)
            lc = jnp.sum(p, axis=0)
            l8 = lc if l8 is None else l8 + lc
            pv = _dot(vt_ref[0, :, rows], p.reshape(kc, 2 * tq).astype(BF16))
            acc = pv if acc is None else acc + pv
        m_scr[r] = jnp.broadcast_to(jnp.max(m8, axis=0, keepdims=True), m8.shape)
        l = jnp.sum(l8, axis=0, keepdims=True)
        ot = acc[:, :tq] / l[:, :tq] - acc[:, tq:] * (lam / l[:, tq:])
        ms = jnp.mean(ot * ot, axis=0, keepdims=True)
        ot = ot * lax.rsqrt(ms + EPS) * g_ref[...] * (1.0 - lam_init)
        o_ref[0, pl.ds(pl.multiple_of(t_b * tq, tq), tq), :] = ot.T.astype(o_ref.dtype)


def _attention_t(q, k, vt, lam_qk, subln_g, lam_init, tq=128, kc=768):
    b, s, w = q.shape
    lk = k.shape[1]
    heads = w // ATTN_VD
    nq = s // tq
    nsteps = -(-(nq + 1) // 2)
    return pl.pallas_call(
        functools.partial(_attn_t_kernel, lam_init=lam_init, tq=tq, lk=lk, kc=kc, nq=nq),
        grid=(b, heads, nsteps),
        in_specs=[pl.BlockSpec(lam_qk.shape, lambda b_, h, i: (0, 0)),
                  pl.BlockSpec((1, s, ATTN_VD), lambda b_, h, i: (b_, 0, h)),
                  pl.BlockSpec((1, lk, ATTN_VD), lambda b_, h, i: (b_, 0, h)),
                  pl.BlockSpec((1, ATTN_VD, lk), lambda b_, h, i: (b_, h, 0)),
                  pl.BlockSpec((ATTN_VD, 1), lambda b_, h, i: (0, 0))],
        out_specs=pl.BlockSpec((1, s, ATTN_VD), lambda b_, h, i: (b_, 0, h)),
        out_shape=jax.ShapeDtypeStruct((b, s, w), F32),
        scratch_shapes=[pltpu.VMEM((lk, 2 * tq), F32)] * 2 + [pltpu.VMEM((2, 8, 2 * tq), F32)],
        compiler_params=_params(3), name="diff_attention_t",
    )(lam_qk, q, k, vt, subln_g.reshape(ATTN_VD, 1))


def _attention(q, k, v, lam_qk, subln_g, lam_init, tq=128):
    b, s, w = q.shape
    lk = k.shape[1]
    heads = w // ATTN_VD
    tq = min(tq, s)
    return pl.pallas_call(
        functools.partial(_attn_kernel, lam_init=lam_init, tq=tq),
        grid=(b, heads, s // tq),
        in_specs=[pl.BlockSpec(lam_qk.shape, lambda b_, h, i: (0, 0)),
                  pl.BlockSpec((1, tq, ATTN_VD), lambda b_, h, i: (b_, i, h)),
                  pl.BlockSpec((1, lk, ATTN_VD), lambda b_, h, i: (b_, 0, h)),
                  pl.BlockSpec((1, lk, ATTN_VD), lambda b_, h, i: (b_, 0, h)),
                  pl.BlockSpec((1, ATTN_VD), lambda b_, h, i: (0, 0))],
        out_specs=pl.BlockSpec((1, tq, ATTN_VD), lambda b_, h, i: (b_, i, h)),
        out_shape=jax.ShapeDtypeStruct((b, s, w), F32),
        compiler_params=_params(3), name="diff_attention",
    )(lam_qk, q, k, v, subln_g.reshape(1, ATTN_VD))


def _filter_kernel(z_ref, w1_ref, b1_ref, f1_ref, w2_ref, b2_ref, f2_ref, w3_ref, dl_ref, h_ref, asum_ref, *,
                   tl, l):
    i = pl.program_id(0)
    half = tl // 2
    z = z_ref[...]
    z2 = jnp.concatenate([z[:half], z[half:]], axis=1)
    pre = jnp.dot(z2, w1_ref[...], precision=HIGHEST, preferred_element_type=F32)
    h1 = jnp.sin(f1_ref[...] * (pre + b1_ref[...]))
    h2 = jnp.sin(f2_ref[...] * (jnp.dot(h1, w2_ref[...], precision=HIGHEST,
                                        preferred_element_type=F32) + b2_ref[...]))
    hh = h2.astype(BF16)
    hl = (h2 - hh.astype(F32)).astype(BF16)
    h2s = jnp.concatenate([hh, hl, hh], axis=1)
    part = None
    for j in range(2):
        rows = slice(j * half, (j + 1) * half)
        h3 = _dot(h2s, w3_ref[0, j])
        n = i * tl + j * half + lax.broadcasted_iota(jnp.int32, (half, 1), 0)
        h = jnp.where(n == l, 0.0, h3 * jnp.exp(-z[rows, 0:1] * dl_ref[...]))
        h_ref[rows, :] = h
        pj = jnp.sum(jnp.abs(h), axis=0, keepdims=True)
        part = pj if part is None else part + pj

    @pl.when(i == 0)
    def _():
        asum_ref[...] = jnp.zeros_like(asum_ref)

    asum_ref[...] += part


def _hyena_time_kernel(l, w1, b1, f1, w2, b2, f2, w3):
    ffn = w1.shape[1]
    nc = w3.shape[1] // 2
    c = nc // 2
    tl = min(l, 512)
    w3d = jnp.swapaxes(w3.reshape(ffn, 2, nc), 0, 1)
    n = jnp.arange(2 * l, dtype=jnp.int32)
    t = jnp.where(n < l, n, 2 * l - n).astype(F32)
    bands = jnp.linspace(1e-4, HYENA_BANDS - 1, HYENA_BANDS, dtype=F32)
    ang = 2.0 * math.pi * t[:, None] * bands[None, :] / l
    emb = 2 * HYENA_BANDS + 1
    z = jnp.concatenate([(t / (l - 1))[:, None], jnp.cos(ang), jnp.sin(ang),
                         jnp.zeros((2 * l, 128 - emb), F32)], axis=1)
    w1p = jnp.zeros((128, ffn), F32).at[:emb].set(w1)
    zf = jnp.zeros((ffn, ffn), F32)
    w1b = jnp.concatenate([jnp.concatenate([w1p, jnp.zeros_like(w1p)], 1),
                           jnp.concatenate([jnp.zeros_like(w1p), w1p], 1)], 0)
    w2b = jnp.concatenate([jnp.concatenate([w2, zf], 1), jnp.concatenate([zf, w2], 1)], 0)
    w3hi = w3d.astype(BF16)
    w3lo = (w3d - w3hi.astype(F32)).astype(BF16)
    zw = jnp.zeros_like(w3hi)
    first = jnp.concatenate([w3hi, zw, w3hi, zw, w3lo, zw], axis=1)
    second = jnp.concatenate([zw, w3hi, zw, w3hi, zw, w3lo], axis=1)
    w3s = jnp.stack([first, second], axis=1)
    max_decay = math.log(HYENA_TARGET) / HYENA_FAST_DECAY
    min_decay = math.log(HYENA_TARGET) / HYENA_SLOW_DECAY
    deltas = jnp.abs(jnp.linspace(min_decay, max_decay, c, dtype=F32))
    dl = jnp.tile(deltas, 2).reshape(1, nc)
    row2 = lambda a: jnp.tile(a.reshape(1, -1), (1, 2))
    nfwd = l // tl
    kt, asum = pl.pallas_call(
        functools.partial(_filter_kernel, tl=tl, l=l),
        grid=(2 * nfwd,),
        in_specs=[pl.BlockSpec((tl, 128), lambda i: (i, 0)), _const_spec(w1b.shape),
                  _const_spec((1, 2 * ffn)), _const_spec((1, 2 * ffn)),
                  _const_spec(w2b.shape), _const_spec((1, 2 * ffn)), _const_spec((1, 2 * ffn)),
                  pl.BlockSpec((1,) + w3s.shape[1:], lambda i: (i // nfwd, 0, 0, 0)), _const_spec((1, nc))],
        out_specs=[pl.BlockSpec((tl, nc), lambda i: (i, 0)), pl.BlockSpec((1, nc), lambda i: (0, 0))],
        out_shape=[jax.ShapeDtypeStruct((2 * l, nc), F32), jax.ShapeDtypeStruct((1, nc), F32)],
        compiler_params=_params(1), name="hyena_filter_mlp",
    )(z, w1b, row2(b1), row2(f1), w2b, row2(b2), row2(f2), w3s, dl)
    return kt, 1.0 / asum


def _angles(prod, n):
    th = (2.0 * math.pi / n) * (prod % n).astype(F32)
    return jnp.cos(th), jnp.sin(th)


def _dft_tables():
    r = DFT_R
    n = r * r
    idx = jnp.arange(r, dtype=jnp.int32)
    c, s = _angles(idx[:, None] * idx[None, :], r)
    ch, sh = c[:, :r // 2], s[:, :r // 2]
    w1_data = jnp.concatenate([jnp.concatenate([ch, sh], 1), jnp.concatenate([-sh, ch], 1)], 0)
    pad = jnp.zeros((HALF_ROWS - (r // 2 + 1), r), F32)
    w1_real = jnp.concatenate([c[:r // 2 + 1], pad, -s[:r // 2 + 1], pad], 0)
    ct, st = c[:r // 2], s[:r // 2]
    w2 = jnp.concatenate([jnp.concatenate([ct, -st], 1), jnp.concatenate([st, ct], 1)], 0)
    k = idx[:, None, None] + r * idx[None, :, None]
    cg, sg = _angles(k * idx[None, None, :], n)
    g = jnp.concatenate([jnp.concatenate([cg, sg], 2), jnp.concatenate([-sg, cg], 2)], 1)
    dp = DFT_DATA_PASSES
    fp = DFT_FILTER_PASSES
    return dict(w1_data=_split3(w1_data, dp), w1_real=_split3(w1_real, fp), w2=_split3(w2, dp),
                g=_split3(g, dp), g_filter=_split3(g, fp))


N2_TILE = 16
K1_TILE = 8


def _pack_pair(re, im):
    rb = lax.bitcast_convert_type(re.astype(BF16).astype(F32), jnp.uint32)
    ib = lax.bitcast_convert_type(im.astype(BF16).astype(F32), jnp.uint32)
    return rb | (ib >> 16)


def _unpack_pair(w):
    re = lax.bitcast_convert_type(w & jnp.uint32(0xFFFF0000), F32)
    im = lax.bitcast_convert_type(w << 16, F32)
    return re, im


def _dft1_kernel(x_ref, w_ref, o_ref, *, parts, pack):
    r = w_ref.shape[0] // 2
    for i in range(N2_TILE):
        x = jnp.concatenate([x_ref[p, :, i, :] for p in range(parts)], axis=0)
        y = _mm3(w_ref[...], x)
        if pack:
            o_ref[0, :, i, :] = _pack_pair(y[:r], y[r:])
        else:
            o_ref[0, 0, :, i, :] = y[:r]
            o_ref[0, 1, :, i, :] = y[r:]


def _dft_major(x, w3, parts, pack=False):
    b, n1, r, c = x.shape
    k1 = w3.shape[0] // 2
    if pack:
        out_spec = pl.BlockSpec((1, k1, N2_TILE, c), lambda p_, j: (p_, 0, j, 0))
        out_shape = jax.ShapeDtypeStruct((b // parts, k1, r, c), jnp.uint32)
    else:
        out_spec = pl.BlockSpec((1, 2, k1, N2_TILE, c), lambda p_, j: (p_, 0, 0, j, 0))
        out_shape = jax.ShapeDtypeStruct((b // parts, 2, k1, r, c), F32)
    return pl.pallas_call(
        functools.partial(_dft1_kernel, parts=parts, pack=pack),
        grid=(b // parts, r // N2_TILE),
        in_specs=[pl.BlockSpec((parts, n1, N2_TILE, c), lambda p_, j: (p_, 0, j, 0)), _const_spec(w3.shape)],
        out_specs=out_spec, out_shape=out_shape,
        compiler_params=_params(2), name="dft_major",
    )(x, w3)


def _spectrum_kernel(a_ref, g_ref, sc_ref, o_ref, *, nyquist):
    sign = jnp.where(pl.program_id(0) <= nyquist, 1.0, -1.0)
    a = jnp.concatenate([a_ref[0, 0, 0], a_ref[0, 1, 0] * sign], axis=0)
    x = _mm3(g_ref[0], a) * sc_ref[...]
    r = x.shape[0] // 2
    o_ref[0, 0] = x[:r]
    o_ref[1, 0] = x[r:]


def _filter_spectrum(kt, scale, tabs):
    r = DFT_R
    c = kt.shape[1]
    a = _dft_major(kt.reshape(1, r, r, c), tabs["w1_real"], 1)
    tcs = c
    return pl.pallas_call(
        functools.partial(_spectrum_kernel, nyquist=r // 2),
        grid=(r, c // tcs),
        in_specs=[pl.BlockSpec((1, 2, 1, r, tcs), lambda k, j: (0, 0, jnp.where(k <= r // 2, k, r - k), 0, j)),
                  pl.BlockSpec((1,) + tabs["g_filter"].shape[1:], lambda k, j: (k, 0, 0)),
                  pl.BlockSpec((1, tcs), lambda k, j: (0, j))],
        out_specs=pl.BlockSpec((2, 1, r, tcs), lambda k, j: (0, k, 0, j)),
        out_shape=jax.ShapeDtypeStruct((2, r, r, c), F32),
        compiler_params=_params(2), name="filter_spectrum",
    )(a, tabs["g_filter"], scale)


def _mm3t(w3, x):
    assert w3.shape[0] == x.shape[0], "transposed product is only used in single-pass mode"
    return lax.dot_general(w3, x.astype(BF16), (((0,), (0,)), ((), ())), preferred_element_type=F32)


def _dft2_kernel(a_ref, g_ref, h_ref, o_ref):
    for kk in range(K1_TILE):
        hr, hi = h_ref[0, kk], h_ref[1, kk]
        for p in range(a_ref.shape[0]):
            a = jnp.concatenate(_unpack_pair(a_ref[p, kk]), axis=0)
            x = _mm3(g_ref[kk], a)
            r = x.shape[0] // 2
            xr, xi = x[:r], x[r:]
            y = jnp.concatenate([xr * hr - xi * hi, xr * hi + xi * hr], axis=0)
            bb = _mm3t(g_ref[kk], y)
            o_ref[p, :, kk, :] = _pack_pair(bb[:r], bb[r:])


def _dft3_kernel(b_ref, w_ref, u_ref, x_ref, bias_ref, o_ref):
    bias = bias_ref[...]
    for i in range(N2_TILE):
        bb = jnp.concatenate(_unpack_pair(b_ref[0, i]), axis=0)
        y = _mm3(w_ref[...], bb)
        half = y.shape[0] // 2
        for h in range(2):
            o_ref[h, :, i, :] = x_ref[h, :, i, :] * (y[h * half:(h + 1) * half] + u_ref[h, :, i, :] * bias)


def _long_conv_gate(u, x, hspec, order, bias, tabs):
    b, l, c = u.shape
    r = DFT_R
    p = b // 2
    nat = (b, r // 2, r, c)
    assert DFT_DATA_PASSES == 1, "packed bf16 pairs between the stages need the single-pass DFT"
    a = _dft_major(u.reshape(nat), tabs["w1_data"], 2, pack=True)
    spec4 = pl.BlockSpec((p, K1_TILE, r, c), lambda k: (0, k, 0, 0))
    bb = pl.pallas_call(
        _dft2_kernel,
        grid=(r // K1_TILE,),
        in_specs=[spec4,
                  pl.BlockSpec((K1_TILE,) + tabs["g"].shape[1:], lambda k: (k, 0, 0)),
                  pl.BlockSpec((2, K1_TILE, r, c), lambda k: (0, k, 0, order))],
        out_specs=pl.BlockSpec((p, r, K1_TILE, c), lambda k: (0, 0, k, 0)),
        out_shape=jax.ShapeDtypeStruct((p, r, r, c), jnp.uint32),
        compiler_params=_params(1), name="dft_minor_filter",
    )(a, tabs["g"], hspec)
    blk = pl.BlockSpec((2, r // 2, N2_TILE, c), lambda p_, j: (p_, 0, j, 0))
    out = pl.pallas_call(
        _dft3_kernel,
        grid=(p, r // N2_TILE),
        in_specs=[pl.BlockSpec((1, N2_TILE, r, c), lambda p_, j: (p_, j, 0, 0)),
                  _const_spec(tabs["w2"].shape), blk, blk, _const_spec((1, c))],
        out_specs=blk,
        out_shape=jax.ShapeDtypeStruct(nat, F32),
        compiler_params=_params(2), name="idft_major_gate",
    )(bb, tabs["w2"], u.reshape(nat), x.reshape(nat), bias.reshape(1, c))
    return out.reshape(b, l, c)


def _dense_tables(l):
    n = 2 * l
    k = jnp.arange(n, dtype=jnp.int32)
    t = jnp.arange(l, dtype=jnp.int32)
    c, s = _angles(k[:, None] * t[None, :], n)
    wf = jnp.concatenate([jnp.concatenate([c, s], 1), jnp.concatenate([-s, c], 1)], 0)
    ct, st = c.T, s.T
    wi = jnp.concatenate([jnp.concatenate([ct, -st], 1), jnp.concatenate([st, ct], 1)], 0)
    ca, sa = _angles(k[:, None] * k[None, :], n)
    wk = jnp.concatenate([ca, -sa], 0)
    dp = DFT_DATA_PASSES
    return dict(wf=_split3(wf, dp), wi=_split3(wi, dp), wk=_split3(wk, DFT_FILTER_PASSES))


def _dense_spectrum_kernel(k_ref, w_ref, sc_ref, o_ref):
    o_ref[...] = _mm3(w_ref[...], k_ref[...]) * sc_ref[...]


def _dense_spectrum(kt, scale, wk):
    n, c = kt.shape
    return pl.pallas_call(
        _dense_spectrum_kernel,
        grid=(1,),
        in_specs=[_const_spec((n, c)), _const_spec(wk.shape), _const_spec((1, c))],
        out_specs=pl.BlockSpec((2 * n, c), lambda i: (0, 0)),
        out_shape=jax.ShapeDtypeStruct((2 * n, c), F32),
        compiler_params=_params(1), name="dense_filter_spectrum",
    )(kt, wk, scale)


def _dense_conv_kernel(u_ref, x_ref, wf_ref, wi_ref, h_ref, bias_ref, o_ref):
    u0, u1 = u_ref[0], u_ref[1]
    z = _mm3(wf_ref[...], jnp.concatenate([u0, u1], axis=0))
    n = z.shape[0] // 2
    zr, zi = z[:n], z[n:]
    hr, hi = h_ref[:n], h_ref[n:]
    y = _mm3(wi_ref[...], jnp.concatenate([zr * hr - zi * hi, zr * hi + zi * hr], axis=0))
    l = y.shape[0] // 2
    bias = bias_ref[...]
    o_ref[0] = x_ref[0] * (y[:l] + u0 * bias)
    o_ref[1] = x_ref[1] * (y[l:] + u1 * bias)


def _dense_conv_gate(u, x, hspec, order, bias, tabs):
    b, l, c = u.shape
    blk = pl.BlockSpec((2, l, c), lambda p_: (p_, 0, 0))
    return pl.pallas_call(
        _dense_conv_kernel,
        grid=(b // 2,),
        in_specs=[blk, blk, _const_spec(tabs["wf"].shape), _const_spec(tabs["wi"].shape),
                  pl.BlockSpec((4 * l, c), lambda p_: (0, order)), _const_spec((1, c))],
        out_specs=blk,
        out_shape=jax.ShapeDtypeStruct((b, l, c), F32),
        compiler_params=_params(1), name="dense_conv_gate",
    )(u, x, tabs["wf"], tabs["wi"], hspec, bias.reshape(1, c))


def _mix_kernel(h_ref, o_ref, pm_ref, y_ref, g_ref, gate_ref, wa_ref, wp_ref, wh_ref, wo_ref, out_ref):
    d = h_ref.shape[2]
    a = _dot(o_ref[0].astype(BF16), wa_ref[...])
    p = _dot(pm_ref[0].astype(BF16), wp_ref[...])
    y = _dot(y_ref[0].astype(BF16), wh_ref[...])
    merged = g_ref[0, :, 0:d] * a + g_ref[0, :, d:2 * d] * p + g_ref[0, :, 2 * d:3 * d] * y
    out = _dot(merged.astype(BF16), wo_ref[...])
    out_ref[0] = h_ref[0] + gate_ref[0] * out


def _mix(h, o, pm, y, g, mod, w_attn_o, w_pool_o, w_hy_o, w_out, tm=512):
    b, s, d = h.shape
    tm = min(tm, s)

    def tok(w):
        return pl.BlockSpec((1, tm, w), lambda b_, i: (b_, i, 0))

    ws = [w.astype(BF16) for w in (w_attn_o, w_pool_o, w_hy_o, w_out)]
    return pl.pallas_call(
        _mix_kernel,
        grid=(b, s // tm),
        in_specs=[tok(d), tok(o.shape[2]), tok(pm.shape[2]), tok(y.shape[2]), tok(g.shape[2]),
                  _mod_spec(mod, d, 2)] + [_const_spec(w.shape) for w in ws],
        out_specs=tok(d),
        out_shape=jax.ShapeDtypeStruct((b, s, d), F32),
        compiler_params=_params(2), name="merge_outproj",
    )(h, o, pm, y, g, mod, *ws)


def _ffn_kernel(h_ref, sh_ref, sc_ref, gate_ref, g_ref, wi_ref, wo_ref, fg_ref, out_ref, *,
                hidden, chunk, final):
    x = h_ref[0]
    a = _norm_mod(x, g_ref[...], sh_ref[0], sc_ref[0]).astype(BF16)
    acc = jnp.zeros(x.shape, F32)
    for c0 in range(0, hidden, chunk):
        gt = _dot(a, wi_ref[:, c0:c0 + chunk])
        up = _dot(a, wi_ref[:, hidden + c0:hidden + c0 + chunk])
        act = (gt * jax.nn.sigmoid(gt) * up).astype(BF16)
        acc = acc + _dot(act, wo_ref[c0:c0 + chunk, :])
    y = x + gate_ref[0] * acc
    if final:
        ms = jnp.mean(y * y, axis=-1, keepdims=True)
        y = y * lax.rsqrt(ms + EPS) * fg_ref[...]
    out_ref[0] = y


def _ffn(h, mod, norm_g, w_in, w_out, final_g, final, tm=512):
    b, s, d = h.shape
    hidden = w_out.shape[0]
    tm = min(tm, s)
    tok = pl.BlockSpec((1, tm, d), lambda b_, i: (b_, i, 0))
    return pl.pallas_call(
        functools.partial(_ffn_kernel, hidden=hidden, chunk=hidden // 2, final=final),
        grid=(b, s // tm),
        in_specs=[tok, _mod_spec(mod, d, 3), _mod_spec(mod, d, 4), _mod_spec(mod, d, 5),
                  _const_spec((1, d)), _const_spec(w_in.shape), _const_spec(w_out.shape),
                  _const_spec((1, d))],
        out_specs=tok,
        out_shape=jax.ShapeDtypeStruct((b, s, d), F32),
        compiler_params=_params(2), name="swiglu_ffn",
    )(h, mod, mod, mod, norm_g.reshape(1, d), w_in.astype(BF16), w_out.astype(BF16),
      final_g.reshape(1, d))


def _rope_tables(seq, width):
    t = jnp.arange(seq, dtype=jnp.int32)
    row = (t // GRID_W).astype(F32)
    col = (t % GRID_W).astype(F32)
    inv = 1.0 / (ROPE_BASE ** (jnp.arange(ROPE_FREQS, dtype=F32) * 2.0 / (2 * ROPE_FREQS)))
    ar, ac = row[:, None] * inv, col[:, None] * inv
    cos = jnp.concatenate([jnp.cos(ar)] * 2 + [jnp.cos(ac)] * 2, axis=1)
    sin = jnp.concatenate([-jnp.sin(ar), jnp.sin(ar), -jnp.sin(ac), jnp.sin(ac)], axis=1)
    reps = width // cos.shape[1]
    return jnp.tile(cos, (1, reps)), jnp.tile(sin, (1, reps))


def kernel(x, c, ctx, c_ctx, w_mod, b_mod, norm1_g, norm2_g, w_in, lam_qk, subln_g, w_attn_o, w_pool, pool_scale, w_pool_o, w_short, b_short, hf_w1, hf_b1, hf_freq1, hf_w2, hf_b2, hf_freq2, hf_w3, hy_bias, w_hy_o, w_out, w_ffn_in, w_ffn_out, final_g):
    batch, seq, d = x.shape
    depth = w_mod.shape[0]
    ctx_len = ctx.shape[1]
    aw = w_attn_o.shape[1]
    pw = w_pool_o.shape[1]
    hw = w_hy_o.shape[1]
    q0, k0, v0, p0 = 0, aw, 2 * aw, 3 * aw
    h0 = p0 + pw
    g0 = h0 + 3 * hw
    assert seq * 2 == DFT_R * DFT_R and batch % 2 == 0

    c8 = jnp.zeros((8, d), F32).at[:batch].set(c).at[batch].set(c_ctx)
    mod_all = _modulation(c8, w_mod, b_mod)
    rope_tabs = _rope_tables(seq, ATTN_VD)
    tabs = _dft_tables()
    ctx_tabs = _dense_tables(ctx_len)

    h_lat, h_ctx = x, ctx
    for l in range(depth):
        last = l == depth - 1
        lam_init = LAMBDA_INIT_BASE - LAMBDA_INIT_AMP * math.exp(-LAMBDA_INIT_RATE * l)
        mod = mod_all[l, :batch].reshape(batch, 1, N_MOD * d)
        mod_c = mod_all[l, batch:batch + 1].reshape(1, 1, N_MOD * d)
        wl = w_in[l]
        fparams = (hf_w1[l], hf_b1[l], hf_freq1[l], hf_w2[l], hf_b2[l], hf_freq2[l], hf_w3[l])

        wl_bf = wl.astype(BF16)
        local_params = (w_pool[l], pool_scale[l], w_short[l], b_short[l])
        out_dt = [F32, F32, F32, F32, F32, BF16, BF16, BF16]
        local_seg = ("local", p0, pw + 3 * hw, 1.0)
        gate_seg = ("sigmoid", g0, 3 * d, 1.0)
        segs = [local_seg, gate_seg, ("rope", q0, aw, Q_SCALE), ("rope", k0, aw, 1.0), ("plain", v0, aw, 1.0)]
        pm, vv, x1, x2, g, q, k, v = _inproj(h_lat, mod, norm1_g[l], wl_bf, segs, out_dt, rope_tabs, local_params)

        if last:
            segs_c = [("plain", k0, aw, 1.0), ("plain", v0, aw, 1.0)]
            k_c, v_c = _inproj(h_ctx, mod_c, norm1_g[l], wl_bf, segs_c, [BF16, BF16])
        else:
            segs_c = [local_seg, gate_seg, ("plain", q0, aw, Q_SCALE), ("plain", k0, aw, 1.0), ("plain", v0, aw, 1.0)]
            pm_c, vv_c, x1_c, x2_c, g_c, q_c, k_c, v_c = _inproj(h_ctx, mod_c, norm1_g[l], wl_bf, segs_c, out_dt,
                                                                 None, local_params)
            o_c = _attention(q_c, k_c, v_c, lam_qk[l], subln_g[l], lam_init)
            kt_c, inv_norm_c = _hyena_time_kernel(ctx_len, *fparams)
            hspec_c = _dense_spectrum(kt_c, inv_norm_c / (2 * ctx_len), ctx_tabs["wk"])
            z_c = _dense_conv_gate(vv_c, x1_c, hspec_c, 0, hy_bias[l, 0], ctx_tabs)
            y_c = _dense_conv_gate(z_c, x2_c, hspec_c, 1, hy_bias[l, 1], ctx_tabs)
            h_ctx_mid = _mix(h_ctx, o_c, pm_c, y_c, g_c, mod_c, w_attn_o[l], w_pool_o[l], w_hy_o[l], w_out[l])
            h_ctx_new = _ffn(h_ctx_mid, mod_c, norm2_g[l], w_ffn_in[l], w_ffn_out[l], final_g, False)

        vt = jnp.swapaxes(jnp.concatenate([v_c, v], axis=1), 1, 2)
        o = _attention_t(q, jnp.concatenate([k_c, k], axis=1), vt, lam_qk[l], subln_g[l], lam_init)
        kt, inv_norm = _hyena_time_kernel(seq, *fparams)
        hspec = _filter_spectrum(kt, inv_norm / (2 * seq), tabs)
        z = _long_conv_gate(vv, x1, hspec, 0, hy_bias[l, 0], tabs)
        y = _long_conv_gate(z, x2, hspec, 1, hy_bias[l, 1], tabs)
        h_mid = _mix(h_lat, o, pm, y, g, mod, w_attn_o[l], w_pool_o[l], w_hy_o[l], w_out[l])
        h_lat = _ffn(h_mid, mod, norm2_g[l], w_ffn_in[l], w_ffn_out[l], final_g, last)
        if not last:
            h_ctx = h_ctx_new
    return h_lat
```

```python
import functools
import math

import jax
import jax.numpy as jnp
from jax import lax
from jax.experimental import pallas as pl
from jax.experimental.pallas import tpu as pltpu

F32 = jnp.float32
BF16 = jnp.bfloat16
HIGHEST = lax.Precision.HIGHEST

GRID_W = 64
N_MOD = 6
ATTN_HD = 64
ATTN_VD = 128
Q_SCALE = ATTN_HD ** -0.5 * math.log2(math.e)
ROPE_BASE = 10000.0
ROPE_FREQS = 16
LAMBDA_INIT_BASE = 0.8
LAMBDA_INIT_AMP = 0.6
LAMBDA_INIT_RATE = 0.3
POOL_WINDOWS = (2, 4, 8, 16)
POOL_GC = 128
HALO = 8
PROJ_PIECE = 1024
HYENA_BANDS = 16
HYENA_FAST_DECAY = 0.3
HYENA_SLOW_DECAY = 1.5
HYENA_TARGET = 1e-2
EPS = 1e-6
DFT_R = 128
HALF_ROWS = 72

VMEM_LIMIT = 56 * 1024 * 1024


def _params(n_grid):
    return pltpu.CompilerParams(dimension_semantics=("arbitrary",) * n_grid,
                                vmem_limit_bytes=VMEM_LIMIT)


def _const_spec(shape):
    zeros = (0,) * len(shape)
    return pl.BlockSpec(shape, lambda *_: zeros, pipeline_mode=pl.Buffered(1))


def _dot(a, b):
    return jnp.dot(a, b, preferred_element_type=F32)


def _dft_mm(w, x):
    return _dot(w, x.astype(BF16))


def _mod_kernel(c_ref, w_ref, b_ref, o_ref):
    c = c_ref[...]
    act = c * jax.nn.sigmoid(c)
    o_ref[0] = jnp.dot(act, w_ref[0], precision=HIGHEST, preferred_element_type=F32) + b_ref[0]


def _modulation(c8, w_mod, b_mod):
    depth, d, n = w_mod.shape
    tn = 1536
    return pl.pallas_call(
        _mod_kernel,
        grid=(depth, n // tn),
        in_specs=[pl.BlockSpec((8, d), lambda l, j: (0, 0)),
                  pl.BlockSpec((1, d, tn), lambda l, j: (l, 0, j)),
                  pl.BlockSpec((1, 1, tn), lambda l, j: (l, 0, j))],
        out_specs=pl.BlockSpec((1, 8, tn), lambda l, j: (l, 0, j)),
        out_shape=jax.ShapeDtypeStruct((depth, 8, n), F32),
        compiler_params=_params(2),
        name="modulation",
    )(c8, w_mod, b_mod.reshape(depth, 1, n))


def _mod_spec(mod, d, col):
    if mod.shape[0] == 1:
        return pl.BlockSpec((1, 1, d), lambda b, i: (0, 0, col))
    return pl.BlockSpec((1, 1, d), lambda b, i: (b, 0, col))


def _norm_mod(x, g, shift, scale):
    ms = jnp.mean(x * x, axis=-1, keepdims=True)
    a = x * lax.rsqrt(ms + EPS) * g
    return a * (1.0 + scale) + shift


def _rope(p, cos, sin):
    lanes = cos.shape[1]
    lane = lax.broadcasted_iota(jnp.int32, cos.shape, 1)
    is_a = lane % (2 * ROPE_FREQS) < ROPE_FREQS
    out = []
    for g in range(p.shape[1] // lanes):
        x = p[:, g * lanes:(g + 1) * lanes]
        partner = jnp.where(is_a, pltpu.roll(x, lanes - ROPE_FREQS, 1), pltpu.roll(x, ROPE_FREQS, 1))
        out.append(x * cos + partner * sin)
    return jnp.concatenate(out, axis=1)


def _local_steps(pbuf, hbuf, wp_ref, ps_ref, ws_ref, bs_ref, pm_ref, v_ref, x1_ref, x2_ref, i, ts, seq):
    t = i * ts + lax.broadcasted_iota(jnp.int32, (ts, 1), 0)

    def pool(g, w):
        cs = slice(g * POOL_GC, (g + 1) * POOL_GC)
        acc = pbuf[HALO - w // 2:HALO - w // 2 + ts, cs]
        for j in range(1 - w // 2, w // 2):
            acc = acc + pbuf[HALO + j:HALO + j + ts, cs]
        cnt = (jnp.minimum(t + w // 2, seq) - jnp.maximum(t - w // 2, 0)).astype(F32)
        pooled = acc / cnt - pbuf[HALO:HALO + ts, cs]
        mixed = _dot(pooled.astype(BF16), wp_ref[g])
        pm_ref[0, :, cs] = (mixed * ps_ref[:, cs]).astype(pm_ref.dtype)

    def conv(o_ref, k):
        c = o_ref.shape[2]
        cs = slice(k * c, (k + 1) * c)
        o_ref[0] = (hbuf[HALO - 1:HALO - 1 + ts, cs] * ws_ref[0:1, cs] + hbuf[HALO:HALO + ts, cs] * ws_ref[1:2, cs]
                    + hbuf[HALO + 1:HALO + 1 + ts, cs] * ws_ref[2:3, cs] + bs_ref[:, cs])

    return ([functools.partial(pool, g, w) for g, w in enumerate(POOL_WINDOWS)]
            + [functools.partial(conv, o_ref, k) for k, o_ref in enumerate((v_ref, x1_ref, x2_ref))])


def _inproj_kernel(*refs, segs, rope, local, tm, seq):
    it = iter(refs)
    h_ref = next(it)
    hp_ref, hn_ref = (next(it), next(it)) if local else (None, None)
    sh_ref, sc_ref, g_ref, w_ref = next(it), next(it), next(it), next(it)
    cos_ref, sin_ref = (next(it), next(it)) if rope else (None, None)
    loc_refs = [next(it) for _ in range(4)] if local else None
    rest = list(it)
    n_out = sum(4 if kind == "local" else 1 for kind, _, _, _ in segs)
    out_refs, scratch = rest[:n_out], rest[n_out:]

    g, sh, sc = g_ref[...], sh_ref[0], sc_ref[0]
    a32 = _norm_mod(h_ref[0], g, sh, sc)
    a = a32.astype(BF16)

    def project(kind, col, width, scale, o_ref, c0):
        p = _dot(a, w_ref[:, col + c0:col + c0 + width])
        if kind == "rope":
            p = _rope(p, cos_ref[...], sin_ref[...])
        elif kind == "sigmoid":
            p = jax.nn.sigmoid(p)
        if scale != 1.0:
            p = p * scale
        o_ref[0, :, c0:c0 + width] = p.astype(o_ref.dtype)

    mxu_work, vpu_work = [], []
    oi = 0
    for kind, col, width, scale in segs:
        if kind == "local":
            i = pl.program_id(1)
            ext = jnp.concatenate([_norm_mod(hp_ref[0], g, sh, sc), a32, _norm_mod(hn_ref[0], g, sh, sc)],
                                  axis=0).astype(BF16)
            u = _dot(ext, w_ref[:, col:col + width])
            row = lax.broadcasted_iota(jnp.int32, (tm + 2 * HALO, 1), 0)
            inside = jnp.logical_and(jnp.logical_or(i > 0, row >= HALO),
                                     jnp.logical_or(i < pl.num_programs(1) - 1, row < HALO + tm))
            u = jnp.where(inside, u, 0.0)
            pbuf, hbuf = scratch
            cp = pbuf.shape[1]
            pbuf[...] = u[:, :cp]
            hbuf[...] = u[:, cp:]
            vpu_work = _local_steps(pbuf, hbuf, *loc_refs, *out_refs[oi:oi + 4], i, tm, seq)
            oi += 4
            continue
        piece = min(width, PROJ_PIECE)
        for c0 in range(0, width, piece):
            mxu_work.append(functools.partial(project, kind, col, piece, scale, out_refs[oi], c0))
        oi += 1
    for k in range(max(len(mxu_work), len(vpu_work))):
        if k < len(vpu_work):
            vpu_work[k]()
        if k < len(mxu_work):
            mxu_work[k]()


def _inproj(h, mod, norm_g, w_cat, segs, out_dtypes, rope_tabs=None, local_params=None, tm=256):
    b, s, d = h.shape
    tm = min(tm, s)
    rope = rope_tabs is not None
    local = local_params is not None
    nh = tm // HALO
    tok = lambda w: pl.BlockSpec((1, tm, w), lambda b_, i: (b_, i, 0))
    in_specs = [tok(d)]
    args = [h]
    if local:
        in_specs += [pl.BlockSpec((1, HALO, d), lambda b_, i: (b_, jnp.maximum(i * nh - 1, 0), 0)),
                     pl.BlockSpec((1, HALO, d), lambda b_, i: (b_, jnp.minimum((i + 1) * nh, s // HALO - 1), 0))]
        args += [h, h]
    in_specs += [_mod_spec(mod, d, 0), _mod_spec(mod, d, 1), _const_spec((1, d)), _const_spec(w_cat.shape)]
    args += [mod, mod, norm_g.reshape(1, d), w_cat]
    if rope:
        wr = rope_tabs[0].shape[1]
        in_specs += [pl.BlockSpec((tm, wr), lambda b_, i: (i, 0))] * 2
        args += list(rope_tabs)
    scratch = []
    if local:
        w_pool, pool_scale, w_short, b_short = local_params
        cp, ch = pool_scale.shape[0], b_short.shape[0]
        in_specs += [_const_spec(w_pool.shape), _const_spec((1, cp)), _const_spec(w_short.shape),
                     _const_spec((1, ch))]
        args += [w_pool.astype(BF16), pool_scale.reshape(1, cp), w_short, b_short.reshape(1, ch)]
        scratch = [pltpu.VMEM((tm + 2 * HALO, cp), F32), pltpu.VMEM((tm + 2 * HALO, ch), F32)]
    widths = []
    for kind, _, w, _ in segs:
        widths += [cp, ch // 3, ch // 3, ch // 3] if kind == "local" else [w]
    out_specs = [tok(w) for w in widths]
    out_shape = [jax.ShapeDtypeStruct((b, s, w), dt) for w, dt in zip(widths, out_dtypes)]
    return pl.pallas_call(
        functools.partial(_inproj_kernel, segs=tuple(segs), rope=rope, local=local, tm=tm, seq=s),
        grid=(b, s // tm), in_specs=in_specs, out_specs=out_specs, out_shape=out_shape,
        scratch_shapes=scratch, compiler_params=_params(2), name="inproj",
    )(*args)


def _lambda(lq_ref, lam_init):
    lq = lq_ref[...]
    return (jnp.exp(jnp.sum(lq[0:1] * lq[1:2], axis=-1, keepdims=True))
            - jnp.exp(jnp.sum(lq[2:3] * lq[3:4], axis=-1, keepdims=True)) + lam_init)


def _split_halves(q):
    lane = lax.broadcasted_iota(jnp.int32, q.shape, 1)
    zero = jnp.zeros_like(q)
    return jnp.concatenate([jnp.where(lane < ATTN_HD, q, zero),
                            jnp.where(lane >= ATTN_HD, q, zero)], axis=0)


def _qk(qq, k):
    return lax.dot_general(qq, k, (((1,), (1,)), ((), ())), preferred_element_type=F32)


def _sub_ln(o, g, lam_init):
    ms = jnp.mean(o * o, axis=-1, keepdims=True)
    return o * lax.rsqrt(ms + EPS) * g * (1.0 - lam_init)


def _attn_kernel(lq_ref, q_ref, k_ref, v_ref, g_ref, o_ref, *, lam_init, tq):
    lam = _lambda(lq_ref, lam_init)
    s = _qk(_split_halves(q_ref[0]), k_ref[0])
    m = jnp.max(s, axis=-1, keepdims=True)
    p = jnp.exp2(s - m)
    l = jnp.sum(p, axis=-1, keepdims=True)
    a = (p[:tq] - p[tq:] * (lam * l[:tq] / l[tq:])).astype(BF16)
    o = _dot(a, v_ref[0]) / l[:tq]
    o_ref[0] = _sub_ln(o, g_ref[...], lam_init).astype(o_ref.dtype)


def _attn_t_kernel(lq_ref, q_ref, k_ref, vt_ref, g_ref, o_ref, s0, s1, m_scr, *, lam_init, tq, lk, kc, nq):
    j = pl.program_id(2)
    bufs = (s0, s1)

    @pl.when(j == 0)
    def _():
        s1[...] = jnp.zeros_like(s1)
        m_scr[1] = jnp.zeros(m_scr.shape[1:], F32)

    lam = _lambda(lq_ref, lam_init)
    for r in range(2):
        sig = 2 * j + r
        buf_a, buf_b = bufs[r], bufs[1 - r]
        t_a = jnp.minimum(sig, nq - 1)
        t_b = jnp.clip(sig - 1, 0, nq - 1)
        qq = _split_halves(q_ref[0, pl.ds(pl.multiple_of(t_a * tq, tq), tq), :])
        m_b = m_scr[1 - r]
        m8 = l8 = acc = None
        for c0 in range(0, lk, kc):
            rows = slice(c0, c0 + kc)
            s = _qk(k_ref[0, rows, :], qq)
            buf_a[rows, :] = s
            for i in range(0, kc, 8):
                m8 = s[i:i + 8] if m8 is None else jnp.maximum(m8, s[i:i + 8])
            p = jnp.exp2(buf_b[rows, :].reshape(kc // 8, 8, 2 * tq) - m_b)
            lc = jnp.sum(p, axis=0)
            l8 = lc if l8 is None else l8 + lc
            pv = _dot(vt_ref[0, :, rows], p.reshape(kc, 2 * tq).astype(BF16))
            acc = pv if acc is None else acc + pv
        m_scr[r] = jnp.broadcast_to(jnp.max(m8, axis=0, keepdims=True), m8.shape)
        l = jnp.sum(l8, axis=0, keepdims=True)
        ot = acc[:, :tq] / l[:, :tq] - acc[:, tq:] * (lam / l[:, tq:])
        ms = jnp.mean(ot * ot, axis=0, keepdims=True)
        ot = ot * lax.rsqrt(ms + EPS) * g_ref[...] * (1.0 - lam_init)
        o_ref[0, pl.ds(pl.multiple_of(t_b * tq, tq), tq), :] = ot.T.astype(o_ref.dtype)


def _attention_t(q, k, vt, lam_qk, subln_g, lam_init, tq=128, kc=768):
    b, s, w = q.shape
    lk = k.shape[1]
    heads = w // ATTN_VD
    nq = s // tq
    nsteps = -(-(nq + 1) // 2)
    return pl.pallas_call(
        functools.partial(_attn_t_kernel, lam_init=lam_init, tq=tq, lk=lk, kc=kc, nq=nq),
        grid=(b, heads, nsteps),
        in_specs=[pl.BlockSpec(lam_qk.shape, lambda b_, h, i: (0, 0)),
                  pl.BlockSpec((1, s, ATTN_VD), lambda b_, h, i: (b_, 0, h)),
                  pl.BlockSpec((1, lk, ATTN_VD), lambda b_, h, i: (b_, 0, h)),
                  pl.BlockSpec((1, ATTN_VD, lk), lambda b_, h, i: (b_, h, 0)),
                  pl.BlockSpec((ATTN_VD, 1), lambda b_, h, i: (0, 0))],
        out_specs=pl.BlockSpec((1, s, ATTN_VD), lambda b_, h, i: (b_, 0, h)),
        out_shape=jax.ShapeDtypeStruct((b, s, w), BF16),
        scratch_shapes=[pltpu.VMEM((lk, 2 * tq), F32)] * 2 + [pltpu.VMEM((2, 8, 2 * tq), F32)],
        compiler_params=_params(3), name="diff_attention_t",
    )(lam_qk, q, k, vt, subln_g.reshape(ATTN_VD, 1))


def _attention(q, k, v, lam_qk, subln_g, lam_init, tq=128):
    b, s, w = q.shape
    lk = k.shape[1]
    heads = w // ATTN_VD
    tq = min(tq, s)
    return pl.pallas_call(
        functools.partial(_attn_kernel, lam_init=lam_init, tq=tq),
        grid=(b, heads, s // tq),
        in_specs=[pl.BlockSpec(lam_qk.shape, lambda b_, h, i: (0, 0)),
                  pl.BlockSpec((1, tq, ATTN_VD), lambda b_, h, i: (b_, i, h)),
                  pl.BlockSpec((1, lk, ATTN_VD), lambda b_, h, i: (b_, 0, h)),
                  pl.BlockSpec((1, lk, ATTN_VD), lambda b_, h, i: (b_, 0, h)),
                  pl.BlockSpec((1, ATTN_VD), lambda b_, h, i: (0, 0))],
        out_specs=pl.BlockSpec((1, tq, ATTN_VD), lambda b_, h, i: (b_, i, h)),
        out_shape=jax.ShapeDtypeStruct((b, s, w), BF16),
        compiler_params=_params(3), name="diff_attention",
    )(lam_qk, q, k, v, subln_g.reshape(1, ATTN_VD))


def _filter_kernel(z_ref, w1_ref, b1_ref, f1_ref, w2_ref, b2_ref, f2_ref, w3_ref, dl_ref, h_ref, asum_ref, *,
                   tl, l):
    i = pl.program_id(0)
    half = tl // 2
    z = z_ref[...]
    z2 = jnp.concatenate([z[:half], z[half:]], axis=1)
    pre = jnp.dot(z2, w1_ref[...], precision=HIGHEST, preferred_element_type=F32)
    h1 = jnp.sin(f1_ref[...] * (pre + b1_ref[...]))
    h2 = jnp.sin(f2_ref[...] * (jnp.dot(h1, w2_ref[...], precision=HIGHEST,
                                        preferred_element_type=F32) + b2_ref[...]))
    hh = h2.astype(BF16)
    hl = (h2 - hh.astype(F32)).astype(BF16)
    h2s = jnp.concatenate([hh, hl, hh], axis=1)
    part = None
    for j in range(2):
        rows = slice(j * half, (j + 1) * half)
        h3 = _dot(h2s, w3_ref[0, j])
        n = i * tl + j * half + lax.broadcasted_iota(jnp.int32, (half, 1), 0)
        h = jnp.where(n == l, 0.0, h3 * jnp.exp(-z[rows, 0:1] * dl_ref[...]))
        h_ref[rows, :] = h
        pj = jnp.sum(jnp.abs(h), axis=0, keepdims=True)
        part = pj if part is None else part + pj

    @pl.when(i == 0)
    def _():
        asum_ref[...] = jnp.zeros_like(asum_ref)

    asum_ref[...] += part


def _hyena_time_kernel(l, w1, b1, f1, w2, b2, f2, w3):
    ffn = w1.shape[1]
    nc = w3.shape[1] // 2
    c = nc // 2
    tl = min(l, 512)
    w3d = jnp.swapaxes(w3.reshape(ffn, 2, nc), 0, 1)
    n = jnp.arange(2 * l, dtype=jnp.int32)
    t = jnp.where(n < l, n, 2 * l - n).astype(F32)
    bands = jnp.linspace(1e-4, HYENA_BANDS - 1, HYENA_BANDS, dtype=F32)
    ang = 2.0 * math.pi * t[:, None] * bands[None, :] / l
    emb = 2 * HYENA_BANDS + 1
    z = jnp.concatenate([(t / (l - 1))[:, None], jnp.cos(ang), jnp.sin(ang),
                         jnp.zeros((2 * l, 128 - emb), F32)], axis=1)
    w1p = jnp.zeros((128, ffn), F32).at[:emb].set(w1)
    zf = jnp.zeros((ffn, ffn), F32)
    w1b = jnp.concatenate([jnp.concatenate([w1p, jnp.zeros_like(w1p)], 1),
                           jnp.concatenate([jnp.zeros_like(w1p), w1p], 1)], 0)
    w2b = jnp.concatenate([jnp.concatenate([w2, zf], 1), jnp.concatenate([zf, w2], 1)], 0)
    w3hi = w3d.astype(BF16)
    w3lo = (w3d - w3hi.astype(F32)).astype(BF16)
    zw = jnp.zeros_like(w3hi)
    first = jnp.concatenate([w3hi, zw, w3hi, zw, w3lo, zw], axis=1)
    second = jnp.concatenate([zw, w3hi, zw, w3hi, zw, w3lo], axis=1)
    w3s = jnp.stack([first, second], axis=1)
    max_decay = math.log(HYENA_TARGET) / HYENA_FAST_DECAY
    min_decay = math.log(HYENA_TARGET) / HYENA_SLOW_DECAY
    deltas = jnp.abs(jnp.linspace(min_decay, max_decay, c, dtype=F32))
    dl = jnp.tile(deltas, 2).reshape(1, nc)
    row2 = lambda a: jnp.tile(a.reshape(1, -1), (1, 2))
    nfwd = l // tl
    kt, asum = pl.pallas_call(
        functools.partial(_filter_kernel, tl=tl, l=l),
        grid=(2 * nfwd,),
        in_specs=[pl.BlockSpec((tl, 128), lambda i: (i, 0)), _const_spec(w1b.shape),
                  _const_spec((1, 2 * ffn)), _const_spec((1, 2 * ffn)),
                  _const_spec(w2b.shape), _const_spec((1, 2 * ffn)), _const_spec((1, 2 * ffn)),
                  pl.BlockSpec((1,) + w3s.shape[1:], lambda i: (i // nfwd, 0, 0, 0)), _const_spec((1, nc))],
        out_specs=[pl.BlockSpec((tl, nc), lambda i: (i, 0)), pl.BlockSpec((1, nc), lambda i: (0, 0))],
        out_shape=[jax.ShapeDtypeStruct((2 * l, nc), F32), jax.ShapeDtypeStruct((1, nc), F32)],
        compiler_params=_params(1), name="hyena_filter_mlp",
    )(z, w1b, row2(b1), row2(f1), w2b, row2(b2), row2(f2), w3s, dl)
    return kt, 1.0 / asum


def _angles(prod, n):
    th = (2.0 * math.pi / n) * (prod % n).astype(F32)
    return jnp.cos(th), jnp.sin(th)


def _dft_tables():
    r = DFT_R
    n = r * r
    idx = jnp.arange(r, dtype=jnp.int32)
    c, s = _angles(idx[:, None] * idx[None, :], r)
    ch, sh = c[:, :r // 2], s[:, :r // 2]
    w1_data = jnp.concatenate([jnp.concatenate([ch, sh], 1), jnp.concatenate([-sh, ch], 1)], 0)
    pad = jnp.zeros((HALF_ROWS - (r // 2 + 1), r), F32)
    w1_real = jnp.concatenate([c[:r // 2 + 1], pad, -s[:r // 2 + 1], pad], 0)
    ct, st = c[:r // 2], s[:r // 2]
    w2 = jnp.concatenate([jnp.concatenate([ct, -st], 1), jnp.concatenate([st, ct], 1)], 0)
    k = idx[:, None, None] + r * idx[None, :, None]
    cg, sg = _angles(k * idx[None, None, :], n)
    g = jnp.concatenate([jnp.concatenate([cg, sg], 2), jnp.concatenate([-sg, cg], 2)], 1)
    return dict(w1_data=w1_data.astype(BF16), w1_real=w1_real.astype(BF16), w2=w2.astype(BF16),
                g=g.astype(BF16))


N2_TILE = 16
K1_TILE = 8


def _pack_pair(re, im):
    rb = lax.bitcast_convert_type(re.astype(BF16).astype(F32), jnp.uint32)
    ib = lax.bitcast_convert_type(im.astype(BF16).astype(F32), jnp.uint32)
    return rb | (ib >> 16)


def _unpack_pair(w):
    re = lax.bitcast_convert_type(w & jnp.uint32(0xFFFF0000), F32)
    im = lax.bitcast_convert_type(w << 16, F32)
    return re, im


def _dft1_kernel(x_ref, w_ref, o_ref, *, parts, pack):
    r = w_ref.shape[0] // 2
    for i in range(N2_TILE):
        x = jnp.concatenate([x_ref[p, :, i, :] for p in range(parts)], axis=0)
        y = _dft_mm(w_ref[...], x)
        if pack:
            o_ref[0, :, i, :] = _pack_pair(y[:r], y[r:])
        else:
            o_ref[0, 0, :, i, :] = y[:r]
            o_ref[0, 1, :, i, :] = y[r:]


def _dft_major(x, w3, parts, pack=False):
    b, n1, r, c = x.shape
    k1 = w3.shape[0] // 2
    if pack:
        out_spec = pl.BlockSpec((1, k1, N2_TILE, c), lambda p_, j: (p_, 0, j, 0))
        out_shape = jax.ShapeDtypeStruct((b // parts, k1, r, c), jnp.uint32)
    else:
        out_spec = pl.BlockSpec((1, 2, k1, N2_TILE, c), lambda p_, j: (p_, 0, 0, j, 0))
        out_shape = jax.ShapeDtypeStruct((b // parts, 2, k1, r, c), F32)
    return pl.pallas_call(
        functools.partial(_dft1_kernel, parts=parts, pack=pack),
        grid=(b // parts, r // N2_TILE),
        in_specs=[pl.BlockSpec((parts, n1, N2_TILE, c), lambda p_, j: (p_, 0, j, 0)), _const_spec(w3.shape)],
        out_specs=out_spec, out_shape=out_shape,
        compiler_params=_params(2), name="dft_major",
    )(x, w3)


def _spectrum_kernel(a_ref, g_ref, sc_ref, o_ref, *, nyquist):
    sign = jnp.where(pl.program_id(0) <= nyquist, 1.0, -1.0)
    a = jnp.concatenate([a_ref[0, 0, 0], a_ref[0, 1, 0] * sign], axis=0)
    x = _dft_mm(g_ref[0], a) * sc_ref[...]
    r = x.shape[0] // 2
    o_ref[0, 0] = x[:r].astype(o_ref.dtype)
    o_ref[1, 0] = x[r:].astype(o_ref.dtype)


def _filter_spectrum(kt, scale, tabs):
    r = DFT_R
    c = kt.shape[1]
    a = _dft_major(kt.reshape(1, r, r, c), tabs["w1_real"], 1)
    tcs = c
    return pl.pallas_call(
        functools.partial(_spectrum_kernel, nyquist=r // 2),
        grid=(r, c // tcs),
        in_specs=[pl.BlockSpec((1, 2, 1, r, tcs), lambda k, j: (0, 0, jnp.where(k <= r // 2, k, r - k), 0, j)),
                  pl.BlockSpec((1,) + tabs["g"].shape[1:], lambda k, j: (k, 0, 0)),
                  pl.BlockSpec((1, tcs), lambda k, j: (0, j))],
        out_specs=pl.BlockSpec((2, 1, r, tcs), lambda k, j: (0, k, 0, j)),
        out_shape=jax.ShapeDtypeStruct((2, r, r, c), BF16),
        compiler_params=_params(2), name="filter_spectrum",
    )(a, tabs["g"], scale)


def _dft_mm_t(w, x):
    return lax.dot_general(w, x.astype(BF16), (((0,), (0,)), ((), ())), preferred_element_type=F32)


def _dft2_kernel(a_ref, g_ref, h_ref, o_ref):
    for kk in range(K1_TILE):
        hr, hi = h_ref[0, kk].astype(F32), h_ref[1, kk].astype(F32)
        for p in range(a_ref.shape[0]):
            a = jnp.concatenate(_unpack_pair(a_ref[p, kk]), axis=0)
            x = _dft_mm(g_ref[kk], a)
            r = x.shape[0] // 2
            xr, xi = x[:r], x[r:]
            y = jnp.concatenate([xr * hr - xi * hi, xr * hi + xi * hr], axis=0)
            bb = _dft_mm_t(g_ref[kk], y)
            o_ref[p, :, kk, :] = _pack_pair(bb[:r], bb[r:])


def _dft3_kernel(b_ref, w_ref, u_ref, x_ref, bias_ref, o_ref):
    bias = bias_ref[...]
    for i in range(N2_TILE):
        bb = jnp.concatenate(_unpack_pair(b_ref[0, i]), axis=0)
        y = _dft_mm(w_ref[...], bb)
        half = y.shape[0] // 2
        for h in range(2):
            o_ref[h, :, i, :] = x_ref[h, :, i, :] * (y[h * half:(h + 1) * half] + u_ref[h, :, i, :] * bias)


def _long_conv_gate(u, x, hspec, order, bias, tabs):
    b, l, c = u.shape
    r = DFT_R
    p = b // 2
    nat = (b, r // 2, r, c)
    a = _dft_major(u.reshape(nat), tabs["w1_data"], 2, pack=True)
    spec4 = pl.BlockSpec((p, K1_TILE, r, c), lambda k: (0, k, 0, 0))
    bb = pl.pallas_call(
        _dft2_kernel,
        grid=(r // K1_TILE,),
        in_specs=[spec4,
                  pl.BlockSpec((K1_TILE,) + tabs["g"].shape[1:], lambda k: (k, 0, 0)),
                  pl.BlockSpec((2, K1_TILE, r, c), lambda k: (0, k, 0, order))],
        out_specs=pl.BlockSpec((p, r, K1_TILE, c), lambda k: (0, 0, k, 0)),
        out_shape=jax.ShapeDtypeStruct((p, r, r, c), jnp.uint32),
        compiler_params=_params(1), name="dft_minor_filter",
    )(a, tabs["g"], hspec)
    blk = pl.BlockSpec((2, r // 2, N2_TILE, c), lambda p_, j: (p_, 0, j, 0))
    out = pl.pallas_call(
        _dft3_kernel,
        grid=(p, r // N2_TILE),
        in_specs=[pl.BlockSpec((1, N2_TILE, r, c), lambda p_, j: (p_, j, 0, 0)),
                  _const_spec(tabs["w2"].shape), blk, blk, _const_spec((1, c))],
        out_specs=blk,
        out_shape=jax.ShapeDtypeStruct(nat, F32),
        compiler_params=_params(2), name="idft_major_gate",
    )(bb, tabs["w2"], u.reshape(nat), x.reshape(nat), bias.reshape(1, c))
    return out.reshape(b, l, c)


def _dense_tables(l):
    n = 2 * l
    k = jnp.arange(n, dtype=jnp.int32)
    t = jnp.arange(l, dtype=jnp.int32)
    c, s = _angles(k[:, None] * t[None, :], n)
    wf = jnp.concatenate([jnp.concatenate([c, s], 1), jnp.concatenate([-s, c], 1)], 0)
    ct, st = c.T, s.T
    wi = jnp.concatenate([jnp.concatenate([ct, -st], 1), jnp.concatenate([st, ct], 1)], 0)
    ca, sa = _angles(k[:, None] * k[None, :], n)
    wk = jnp.concatenate([ca, -sa], 0)
    return dict(wf=wf.astype(BF16), wi=wi.astype(BF16), wk=wk.astype(BF16))


def _dense_spectrum_kernel(k_ref, w_ref, sc_ref, o_ref):
    o_ref[...] = _dft_mm(w_ref[...], k_ref[...]) * sc_ref[...]


def _dense_spectrum(kt, scale, wk):
    n, c = kt.shape
    return pl.pallas_call(
        _dense_spectrum_kernel,
        grid=(1,),
        in_specs=[_const_spec((n, c)), _const_spec(wk.shape), _const_spec((1, c))],
        out_specs=pl.BlockSpec((2 * n, c), lambda i: (0, 0)),
        out_shape=jax.ShapeDtypeStruct((2 * n, c), F32),
        compiler_params=_params(1), name="dense_filter_spectrum",
    )(kt, wk, scale)


def _dense_conv_kernel(u_ref, x_ref, wf_ref, wi_ref, h_ref, bias_ref, o_ref):
    u0, u1 = u_ref[0], u_ref[1]
    z = _dft_mm(wf_ref[...], jnp.concatenate([u0, u1], axis=0))
    n = z.shape[0] // 2
    zr, zi = z[:n], z[n:]
    hr, hi = h_ref[:n], h_ref[n:]
    y = _dft_mm(wi_ref[...], jnp.concatenate([zr * hr - zi * hi, zr * hi + zi * hr], axis=0))
    l = y.shape[0] // 2
    bias = bias_ref[...]
    o_ref[0] = x_ref[0] * (y[:l] + u0 * bias)
    o_ref[1] = x_ref[1] * (y[l:] + u1 * bias)


def _dense_conv_gate(u, x, hspec, order, bias, tabs):
    b, l, c = u.shape
    blk = pl.BlockSpec((2, l, c), lambda p_: (p_, 0, 0))
    return pl.pallas_call(
        _dense_conv_kernel,
        grid=(b // 2,),
        in_specs=[blk, blk, _const_spec(tabs["wf"].shape), _const_spec(tabs["wi"].shape),
                  pl.BlockSpec((4 * l, c), lambda p_: (0, order)), _const_spec((1, c))],
        out_specs=blk,
        out_shape=jax.ShapeDtypeStruct((b, l, c), F32),
        compiler_params=_params(1), name="dense_conv_gate",
    )(u, x, tabs["wf"], tabs["wi"], hspec, bias.reshape(1, c))


def _mix_kernel(h_ref, o_ref, pm_ref, y_ref, g_ref, gate_ref, wa_ref, wp_ref, wh_ref, wo_ref, out_ref):
    d = h_ref.shape[2]
    a = _dot(o_ref[0].astype(BF16), wa_ref[...])
    p = _dot(pm_ref[0].astype(BF16), wp_ref[...])
    y = _dot(y_ref[0].astype(BF16), wh_ref[...])
    merged = g_ref[0, :, 0:d] * a + g_ref[0, :, d:2 * d] * p + g_ref[0, :, 2 * d:3 * d] * y
    out = _dot(merged.astype(BF16), wo_ref[...])
    out_ref[0] = h_ref[0] + gate_ref[0] * out


def _mix(h, o, pm, y, g, mod, w_attn_o, w_pool_o, w_hy_o, w_out, tm=512):
    b, s, d = h.shape
    tm = min(tm, s)

    def tok(w):
        return pl.BlockSpec((1, tm, w), lambda b_, i: (b_, i, 0))

    ws = [w.astype(BF16) for w in (w_attn_o, w_pool_o, w_hy_o, w_out)]
    return pl.pallas_call(
        _mix_kernel,
        grid=(b, s // tm),
        in_specs=[tok(d), tok(o.shape[2]), tok(pm.shape[2]), tok(y.shape[2]), tok(g.shape[2]),
                  _mod_spec(mod, d, 2)] + [_const_spec(w.shape) for w in ws],
        out_specs=tok(d),
        out_shape=jax.ShapeDtypeStruct((b, s, d), F32),
        compiler_params=_params(2), name="merge_outproj",
    )(h, o, pm, y, g, mod, *ws)


def _ffn_kernel(h_ref, sh_ref, sc_ref, gate_ref, g_ref, wi_ref, wo_ref, fg_ref, out_ref, *,
                hidden, chunk, final):
    x = h_ref[0]
    a = _norm_mod(x, g_ref[...], sh_ref[0], sc_ref[0]).astype(BF16)
    acc = jnp.zeros(x.shape, F32)
    for c0 in range(0, hidden, chunk):
        gt = _dot(a, wi_ref[:, c0:c0 + chunk])
        up = _dot(a, wi_ref[:, hidden + c0:hidden + c0 + chunk])
        act = (gt * jax.nn.sigmoid(gt) * up).astype(BF16)
        acc = acc + _dot(act, wo_ref[c0:c0 + chunk, :])
    y = x + gate_ref[0] * acc
    if final:
        ms = jnp.mean(y * y, axis=-1, keepdims=True)
        y = y * lax.rsqrt(ms + EPS) * fg_ref[...]
    out_ref[0] = y


def _ffn(h, mod, norm_g, w_in, w_out, final_g, final, tm=512):
    b, s, d = h.shape
    hidden = w_out.shape[0]
    tm = min(tm, s)
    tok = pl.BlockSpec((1, tm, d), lambda b_, i: (b_, i, 0))
    return pl.pallas_call(
        functools.partial(_ffn_kernel, hidden=hidden, chunk=hidden // 2, final=final),
        grid=(b, s // tm),
        in_specs=[tok, _mod_spec(mod, d, 3), _mod_spec(mod, d, 4), _mod_spec(mod, d, 5),
                  _const_spec((1, d)), _const_spec(w_in.shape), _const_spec(w_out.shape),
                  _const_spec((1, d))],
        out_specs=tok,
        out_shape=jax.ShapeDtypeStruct((b, s, d), F32),
        compiler_params=_params(2), name="swiglu_ffn",
    )(h, mod, mod, mod, norm_g.reshape(1, d), w_in.astype(BF16), w_out.astype(BF16),
      final_g.reshape(1, d))


def _rope_tables(seq, width):
    t = jnp.arange(seq, dtype=jnp.int32)
    row = (t // GRID_W).astype(F32)
    col = (t % GRID_W).astype(F32)
    inv = 1.0 / (ROPE_BASE ** (jnp.arange(ROPE_FREQS, dtype=F32) * 2.0 / (2 * ROPE_FREQS)))
    ar, ac = row[:, None] * inv, col[:, None] * inv
    cos = jnp.concatenate([jnp.cos(ar)] * 2 + [jnp.cos(ac)] * 2, axis=1)
    sin = jnp.concatenate([-jnp.sin(ar), jnp.sin(ar), -jnp.sin(ac), jnp.sin(ac)], axis=1)
    reps = width // cos.shape[1]
    return jnp.tile(cos, (1, reps)), jnp.tile(sin, (1, reps))


def kernel(x, c, ctx, c_ctx, w_mod, b_mod, norm1_g, norm2_g, w_in, lam_qk, subln_g, w_attn_o, w_pool, pool_scale, w_pool_o, w_short, b_short, hf_w1, hf_b1, hf_freq1, hf_w2, hf_b2, hf_freq2, hf_w3, hy_bias, w_hy_o, w_out, w_ffn_in, w_ffn_out, final_g):
    batch, seq, d = x.shape
    depth = w_mod.shape[0]
    ctx_len = ctx.shape[1]
    aw = w_attn_o.shape[1]
    pw = w_pool_o.shape[1]
    hw = w_hy_o.shape[1]
    q0, k0, v0, p0 = 0, aw, 2 * aw, 3 * aw
    h0 = p0 + pw
    g0 = h0 + 3 * hw
    assert seq * 2 == DFT_R * DFT_R and batch % 2 == 0

    c8 = jnp.zeros((8, d), F32).at[:batch].set(c).at[batch].set(c_ctx)
    mod_all = _modulation(c8, w_mod, b_mod)
    rope_tabs = _rope_tables(seq, ATTN_VD)
    tabs = _dft_tables()
    ctx_tabs = _dense_tables(ctx_len)

    h_lat, h_ctx = x, ctx
    for l in range(depth):
        last = l == depth - 1
        lam_init = LAMBDA_INIT_BASE - LAMBDA_INIT_AMP * math.exp(-LAMBDA_INIT_RATE * l)
        mod = mod_all[l, :batch].reshape(batch, 1, N_MOD * d)
        mod_c = mod_all[l, batch:batch + 1].reshape(1, 1, N_MOD * d)
        wl = w_in[l]
        fparams = (hf_w1[l], hf_b1[l], hf_freq1[l], hf_w2[l], hf_b2[l], hf_freq2[l], hf_w3[l])

        wl_bf = wl.astype(BF16)
        local_params = (w_pool[l], pool_scale[l], w_short[l], b_short[l])
        out_dt = [BF16, F32, F32, F32, F32, BF16, BF16, BF16]
        local_seg = ("local", p0, pw + 3 * hw, 1.0)
        gate_seg = ("sigmoid", g0, 3 * d, 1.0)
        segs = [local_seg, gate_seg, ("rope", q0, aw, Q_SCALE), ("rope", k0, aw, 1.0), ("plain", v0, aw, 1.0)]
        pm, vv, x1, x2, g, q, k, v = _inproj(h_lat, mod, norm1_g[l], wl_bf, segs, out_dt, rope_tabs, local_params)

        if last:
            segs_c = [("plain", k0, aw, 1.0), ("plain", v0, aw, 1.0)]
            k_c, v_c = _inproj(h_ctx, mod_c, norm1_g[l], wl_bf, segs_c, [BF16, BF16])
        else:
            segs_c = [local_seg, gate_seg, ("plain", q0, aw, Q_SCALE), ("plain", k0, aw, 1.0), ("plain", v0, aw, 1.0)]
            pm_c, vv_c, x1_c, x2_c, g_c, q_c, k_c, v_c = _inproj(h_ctx, mod_c, norm1_g[l], wl_bf, segs_c, out_dt,
                                                                 None, local_params)
            o_c = _attention(q_c, k_c, v_c, lam_qk[l], subln_g[l], lam_init)
            kt_c, inv_norm_c = _hyena_time_kernel(ctx_len, *fparams)
            hspec_c = _dense_spectrum(kt_c, inv_norm_c / (2 * ctx_len), ctx_tabs["wk"])
            z_c = _dense_conv_gate(vv_c, x1_c, hspec_c, 0, hy_bias[l, 0], ctx_tabs)
            y_c = _dense_conv_gate(z_c, x2_c, hspec_c, 1, hy_bias[l, 1], ctx_tabs)
            h_ctx_mid = _mix(h_ctx, o_c, pm_c, y_c, g_c, mod_c, w_attn_o[l], w_pool_o[l], w_hy_o[l], w_out[l])
            h_ctx_new = _ffn(h_ctx_mid, mod_c, norm2_g[l], w_ffn_in[l], w_ffn_out[l], final_g, False)

        vt = jnp.swapaxes(jnp.concatenate([v_c, v], axis=1), 1, 2)
        o = _attention_t(q, jnp.concatenate([k_c, k], axis=1), vt, lam_qk[l], subln_g[l], lam_init)
        kt, inv_norm = _hyena_time_kernel(seq, *fparams)
        hspec = _filter_spectrum(kt, inv_norm / (2 * seq), tabs)
        z = _long_conv_gate(vv, x1, hspec, 0, hy_bias[l, 0], tabs)
        y = _long_conv_gate(z, x2, hspec, 1, hy_bias[l, 1], tabs)
        h_mid = _mix(h_lat, o, pm, y, g, mod, w_attn_o[l], w_pool_o[l], w_hy_o[l], w_out[l])
        h_lat = _ffn(h_mid, mod, norm2_g[l], w_ffn_in[l], w_ffn_out[l], final_g, last)
        if not last:
            h_ctx = h_ctx_new
    return h_lat
```

```python
import functools
import math

import jax
import jax.numpy as jnp
from jax import lax
from jax.experimental import pallas as pl
from jax.experimental.pallas import tpu as pltpu

F32 = jnp.float32
BF16 = jnp.bfloat16
HIGHEST = lax.Precision.HIGHEST

GRID_W = 64
N_MOD = 6
ATTN_HD = 64
ATTN_VD = 128
Q_SCALE = ATTN_HD ** -0.5 * math.log2(math.e)
ROPE_BASE = 10000.0
ROPE_FREQS = 16
LAMBDA_INIT_BASE = 0.8
LAMBDA_INIT_AMP = 0.6
LAMBDA_INIT_RATE = 0.3
POOL_WINDOWS = (2, 4, 8, 16)
POOL_GC = 128
HALO = 8
PROJ_PIECE = 1024
HYENA_BANDS = 16
HYENA_FAST_DECAY = 0.3
HYENA_SLOW_DECAY = 1.5
HYENA_TARGET = 1e-2
EPS = 1e-6
DFT_R = 128
HALF_ROWS = 72

VMEM_LIMIT = 56 * 1024 * 1024


def _params(n_grid):
    return pltpu.CompilerParams(dimension_semantics=("arbitrary",) * n_grid,
                                vmem_limit_bytes=VMEM_LIMIT)


def _const_spec(shape):
    zeros = (0,) * len(shape)
    return pl.BlockSpec(shape, lambda *_: zeros, pipeline_mode=pl.Buffered(1))


def _dot(a, b):
    return jnp.dot(a, b, preferred_element_type=F32)


def _dft_mm(w, x):
    return _dot(w, x.astype(BF16))


def _mod_kernel(c_ref, w_ref, b_ref, o_ref):
    c = c_ref[...]
    act = c * jax.nn.sigmoid(c)
    o_ref[0] = jnp.dot(act, w_ref[0], precision=HIGHEST, preferred_element_type=F32) + b_ref[0]


def _modulation(c8, w_mod, b_mod):
    depth, d, n = w_mod.shape
    tn = 1536
    return pl.pallas_call(
        _mod_kernel,
        grid=(depth, n // tn),
        in_specs=[pl.BlockSpec((8, d), lambda l, j: (0, 0)),
                  pl.BlockSpec((1, d, tn), lambda l, j: (l, 0, j)),
                  pl.BlockSpec((1, 1, tn), lambda l, j: (l, 0, j))],
        out_specs=pl.BlockSpec((1, 8, tn), lambda l, j: (l, 0, j)),
        out_shape=jax.ShapeDtypeStruct((depth, 8, n), F32),
        compiler_params=_params(2),
        name="modulation",
    )(c8, w_mod, b_mod.reshape(depth, 1, n))


def _mod_spec(mod, d, col):
    if mod.shape[0] == 1:
        return pl.BlockSpec((1, 1, d), lambda b, i: (0, 0, col))
    return pl.BlockSpec((1, 1, d), lambda b, i: (b, 0, col))


def _norm_mod(x, g, shift, scale):
    ms = jnp.mean(x * x, axis=-1, keepdims=True)
    a = x * lax.rsqrt(ms + EPS) * g
    return a * (1.0 + scale) + shift


def _rope(p, cos, sin):
    lanes = cos.shape[1]
    lane = lax.broadcasted_iota(jnp.int32, cos.shape, 1)
    is_a = lane % (2 * ROPE_FREQS) < ROPE_FREQS
    out = []
    for g in range(p.shape[1] // lanes):
        x = p[:, g * lanes:(g + 1) * lanes]
        partner = jnp.where(is_a, pltpu.roll(x, lanes - ROPE_FREQS, 1), pltpu.roll(x, ROPE_FREQS, 1))
        out.append(x * cos + partner * sin)
    return jnp.concatenate(out, axis=1)


def _local_steps(pbuf, hbuf, wp_ref, ps_ref, ws_ref, bs_ref, pm_ref, v_ref, x1_ref, x2_ref, i, ts, seq):
    t = i * ts + lax.broadcasted_iota(jnp.int32, (ts, 1), 0)

    def pool(g, w):
        cs = slice(g * POOL_GC, (g + 1) * POOL_GC)
        acc = pbuf[HALO - w // 2:HALO - w // 2 + ts, cs]
        for j in range(1 - w // 2, w // 2):
            acc = acc + pbuf[HALO + j:HALO + j + ts, cs]
        cnt = (jnp.minimum(t + w // 2, seq) - jnp.maximum(t - w // 2, 0)).astype(F32)
        pooled = acc / cnt - pbuf[HALO:HALO + ts, cs]
        mixed = _dot(pooled.astype(BF16), wp_ref[g])
        pm_ref[0, :, cs] = (mixed * ps_ref[:, cs]).astype(pm_ref.dtype)

    def conv(o_ref, k):
        c = o_ref.shape[2]
        cs = slice(k * c, (k + 1) * c)
        o_ref[0] = (hbuf[HALO - 1:HALO - 1 + ts, cs] * ws_ref[0:1, cs] + hbuf[HALO:HALO + ts, cs] * ws_ref[1:2, cs]
                    + hbuf[HALO + 1:HALO + 1 + ts, cs] * ws_ref[2:3, cs] + bs_ref[:, cs])

    return ([functools.partial(pool, g, w) for g, w in enumerate(POOL_WINDOWS)]
            + [functools.partial(conv, o_ref, k) for k, o_ref in enumerate((v_ref, x1_ref, x2_ref))])


def _inproj_kernel(*refs, segs, rope, local, tm, seq):
    it = iter(refs)
    h_ref = next(it)
    hp_ref, hn_ref = (next(it), next(it)) if local else (None, None)
    sh_ref, sc_ref, g_ref, w_ref = next(it), next(it), next(it), next(it)
    cos_ref, sin_ref = (next(it), next(it)) if rope else (None, None)
    loc_refs = [next(it) for _ in range(4)] if local else None
    rest = list(it)
    n_out = sum(4 if kind == "local" else 1 for kind, _, _, _ in segs)
    out_refs, scratch = rest[:n_out], rest[n_out:]

    g, sh, sc = g_ref[...], sh_ref[0], sc_ref[0]
    a32 = _norm_mod(h_ref[0], g, sh, sc)
    a = a32.astype(BF16)

    def project(kind, col, width, scale, o_ref, c0):
        p = _dot(a, w_ref[:, col + c0:col + c0 + width])
        if kind == "rope":
            p = _rope(p, cos_ref[...], sin_ref[...])
        elif kind == "sigmoid":
            p = jax.nn.sigmoid(p)
        if scale != 1.0:
            p = p * scale
        o_ref[0, :, c0:c0 + width] = p.astype(o_ref.dtype)

    mxu_work, vpu_work = [], []
    oi = 0
    for kind, col, width, scale in segs:
        if kind == "local":
            i = pl.program_id(1)
            ext = jnp.concatenate([_norm_mod(hp_ref[0], g, sh, sc), a32, _norm_mod(hn_ref[0], g, sh, sc)],
                                  axis=0).astype(BF16)
            u = _dot(ext, w_ref[:, col:col + width])
            row = lax.broadcasted_iota(jnp.int32, (tm + 2 * HALO, 1), 0)
            inside = jnp.logical_and(jnp.logical_or(i > 0, row >= HALO),
                                     jnp.logical_or(i < pl.num_programs(1) - 1, row < HALO + tm))
            u = jnp.where(inside, u, 0.0)
            pbuf, hbuf = scratch
            cp = pbuf.shape[1]
            pbuf[...] = u[:, :cp]
            hbuf[...] = u[:, cp:]
            vpu_work = _local_steps(pbuf, hbuf, *loc_refs, *out_refs[oi:oi + 4], i, tm, seq)
            oi += 4
            continue
        piece = min(width, PROJ_PIECE)
        for c0 in range(0, width, piece):
            mxu_work.append(functools.partial(project, kind, col, piece, scale, out_refs[oi], c0))
        oi += 1
    for k in range(max(len(mxu_work), len(vpu_work))):
        if k < len(vpu_work):
            vpu_work[k]()
        if k < len(mxu_work):
            mxu_work[k]()


def _inproj(h, mod, norm_g, w_cat, segs, out_dtypes, rope_tabs=None, local_params=None, tm=256):
    b, s, d = h.shape
    tm = min(tm, s)
    rope = rope_tabs is not None
    local = local_params is not None
    nh = tm // HALO
    tok = lambda w: pl.BlockSpec((1, tm, w), lambda b_, i: (b_, i, 0))
    in_specs = [tok(d)]
    args = [h]
    if local:
        in_specs += [pl.BlockSpec((1, HALO, d), lambda b_, i: (b_, jnp.maximum(i * nh - 1, 0), 0)),
                     pl.BlockSpec((1, HALO, d), lambda b_, i: (b_, jnp.minimum((i + 1) * nh, s // HALO - 1), 0))]
        args += [h, h]
    in_specs += [_mod_spec(mod, d, 0), _mod_spec(mod, d, 1), _const_spec((1, d)), _const_spec(w_cat.shape)]
    args += [mod, mod, norm_g.reshape(1, d), w_cat]
    if rope:
        wr = rope_tabs[0].shape[1]
        in_specs += [pl.BlockSpec((tm, wr), lambda b_, i: (i, 0))] * 2
        args += list(rope_tabs)
    scratch = []
    if local:
        w_pool, pool_scale, w_short, b_short = local_params
        cp, ch = pool_scale.shape[0], b_short.shape[0]
        in_specs += [_const_spec(w_pool.shape), _const_spec((1, cp)), _const_spec(w_short.shape),
                     _const_spec((1, ch))]
        args += [w_pool.astype(BF16), pool_scale.reshape(1, cp), w_short, b_short.reshape(1, ch)]
        scratch = [pltpu.VMEM((tm + 2 * HALO, cp), F32), pltpu.VMEM((tm + 2 * HALO, ch), F32)]
    widths = []
    for kind, _, w, _ in segs:
        widths += [cp, ch // 3, ch // 3, ch // 3] if kind == "local" else [w]
    out_specs = [tok(w) for w in widths]
    out_shape = [jax.ShapeDtypeStruct((b, s, w), dt) for w, dt in zip(widths, out_dtypes)]
    return pl.pallas_call(
        functools.partial(_inproj_kernel, segs=tuple(segs), rope=rope, local=local, tm=tm, seq=s),
        grid=(b, s // tm), in_specs=in_specs, out_specs=out_specs, out_shape=out_shape,
        scratch_shapes=scratch, compiler_params=_params(2), name="inproj",
    )(*args)


def _lambda(lq_ref, lam_init):
    lq = lq_ref[...]
    return (jnp.exp(jnp.sum(lq[0:1] * lq[1:2], axis=-1, keepdims=True))
            - jnp.exp(jnp.sum(lq[2:3] * lq[3:4], axis=-1, keepdims=True)) + lam_init)


def _split_halves(q):
    lane = lax.broadcasted_iota(jnp.int32, q.shape, 1)
    zero = jnp.zeros_like(q)
    return jnp.concatenate([jnp.where(lane < ATTN_HD, q, zero),
                            jnp.where(lane >= ATTN_HD, q, zero)], axis=0)


def _qk(qq, k):
    return lax.dot_general(qq, k, (((1,), (1,)), ((), ())), preferred_element_type=F32)


def _sub_ln(o, g, lam_init):
    ms = jnp.mean(o * o, axis=-1, keepdims=True)
    return o * lax.rsqrt(ms + EPS) * g * (1.0 - lam_init)


def _attn_kernel(lq_ref, q_ref, k_ref, v_ref, g_ref, o_ref, *, lam_init, tq):
    lam = _lambda(lq_ref, lam_init)
    s = _qk(_split_halves(q_ref[0]), k_ref[0])
    m = jnp.max(s, axis=-1, keepdims=True)
    p = jnp.exp2(s - m)
    l = jnp.sum(p, axis=-1, keepdims=True)
    a = (p[:tq] - p[tq:] * (lam * l[:tq] / l[tq:])).astype(BF16)
    o = _dot(a, v_ref[0]) / l[:tq]
    o_ref[0] = _sub_ln(o, g_ref[...], lam_init).astype(o_ref.dtype)


def _attn_t_kernel(lq_ref, q_ref, k_ref, vt_ref, g_ref, o_ref, s0, s1, m_scr, *, lam_init, tq, lk, kc, nq):
    j = pl.program_id(2)
    bufs = (s0, s1)

    @pl.when(j == 0)
    def _():
        s1[...] = jnp.zeros_like(s1)
        m_scr[1] = jnp.zeros(m_scr.shape[1:], F32)

    lam = _lambda(lq_ref, lam_init)
    for r in range(2):
        sig = 2 * j + r
        buf_a, buf_b = bufs[r], bufs[1 - r]
        t_a = jnp.minimum(sig, nq - 1)
        t_b = jnp.clip(sig - 1, 0, nq - 1)
        qq = _split_halves(q_ref[0, pl.ds(pl.multiple_of(t_a * tq, tq), tq), :])
        m_b = m_scr[1 - r]
        m8 = l8 = acc = None
        for c0 in range(0, lk, kc):
            rows = slice(c0, c0 + kc)
            s = _qk(k_ref[0, rows, :], qq)
            buf_a[rows, :] = s
            for i in range(0, kc, 8):
                m8 = s[i:i + 8] if m8 is None else jnp.maximum(m8, s[i:i + 8])
            p = jnp.exp2(buf_b[rows, :].reshape(kc // 8, 8, 2 * tq) - m_b)
            lc = jnp.sum(p, axis=0)
            l8 = lc if l8 is None else l8 + lc
            pv = _dot(vt_ref[0, :, rows], p.reshape(kc, 2 * tq).astype(BF16))
            acc = pv if acc is None else acc + pv
        m_scr[r] = jnp.broadcast_to(jnp.max(m8, axis=0, keepdims=True), m8.shape)
        l = jnp.sum(l8, axis=0, keepdims=True)
        ot = acc[:, :tq] / l[:, :tq] - acc[:, tq:] * (lam / l[:, tq:])
        ms = jnp.mean(ot * ot, axis=0, keepdims=True)
        ot = ot * lax.rsqrt(ms + EPS) * g_ref[...] * (1.0 - lam_init)
        o_ref[0, pl.ds(pl.multiple_of(t_b * tq, tq), tq), :] = ot.T.astype(o_ref.dtype)


def _attention_t(q, k, vt, lam_qk, subln_g, lam_init, tq=128, kc=768):
    b, s, w = q.shape
    lk = k.shape[1]
    heads = w // ATTN_VD
    nq = s // tq
    nsteps = -(-(nq + 1) // 2)
    return pl.pallas_call(
        functools.partial(_attn_t_kernel, lam_init=lam_init, tq=tq, lk=lk, kc=kc, nq=nq),
        grid=(b, heads, nsteps),
        in_specs=[pl.BlockSpec(lam_qk.shape, lambda b_, h, i: (0, 0)),
                  pl.BlockSpec((1, s, ATTN_VD), lambda b_, h, i: (b_, 0, h)),
                  pl.BlockSpec((1, lk, ATTN_VD), lambda b_, h, i: (b_, 0, h)),
                  pl.BlockSpec((1, ATTN_VD, lk), lambda b_, h, i: (b_, h, 0)),
                  pl.BlockSpec((ATTN_VD, 1), lambda b_, h, i: (0, 0))],
        out_specs=pl.BlockSpec((1, s, ATTN_VD), lambda b_, h, i: (b_, 0, h)),
        out_shape=jax.ShapeDtypeStruct((b, s, w), BF16),
        scratch_shapes=[pltpu.VMEM((lk, 2 * tq), F32)] * 2 + [pltpu.VMEM((2, 8, 2 * tq), F32)],
        compiler_params=_params(3), name="diff_attention_t",
    )(lam_qk, q, k, vt, subln_g.reshape(ATTN_VD, 1))


def _attention(q, k, v, lam_qk, subln_g, lam_init, tq=128):
    b, s, w = q.shape
    lk = k.shape[1]
    heads = w // ATTN_VD
    tq = min(tq, s)
    return pl.pallas_call(
        functools.partial(_attn_kernel, lam_init=lam_init, tq=tq),
        grid=(b, heads, s // tq),
        in_specs=[pl.BlockSpec(lam_qk.shape, lambda b_, h, i: (0, 0)),
                  pl.BlockSpec((1, tq, ATTN_VD), lambda b_, h, i: (b_, i, h)),
                  pl.BlockSpec((1, lk, ATTN_VD), lambda b_, h, i: (b_, 0, h)),
                  pl.BlockSpec((1, lk, ATTN_VD), lambda b_, h, i: (b_, 0, h)),
                  pl.BlockSpec((1, ATTN_VD), lambda b_, h, i: (0, 0))],
        out_specs=pl.BlockSpec((1, tq, ATTN_VD), lambda b_, h, i: (b_, i, h)),
        out_shape=jax.ShapeDtypeStruct((b, s, w), BF16),
        compiler_params=_params(3), name="diff_attention",
    )(lam_qk, q, k, v, subln_g.reshape(1, ATTN_VD))


def _filter_kernel(z_ref, w1_ref, b1_ref, f1_ref, w2_ref, b2_ref, f2_ref, w3_ref, dl_ref, h_ref, asum_ref, *,
                   tl, l):
    i = pl.program_id(0)
    half = tl // 2
    z = z_ref[...]
    z2 = jnp.concatenate([z[:half], z[half:]], axis=1)
    pre = jnp.dot(z2, w1_ref[...], precision=HIGHEST, preferred_element_type=F32)
    h1 = jnp.sin(f1_ref[...] * (pre + b1_ref[...]))
    h2 = jnp.sin(f2_ref[...] * (jnp.dot(h1, w2_ref[...], precision=HIGHEST,
                                        preferred_element_type=F32) + b2_ref[...]))
    hh = h2.astype(BF16)
    hl = (h2 - hh.astype(F32)).astype(BF16)
    h2s = jnp.concatenate([hh, hl, hh], axis=1)
    part = None
    for j in range(2):
        rows = slice(j * half, (j + 1) * half)
        h3 = _dot(h2s, w3_ref[0, j])
        n = i * tl + j * half + lax.broadcasted_iota(jnp.int32, (half, 1), 0)
        h = jnp.where(n == l, 0.0, h3 * jnp.exp(-z[rows, 0:1] * dl_ref[...]))
        h_ref[rows, :] = h
        pj = jnp.sum(jnp.abs(h), axis=0, keepdims=True)
        part = pj if part is None else part + pj

    @pl.when(i == 0)
    def _():
        asum_ref[...] = jnp.zeros_like(asum_ref)

    asum_ref[...] += part


def _hyena_time_kernel(l, w1, b1, f1, w2, b2, f2, w3):
    ffn = w1.shape[1]
    nc = w3.shape[1] // 2
    c = nc // 2
    tl = min(l, 512)
    w3d = jnp.swapaxes(w3.reshape(ffn, 2, nc), 0, 1)
    n = jnp.arange(2 * l, dtype=jnp.int32)
    t = jnp.where(n < l, n, 2 * l - n).astype(F32)
    bands = jnp.linspace(1e-4, HYENA_BANDS - 1, HYENA_BANDS, dtype=F32)
    ang = 2.0 * math.pi * t[:, None] * bands[None, :] / l
    emb = 2 * HYENA_BANDS + 1
    z = jnp.concatenate([(t / (l - 1))[:, None], jnp.cos(ang), jnp.sin(ang),
                         jnp.zeros((2 * l, 128 - emb), F32)], axis=1)
    w1p = jnp.zeros((128, ffn), F32).at[:emb].set(w1)
    zf = jnp.zeros((ffn, ffn), F32)
    w1b = jnp.concatenate([jnp.concatenate([w1p, jnp.zeros_like(w1p)], 1),
                           jnp.concatenate([jnp.zeros_like(w1p), w1p], 1)], 0)
    w2b = jnp.concatenate([jnp.concatenate([w2, zf], 1), jnp.concatenate([zf, w2], 1)], 0)
    w3hi = w3d.astype(BF16)
    w3lo = (w3d - w3hi.astype(F32)).astype(BF16)
    zw = jnp.zeros_like(w3hi)
    first = jnp.concatenate([w3hi, zw, w3hi, zw, w3lo, zw], axis=1)
    second = jnp.concatenate([zw, w3hi, zw, w3hi, zw, w3lo], axis=1)
    w3s = jnp.stack([first, second], axis=1)
    max_decay = math.log(HYENA_TARGET) / HYENA_FAST_DECAY
    min_decay = math.log(HYENA_TARGET) / HYENA_SLOW_DECAY
    deltas = jnp.abs(jnp.linspace(min_decay, max_decay, c, dtype=F32))
    dl = jnp.tile(deltas, 2).reshape(1, nc)
    row2 = lambda a: jnp.tile(a.reshape(1, -1), (1, 2))
    nfwd = l // tl
    kt, asum = pl.pallas_call(
        functools.partial(_filter_kernel, tl=tl, l=l),
        grid=(2 * nfwd,),
        in_specs=[pl.BlockSpec((tl, 128), lambda i: (i, 0)), _const_spec(w1b.shape),
                  _const_spec((1, 2 * ffn)), _const_spec((1, 2 * ffn)),
                  _const_spec(w2b.shape), _const_spec((1, 2 * ffn)), _const_spec((1, 2 * ffn)),
                  pl.BlockSpec((1,) + w3s.shape[1:], lambda i: (i // nfwd, 0, 0, 0)), _const_spec((1, nc))],
        out_specs=[pl.BlockSpec((tl, nc), lambda i: (i, 0)), pl.BlockSpec((1, nc), lambda i: (0, 0))],
        out_shape=[jax.ShapeDtypeStruct((2 * l, nc), F32), jax.ShapeDtypeStruct((1, nc), F32)],
        compiler_params=_params(1), name="hyena_filter_mlp",
    )(z, w1b, row2(b1), row2(f1), w2b, row2(b2), row2(f2), w3s, dl)
    return kt, 1.0 / asum


def _angles(prod, n):
    th = (2.0 * math.pi / n) * (prod % n).astype(F32)
    return jnp.cos(th), jnp.sin(th)


def _dft_tables():
    r = DFT_R
    n = r * r
    idx = jnp.arange(r, dtype=jnp.int32)
    c, s = _angles(idx[:, None] * idx[None, :], r)
    ch, sh = c[:, :r // 2], s[:, :r // 2]
    w1_data = jnp.concatenate([jnp.concatenate([ch, sh], 1), jnp.concatenate([-sh, ch], 1)], 0)
    pad = jnp.zeros((HALF_ROWS - (r // 2 + 1), r), F32)
    w1_real = jnp.concatenate([c[:r // 2 + 1], pad, -s[:r // 2 + 1], pad], 0)
    ct, st = c[:r // 2], s[:r // 2]
    w2 = jnp.concatenate([jnp.concatenate([ct, -st], 1), jnp.concatenate([st, ct], 1)], 0)
    k = idx[:, None, None] + r * idx[None, :, None]
    cg, sg = _angles(k * idx[None, None, :], n)
    g = jnp.concatenate([jnp.concatenate([cg, sg], 2), jnp.concatenate([-sg, cg], 2)], 1)
    return dict(w1_data=w1_data.astype(BF16), w1_real=w1_real.astype(BF16), w2=w2.astype(BF16),
                g=g.astype(BF16))


N2_TILE = 16
K1_TILE = 8


def _pack_pair(re, im):
    rb = lax.bitcast_convert_type(re.astype(BF16).astype(F32), jnp.uint32)
    ib = lax.bitcast_convert_type(im.astype(BF16).astype(F32), jnp.uint32)
    return rb | (ib >> 16)


def _unpack_pair(w):
    re = lax.bitcast_convert_type(w & jnp.uint32(0xFFFF0000), F32)
    im = lax.bitcast_convert_type(w << 16, F32)
    return re, im


def _dft1_kernel(x_ref, w_ref, o_ref, *, parts, pack):
    r = w_ref.shape[0] // 2
    for i in range(N2_TILE):
        x = jnp.concatenate([x_ref[p, :, i, :] for p in range(parts)], axis=0)
        y = _dft_mm(w_ref[...], x)
        if pack:
            o_ref[0, :, i, :] = _pack_pair(y[:r], y[r:])
        else:
            o_ref[0, 0, :, i, :] = y[:r]
            o_ref[0, 1, :, i, :] = y[r:]


def _dft_major(x, w3, parts, pack=False):
    b, n1, r, c = x.shape
    k1 = w3.shape[0] // 2
    if pack:
        out_spec = pl.BlockSpec((1, k1, N2_TILE, c), lambda p_, j: (p_, 0, j, 0))
        out_shape = jax.ShapeDtypeStruct((b // parts, k1, r, c), jnp.uint32)
    else:
        out_spec = pl.BlockSpec((1, 2, k1, N2_TILE, c), lambda p_, j: (p_, 0, 0, j, 0))
        out_shape = jax.ShapeDtypeStruct((b // parts, 2, k1, r, c), F32)
    return pl.pallas_call(
        functools.partial(_dft1_kernel, parts=parts, pack=pack),
        grid=(b // parts, r // N2_TILE),
        in_specs=[pl.BlockSpec((parts, n1, N2_TILE, c), lambda p_, j: (p_, 0, j, 0)), _const_spec(w3.shape)],
        out_specs=out_spec, out_shape=out_shape,
        compiler_params=_params(2), name="dft_major",
    )(x, w3)


def _spectrum_kernel(a_ref, g_ref, sc_ref, o_ref, *, nyquist):
    sign = jnp.where(pl.program_id(0) <= nyquist, 1.0, -1.0)
    re, im = _unpack_pair(a_ref[0, 0])
    a = jnp.concatenate([re, im * sign], axis=0)
    x = _dft_mm(g_ref[0], a) * sc_ref[...]
    r = x.shape[0] // 2
    o_ref[0, 0] = x[:r].astype(o_ref.dtype)
    o_ref[1, 0] = x[r:].astype(o_ref.dtype)


def _filter_spectrum(kt, scale, tabs):
    r = DFT_R
    c = kt.shape[1]
    a = _dft_major(kt.reshape(1, r, r, c), tabs["w1_real"], 1, pack=True)
    tcs = c
    return pl.pallas_call(
        functools.partial(_spectrum_kernel, nyquist=r // 2),
        grid=(r, c // tcs),
        in_specs=[pl.BlockSpec((1, 1, r, tcs), lambda k, j: (0, jnp.where(k <= r // 2, k, r - k), 0, j)),
                  pl.BlockSpec((1,) + tabs["g"].shape[1:], lambda k, j: (k, 0, 0)),
                  pl.BlockSpec((1, tcs), lambda k, j: (0, j))],
        out_specs=pl.BlockSpec((2, 1, r, tcs), lambda k, j: (0, k, 0, j)),
        out_shape=jax.ShapeDtypeStruct((2, r, r, c), BF16),
        compiler_params=_params(2), name="filter_spectrum",
    )(a, tabs["g"], scale)


def _dft_mm_t(w, x):
    return lax.dot_general(w, x.astype(BF16), (((0,), (0,)), ((), ())), preferred_element_type=F32)


def _dft2_kernel(a_ref, g_ref, h_ref, o_ref):
    for kk in range(K1_TILE):
        hr, hi = h_ref[0, kk].astype(F32), h_ref[1, kk].astype(F32)
        for p in range(a_ref.shape[0]):
            a = jnp.concatenate(_unpack_pair(a_ref[p, kk]), axis=0)
            x = _dft_mm(g_ref[kk], a)
            r = x.shape[0] // 2
            xr, xi = x[:r], x[r:]
            y = jnp.concatenate([xr * hr - xi * hi, xr * hi + xi * hr], axis=0)
            bb = _dft_mm_t(g_ref[kk], y)
            o_ref[p, :, kk, :] = _pack_pair(bb[:r], bb[r:])


def _dft3_kernel(b_ref, w_ref, u_ref, x_ref, bias_ref, o_ref):
    bias = bias_ref[...]
    for i in range(N2_TILE):
        bb = jnp.concatenate(_unpack_pair(b_ref[0, i]), axis=0)
        y = _dft_mm(w_ref[...], bb)
        half = y.shape[0] // 2
        for h in range(2):
            o_ref[h, :, i, :] = x_ref[h, :, i, :] * (y[h * half:(h + 1) * half] + u_ref[h, :, i, :] * bias)


def _long_conv_gate(u, x, hspec, order, bias, tabs):
    b, l, c = u.shape
    r = DFT_R
    p = b // 2
    nat = (b, r // 2, r, c)
    a = _dft_major(u.reshape(nat), tabs["w1_data"], 2, pack=True)
    spec4 = pl.BlockSpec((p, K1_TILE, r, c), lambda k: (0, k, 0, 0))
    bb = pl.pallas_call(
        _dft2_kernel,
        grid=(r // K1_TILE,),
        in_specs=[spec4,
                  pl.BlockSpec((K1_TILE,) + tabs["g"].shape[1:], lambda k: (k, 0, 0)),
                  pl.BlockSpec((2, K1_TILE, r, c), lambda k: (0, k, 0, order))],
        out_specs=pl.BlockSpec((p, r, K1_TILE, c), lambda k: (0, 0, k, 0)),
        out_shape=jax.ShapeDtypeStruct((p, r, r, c), jnp.uint32),
        compiler_params=_params(1), name="dft_minor_filter",
    )(a, tabs["g"], hspec)
    blk = pl.BlockSpec((2, r // 2, N2_TILE, c), lambda p_, j: (p_, 0, j, 0))
    out = pl.pallas_call(
        _dft3_kernel,
        grid=(p, r // N2_TILE),
        in_specs=[pl.BlockSpec((1, N2_TILE, r, c), lambda p_, j: (p_, j, 0, 0)),
                  _const_spec(tabs["w2"].shape), blk, blk, _const_spec((1, c))],
        out_specs=blk,
        out_shape=jax.ShapeDtypeStruct(nat, F32),
        compiler_params=_params(2), name="idft_major_gate",
    )(bb, tabs["w2"], u.reshape(nat), x.reshape(nat), bias.reshape(1, c))
    return out.reshape(b, l, c)


def _dense_tables(l):
    n = 2 * l
    k = jnp.arange(n, dtype=jnp.int32)
    t = jnp.arange(l, dtype=jnp.int32)
    c, s = _angles(k[:, None] * t[None, :], n)
    wf = jnp.concatenate([jnp.concatenate([c, s], 1), jnp.concatenate([-s, c], 1)], 0)
    ct, st = c.T, s.T
    wi = jnp.concatenate([jnp.concatenate([ct, -st], 1), jnp.concatenate([st, ct], 1)], 0)
    ca, sa = _angles(k[:, None] * k[None, :], n)
    wk = jnp.concatenate([ca, -sa], 0)
    return dict(wf=wf.astype(BF16), wi=wi.astype(BF16), wk=wk.astype(BF16))


def _dense_spectrum_kernel(k_ref, w_ref, sc_ref, o_ref):
    o_ref[...] = _dft_mm(w_ref[...], k_ref[...]) * sc_ref[...]


def _dense_spectrum(kt, scale, wk):
    n, c = kt.shape
    return pl.pallas_call(
        _dense_spectrum_kernel,
        grid=(1,),
        in_specs=[_const_spec((n, c)), _const_spec(wk.shape), _const_spec((1, c))],
        out_specs=pl.BlockSpec((2 * n, c), lambda i: (0, 0)),
        out_shape=jax.ShapeDtypeStruct((2 * n, c), F32),
        compiler_params=_params(1), name="dense_filter_spectrum",
    )(kt, wk, scale)


def _dense_conv_kernel(u_ref, x_ref, wf_ref, wi_ref, h_ref, bias_ref, o_ref):
    u0, u1 = u_ref[0], u_ref[1]
    z = _dft_mm(wf_ref[...], jnp.concatenate([u0, u1], axis=0))
    n = z.shape[0] // 2
    zr, zi = z[:n], z[n:]
    hr, hi = h_ref[:n], h_ref[n:]
    y = _dft_mm(wi_ref[...], jnp.concatenate([zr * hr - zi * hi, zr * hi + zi * hr], axis=0))
    l = y.shape[0] // 2
    bias = bias_ref[...]
    o_ref[0] = x_ref[0] * (y[:l] + u0 * bias)
    o_ref[1] = x_ref[1] * (y[l:] + u1 * bias)


def _dense_conv_gate(u, x, hspec, order, bias, tabs):
    b, l, c = u.shape
    blk = pl.BlockSpec((2, l, c), lambda p_: (p_, 0, 0))
    return pl.pallas_call(
        _dense_conv_kernel,
        grid=(b // 2,),
        in_specs=[blk, blk, _const_spec(tabs["wf"].shape), _const_spec(tabs["wi"].shape),
                  pl.BlockSpec((4 * l, c), lambda p_: (0, order)), _const_spec((1, c))],
        out_specs=blk,
        out_shape=jax.ShapeDtypeStruct((b, l, c), F32),
        compiler_params=_params(1), name="dense_conv_gate",
    )(u, x, tabs["wf"], tabs["wi"], hspec, bias.reshape(1, c))


def _mix_kernel(h_ref, o_ref, pm_ref, y_ref, g_ref, gate_ref, wa_ref, wp_ref, wh_ref, wo_ref, out_ref):
    d = h_ref.shape[2]
    a = _dot(o_ref[0].astype(BF16), wa_ref[...])
    p = _dot(pm_ref[0].astype(BF16), wp_ref[...])
    y = _dot(y_ref[0].astype(BF16), wh_ref[...])
    merged = g_ref[0, :, 0:d] * a + g_ref[0, :, d:2 * d] * p + g_ref[0, :, 2 * d:3 * d] * y
    out = _dot(merged.astype(BF16), wo_ref[...])
    out_ref[0] = h_ref[0] + gate_ref[0] * out


def _mix(h, o, pm, y, g, mod, w_attn_o, w_pool_o, w_hy_o, w_out, tm=512):
    b, s, d = h.shape
    tm = min(tm, s)

    def tok(w):
        return pl.BlockSpec((1, tm, w), lambda b_, i: (b_, i, 0))

    ws = [w.astype(BF16) for w in (w_attn_o, w_pool_o, w_hy_o, w_out)]
    return pl.pallas_call(
        _mix_kernel,
        grid=(b, s // tm),
        in_specs=[tok(d), tok(o.shape[2]), tok(pm.shape[2]), tok(y.shape[2]), tok(g.shape[2]),
                  _mod_spec(mod, d, 2)] + [_const_spec(w.shape) for w in ws],
        out_specs=tok(d),
        out_shape=jax.ShapeDtypeStruct((b, s, d), F32),
        compiler_params=_params(2), name="merge_outproj",
    )(h, o, pm, y, g, mod, *ws)


def _ffn_kernel(h_ref, sh_ref, sc_ref, gate_ref, g_ref, wi_ref, wo_ref, fg_ref, out_ref, *,
                hidden, chunk, final):
    x = h_ref[0]
    a = _norm_mod(x, g_ref[...], sh_ref[0], sc_ref[0]).astype(BF16)
    acc = jnp.zeros(x.shape, F32)
    for c0 in range(0, hidden, chunk):
        gt = _dot(a, wi_ref[:, c0:c0 + chunk])
        up = _dot(a, wi_ref[:, hidden + c0:hidden + c0 + chunk])
        act = (gt * jax.nn.sigmoid(gt) * up).astype(BF16)
        acc = acc + _dot(act, wo_ref[c0:c0 + chunk, :])
    y = x + gate_ref[0] * acc
    if final:
        ms = jnp.mean(y * y, axis=-1, keepdims=True)
        y = y * lax.rsqrt(ms + EPS) * fg_ref[...]
    out_ref[0] = y


def _ffn(h, mod, norm_g, w_in, w_out, final_g, final, tm=512):
    b, s, d = h.shape
    hidden = w_out.shape[0]
    tm = min(tm, s)
    tok = pl.BlockSpec((1, tm, d), lambda b_, i: (b_, i, 0))
    return pl.pallas_call(
        functools.partial(_ffn_kernel, hidden=hidden, chunk=hidden // 2, final=final),
        grid=(b, s // tm),
        in_specs=[tok, _mod_spec(mod, d, 3), _mod_spec(mod, d, 4), _mod_spec(mod, d, 5),
                  _const_spec((1, d)), _const_spec(w_in.shape), _const_spec(w_out.shape),
                  _const_spec((1, d))],
        out_specs=tok,
        out_shape=jax.ShapeDtypeStruct((b, s, d), F32),
        compiler_params=_params(2), name="swiglu_ffn",
    )(h, mod, mod, mod, norm_g.reshape(1, d), w_in.astype(BF16), w_out.astype(BF16),
      final_g.reshape(1, d))


def _rope_tables(seq, width):
    t = jnp.arange(seq, dtype=jnp.int32)
    row = (t // GRID_W).astype(F32)
    col = (t % GRID_W).astype(F32)
    inv = 1.0 / (ROPE_BASE ** (jnp.arange(ROPE_FREQS, dtype=F32) * 2.0 / (2 * ROPE_FREQS)))
    ar, ac = row[:, None] * inv, col[:, None] * inv
    cos = jnp.concatenate([jnp.cos(ar)] * 2 + [jnp.cos(ac)] * 2, axis=1)
    sin = jnp.concatenate([-jnp.sin(ar), jnp.sin(ar), -jnp.sin(ac), jnp.sin(ac)], axis=1)
    reps = width // cos.shape[1]
    return jnp.tile(cos, (1, reps)), jnp.tile(sin, (1, reps))


def kernel(x, c, ctx, c_ctx, w_mod, b_mod, norm1_g, norm2_g, w_in, lam_qk, subln_g, w_attn_o, w_pool, pool_scale, w_pool_o, w_short, b_short, hf_w1, hf_b1, hf_freq1, hf_w2, hf_b2, hf_freq2, hf_w3, hy_bias, w_hy_o, w_out, w_ffn_in, w_ffn_out, final_g):
    batch, seq, d = x.shape
    depth = w_mod.shape[0]
    ctx_len = ctx.shape[1]
    aw = w_attn_o.shape[1]
    pw = w_pool_o.shape[1]
    hw = w_hy_o.shape[1]
    q0, k0, v0, p0 = 0, aw, 2 * aw, 3 * aw
    h0 = p0 + pw
    g0 = h0 + 3 * hw
    assert seq * 2 == DFT_R * DFT_R and batch % 2 == 0

    c8 = jnp.zeros((8, d), F32).at[:batch].set(c).at[batch].set(c_ctx)
    mod_all = _modulation(c8, w_mod, b_mod)
    rope_tabs = _rope_tables(seq, ATTN_VD)
    tabs = _dft_tables()
    ctx_tabs = _dense_tables(ctx_len)

    h_lat, h_ctx = x, ctx
    for l in range(depth):
        last = l == depth - 1
        lam_init = LAMBDA_INIT_BASE - LAMBDA_INIT_AMP * math.exp(-LAMBDA_INIT_RATE * l)
        mod = mod_all[l, :batch].reshape(batch, 1, N_MOD * d)
        mod_c = mod_all[l, batch:batch + 1].reshape(1, 1, N_MOD * d)
        wl = w_in[l]
        fparams = (hf_w1[l], hf_b1[l], hf_freq1[l], hf_w2[l], hf_b2[l], hf_freq2[l], hf_w3[l])

        wl_bf = wl.astype(BF16)
        local_params = (w_pool[l], pool_scale[l], w_short[l], b_short[l])
        out_dt = [BF16, F32, F32, F32, F32, BF16, BF16, BF16]
        local_seg = ("local", p0, pw + 3 * hw, 1.0)
        gate_seg = ("sigmoid", g0, 3 * d, 1.0)
        segs = [local_seg, gate_seg, ("rope", q0, aw, Q_SCALE), ("rope", k0, aw, 1.0), ("plain", v0, aw, 1.0)]
        pm, vv, x1, x2, g, q, k, v = _inproj(h_lat, mod, norm1_g[l], wl_bf, segs, out_dt, rope_tabs, local_params)

        if last:
            segs_c = [("plain", k0, aw, 1.0), ("plain", v0, aw, 1.0)]
            k_c, v_c = _inproj(h_ctx, mod_c, norm1_g[l], wl_bf, segs_c, [BF16, BF16])
        else:
            segs_c = [local_seg, gate_seg, ("plain", q0, aw, Q_SCALE), ("plain", k0, aw, 1.0), ("plain", v0, aw, 1.0)]
            pm_c, vv_c, x1_c, x2_c, g_c, q_c, k_c, v_c = _inproj(h_ctx, mod_c, norm1_g[l], wl_bf, segs_c, out_dt,
                                                                 None, local_params)
            o_c = _attention(q_c, k_c, v_c, lam_qk[l], subln_g[l], lam_init)
            kt_c, inv_norm_c = _hyena_time_kernel(ctx_len, *fparams)
            hspec_c = _dense_spectrum(kt_c, inv_norm_c / (2 * ctx_len), ctx_tabs["wk"])
            z_c = _dense_conv_gate(vv_c, x1_c, hspec_c, 0, hy_bias[l, 0], ctx_tabs)
            y_c = _dense_conv_gate(z_c, x2_c, hspec_c, 1, hy_bias[l, 1], ctx_tabs)
            h_ctx_mid = _mix(h_ctx, o_c, pm_c, y_c, g_c, mod_c, w_attn_o[l], w_pool_o[l], w_hy_o[l], w_out[l])
            h_ctx_new = _ffn(h_ctx_mid, mod_c, norm2_g[l], w_ffn_in[l], w_ffn_out[l], final_g, False)

        vt = jnp.swapaxes(jnp.concatenate([v_c, v], axis=1), 1, 2)
        o = _attention_t(q, jnp.concatenate([k_c, k], axis=1), vt, lam_qk[l], subln_g[l], lam_init)
        kt, inv_norm = _hyena_time_kernel(seq, *fparams)
        hspec = _filter_spectrum(kt, inv_norm / (2 * seq), tabs)
        z = _long_conv_gate(vv, x1, hspec, 0, hy_bias[l, 0], tabs)
        y = _long_conv_gate(z, x2, hspec, 1, hy_bias[l, 1], tabs)
        h_mid = _mix(h_lat, o, pm, y, g, mod, w_attn_o[l], w_pool_o[l], w_hy_o[l], w_out[l])
        h_lat = _ffn(h_mid, mod, norm2_g[l], w_ffn_in[l], w_ffn_out[l], final_g, last)
        if not last:
            h_ctx = h_ctx_new
    return h_lat
```

```python
import functools
import math

import jax
import jax.numpy as jnp
from jax import lax
from jax.experimental import pallas as pl
from jax.experimental.pallas import tpu as pltpu

F32 = jnp.float32
BF16 = jnp.bfloat16
HIGHEST = lax.Precision.HIGHEST

GRID_W = 64
N_MOD = 6
ATTN_HD = 64
ATTN_VD = 128
Q_SCALE = ATTN_HD ** -0.5 * math.log2(math.e)
ROPE_BASE = 10000.0
ROPE_FREQS = 16
LAMBDA_INIT_BASE = 0.8
LAMBDA_INIT_AMP = 0.6
LAMBDA_INIT_RATE = 0.3
POOL_WINDOWS = (2, 4, 8, 16)
POOL_GC = 128
HALO = 8
PROJ_PIECE = 1024
HYENA_BANDS = 16
HYENA_FAST_DECAY = 0.3
HYENA_SLOW_DECAY = 1.5
HYENA_TARGET = 1e-2
EPS = 1e-6
DFT_R = 128
HALF_ROWS = 72

VMEM_LIMIT = 56 * 1024 * 1024


def _params(n_grid):
    return pltpu.CompilerParams(dimension_semantics=("arbitrary",) * n_grid,
                                vmem_limit_bytes=VMEM_LIMIT)


def _const_spec(shape):
    zeros = (0,) * len(shape)
    return pl.BlockSpec(shape, lambda *_: zeros, pipeline_mode=pl.Buffered(1))


def _dot(a, b):
    return jnp.dot(a, b, preferred_element_type=F32)


def _dft_mm(w, x):
    return _dot(w, x.astype(BF16))


def _mod_kernel(c_ref, w_ref, b_ref, o_ref):
    c = c_ref[...]
    act = c * jax.nn.sigmoid(c)
    o_ref[0] = jnp.dot(act, w_ref[0], precision=HIGHEST, preferred_element_type=F32) + b_ref[0]


def _modulation(c8, w_mod, b_mod):
    depth, d, n = w_mod.shape
    tn = 1536
    return pl.pallas_call(
        _mod_kernel,
        grid=(depth, n // tn),
        in_specs=[pl.BlockSpec((8, d), lambda l, j: (0, 0)),
                  pl.BlockSpec((1, d, tn), lambda l, j: (l, 0, j)),
                  pl.BlockSpec((1, 1, tn), lambda l, j: (l, 0, j))],
        out_specs=pl.BlockSpec((1, 8, tn), lambda l, j: (l, 0, j)),
        out_shape=jax.ShapeDtypeStruct((depth, 8, n), F32),
        compiler_params=_params(2),
        name="modulation",
    )(c8, w_mod, b_mod.reshape(depth, 1, n))


def _mod_spec(mod, d, col):
    if mod.shape[0] == 1:
        return pl.BlockSpec((1, 1, d), lambda b, i: (0, 0, col))
    return pl.BlockSpec((1, 1, d), lambda b, i: (b, 0, col))


def _norm_mod(x, g, shift, scale):
    ms = jnp.mean(x * x, axis=-1, keepdims=True)
    a = x * lax.rsqrt(ms + EPS) * g
    return a * (1.0 + scale) + shift


def _rope(p, cos, sin):
    lanes = cos.shape[1]
    lane = lax.broadcasted_iota(jnp.int32, cos.shape, 1)
    is_a = lane % (2 * ROPE_FREQS) < ROPE_FREQS
    out = []
    for g in range(p.shape[1] // lanes):
        x = p[:, g * lanes:(g + 1) * lanes]
        partner = jnp.where(is_a, pltpu.roll(x, lanes - ROPE_FREQS, 1), pltpu.roll(x, ROPE_FREQS, 1))
        out.append(x * cos + partner * sin)
    return jnp.concatenate(out, axis=1)


def _local_steps(pbuf, hbuf, wp_ref, ps_ref, ws_ref, bs_ref, pm_ref, v_ref, x1_ref, x2_ref, i, ts, seq):
    t = i * ts + lax.broadcasted_iota(jnp.int32, (ts, 1), 0)

    def pool(g, w):
        cs = slice(g * POOL_GC, (g + 1) * POOL_GC)
        acc = pbuf[HALO - w // 2:HALO - w // 2 + ts, cs]
        for j in range(1 - w // 2, w // 2):
            acc = acc + pbuf[HALO + j:HALO + j + ts, cs]
        cnt = (jnp.minimum(t + w // 2, seq) - jnp.maximum(t - w // 2, 0)).astype(F32)
        pooled = acc / cnt - pbuf[HALO:HALO + ts, cs]
        mixed = _dot(pooled.astype(BF16), wp_ref[g])
        pm_ref[0, :, cs] = (mixed * ps_ref[:, cs]).astype(pm_ref.dtype)

    def conv(o_ref, k):
        c = o_ref.shape[2]
        cs = slice(k * c, (k + 1) * c)
        o_ref[0] = (hbuf[HALO - 1:HALO - 1 + ts, cs] * ws_ref[0:1, cs] + hbuf[HALO:HALO + ts, cs] * ws_ref[1:2, cs]
                    + hbuf[HALO + 1:HALO + 1 + ts, cs] * ws_ref[2:3, cs] + bs_ref[:, cs])

    return ([functools.partial(pool, g, w) for g, w in enumerate(POOL_WINDOWS)]
            + [functools.partial(conv, o_ref, k) for k, o_ref in enumerate((v_ref, x1_ref, x2_ref))])


def _inproj_kernel(*refs, segs, rope, local, tm, seq):
    it = iter(refs)
    h_ref = next(it)
    hp_ref, hn_ref = (next(it), next(it)) if local else (None, None)
    sh_ref, sc_ref, g_ref, w_ref = next(it), next(it), next(it), next(it)
    cos_ref, sin_ref = (next(it), next(it)) if rope else (None, None)
    loc_refs = [next(it) for _ in range(4)] if local else None
    rest = list(it)
    n_out = sum(4 if kind == "local" else 1 for kind, _, _, _ in segs)
    out_refs, scratch = rest[:n_out], rest[n_out:]

    g, sh, sc = g_ref[...], sh_ref[0], sc_ref[0]
    a32 = _norm_mod(h_ref[0], g, sh, sc)
    a = a32.astype(BF16)

    def project(kind, col, width, scale, o_ref, c0):
        p = _dot(a, w_ref[:, col + c0:col + c0 + width])
        if kind == "rope":
            p = _rope(p, cos_ref[...], sin_ref[...])
        elif kind == "sigmoid":
            p = jax.nn.sigmoid(p)
        if scale != 1.0:
            p = p * scale
        o_ref[0, :, c0:c0 + width] = p.astype(o_ref.dtype)

    mxu_work, vpu_work = [], []
    oi = 0
    for kind, col, width, scale in segs:
        if kind == "local":
            i = pl.program_id(1)
            ext = jnp.concatenate([_norm_mod(hp_ref[0], g, sh, sc), a32, _norm_mod(hn_ref[0], g, sh, sc)],
                                  axis=0).astype(BF16)
            u = _dot(ext, w_ref[:, col:col + width])
            row = lax.broadcasted_iota(jnp.int32, (tm + 2 * HALO, 1), 0)
            inside = jnp.logical_and(jnp.logical_or(i > 0, row >= HALO),
                                     jnp.logical_or(i < pl.num_programs(1) - 1, row < HALO + tm))
            u = jnp.where(inside, u, 0.0)
            pbuf, hbuf = scratch
            cp = pbuf.shape[1]
            pbuf[...] = u[:, :cp]
            hbuf[...] = u[:, cp:]
            vpu_work = _local_steps(pbuf, hbuf, *loc_refs, *out_refs[oi:oi + 4], i, tm, seq)
            oi += 4
            continue
        piece = min(width, PROJ_PIECE)
        for c0 in range(0, width, piece):
            mxu_work.append(functools.partial(project, kind, col, piece, scale, out_refs[oi], c0))
        oi += 1
    for k in range(max(len(mxu_work), len(vpu_work))):
        if k < len(vpu_work):
            vpu_work[k]()
        if k < len(mxu_work):
            mxu_work[k]()


def _inproj(h, mod, norm_g, w_cat, segs, out_dtypes, rope_tabs=None, local_params=None, tm=256):
    b, s, d = h.shape
    tm = min(tm, s)
    rope = rope_tabs is not None
    local = local_params is not None
    nh = tm // HALO
    tok = lambda w: pl.BlockSpec((1, tm, w), lambda b_, i: (b_, i, 0))
    in_specs = [tok(d)]
    args = [h]
    if local:
        in_specs += [pl.BlockSpec((1, HALO, d), lambda b_, i: (b_, jnp.maximum(i * nh - 1, 0), 0)),
                     pl.BlockSpec((1, HALO, d), lambda b_, i: (b_, jnp.minimum((i + 1) * nh, s // HALO - 1), 0))]
        args += [h, h]
    in_specs += [_mod_spec(mod, d, 0), _mod_spec(mod, d, 1), _const_spec((1, d)), _const_spec(w_cat.shape)]
    args += [mod, mod, norm_g.reshape(1, d), w_cat]
    if rope:
        wr = rope_tabs[0].shape[1]
        in_specs += [pl.BlockSpec((tm, wr), lambda b_, i: (i, 0))] * 2
        args += list(rope_tabs)
    scratch = []
    if local:
        w_pool, pool_scale, w_short, b_short = local_params
        cp, ch = pool_scale.shape[0], b_short.shape[0]
        in_specs += [_const_spec(w_pool.shape), _const_spec((1, cp)), _const_spec(w_short.shape),
                     _const_spec((1, ch))]
        args += [w_pool.astype(BF16), pool_scale.reshape(1, cp), w_short, b_short.reshape(1, ch)]
        scratch = [pltpu.VMEM((tm + 2 * HALO, cp), F32), pltpu.VMEM((tm + 2 * HALO, ch), F32)]
    widths = []
    for kind, _, w, _ in segs:
        widths += [cp, ch // 3, ch // 3, ch // 3] if kind == "local" else [w]
    out_specs = [tok(w) for w in widths]
    out_shape = [jax.ShapeDtypeStruct((b, s, w), dt) for w, dt in zip(widths, out_dtypes)]
    return pl.pallas_call(
        functools.partial(_inproj_kernel, segs=tuple(segs), rope=rope, local=local, tm=tm, seq=s),
        grid=(b, s // tm), in_specs=in_specs, out_specs=out_specs, out_shape=out_shape,
        scratch_shapes=scratch, compiler_params=_params(2), name="inproj",
    )(*args)


def _lambda(lq_ref, lam_init):
    lq = lq_ref[...]
    return (jnp.exp(jnp.sum(lq[0:1] * lq[1:2], axis=-1, keepdims=True))
            - jnp.exp(jnp.sum(lq[2:3] * lq[3:4], axis=-1, keepdims=True)) + lam_init)


def _split_halves(q):
    lane = lax.broadcasted_iota(jnp.int32, q.shape, 1)
    zero = jnp.zeros_like(q)
    return jnp.concatenate([jnp.where(lane < ATTN_HD, q, zero),
                            jnp.where(lane >= ATTN_HD, q, zero)], axis=0)


def _qk(qq, k):
    return lax.dot_general(qq, k, (((1,), (1,)), ((), ())), preferred_element_type=F32)


def _sub_ln(o, g, lam_init):
    ms = jnp.mean(o * o, axis=-1, keepdims=True)
    return o * lax.rsqrt(ms + EPS) * g * (1.0 - lam_init)


def _attn_kernel(lq_ref, q_ref, k_ref, v_ref, g_ref, o_ref, *, lam_init, tq):
    lam = _lambda(lq_ref, lam_init)
    s = _qk(_split_halves(q_ref[0]), k_ref[0])
    m = jnp.max(s, axis=-1, keepdims=True)
    p = jnp.exp2(s - m)
    l = jnp.sum(p, axis=-1, keepdims=True)
    a = (p[:tq] - p[tq:] * (lam * l[:tq] / l[tq:])).astype(BF16)
    o = _dot(a, v_ref[0]) / l[:tq]
    o_ref[0] = _sub_ln(o, g_ref[...], lam_init).astype(o_ref.dtype)


def _attn_t_kernel(lq_ref, q_ref, k_ref, vt_ref, g_ref, o_ref, s0, s1, m_scr, *, lam_init, tq, lk, kc, nq):
    j = pl.program_id(2)
    bufs = (s0, s1)

    @pl.when(j == 0)
    def _():
        s1[...] = jnp.zeros_like(s1)
        m_scr[1] = jnp.zeros(m_scr.shape[1:], F32)

    lam = _lambda(lq_ref, lam_init)
    for r in range(2):
        sig = 2 * j + r
        buf_a, buf_b = bufs[r], bufs[1 - r]
        t_a = jnp.minimum(sig, nq - 1)
        t_b = jnp.clip(sig - 1, 0, nq - 1)
        qq = _split_halves(q_ref[0, pl.ds(pl.multiple_of(t_a * tq, tq), tq), :])
        m_b = m_scr[1 - r]
        m8 = l8 = acc = None
        for c0 in range(0, lk, kc):
            rows = slice(c0, c0 + kc)
            s = _qk(k_ref[0, rows, :], qq)
            buf_a[rows, :] = s
            for i in range(0, kc, 8):
                m8 = s[i:i + 8] if m8 is None else jnp.maximum(m8, s[i:i + 8])
            p = jnp.exp2(buf_b[rows, :].reshape(kc // 8, 8, 2 * tq) - m_b)
            lc = jnp.sum(p, axis=0)
            l8 = lc if l8 is None else l8 + lc
            pv = _dot(vt_ref[0, :, rows], p.reshape(kc, 2 * tq).astype(BF16))
            acc = pv if acc is None else acc + pv
        m_scr[r] = jnp.broadcast_to(jnp.max(m8, axis=0, keepdims=True), m8.shape)
        l = jnp.sum(l8, axis=0, keepdims=True)
        ot = acc[:, :tq] / l[:, :tq] - acc[:, tq:] * (lam / l[:, tq:])
        ms = jnp.mean(ot * ot, axis=0, keepdims=True)
        ot = ot * lax.rsqrt(ms + EPS) * g_ref[...] * (1.0 - lam_init)
        o_ref[0, pl.ds(pl.multiple_of(t_b * tq, tq), tq), :] = ot.T.astype(o_ref.dtype)


def _attention_t(q, k, vt, lam_qk, subln_g, lam_init, tq=128, kc=768):
    b, s, w = q.shape
    lk = k.shape[1]
    heads = w // ATTN_VD
    nq = s // tq
    nsteps = -(-(nq + 1) // 2)
    return pl.pallas_call(
        functools.partial(_attn_t_kernel, lam_init=lam_init, tq=tq, lk=lk, kc=kc, nq=nq),
        grid=(b, heads, nsteps),
        in_specs=[pl.BlockSpec(lam_qk.shape, lambda b_, h, i: (0, 0)),
                  pl.BlockSpec((1, s, ATTN_VD), lambda b_, h, i: (b_, 0, h)),
                  pl.BlockSpec((1, lk, ATTN_VD), lambda b_, h, i: (b_, 0, h)),
                  pl.BlockSpec((1, ATTN_VD, lk), lambda b_, h, i: (b_, h, 0)),
                  pl.BlockSpec((ATTN_VD, 1), lambda b_, h, i: (0, 0))],
        out_specs=pl.BlockSpec((1, s, ATTN_VD), lambda b_, h, i: (b_, 0, h)),
        out_shape=jax.ShapeDtypeStruct((b, s, w), BF16),
        scratch_shapes=[pltpu.VMEM((lk, 2 * tq), F32)] * 2 + [pltpu.VMEM((2, 8, 2 * tq), F32)],
        compiler_params=_params(3), name="diff_attention_t",
    )(lam_qk, q, k, vt, subln_g.reshape(ATTN_VD, 1))


def _attention(q, k, v, lam_qk, subln_g, lam_init, tq=128):
    b, s, w = q.shape
    lk = k.shape[1]
    heads = w // ATTN_VD
    tq = min(tq, s)
    return pl.pallas_call(
        functools.partial(_attn_kernel, lam_init=lam_init, tq=tq),
        grid=(b, heads, s // tq),
        in_specs=[pl.BlockSpec(lam_qk.shape, lambda b_, h, i: (0, 0)),
                  pl.BlockSpec((1, tq, ATTN_VD), lambda b_, h, i: (b_, i, h)),
                  pl.BlockSpec((1, lk, ATTN_VD), lambda b_, h, i: (b_, 0, h)),
                  pl.BlockSpec((1, lk, ATTN_VD), lambda b_, h, i: (b_, 0, h)),
                  pl.BlockSpec((1, ATTN_VD), lambda b_, h, i: (0, 0))],
        out_specs=pl.BlockSpec((1, tq, ATTN_VD), lambda b_, h, i: (b_, i, h)),
        out_shape=jax.ShapeDtypeStruct((b, s, w), BF16),
        compiler_params=_params(3), name="diff_attention",
    )(lam_qk, q, k, v, subln_g.reshape(1, ATTN_VD))


def _filter_kernel(z_ref, w1_ref, b1_ref, f1_ref, w2_ref, b2_ref, f2_ref, w3_ref, dl_ref, h_ref, asum_ref, *,
                   tl, l):
    i = pl.program_id(0)
    half = tl // 2
    z = z_ref[...]
    z2 = jnp.concatenate([z[:half], z[half:]], axis=1)
    pre = jnp.dot(z2, w1_ref[...], precision=HIGHEST, preferred_element_type=F32)
    h1 = jnp.sin(f1_ref[...] * (pre + b1_ref[...]))
    h2 = jnp.sin(f2_ref[...] * (jnp.dot(h1, w2_ref[...], precision=HIGHEST,
                                        preferred_element_type=F32) + b2_ref[...]))
    hh = h2.astype(BF16)
    hl = (h2 - hh.astype(F32)).astype(BF16)
    h2s = jnp.concatenate([hh, hl, hh], axis=1)
    part = None
    for j in range(2):
        rows = slice(j * half, (j + 1) * half)
        h3 = _dot(h2s, w3_ref[0, j])
        n = i * tl + j * half + lax.broadcasted_iota(jnp.int32, (half, 1), 0)
        h = jnp.where(n == l, 0.0, h3 * jnp.exp(-z[rows, 0:1] * dl_ref[...]))
        h_ref[rows, :] = h
        pj = jnp.sum(jnp.abs(h), axis=0, keepdims=True)
        part = pj if part is None else part + pj

    @pl.when(i == 0)
    def _():
        asum_ref[...] = jnp.zeros_like(asum_ref)

    asum_ref[...] += part


def _hyena_time_kernel(l, w1, b1, f1, w2, b2, f2, w3):
    ffn = w1.shape[1]
    nc = w3.shape[1] // 2
    c = nc // 2
    tl = min(l, 512)
    w3d = jnp.swapaxes(w3.reshape(ffn, 2, nc), 0, 1)
    n = jnp.arange(2 * l, dtype=jnp.int32)
    t = jnp.where(n < l, n, 2 * l - n).astype(F32)
    bands = jnp.linspace(1e-4, HYENA_BANDS - 1, HYENA_BANDS, dtype=F32)
    ang = 2.0 * math.pi * t[:, None] * bands[None, :] / l
    emb = 2 * HYENA_BANDS + 1
    z = jnp.concatenate([(t / (l - 1))[:, None], jnp.cos(ang), jnp.sin(ang),
                         jnp.zeros((2 * l, 128 - emb), F32)], axis=1)
    w1p = jnp.zeros((128, ffn), F32).at[:emb].set(w1)
    zf = jnp.zeros((ffn, ffn), F32)
    w1b = jnp.concatenate([jnp.concatenate([w1p, jnp.zeros_like(w1p)], 1),
                           jnp.concatenate([jnp.zeros_like(w1p), w1p], 1)], 0)
    w2b = jnp.concatenate([jnp.concatenate([w2, zf], 1), jnp.concatenate([zf, w2], 1)], 0)
    w3hi = w3d.astype(BF16)
    w3lo = (w3d - w3hi.astype(F32)).astype(BF16)
    zw = jnp.zeros_like(w3hi)
    first = jnp.concatenate([w3hi, zw, w3hi, zw, w3lo, zw], axis=1)
    second = jnp.concatenate([zw, w3hi, zw, w3hi, zw, w3lo], axis=1)
    w3s = jnp.stack([first, second], axis=1)
    max_decay = math.log(HYENA_TARGET) / HYENA_FAST_DECAY
    min_decay = math.log(HYENA_TARGET) / HYENA_SLOW_DECAY
    deltas = jnp.abs(jnp.linspace(min_decay, max_decay, c, dtype=F32))
    dl = jnp.tile(deltas, 2).reshape(1, nc)
    row2 = lambda a: jnp.tile(a.reshape(1, -1), (1, 2))
    nfwd = l // tl
    kt, asum = pl.pallas_call(
        functools.partial(_filter_kernel, tl=tl, l=l),
        grid=(2 * nfwd,),
        in_specs=[pl.BlockSpec((tl, 128), lambda i: (i, 0)), _const_spec(w1b.shape),
                  _const_spec((1, 2 * ffn)), _const_spec((1, 2 * ffn)),
                  _const_spec(w2b.shape), _const_spec((1, 2 * ffn)), _const_spec((1, 2 * ffn)),
                  pl.BlockSpec((1,) + w3s.shape[1:], lambda i: (i // nfwd, 0, 0, 0)), _const_spec((1, nc))],
        out_specs=[pl.BlockSpec((tl, nc), lambda i: (i, 0)), pl.BlockSpec((1, nc), lambda i: (0, 0))],
        out_shape=[jax.ShapeDtypeStruct((2 * l, nc), F32), jax.ShapeDtypeStruct((1, nc), F32)],
        compiler_params=_params(1), name="hyena_filter_mlp",
    )(z, w1b, row2(b1), row2(f1), w2b, row2(b2), row2(f2), w3s, dl)
    return kt, 1.0 / asum


def _angles(prod, n):
    th = (2.0 * math.pi / n) * (prod % n).astype(F32)
    return jnp.cos(th), jnp.sin(th)


def _dft_tables():
    r = DFT_R
    n = r * r
    idx = jnp.arange(r, dtype=jnp.int32)
    c, s = _angles(idx[:, None] * idx[None, :], r)
    ch, sh = c[:, :r // 2], s[:, :r // 2]
    w1_data = jnp.concatenate([jnp.concatenate([ch, sh], 1), jnp.concatenate([-sh, ch], 1)], 0)
    pad = jnp.zeros((HALF_ROWS - (r // 2 + 1), r), F32)
    w1_real = jnp.concatenate([c[:r // 2 + 1], pad, -s[:r // 2 + 1], pad], 0)
    ct, st = c[:r // 2], s[:r // 2]
    w2 = jnp.concatenate([jnp.concatenate([ct, -st], 1), jnp.concatenate([st, ct], 1)], 0)
    k = idx[:, None, None] + r * idx[None, :, None]
    cg, sg = _angles(k * idx[None, None, :], n)
    g = jnp.concatenate([jnp.concatenate([cg, sg], 2), jnp.concatenate([-sg, cg], 2)], 1)
    return dict(w1_data=w1_data.astype(BF16), w1_real=w1_real.astype(BF16), w2=w2.astype(BF16),
                g=g.astype(BF16))


N2_TILE = 16
K1_TILE = 8


def _pack_pair(re, im):
    rb = lax.bitcast_convert_type(re.astype(BF16).astype(F32), jnp.uint32)
    ib = lax.bitcast_convert_type(im.astype(BF16).astype(F32), jnp.uint32)
    return rb | (ib >> 16)


def _unpack_pair(w):
    re = lax.bitcast_convert_type(w & jnp.uint32(0xFFFF0000), F32)
    im = lax.bitcast_convert_type(w << 16, F32)
    return re, im


def _dft1_kernel(x_ref, w_ref, o_ref, *, parts, pack):
    r = w_ref.shape[0] // 2
    for i in range(N2_TILE):
        x = jnp.concatenate([x_ref[p, :, i, :] for p in range(parts)], axis=0)
        y = _dft_mm(w_ref[...], x)
        if pack:
            o_ref[0, :, i, :] = _pack_pair(y[:r], y[r:])
        else:
            o_ref[0, 0, :, i, :] = y[:r]
            o_ref[0, 1, :, i, :] = y[r:]


def _dft_major(x, w3, parts, pack=False):
    b, n1, r, c = x.shape
    k1 = w3.shape[0] // 2
    if pack:
        out_spec = pl.BlockSpec((1, k1, N2_TILE, c), lambda p_, j: (p_, 0, j, 0))
        out_shape = jax.ShapeDtypeStruct((b // parts, k1, r, c), jnp.uint32)
    else:
        out_spec = pl.BlockSpec((1, 2, k1, N2_TILE, c), lambda p_, j: (p_, 0, 0, j, 0))
        out_shape = jax.ShapeDtypeStruct((b // parts, 2, k1, r, c), F32)
    return pl.pallas_call(
        functools.partial(_dft1_kernel, parts=parts, pack=pack),
        grid=(b // parts, r // N2_TILE),
        in_specs=[pl.BlockSpec((parts, n1, N2_TILE, c), lambda p_, j: (p_, 0, j, 0)), _const_spec(w3.shape)],
        out_specs=out_spec, out_shape=out_shape,
        compiler_params=_params(2), name="dft_major",
    )(x, w3)


def _spectrum_kernel(a_ref, g_ref, sc_ref, o_ref, *, nyquist):
    sign = jnp.where(pl.program_id(0) <= nyquist, 1.0, -1.0)
    re, im = _unpack_pair(a_ref[0, 0])
    a = jnp.concatenate([re, im * sign], axis=0)
    x = _dft_mm(g_ref[0], a) * sc_ref[...]
    r = x.shape[0] // 2
    o_ref[0, 0] = x[:r].astype(o_ref.dtype)
    o_ref[1, 0] = x[r:].astype(o_ref.dtype)


def _filter_spectrum(kt, scale, tabs):
    r = DFT_R
    c = kt.shape[1]
    a = _dft_major(kt.reshape(1, r, r, c), tabs["w1_real"], 1, pack=True)
    tcs = c
    return pl.pallas_call(
        functools.partial(_spectrum_kernel, nyquist=r // 2),
        grid=(r, c // tcs),
        in_specs=[pl.BlockSpec((1, 1, r, tcs), lambda k, j: (0, jnp.where(k <= r // 2, k, r - k), 0, j)),
                  pl.BlockSpec((1,) + tabs["g"].shape[1:], lambda k, j: (k, 0, 0)),
                  pl.BlockSpec((1, tcs), lambda k, j: (0, j))],
        out_specs=pl.BlockSpec((2, 1, r, tcs), lambda k, j: (0, k, 0, j)),
        out_shape=jax.ShapeDtypeStruct((2, r, r, c), BF16),
        compiler_params=_params(2), name="filter_spectrum",
    )(a, tabs["g"], scale)


def _dft_mm_t(w, x):
    return lax.dot_general(w, x.astype(BF16), (((0,), (0,)), ((), ())), preferred_element_type=F32)


def _dft2_kernel(a_ref, g_ref, h_ref, o_ref):
    for kk in range(K1_TILE):
        hr, hi = h_ref[0, kk].astype(F32), h_ref[1, kk].astype(F32)
        for p in range(a_ref.shape[0]):
            a = jnp.concatenate(_unpack_pair(a_ref[p, kk]), axis=0)
            x = _dft_mm(g_ref[kk], a)
            r = x.shape[0] // 2
            xr, xi = x[:r], x[r:]
            y = jnp.concatenate([xr * hr - xi * hi, xr * hi + xi * hr], axis=0)
            bb = _dft_mm_t(g_ref[kk], y)
            o_ref[p, :, kk, :] = _pack_pair(bb[:r], bb[r:])


def _dft3_kernel(b_ref, w_ref, u_ref, x_ref, bias_ref, o_ref):
    bias = bias_ref[...]
    for i in range(N2_TILE):
        bb = jnp.concatenate(_unpack_pair(b_ref[0, i]), axis=0)
        y = _dft_mm(w_ref[...], bb)
        half = y.shape[0] // 2
        for h in range(2):
            o_ref[h, :, i, :] = x_ref[h, :, i, :] * (y[h * half:(h + 1) * half] + u_ref[h, :, i, :] * bias)


def _long_conv_gate(u, x, hspec, order, bias, tabs):
    b, l, c = u.shape
    r = DFT_R
    p = b // 2
    nat = (b, r // 2, r, c)
    a = _dft_major(u.reshape(nat), tabs["w1_data"], 2, pack=True)
    spec4 = pl.BlockSpec((p, K1_TILE, r, c), lambda k: (0, k, 0, 0))
    bb = pl.pallas_call(
        _dft2_kernel,
        grid=(r // K1_TILE,),
        in_specs=[spec4,
                  pl.BlockSpec((K1_TILE,) + tabs["g"].shape[1:], lambda k: (k, 0, 0)),
                  pl.BlockSpec((2, K1_TILE, r, c), lambda k: (0, k, 0, order))],
        out_specs=pl.BlockSpec((p, r, K1_TILE, c), lambda k: (0, 0, k, 0)),
        out_shape=jax.ShapeDtypeStruct((p, r, r, c), jnp.uint32),
        compiler_params=_params(1), name="dft_minor_filter",
    )(a, tabs["g"], hspec)
    blk = pl.BlockSpec((2, r // 2, N2_TILE, c), lambda p_, j: (p_, 0, j, 0))
    out = pl.pallas_call(
        _dft3_kernel,
        grid=(p, r // N2_TILE),
        in_specs=[pl.BlockSpec((1, N2_TILE, r, c), lambda p_, j: (p_, j, 0, 0)),
                  _const_spec(tabs["w2"].shape), blk, blk, _const_spec((1, c))],
        out_specs=blk,
        out_shape=jax.ShapeDtypeStruct(nat, F32),
        compiler_params=_params(2), name="idft_major_gate",
    )(bb, tabs["w2"], u.reshape(nat), x.reshape(nat), bias.reshape(1, c))
    return out.reshape(b, l, c)


def _dense_tables(l):
    n = 2 * l
    k = jnp.arange(n, dtype=jnp.int32)
    t = jnp.arange(l, dtype=jnp.int32)
    c, s = _angles(k[:, None] * t[None, :], n)
    wf = jnp.concatenate([jnp.concatenate([c, s], 1), jnp.concatenate([-s, c], 1)], 0)
    ct, st = c.T, s.T
    wi = jnp.concatenate([jnp.concatenate([ct, -st], 1), jnp.concatenate([st, ct], 1)], 0)
    ca, sa = _angles(k[:, None] * k[None, :], n)
    wk = jnp.concatenate([ca, -sa], 0)
    return dict(wf=wf.astype(BF16), wi=wi.astype(BF16), wk=wk.astype(BF16))


def _dense_spectrum_kernel(k_ref, w_ref, sc_ref, o_ref):
    o_ref[...] = _dft_mm(w_ref[...], k_ref[...]) * sc_ref[...]


def _dense_spectrum(kt, scale, wk):
    n, c = kt.shape
    return pl.pallas_call(
        _dense_spectrum_kernel,
        grid=(1,),
        in_specs=[_const_spec((n, c)), _const_spec(wk.shape), _const_spec((1, c))],
        out_specs=pl.BlockSpec((2 * n, c), lambda i: (0, 0)),
        out_shape=jax.ShapeDtypeStruct((2 * n, c), F32),
        compiler_params=_params(1), name="dense_filter_spectrum",
    )(kt, wk, scale)


def _dense_conv_kernel(u_ref, x_ref, wf_ref, wi_ref, h_ref, bias_ref, o_ref):
    u0, u1 = u_ref[0], u_ref[1]
    z = _dft_mm(wf_ref[...], jnp.concatenate([u0, u1], axis=0))
    n = z.shape[0] // 2
    zr, zi = z[:n], z[n:]
    hr, hi = h_ref[:n], h_ref[n:]
    y = _dft_mm(wi_ref[...], jnp.concatenate([zr * hr - zi * hi, zr * hi + zi * hr], axis=0))
    l = y.shape[0] // 2
    bias = bias_ref[...]
    o_ref[0] = x_ref[0] * (y[:l] + u0 * bias)
    o_ref[1] = x_ref[1] * (y[l:] + u1 * bias)


def _dense_conv_gate(u, x, hspec, order, bias, tabs):
    b, l, c = u.shape
    blk = pl.BlockSpec((2, l, c), lambda p_: (p_, 0, 0))
    return pl.pallas_call(
        _dense_conv_kernel,
        grid=(b // 2,),
        in_specs=[blk, blk, _const_spec(tabs["wf"].shape), _const_spec(tabs["wi"].shape),
                  pl.BlockSpec((4 * l, c), lambda p_: (0, order)), _const_spec((1, c))],
        out_specs=blk,
        out_shape=jax.ShapeDtypeStruct((b, l, c), F32),
        compiler_params=_params(1), name="dense_conv_gate",
    )(u, x, tabs["wf"], tabs["wi"], hspec, bias.reshape(1, c))


def _mix_kernel(h_ref, o_ref, pm_ref, y_ref, g_ref, gate_ref, wa_ref, wp_ref, wh_ref, wo_ref, out_ref):
    d = h_ref.shape[2]
    a = _dot(o_ref[0].astype(BF16), wa_ref[...])
    p = _dot(pm_ref[0].astype(BF16), wp_ref[...])
    y = _dot(y_ref[0].astype(BF16), wh_ref[...])
    gate = lambda k: g_ref[0, :, k * d:(k + 1) * d].astype(F32)
    merged = gate(0) * a + gate(1) * p + gate(2) * y
    out = _dot(merged.astype(BF16), wo_ref[...])
    out_ref[0] = h_ref[0] + gate_ref[0] * out


def _mix(h, o, pm, y, g, mod, w_attn_o, w_pool_o, w_hy_o, w_out, tm=512):
    b, s, d = h.shape
    tm = min(tm, s)

    def tok(w):
        return pl.BlockSpec((1, tm, w), lambda b_, i: (b_, i, 0))

    ws = [w.astype(BF16) for w in (w_attn_o, w_pool_o, w_hy_o, w_out)]
    return pl.pallas_call(
        _mix_kernel,
        grid=(b, s // tm),
        in_specs=[tok(d), tok(o.shape[2]), tok(pm.shape[2]), tok(y.shape[2]), tok(g.shape[2]),
                  _mod_spec(mod, d, 2)] + [_const_spec(w.shape) for w in ws],
        out_specs=tok(d),
        out_shape=jax.ShapeDtypeStruct((b, s, d), F32),
        compiler_params=_params(2), name="merge_outproj",
    )(h, o, pm, y, g, mod, *ws)


def _ffn_kernel(h_ref, sh_ref, sc_ref, gate_ref, g_ref, wi_ref, wo_ref, fg_ref, out_ref, *,
                hidden, chunk, final):
    x = h_ref[0]
    a = _norm_mod(x, g_ref[...], sh_ref[0], sc_ref[0]).astype(BF16)
    acc = jnp.zeros(x.shape, F32)
    for c0 in range(0, hidden, chunk):
        gt = _dot(a, wi_ref[:, c0:c0 + chunk])
        up = _dot(a, wi_ref[:, hidden + c0:hidden + c0 + chunk])
        act = (gt * jax.nn.sigmoid(gt) * up).astype(BF16)
        acc = acc + _dot(act, wo_ref[c0:c0 + chunk, :])
    y = x + gate_ref[0] * acc
    if final:
        ms = jnp.mean(y * y, axis=-1, keepdims=True)
        y = y * lax.rsqrt(ms + EPS) * fg_ref[...]
    out_ref[0] = y


def _ffn(h, mod, norm_g, w_in, w_out, final_g, final, tm=512):
    b, s, d = h.shape
    hidden = w_out.shape[0]
    tm = min(tm, s)
    tok = pl.BlockSpec((1, tm, d), lambda b_, i: (b_, i, 0))
    return pl.pallas_call(
        functools.partial(_ffn_kernel, hidden=hidden, chunk=hidden // 2, final=final),
        grid=(b, s // tm),
        in_specs=[tok, _mod_spec(mod, d, 3), _mod_spec(mod, d, 4), _mod_spec(mod, d, 5),
                  _const_spec((1, d)), _const_spec(w_in.shape), _const_spec(w_out.shape),
                  _const_spec((1, d))],
        out_specs=tok,
        out_shape=jax.ShapeDtypeStruct((b, s, d), F32),
        compiler_params=_params(2), name="swiglu_ffn",
    )(h, mod, mod, mod, norm_g.reshape(1, d), w_in.astype(BF16), w_out.astype(BF16),
      final_g.reshape(1, d))


def _rope_tables(seq, width):
    t = jnp.arange(seq, dtype=jnp.int32)
    row = (t // GRID_W).astype(F32)
    col = (t % GRID_W).astype(F32)
    inv = 1.0 / (ROPE_BASE ** (jnp.arange(ROPE_FREQS, dtype=F32) * 2.0 / (2 * ROPE_FREQS)))
    ar, ac = row[:, None] * inv, col[:, None] * inv
    cos = jnp.concatenate([jnp.cos(ar)] * 2 + [jnp.cos(ac)] * 2, axis=1)
    sin = jnp.concatenate([-jnp.sin(ar), jnp.sin(ar), -jnp.sin(ac), jnp.sin(ac)], axis=1)
    reps = width // cos.shape[1]
    return jnp.tile(cos, (1, reps)), jnp.tile(sin, (1, reps))


def kernel(x, c, ctx, c_ctx, w_mod, b_mod, norm1_g, norm2_g, w_in, lam_qk, subln_g, w_attn_o, w_pool, pool_scale, w_pool_o, w_short, b_short, hf_w1, hf_b1, hf_freq1, hf_w2, hf_b2, hf_freq2, hf_w3, hy_bias, w_hy_o, w_out, w_ffn_in, w_ffn_out, final_g):
    batch, seq, d = x.shape
    depth = w_mod.shape[0]
    ctx_len = ctx.shape[1]
    aw = w_attn_o.shape[1]
    pw = w_pool_o.shape[1]
    hw = w_hy_o.shape[1]
    q0, k0, v0, p0 = 0, aw, 2 * aw, 3 * aw
    h0 = p0 + pw
    g0 = h0 + 3 * hw
    assert seq * 2 == DFT_R * DFT_R and batch % 2 == 0

    c8 = jnp.zeros((8, d), F32).at[:batch].set(c).at[batch].set(c_ctx)
    mod_all = _modulation(c8, w_mod, b_mod)
    rope_tabs = _rope_tables(seq, ATTN_VD)
    tabs = _dft_tables()
    ctx_tabs = _dense_tables(ctx_len)

    h_lat, h_ctx = x, ctx
    for l in range(depth):
        last = l == depth - 1
        lam_init = LAMBDA_INIT_BASE - LAMBDA_INIT_AMP * math.exp(-LAMBDA_INIT_RATE * l)
        mod = mod_all[l, :batch].reshape(batch, 1, N_MOD * d)
        mod_c = mod_all[l, batch:batch + 1].reshape(1, 1, N_MOD * d)
        wl = w_in[l]
        fparams = (hf_w1[l], hf_b1[l], hf_freq1[l], hf_w2[l], hf_b2[l], hf_freq2[l], hf_w3[l])

        wl_bf = wl.astype(BF16)
        local_params = (w_pool[l], pool_scale[l], w_short[l], b_short[l])
        out_dt = [BF16, F32, F32, F32, BF16, BF16, BF16, BF16]
        local_seg = ("local", p0, pw + 3 * hw, 1.0)
        gate_seg = ("sigmoid", g0, 3 * d, 1.0)
        segs = [local_seg, gate_seg, ("rope", q0, aw, Q_SCALE), ("rope", k0, aw, 1.0), ("plain", v0, aw, 1.0)]
        pm, vv, x1, x2, g, q, k, v = _inproj(h_lat, mod, norm1_g[l], wl_bf, segs, out_dt, rope_tabs, local_params)

        if last:
            segs_c = [("plain", k0, aw, 1.0), ("plain", v0, aw, 1.0)]
            k_c, v_c = _inproj(h_ctx, mod_c, norm1_g[l], wl_bf, segs_c, [BF16, BF16])
        else:
            segs_c = [local_seg, gate_seg, ("plain", q0, aw, Q_SCALE), ("plain", k0, aw, 1.0), ("plain", v0, aw, 1.0)]
            pm_c, vv_c, x1_c, x2_c, g_c, q_c, k_c, v_c = _inproj(h_ctx, mod_c, norm1_g[l], wl_bf, segs_c, out_dt,
                                                                 None, local_params)
            o_c = _attention(q_c, k_c, v_c, lam_qk[l], subln_g[l], lam_init)
            kt_c, inv_norm_c = _hyena_time_kernel(ctx_len, *fparams)
            hspec_c = _dense_spectrum(kt_c, inv_norm_c / (2 * ctx_len), ctx_tabs["wk"])
            z_c = _dense_conv_gate(vv_c, x1_c, hspec_c, 0, hy_bias[l, 0], ctx_tabs)
            y_c = _dense_conv_gate(z_c, x2_c, hspec_c, 1, hy_bias[l, 1], ctx_tabs)
            h_ctx_mid = _mix(h_ctx, o_c, pm_c, y_c, g_c, mod_c, w_attn_o[l], w_pool_o[l], w_hy_o[l], w_out[l])
            h_ctx_new = _ffn(h_ctx_mid, mod_c, norm2_g[l], w_ffn_in[l], w_ffn_out[l], final_g, False)

        vt = jnp.swapaxes(jnp.concatenate([v_c, v], axis=1), 1, 2)
        o = _attention_t(q, jnp.concatenate([k_c, k], axis=1), vt, lam_qk[l], subln_g[l], lam_init)
        kt, inv_norm = _hyena_time_kernel(seq, *fparams)
        hspec = _filter_spectrum(kt, inv_norm / (2 * seq), tabs)
        z = _long_conv_gate(vv, x1, hspec, 0, hy_bias[l, 0], tabs)
        y = _long_conv_gate(z, x2, hspec, 1, hy_bias[l, 1], tabs)
        h_mid = _mix(h_lat, o, pm, y, g, mod, w_attn_o[l], w_pool_o[l], w_hy_o[l], w_out[l])
        h_lat = _ffn(h_mid, mod, norm2_g[l], w_ffn_in[l], w_ffn_out[l], final_g, last)
        if not last:
            h_ctx = h_ctx_new
    return h_lat
```

```python
import functools
import math

import jax
import jax.numpy as jnp
from jax import lax
from jax.experimental import pallas as pl
from jax.experimental.pallas import tpu as pltpu

F32 = jnp.float32
BF16 = jnp.bfloat16
HIGHEST = lax.Precision.HIGHEST

GRID_W = 64
N_MOD = 6
ATTN_HD = 64
ATTN_VD = 128
Q_SCALE = ATTN_HD ** -0.5 * math.log2(math.e)
ROPE_BASE = 10000.0
ROPE_FREQS = 16
LAMBDA_INIT_BASE = 0.8
LAMBDA_INIT_AMP = 0.6
LAMBDA_INIT_RATE = 0.3
POOL_WINDOWS = (2, 4, 8, 16)
POOL_GC = 128
HALO = 8
PROJ_PIECE = 1024
HYENA_BANDS = 16
HYENA_FAST_DECAY = 0.3
HYENA_SLOW_DECAY = 1.5
HYENA_TARGET = 1e-2
EPS = 1e-6
DFT_R = 128
HALF_ROWS = 72

VMEM_LIMIT = 56 * 1024 * 1024


def _params(n_grid):
    return pltpu.CompilerParams(dimension_semantics=("arbitrary",) * n_grid,
                                vmem_limit_bytes=VMEM_LIMIT)


def _const_spec(shape):
    zeros = (0,) * len(shape)
    return pl.BlockSpec(shape, lambda *_: zeros, pipeline_mode=pl.Buffered(1))


def _dot(a, b):
    return jnp.dot(a, b, preferred_element_type=F32)


def _dft_mm(w, x):
    return _dot(w, x.astype(BF16))


def _mod_kernel(c_ref, w_ref, b_ref, o_ref):
    c = c_ref[...]
    act = c * jax.nn.sigmoid(c)
    o_ref[0] = jnp.dot(act, w_ref[0], precision=HIGHEST, preferred_element_type=F32) + b_ref[0]


def _modulation(c8, w_mod, b_mod):
    depth, d, n = w_mod.shape
    tn = 1536
    return pl.pallas_call(
        _mod_kernel,
        grid=(depth, n // tn),
        in_specs=[pl.BlockSpec((8, d), lambda l, j: (0, 0)),
                  pl.BlockSpec((1, d, tn), lambda l, j: (l, 0, j)),
                  pl.BlockSpec((1, 1, tn), lambda l, j: (l, 0, j))],
        out_specs=pl.BlockSpec((1, 8, tn), lambda l, j: (l, 0, j)),
        out_shape=jax.ShapeDtypeStruct((depth, 8, n), F32),
        compiler_params=_params(2),
        name="modulation",
    )(c8, w_mod, b_mod.reshape(depth, 1, n))


def _mod_spec(mod, d, col):
    if mod.shape[0] == 1:
        return pl.BlockSpec((1, 1, d), lambda b, i: (0, 0, col))
    return pl.BlockSpec((1, 1, d), lambda b, i: (b, 0, col))


def _norm_mod(x, g, shift, scale):
    ms = jnp.mean(x * x, axis=-1, keepdims=True)
    a = x * lax.rsqrt(ms + EPS) * g
    return a * (1.0 + scale) + shift


def _rope(p, cos, sin):
    lanes = cos.shape[1]
    lane = lax.broadcasted_iota(jnp.int32, cos.shape, 1)
    is_a = lane % (2 * ROPE_FREQS) < ROPE_FREQS
    out = []
    for g in range(p.shape[1] // lanes):
        x = p[:, g * lanes:(g + 1) * lanes]
        partner = jnp.where(is_a, pltpu.roll(x, lanes - ROPE_FREQS, 1), pltpu.roll(x, ROPE_FREQS, 1))
        out.append(x * cos + partner * sin)
    return jnp.concatenate(out, axis=1)


def _local_steps(pbuf, hbuf, wp_ref, ps_ref, ws_ref, bs_ref, pm_ref, v_ref, x1_ref, x2_ref, i, ts, seq):
    t = i * ts + lax.broadcasted_iota(jnp.int32, (ts, 1), 0)

    def pool(g, w):
        cs = slice(g * POOL_GC, (g + 1) * POOL_GC)
        acc = pbuf[HALO - w // 2:HALO - w // 2 + ts, cs]
        for j in range(1 - w // 2, w // 2):
            acc = acc + pbuf[HALO + j:HALO + j + ts, cs]
        cnt = (jnp.minimum(t + w // 2, seq) - jnp.maximum(t - w // 2, 0)).astype(F32)
        pooled = acc / cnt - pbuf[HALO:HALO + ts, cs]
        mixed = _dot(pooled.astype(BF16), wp_ref[g])
        pm_ref[0, :, cs] = (mixed * ps_ref[:, cs]).astype(pm_ref.dtype)

    def conv(o_ref, k):
        c = o_ref.shape[2]
        cs = slice(k * c, (k + 1) * c)
        o_ref[0] = (hbuf[HALO - 1:HALO - 1 + ts, cs] * ws_ref[0:1, cs] + hbuf[HALO:HALO + ts, cs] * ws_ref[1:2, cs]
                    + hbuf[HALO + 1:HALO + 1 + ts, cs] * ws_ref[2:3, cs] + bs_ref[:, cs])

    return ([functools.partial(pool, g, w) for g, w in enumerate(POOL_WINDOWS)]
            + [functools.partial(conv, o_ref, k) for k, o_ref in enumerate((v_ref, x1_ref, x2_ref))])


def _inproj_kernel(*refs, segs, rope, local, tm, seq):
    it = iter(refs)
    h_ref = next(it)
    hp_ref, hn_ref = (next(it), next(it)) if local else (None, None)
    sh_ref, sc_ref, g_ref, w_ref = next(it), next(it), next(it), next(it)
    cos_ref, sin_ref = (next(it), next(it)) if rope else (None, None)
    loc_refs = [next(it) for _ in range(4)] if local else None
    rest = list(it)
    n_out = sum(4 if kind == "local" else 1 for kind, _, _, _ in segs)
    out_refs, scratch = rest[:n_out], rest[n_out:]

    g, sh, sc = g_ref[...], sh_ref[0], sc_ref[0]
    a32 = _norm_mod(h_ref[0], g, sh, sc)
    a = a32.astype(BF16)

    def project(kind, col, width, scale, o_ref, c0):
        p = _dot(a, w_ref[:, col + c0:col + c0 + width])
        if kind == "rope":
            p = _rope(p, cos_ref[...], sin_ref[...])
        elif kind == "sigmoid":
            p = jax.nn.sigmoid(p)
        if scale != 1.0:
            p = p * scale
        o_ref[0, :, c0:c0 + width] = p.astype(o_ref.dtype)

    mxu_work, vpu_work = [], []
    oi = 0
    for kind, col, width, scale in segs:
        if kind == "local":
            i = pl.program_id(1)
            ext = jnp.concatenate([_norm_mod(hp_ref[0], g, sh, sc), a32, _norm_mod(hn_ref[0], g, sh, sc)],
                                  axis=0).astype(BF16)
            u = _dot(ext, w_ref[:, col:col + width])
            row = lax.broadcasted_iota(jnp.int32, (tm + 2 * HALO, 1), 0)
            inside = jnp.logical_and(jnp.logical_or(i > 0, row >= HALO),
                                     jnp.logical_or(i < pl.num_programs(1) - 1, row < HALO + tm))
            u = jnp.where(inside, u, 0.0)
            pbuf, hbuf = scratch
            cp = pbuf.shape[1]
            pbuf[...] = u[:, :cp]
            hbuf[...] = u[:, cp:]
            vpu_work = _local_steps(pbuf, hbuf, *loc_refs, *out_refs[oi:oi + 4], i, tm, seq)
            oi += 4
            continue
        piece = min(width, PROJ_PIECE)
        for c0 in range(0, width, piece):
            mxu_work.append(functools.partial(project, kind, col, piece, scale, out_refs[oi], c0))
        oi += 1
    for k in range(max(len(mxu_work), len(vpu_work))):
        if k < len(vpu_work):
            vpu_work[k]()
        if k < len(mxu_work):
            mxu_work[k]()


def _inproj(h, mod, norm_g, w_cat, segs, out_dtypes, rope_tabs=None, local_params=None, tm=256):
    b, s, d = h.shape
    tm = min(tm, s)
    rope = rope_tabs is not None
    local = local_params is not None
    nh = tm // HALO
    tok = lambda w: pl.BlockSpec((1, tm, w), lambda b_, i: (b_, i, 0))
    in_specs = [tok(d)]
    args = [h]
    if local:
        in_specs += [pl.BlockSpec((1, HALO, d), lambda b_, i: (b_, jnp.maximum(i * nh - 1, 0), 0)),
                     pl.BlockSpec((1, HALO, d), lambda b_, i: (b_, jnp.minimum((i + 1) * nh, s // HALO - 1), 0))]
        args += [h, h]
    in_specs += [_mod_spec(mod, d, 0), _mod_spec(mod, d, 1), _const_spec((1, d)), _const_spec(w_cat.shape)]
    args += [mod, mod, norm_g.reshape(1, d), w_cat]
    if rope:
        wr = rope_tabs[0].shape[1]
        in_specs += [pl.BlockSpec((tm, wr), lambda b_, i: (i, 0))] * 2
        args += list(rope_tabs)
    scratch = []
    if local:
        w_pool, pool_scale, w_short, b_short = local_params
        cp, ch = pool_scale.shape[0], b_short.shape[0]
        in_specs += [_const_spec(w_pool.shape), _const_spec((1, cp)), _const_spec(w_short.shape),
                     _const_spec((1, ch))]
        args += [w_pool.astype(BF16), pool_scale.reshape(1, cp), w_short, b_short.reshape(1, ch)]
        scratch = [pltpu.VMEM((tm + 2 * HALO, cp), F32), pltpu.VMEM((tm + 2 * HALO, ch), F32)]
    widths = []
    for kind, _, w, _ in segs:
        widths += [cp, ch // 3, ch // 3, ch // 3] if kind == "local" else [w]
    out_specs = [tok(w) for w in widths]
    out_shape = [jax.ShapeDtypeStruct((b, s, w), dt) for w, dt in zip(widths, out_dtypes)]
    return pl.pallas_call(
        functools.partial(_inproj_kernel, segs=tuple(segs), rope=rope, local=local, tm=tm, seq=s),
        grid=(b, s // tm), in_specs=in_specs, out_specs=out_specs, out_shape=out_shape,
        scratch_shapes=scratch, compiler_params=_params(2), name="inproj",
    )(*args)


def _lambda(lq_ref, lam_init):
    lq = lq_ref[...]
    return (jnp.exp(jnp.sum(lq[0:1] * lq[1:2], axis=-1, keepdims=True))
            - jnp.exp(jnp.sum(lq[2:3] * lq[3:4], axis=-1, keepdims=True)) + lam_init)


def _split_halves(q):
    lane = lax.broadcasted_iota(jnp.int32, q.shape, 1)
    zero = jnp.zeros_like(q)
    return jnp.concatenate([jnp.where(lane < ATTN_HD, q, zero),
                            jnp.where(lane >= ATTN_HD, q, zero)], axis=0)


def _qk(qq, k):
    return lax.dot_general(qq, k, (((1,), (1,)), ((), ())), preferred_element_type=F32)


def _sub_ln(o, g, lam_init):
    ms = jnp.mean(o * o, axis=-1, keepdims=True)
    return o * lax.rsqrt(ms + EPS) * g * (1.0 - lam_init)


def _attn_kernel(lq_ref, q_ref, k_ref, v_ref, g_ref, o_ref, *, lam_init, tq):
    lam = _lambda(lq_ref, lam_init)
    s = _qk(_split_halves(q_ref[0]), k_ref[0])
    m = jnp.max(s, axis=-1, keepdims=True)
    p = jnp.exp2(s - m)
    l = jnp.sum(p, axis=-1, keepdims=True)
    a = (p[:tq] - p[tq:] * (lam * l[:tq] / l[tq:])).astype(BF16)
    o = _dot(a, v_ref[0]) / l[:tq]
    o_ref[0] = _sub_ln(o, g_ref[...], lam_init).astype(o_ref.dtype)


def _attn_t_kernel(lq_ref, q_ref, k_ref, vt_ref, g_ref, o_ref, s0, s1, m_scr, *, lam_init, tq, lk, kc, nq):
    j = pl.program_id(2)
    bufs = (s0, s1)

    @pl.when(j == 0)
    def _():
        s1[...] = jnp.zeros_like(s1)
        m_scr[1] = jnp.zeros(m_scr.shape[1:], F32)

    lam = _lambda(lq_ref, lam_init)
    for r in range(2):
        sig = 2 * j + r
        buf_a, buf_b = bufs[r], bufs[1 - r]
        t_a = jnp.minimum(sig, nq - 1)
        t_b = jnp.clip(sig - 1, 0, nq - 1)
        qq = _split_halves(q_ref[0, pl.ds(pl.multiple_of(t_a * tq, tq), tq), :])
        m_b = m_scr[1 - r]
        m8 = l8 = acc = None
        for c0 in range(0, lk, kc):
            rows = slice(c0, c0 + kc)
            s = _qk(k_ref[0, rows, :], qq)
            buf_a[rows, :] = s
            for i in range(0, kc, 8):
                m8 = s[i:i + 8] if m8 is None else jnp.maximum(m8, s[i:i + 8])
            p = jnp.exp2(buf_b[rows, :].reshape(kc // 8, 8, 2 * tq) - m_b)
            lc = jnp.sum(p, axis=0)
            l8 = lc if l8 is None else l8 + lc
            pv = _dot(vt_ref[0, :, rows], p.reshape(kc, 2 * tq).astype(BF16))
            acc = pv if acc is None else acc + pv
        m_scr[r] = jnp.broadcast_to(jnp.max(m8, axis=0, keepdims=True), m8.shape)
        l = jnp.sum(l8, axis=0, keepdims=True)
        ot = acc[:, :tq] / l[:, :tq] - acc[:, tq:] * (lam / l[:, tq:])
        ms = jnp.mean(ot * ot, axis=0, keepdims=True)
        ot = ot * lax.rsqrt(ms + EPS) * g_ref[...] * (1.0 - lam_init)
        o_ref[0, pl.ds(pl.multiple_of(t_b * tq, tq), tq), :] = ot.T.astype(o_ref.dtype)


def _attention_t(q, k, vt, lam_qk, subln_g, lam_init, tq=128, kc=768):
    b, s, w = q.shape
    lk = k.shape[1]
    heads = w // ATTN_VD
    nq = s // tq
    nsteps = -(-(nq + 1) // 2)
    return pl.pallas_call(
        functools.partial(_attn_t_kernel, lam_init=lam_init, tq=tq, lk=lk, kc=kc, nq=nq),
        grid=(b, heads, nsteps),
        in_specs=[pl.BlockSpec(lam_qk.shape, lambda b_, h, i: (0, 0)),
                  pl.BlockSpec((1, s, ATTN_VD), lambda b_, h, i: (b_, 0, h)),
                  pl.BlockSpec((1, lk, ATTN_VD), lambda b_, h, i: (b_, 0, h)),
                  pl.BlockSpec((1, ATTN_VD, lk), lambda b_, h, i: (b_, h, 0)),
                  pl.BlockSpec((ATTN_VD, 1), lambda b_, h, i: (0, 0))],
        out_specs=pl.BlockSpec((1, s, ATTN_VD), lambda b_, h, i: (b_, 0, h)),
        out_shape=jax.ShapeDtypeStruct((b, s, w), BF16),
        scratch_shapes=[pltpu.VMEM((lk, 2 * tq), F32)] * 2 + [pltpu.VMEM((2, 8, 2 * tq), F32)],
        compiler_params=_params(3), name="diff_attention_t",
    )(lam_qk, q, k, vt, subln_g.reshape(ATTN_VD, 1))


def _attention(q, k, v, lam_qk, subln_g, lam_init, tq=128):
    b, s, w = q.shape
    lk = k.shape[1]
    heads = w // ATTN_VD
    tq = min(tq, s)
    return pl.pallas_call(
        functools.partial(_attn_kernel, lam_init=lam_init, tq=tq),
        grid=(b, heads, s // tq),
        in_specs=[pl.BlockSpec(lam_qk.shape, lambda b_, h, i: (0, 0)),
                  pl.BlockSpec((1, tq, ATTN_VD), lambda b_, h, i: (b_, i, h)),
                  pl.BlockSpec((1, lk, ATTN_VD), lambda b_, h, i: (b_, 0, h)),
                  pl.BlockSpec((1, lk, ATTN_VD), lambda b_, h, i: (b_, 0, h)),
                  pl.BlockSpec((1, ATTN_VD), lambda b_, h, i: (0, 0))],
        out_specs=pl.BlockSpec((1, tq, ATTN_VD), lambda b_, h, i: (b_, i, h)),
        out_shape=jax.ShapeDtypeStruct((b, s, w), BF16),
        compiler_params=_params(3), name="diff_attention",
    )(lam_qk, q, k, v, subln_g.reshape(1, ATTN_VD))


def _filter_kernel(z_ref, w1_ref, b1_ref, f1_ref, w2_ref, b2_ref, f2_ref, w3_ref, dl_ref, h_ref, asum_ref, *,
                   tl, l):
    i = pl.program_id(0)
    half = tl // 2
    z = z_ref[...]
    z2 = jnp.concatenate([z[:half], z[half:]], axis=1)
    pre = jnp.dot(z2, w1_ref[...], precision=HIGHEST, preferred_element_type=F32)
    h1 = jnp.sin(f1_ref[...] * (pre + b1_ref[...]))
    h2 = jnp.sin(f2_ref[...] * (jnp.dot(h1, w2_ref[...], precision=HIGHEST,
                                        preferred_element_type=F32) + b2_ref[...]))
    hh = h2.astype(BF16)
    hl = (h2 - hh.astype(F32)).astype(BF16)
    h2s = jnp.concatenate([hh, hl, hh], axis=1)
    part = None
    for j in range(2):
        rows = slice(j * half, (j + 1) * half)
        h3 = _dot(h2s, w3_ref[0, j])
        n = i * tl + j * half + lax.broadcasted_iota(jnp.int32, (half, 1), 0)
        h = jnp.where(n == l, 0.0, h3 * jnp.exp(-z[rows, 0:1] * dl_ref[...]))
        h_ref[rows, :] = h
        pj = jnp.sum(jnp.abs(h), axis=0, keepdims=True)
        part = pj if part is None else part + pj

    @pl.when(i == 0)
    def _():
        asum_ref[...] = jnp.zeros_like(asum_ref)

    asum_ref[...] += part


def _hyena_time_kernel(l, w1, b1, f1, w2, b2, f2, w3):
    ffn = w1.shape[1]
    nc = w3.shape[1] // 2
    c = nc // 2
    tl = min(l, 512)
    w3d = jnp.swapaxes(w3.reshape(ffn, 2, nc), 0, 1)
    n = jnp.arange(2 * l, dtype=jnp.int32)
    t = jnp.where(n < l, n, 2 * l - n).astype(F32)
    bands = jnp.linspace(1e-4, HYENA_BANDS - 1, HYENA_BANDS, dtype=F32)
    ang = 2.0 * math.pi * t[:, None] * bands[None, :] / l
    emb = 2 * HYENA_BANDS + 1
    z = jnp.concatenate([(t / (l - 1))[:, None], jnp.cos(ang), jnp.sin(ang),
                         jnp.zeros((2 * l, 128 - emb), F32)], axis=1)
    w1p = jnp.zeros((128, ffn), F32).at[:emb].set(w1)
    zf = jnp.zeros((ffn, ffn), F32)
    w1b = jnp.concatenate([jnp.concatenate([w1p, jnp.zeros_like(w1p)], 1),
                           jnp.concatenate([jnp.zeros_like(w1p), w1p], 1)], 0)
    w2b = jnp.concatenate([jnp.concatenate([w2, zf], 1), jnp.concatenate([zf, w2], 1)], 0)
    w3hi = w3d.astype(BF16)
    w3lo = (w3d - w3hi.astype(F32)).astype(BF16)
    zw = jnp.zeros_like(w3hi)
    first = jnp.concatenate([w3hi, zw, w3hi, zw, w3lo, zw], axis=1)
    second = jnp.concatenate([zw, w3hi, zw, w3hi, zw, w3lo], axis=1)
    w3s = jnp.stack([first, second], axis=1)
    max_decay = math.log(HYENA_TARGET) / HYENA_FAST_DECAY
    min_decay = math.log(HYENA_TARGET) / HYENA_SLOW_DECAY
    deltas = jnp.abs(jnp.linspace(min_decay, max_decay, c, dtype=F32))
    dl = jnp.tile(deltas, 2).reshape(1, nc)
    row2 = lambda a: jnp.tile(a.reshape(1, -1), (1, 2))
    nfwd = l // tl
    kt, asum = pl.pallas_call(
        functools.partial(_filter_kernel, tl=tl, l=l),
        grid=(2 * nfwd,),
        in_specs=[pl.BlockSpec((tl, 128), lambda i: (i, 0)), _const_spec(w1b.shape),
                  _const_spec((1, 2 * ffn)), _const_spec((1, 2 * ffn)),
                  _const_spec(w2b.shape), _const_spec((1, 2 * ffn)), _const_spec((1, 2 * ffn)),
                  pl.BlockSpec((1,) + w3s.shape[1:], lambda i: (i // nfwd, 0, 0, 0)), _const_spec((1, nc))],
        out_specs=[pl.BlockSpec((tl, nc), lambda i: (i, 0)), pl.BlockSpec((1, nc), lambda i: (0, 0))],
        out_shape=[jax.ShapeDtypeStruct((2 * l, nc), F32), jax.ShapeDtypeStruct((1, nc), F32)],
        compiler_params=_params(1), name="hyena_filter_mlp",
    )(z, w1b, row2(b1), row2(f1), w2b, row2(b2), row2(f2), w3s, dl)
    return kt, 1.0 / asum


def _angles(prod, n):
    th = (2.0 * math.pi / n) * (prod % n).astype(F32)
    return jnp.cos(th), jnp.sin(th)


def _dft_tables():
    r = DFT_R
    n = r * r
    idx = jnp.arange(r, dtype=jnp.int32)
    c, s = _angles(idx[:, None] * idx[None, :], r)
    ch, sh = c[:, :r // 2], s[:, :r // 2]
    w1_data = jnp.concatenate([jnp.concatenate([ch, sh], 1), jnp.concatenate([-sh, ch], 1)], 0)
    pad = jnp.zeros((HALF_ROWS - (r // 2 + 1), r), F32)
    w1_real = jnp.concatenate([c[:r // 2 + 1], pad, -s[:r // 2 + 1], pad], 0)
    ct, st = c[:r // 2], s[:r // 2]
    w2 = jnp.concatenate([jnp.concatenate([ct, -st], 1), jnp.concatenate([st, ct], 1)], 0)
    k = idx[:, None, None] + r * idx[None, :, None]
    cg, sg = _angles(k * idx[None, None, :], n)
    g = jnp.concatenate([jnp.concatenate([cg, sg], 2), jnp.concatenate([-sg, cg], 2)], 1)
    return dict(w1_data=w1_data.astype(BF16), w1_real=w1_real.astype(BF16), w2=w2.astype(BF16),
                g=g.astype(BF16))


N2_TILE = 16
K1_TILE = 8


def _pack_pair(re, im):
    rb = lax.bitcast_convert_type(re.astype(BF16).astype(F32), jnp.uint32)
    ib = lax.bitcast_convert_type(im.astype(BF16).astype(F32), jnp.uint32)
    return rb | (ib >> 16)


def _unpack_pair(w):
    re = lax.bitcast_convert_type(w & jnp.uint32(0xFFFF0000), F32)
    im = lax.bitcast_convert_type(w << 16, F32)
    return re, im


def _dft1_kernel(x_ref, w_ref, o_ref, *, parts, pack):
    r = w_ref.shape[0] // 2
    for i in range(N2_TILE):
        x = jnp.concatenate([x_ref[p, :, i, :] for p in range(parts)], axis=0)
        y = _dft_mm(w_ref[...], x)
        if pack:
            o_ref[0, :, i, :] = _pack_pair(y[:r], y[r:])
        else:
            o_ref[0, 0, :, i, :] = y[:r]
            o_ref[0, 1, :, i, :] = y[r:]


def _dft_major(x, w3, parts, pack=False):
    b, n1, r, c = x.shape
    k1 = w3.shape[0] // 2
    if pack:
        out_spec = pl.BlockSpec((1, k1, N2_TILE, c), lambda p_, j: (p_, 0, j, 0))
        out_shape = jax.ShapeDtypeStruct((b // parts, k1, r, c), jnp.uint32)
    else:
        out_spec = pl.BlockSpec((1, 2, k1, N2_TILE, c), lambda p_, j: (p_, 0, 0, j, 0))
        out_shape = jax.ShapeDtypeStruct((b // parts, 2, k1, r, c), F32)
    return pl.pallas_call(
        functools.partial(_dft1_kernel, parts=parts, pack=pack),
        grid=(b // parts, r // N2_TILE),
        in_specs=[pl.BlockSpec((parts, n1, N2_TILE, c), lambda p_, j: (p_, 0, j, 0)), _const_spec(w3.shape)],
        out_specs=out_spec, out_shape=out_shape,
        compiler_params=_params(2), name="dft_major",
    )(x, w3)


def _spectrum_kernel(lo_ref, hi_ref, g_ref, sc_ref, o_ref):
    for half, a_ref, sign in ((0, lo_ref, 1.0), (1, hi_ref, -1.0)):
        re, im = _unpack_pair(a_ref[0, 0])
        a = jnp.concatenate([re, im * sign], axis=0)
        x = _dft_mm(g_ref[half, 0], a) * sc_ref[...]
        r = x.shape[0] // 2
        o_ref[0, half, 0] = x[:r].astype(o_ref.dtype)
        o_ref[1, half, 0] = x[r:].astype(o_ref.dtype)


def _filter_spectrum(kt, scale, tabs):
    r = DFT_R
    h = r // 2
    c = kt.shape[1]
    a = _dft_major(kt.reshape(1, r, r, c), tabs["w1_real"], 1, pack=True)
    g = tabs["g"].reshape((2, h) + tabs["g"].shape[1:])
    out = pl.pallas_call(
        _spectrum_kernel,
        grid=(h,),
        in_specs=[pl.BlockSpec((1, 1, r, c), lambda k: (0, k, 0, 0)),
                  pl.BlockSpec((1, 1, r, c), lambda k: (0, h - k, 0, 0)),
                  pl.BlockSpec((2, 1) + g.shape[2:], lambda k: (0, k, 0, 0)),
                  _const_spec((1, c))],
        out_specs=pl.BlockSpec((2, 2, 1, r, c), lambda k: (0, 0, k, 0, 0)),
        out_shape=jax.ShapeDtypeStruct((2, 2, h, r, c), BF16),
        compiler_params=_params(1), name="filter_spectrum",
    )(a, a, g, scale)
    return out.reshape(2, r, r, c)


def _dft_mm_t(w, x):
    return lax.dot_general(w, x.astype(BF16), (((0,), (0,)), ((), ())), preferred_element_type=F32)


def _dft2_kernel(a_ref, g_ref, h_ref, o_ref):
    for kk in range(K1_TILE):
        hr, hi = h_ref[0, kk].astype(F32), h_ref[1, kk].astype(F32)
        for p in range(a_ref.shape[0]):
            a = jnp.concatenate(_unpack_pair(a_ref[p, kk]), axis=0)
            x = _dft_mm(g_ref[kk], a)
            r = x.shape[0] // 2
            xr, xi = x[:r], x[r:]
            y = jnp.concatenate([xr * hr - xi * hi, xr * hi + xi * hr], axis=0)
            bb = _dft_mm_t(g_ref[kk], y)
            o_ref[p, :, kk, :] = _pack_pair(bb[:r], bb[r:])


def _dft3_kernel(b_ref, w_ref, u_ref, x_ref, bias_ref, o_ref):
    bias = bias_ref[...]
    for i in range(N2_TILE):
        bb = jnp.concatenate(_unpack_pair(b_ref[0, i]), axis=0)
        y = _dft_mm(w_ref[...], bb)
        half = y.shape[0] // 2
        for h in range(2):
            o_ref[h, :, i, :] = x_ref[h, :, i, :] * (y[h * half:(h + 1) * half] + u_ref[h, :, i, :] * bias)


def _long_conv_gate(u, x, hspec, order, bias, tabs):
    b, l, c = u.shape
    r = DFT_R
    p = b // 2
    nat = (b, r // 2, r, c)
    a = _dft_major(u.reshape(nat), tabs["w1_data"], 2, pack=True)
    spec4 = pl.BlockSpec((p, K1_TILE, r, c), lambda k: (0, k, 0, 0))
    bb = pl.pallas_call(
        _dft2_kernel,
        grid=(r // K1_TILE,),
        in_specs=[spec4,
                  pl.BlockSpec((K1_TILE,) + tabs["g"].shape[1:], lambda k: (k, 0, 0)),
                  pl.BlockSpec((2, K1_TILE, r, c), lambda k: (0, k, 0, order))],
        out_specs=pl.BlockSpec((p, r, K1_TILE, c), lambda k: (0, 0, k, 0)),
        out_shape=jax.ShapeDtypeStruct((p, r, r, c), jnp.uint32),
        compiler_params=_params(1), name="dft_minor_filter",
    )(a, tabs["g"], hspec)
    blk = pl.BlockSpec((2, r // 2, N2_TILE, c), lambda p_, j: (p_, 0, j, 0))
    out = pl.pallas_call(
        _dft3_kernel,
        grid=(p, r // N2_TILE),
        in_specs=[pl.BlockSpec((1, N2_TILE, r, c), lambda p_, j: (p_, j, 0, 0)),
                  _const_spec(tabs["w2"].shape), blk, blk, _const_spec((1, c))],
        out_specs=blk,
        out_shape=jax.ShapeDtypeStruct(nat, F32),
        compiler_params=_params(2), name="idft_major_gate",
    )(bb, tabs["w2"], u.reshape(nat), x.reshape(nat), bias.reshape(1, c))
    return out.reshape(b, l, c)


def _dense_tables(l):
    n = 2 * l
    k = jnp.arange(n, dtype=jnp.int32)
    t = jnp.arange(l, dtype=jnp.int32)
    c, s = _angles(k[:, None] * t[None, :], n)
    wf = jnp.concatenate([jnp.concatenate([c, s], 1), jnp.concatenate([-s, c], 1)], 0)
    ct, st = c.T, s.T
    wi = jnp.concatenate([jnp.concatenate([ct, -st], 1), jnp.concatenate([st, ct], 1)], 0)
    ca, sa = _angles(k[:, None] * k[None, :], n)
    wk = jnp.concatenate([ca, -sa], 0)
    return dict(wf=wf.astype(BF16), wi=wi.astype(BF16), wk=wk.astype(BF16))


def _dense_spectrum_kernel(k_ref, w_ref, sc_ref, o_ref):
    o_ref[...] = _dft_mm(w_ref[...], k_ref[...]) * sc_ref[...]


def _dense_spectrum(kt, scale, wk):
    n, c = kt.shape
    return pl.pallas_call(
        _dense_spectrum_kernel,
        grid=(1,),
        in_specs=[_const_spec((n, c)), _const_spec(wk.shape), _const_spec((1, c))],
        out_specs=pl.BlockSpec((2 * n, c), lambda i: (0, 0)),
        out_shape=jax.ShapeDtypeStruct((2 * n, c), F32),
        compiler_params=_params(1), name="dense_filter_spectrum",
    )(kt, wk, scale)


def _dense_conv_kernel(u_ref, x_ref, wf_ref, wi_ref, h_ref, bias_ref, o_ref):
    u0, u1 = u_ref[0], u_ref[1]
    z = _dft_mm(wf_ref[...], jnp.concatenate([u0, u1], axis=0))
    n = z.shape[0] // 2
    zr, zi = z[:n], z[n:]
    hr, hi = h_ref[:n], h_ref[n:]
    y = _dft_mm(wi_ref[...], jnp.concatenate([zr * hr - zi * hi, zr * hi + zi * hr], axis=0))
    l = y.shape[0] // 2
    bias = bias_ref[...]
    o_ref[0] = x_ref[0] * (y[:l] + u0 * bias)
    o_ref[1] = x_ref[1] * (y[l:] + u1 * bias)


def _dense_conv_gate(u, x, hspec, order, bias, tabs):
    b, l, c = u.shape
    blk = pl.BlockSpec((2, l, c), lambda p_: (p_, 0, 0))
    return pl.pallas_call(
        _dense_conv_kernel,
        grid=(b // 2,),
        in_specs=[blk, blk, _const_spec(tabs["wf"].shape), _const_spec(tabs["wi"].shape),
                  pl.BlockSpec((4 * l, c), lambda p_: (0, order)), _const_spec((1, c))],
        out_specs=blk,
        out_shape=jax.ShapeDtypeStruct((b, l, c), F32),
        compiler_params=_params(1), name="dense_conv_gate",
    )(u, x, tabs["wf"], tabs["wi"], hspec, bias.reshape(1, c))


def _mix_kernel(h_ref, o_ref, pm_ref, y_ref, g_ref, gate_ref, wa_ref, wp_ref, wh_ref, wo_ref, out_ref):
    d = h_ref.shape[2]
    a = _dot(o_ref[0].astype(BF16), wa_ref[...])
    p = _dot(pm_ref[0].astype(BF16), wp_ref[...])
    y = _dot(y_ref[0].astype(BF16), wh_ref[...])
    gate = lambda k: g_ref[0, :, k * d:(k + 1) * d].astype(F32)
    merged = gate(0) * a + gate(1) * p + gate(2) * y
    out = _dot(merged.astype(BF16), wo_ref[...])
    out_ref[0] = h_ref[0] + gate_ref[0] * out


def _mix(h, o, pm, y, g, mod, w_attn_o, w_pool_o, w_hy_o, w_out, tm=512):
    b, s, d = h.shape
    tm = min(tm, s)

    def tok(w):
        return pl.BlockSpec((1, tm, w), lambda b_, i: (b_, i, 0))

    ws = [w.astype(BF16) for w in (w_attn_o, w_pool_o, w_hy_o, w_out)]
    return pl.pallas_call(
        _mix_kernel,
        grid=(b, s // tm),
        in_specs=[tok(d), tok(o.shape[2]), tok(pm.shape[2]), tok(y.shape[2]), tok(g.shape[2]),
                  _mod_spec(mod, d, 2)] + [_const_spec(w.shape) for w in ws],
        out_specs=tok(d),
        out_shape=jax.ShapeDtypeStruct((b, s, d), F32),
        compiler_params=_params(2), name="merge_outproj",
    )(h, o, pm, y, g, mod, *ws)


def _ffn_kernel(h_ref, sh_ref, sc_ref, gate_ref, g_ref, wi_ref, wo_ref, fg_ref, out_ref, *,
                hidden, chunk, final):
    x = h_ref[0]
    a = _norm_mod(x, g_ref[...], sh_ref[0], sc_ref[0]).astype(BF16)
    acc = jnp.zeros(x.shape, F32)
    for c0 in range(0, hidden, chunk):
        gt = _dot(a, wi_ref[:, c0:c0 + chunk])
        up = _dot(a, wi_ref[:, hidden + c0:hidden + c0 + chunk])
        act = (gt * jax.nn.sigmoid(gt) * up).astype(BF16)
        acc = acc + _dot(act, wo_ref[c0:c0 + chunk, :])
    y = x + gate_ref[0] * acc
    if final:
        ms = jnp.mean(y * y, axis=-1, keepdims=True)
        y = y * lax.rsqrt(ms + EPS) * fg_ref[...]
    out_ref[0] = y


def _ffn(h, mod, norm_g, w_in, w_out, final_g, final, tm=512):
    b, s, d = h.shape
    hidden = w_out.shape[0]
    tm = min(tm, s)
    tok = pl.BlockSpec((1, tm, d), lambda b_, i: (b_, i, 0))
    return pl.pallas_call(
        functools.partial(_ffn_kernel, hidden=hidden, chunk=hidden // 2, final=final),
        grid=(b, s // tm),
        in_specs=[tok, _mod_spec(mod, d, 3), _mod_spec(mod, d, 4), _mod_spec(mod, d, 5),
                  _const_spec((1, d)), _const_spec(w_in.shape), _const_spec(w_out.shape),
                  _const_spec((1, d))],
        out_specs=tok,
        out_shape=jax.ShapeDtypeStruct((b, s, d), F32),
        compiler_params=_params(2), name="swiglu_ffn",
    )(h, mod, mod, mod, norm_g.reshape(1, d), w_in.astype(BF16), w_out.astype(BF16),
      final_g.reshape(1, d))


def _rope_tables(seq, width):
    t = jnp.arange(seq, dtype=jnp.int32)
    row = (t // GRID_W).astype(F32)
    col = (t % GRID_W).astype(F32)
    inv = 1.0 / (ROPE_BASE ** (jnp.arange(ROPE_FREQS, dtype=F32) * 2.0 / (2 * ROPE_FREQS)))
    ar, ac = row[:, None] * inv, col[:, None] * inv
    cos = jnp.concatenate([jnp.cos(ar)] * 2 + [jnp.cos(ac)] * 2, axis=1)
    sin = jnp.concatenate([-jnp.sin(ar), jnp.sin(ar), -jnp.sin(ac), jnp.sin(ac)], axis=1)
    reps = width // cos.shape[1]
    return jnp.tile(cos, (1, reps)), jnp.tile(sin, (1, reps))


def kernel(x, c, ctx, c_ctx, w_mod, b_mod, norm1_g, norm2_g, w_in, lam_qk, subln_g, w_attn_o, w_pool, pool_scale, w_pool_o, w_short, b_short, hf_w1, hf_b1, hf_freq1, hf_w2, hf_b2, hf_freq2, hf_w3, hy_bias, w_hy_o, w_out, w_ffn_in, w_ffn_out, final_g):
    batch, seq, d = x.shape
    depth = w_mod.shape[0]
    ctx_len = ctx.shape[1]
    aw = w_attn_o.shape[1]
    pw = w_pool_o.shape[1]
    hw = w_hy_o.shape[1]
    q0, k0, v0, p0 = 0, aw, 2 * aw, 3 * aw
    h0 = p0 + pw
    g0 = h0 + 3 * hw
    assert seq * 2 == DFT_R * DFT_R and batch % 2 == 0

    c8 = jnp.zeros((8, d), F32).at[:batch].set(c).at[batch].set(c_ctx)
    mod_all = _modulation(c8, w_mod, b_mod)
    rope_tabs = _rope_tables(seq, ATTN_VD)
    tabs = _dft_tables()
    ctx_tabs = _dense_tables(ctx_len)

    h_lat, h_ctx = x, ctx
    for l in range(depth):
        last = l == depth - 1
        lam_init = LAMBDA_INIT_BASE - LAMBDA_INIT_AMP * math.exp(-LAMBDA_INIT_RATE * l)
        mod = mod_all[l, :batch].reshape(batch, 1, N_MOD * d)
        mod_c = mod_all[l, batch:batch + 1].reshape(1, 1, N_MOD * d)
        wl = w_in[l]
        fparams = (hf_w1[l], hf_b1[l], hf_freq1[l], hf_w2[l], hf_b2[l], hf_freq2[l], hf_w3[l])

        wl_bf = wl.astype(BF16)
        local_params = (w_pool[l], pool_scale[l], w_short[l], b_short[l])
        out_dt = [BF16, F32, F32, F32, BF16, BF16, BF16, BF16]
        local_seg = ("local", p0, pw + 3 * hw, 1.0)
        gate_seg = ("sigmoid", g0, 3 * d, 1.0)
        segs = [local_seg, gate_seg, ("rope", q0, aw, Q_SCALE), ("rope", k0, aw, 1.0), ("plain", v0, aw, 1.0)]
        pm, vv, x1, x2, g, q, k, v = _inproj(h_lat, mod, norm1_g[l], wl_bf, segs, out_dt, rope_tabs, local_params)

        if last:
            segs_c = [("plain", k0, aw, 1.0), ("plain", v0, aw, 1.0)]
            k_c, v_c = _inproj(h_ctx, mod_c, norm1_g[l], wl_bf, segs_c, [BF16, BF16])
        else:
            segs_c = [local_seg, gate_seg, ("plain", q0, aw, Q_SCALE), ("plain", k0, aw, 1.0), ("plain", v0, aw, 1.0)]
            pm_c, vv_c, x1_c, x2_c, g_c, q_c, k_c, v_c = _inproj(h_ctx, mod_c, norm1_g[l], wl_bf, segs_c, out_dt,
                                                                 None, local_params)
            o_c = _attention(q_c, k_c, v_c, lam_qk[l], subln_g[l], lam_init)
            kt_c, inv_norm_c = _hyena_time_kernel(ctx_len, *fparams)
            hspec_c = _dense_spectrum(kt_c, inv_norm_c / (2 * ctx_len), ctx_tabs["wk"])
            z_c = _dense_conv_gate(vv_c, x1_c, hspec_c, 0, hy_bias[l, 0], ctx_tabs)
            y_c = _dense_conv_gate(z_c, x2_c, hspec_c, 1, hy_bias[l, 1], ctx_tabs)
            h_ctx_mid = _mix(h_ctx, o_c, pm_c, y_c, g_c, mod_c, w_attn_o[l], w_pool_o[l], w_hy_o[l], w_out[l])
            h_ctx_new = _ffn(h_ctx_mid, mod_c, norm2_g[l], w_ffn_in[l], w_ffn_out[l], final_g, False)

        vt = jnp.swapaxes(jnp.concatenate([v_c, v], axis=1), 1, 2)
        o = _attention_t(q, jnp.concatenate([k_c, k], axis=1), vt, lam_qk[l], subln_g[l], lam_init)
        kt, inv_norm = _hyena_time_kernel(seq, *fparams)
        hspec = _filter_spectrum(kt, inv_norm / (2 * seq), tabs)
        z = _long_conv_gate(vv, x1, hspec, 0, hy_bias[l, 0], tabs)
        y = _long_conv_gate(z, x2, hspec, 1, hy_bias[l, 1], tabs)
        h_mid = _mix(h_lat, o, pm, y, g, mod, w_attn_o[l], w_pool_o[l], w_hy_o[l], w_out[l])
        h_lat = _ffn(h_mid, mod, norm2_g[l], w_ffn_in[l], w_ffn_out[l], final_g, last)
        if not last:
            h_ctx = h_ctx_new
    return h_lat
```

```python
import functools
import math

import jax
import jax.numpy as jnp
from jax import lax
from jax.experimental import pallas as pl
from jax.experimental.pallas import tpu as pltpu

F32 = jnp.float32
BF16 = jnp.bfloat16
HIGHEST = lax.Precision.HIGHEST

GRID_W = 64
N_MOD = 6
ATTN_HD = 64
ATTN_VD = 128
Q_SCALE = ATTN_HD ** -0.5 * math.log2(math.e)
ROPE_BASE = 10000.0
ROPE_FREQS = 16
LAMBDA_INIT_BASE = 0.8
LAMBDA_INIT_AMP = 0.6
LAMBDA_INIT_RATE = 0.3
POOL_WINDOWS = (2, 4, 8, 16)
POOL_GC = 128
LANES = 128
SUBLANES = 8
HALO = SUBLANES
PROJ_PIECE = 1024
HYENA_BANDS = 16
HYENA_FAST_DECAY = 0.3
HYENA_SLOW_DECAY = 1.5
HYENA_TARGET = 1e-2
EPS = 1e-6
DFT_R = 128
HALF_ROWS = 72

VMEM_LIMIT = 56 * 1024 * 1024


def _params(n_grid):
    return pltpu.CompilerParams(dimension_semantics=("arbitrary",) * n_grid,
                                vmem_limit_bytes=VMEM_LIMIT)


def _const_spec(shape):
    zeros = (0,) * len(shape)
    return pl.BlockSpec(shape, lambda *_: zeros, pipeline_mode=pl.Buffered(1))


def _dot(a, b):
    return jnp.dot(a, b, preferred_element_type=F32)


def _dft_mm(w, x):
    return _dot(w, x.astype(BF16))


def _mod_kernel(c_ref, w_ref, b_ref, o_ref):
    c = c_ref[...]
    act = c * jax.nn.sigmoid(c)
    o_ref[0] = jnp.dot(act, w_ref[0], precision=HIGHEST, preferred_element_type=F32) + b_ref[0]


def _modulation(cond, w_mod, b_mod):
    depth, d, n = w_mod.shape
    rows = cond.shape[0]
    tn = n // 4
    return pl.pallas_call(
        _mod_kernel,
        grid=(depth, n // tn),
        in_specs=[pl.BlockSpec((rows, d), lambda l, j: (0, 0)),
                  pl.BlockSpec((1, d, tn), lambda l, j: (l, 0, j)),
                  pl.BlockSpec((1, 1, tn), lambda l, j: (l, 0, j))],
        out_specs=pl.BlockSpec((1, rows, tn), lambda l, j: (l, 0, j)),
        out_shape=jax.ShapeDtypeStruct((depth, rows, n), F32),
        compiler_params=_params(2),
        name="modulation",
    )(cond, w_mod, b_mod.reshape(depth, 1, n))


def _mod_spec(mod, d, col):
    if mod.shape[0] == 1:
        return pl.BlockSpec((1, 1, d), lambda b, i: (0, 0, col))
    return pl.BlockSpec((1, 1, d), lambda b, i: (b, 0, col))


def _norm_mod(x, g, shift, scale):
    ms = jnp.mean(x * x, axis=-1, keepdims=True)
    a = x * lax.rsqrt(ms + EPS) * g
    return a * (1.0 + scale) + shift


def _rope(p, cos, sin):
    lanes = cos.shape[1]
    lane = lax.broadcasted_iota(jnp.int32, cos.shape, 1)
    is_a = lane % (2 * ROPE_FREQS) < ROPE_FREQS
    out = []
    for g in range(p.shape[1] // lanes):
        x = p[:, g * lanes:(g + 1) * lanes]
        partner = jnp.where(is_a, pltpu.roll(x, lanes - ROPE_FREQS, 1), pltpu.roll(x, ROPE_FREQS, 1))
        out.append(x * cos + partner * sin)
    return jnp.concatenate(out, axis=1)


def _local_steps(pbuf, hbuf, wp_ref, ps_ref, ws_ref, bs_ref, pm_ref, v_ref, x1_ref, x2_ref, i, ts, seq):
    t = i * ts + lax.broadcasted_iota(jnp.int32, (ts, 1), 0)

    def pool(g, w):
        cs = slice(g * POOL_GC, (g + 1) * POOL_GC)
        acc = pbuf[HALO - w // 2:HALO - w // 2 + ts, cs]
        for j in range(1 - w // 2, w // 2):
            acc = acc + pbuf[HALO + j:HALO + j + ts, cs]
        cnt = (jnp.minimum(t + w // 2, seq) - jnp.maximum(t - w // 2, 0)).astype(F32)
        pooled = acc / cnt - pbuf[HALO:HALO + ts, cs]
        mixed = _dot(pooled.astype(BF16), wp_ref[g])
        pm_ref[0, :, cs] = (mixed * ps_ref[:, cs]).astype(pm_ref.dtype)

    def conv(o_ref, k):
        c = o_ref.shape[2]
        cs = slice(k * c, (k + 1) * c)
        o_ref[0] = (hbuf[HALO - 1:HALO - 1 + ts, cs] * ws_ref[0:1, cs] + hbuf[HALO:HALO + ts, cs] * ws_ref[1:2, cs]
                    + hbuf[HALO + 1:HALO + 1 + ts, cs] * ws_ref[2:3, cs] + bs_ref[:, cs])

    return ([functools.partial(pool, g, w) for g, w in enumerate(POOL_WINDOWS)]
            + [functools.partial(conv, o_ref, k) for k, o_ref in enumerate((v_ref, x1_ref, x2_ref))])


def _inproj_kernel(*refs, segs, rope, local, tm, seq):
    it = iter(refs)
    h_ref = next(it)
    hp_ref, hn_ref = (next(it), next(it)) if local else (None, None)
    sh_ref, sc_ref, g_ref, w_ref = next(it), next(it), next(it), next(it)
    cos_ref, sin_ref = (next(it), next(it)) if rope else (None, None)
    loc_refs = [next(it) for _ in range(4)] if local else None
    rest = list(it)
    n_out = sum(4 if kind == "local" else 1 for kind, _, _, _ in segs)
    out_refs, scratch = rest[:n_out], rest[n_out:]

    g, sh, sc = g_ref[...], sh_ref[0], sc_ref[0]
    a32 = _norm_mod(h_ref[0], g, sh, sc)
    a = a32.astype(BF16)

    def project(kind, col, width, scale, o_ref, c0):
        p = _dot(a, w_ref[:, col + c0:col + c0 + width])
        if kind == "rope":
            p = _rope(p, cos_ref[...], sin_ref[...])
        elif kind == "sigmoid":
            p = jax.nn.sigmoid(p)
        if scale != 1.0:
            p = p * scale
        o_ref[0, :, c0:c0 + width] = p.astype(o_ref.dtype)

    mxu_work, vpu_work = [], []
    oi = 0
    for kind, col, width, scale in segs:
        if kind == "local":
            i = pl.program_id(1)
            ext = jnp.concatenate([_norm_mod(hp_ref[0], g, sh, sc), a32, _norm_mod(hn_ref[0], g, sh, sc)],
                                  axis=0).astype(BF16)
            u = _dot(ext, w_ref[:, col:col + width])
            row = lax.broadcasted_iota(jnp.int32, (tm + 2 * HALO, 1), 0)
            inside = jnp.logical_and(jnp.logical_or(i > 0, row >= HALO),
                                     jnp.logical_or(i < pl.num_programs(1) - 1, row < HALO + tm))
            u = jnp.where(inside, u, 0.0)
            pbuf, hbuf = scratch
            cp = pbuf.shape[1]
            pbuf[...] = u[:, :cp]
            hbuf[...] = u[:, cp:]
            vpu_work = _local_steps(pbuf, hbuf, *loc_refs, *out_refs[oi:oi + 4], i, tm, seq)
            oi += 4
            continue
        piece = min(width, PROJ_PIECE)
        for c0 in range(0, width, piece):
            mxu_work.append(functools.partial(project, kind, col, piece, scale, out_refs[oi], c0))
        oi += 1
    for k in range(max(len(mxu_work), len(vpu_work))):
        if k < len(vpu_work):
            vpu_work[k]()
        if k < len(mxu_work):
            mxu_work[k]()


def _inproj(h, mod, norm_g, w_cat, segs, out_dtypes, rope_tabs=None, local_params=None, tm=256):
    b, s, d = h.shape
    tm = min(tm, s)
    rope = rope_tabs is not None
    local = local_params is not None
    nh = tm // HALO
    tok = lambda w: pl.BlockSpec((1, tm, w), lambda b_, i: (b_, i, 0))
    in_specs = [tok(d)]
    args = [h]
    if local:
        in_specs += [pl.BlockSpec((1, HALO, d), lambda b_, i: (b_, jnp.maximum(i * nh - 1, 0), 0)),
                     pl.BlockSpec((1, HALO, d), lambda b_, i: (b_, jnp.minimum((i + 1) * nh, s // HALO - 1), 0))]
        args += [h, h]
    in_specs += [_mod_spec(mod, d, 0), _mod_spec(mod, d, 1), _const_spec((1, d)), _const_spec(w_cat.shape)]
    args += [mod, mod, norm_g.reshape(1, d), w_cat]
    if rope:
        wr = rope_tabs[0].shape[1]
        in_specs += [pl.BlockSpec((tm, wr), lambda b_, i: (i, 0))] * 2
        args += list(rope_tabs)
    scratch = []
    if local:
        w_pool, pool_scale, w_short, b_short = local_params
        cp, ch = pool_scale.shape[0], b_short.shape[0]
        in_specs += [_const_spec(w_pool.shape), _const_spec((1, cp)), _const_spec(w_short.shape),
                     _const_spec((1, ch))]
        args += [w_pool.astype(BF16), pool_scale.reshape(1, cp), w_short, b_short.reshape(1, ch)]
        scratch = [pltpu.VMEM((tm + 2 * HALO, cp), F32), pltpu.VMEM((tm + 2 * HALO, ch), F32)]
    widths = []
    for kind, _, w, _ in segs:
        widths += [cp, ch // 3, ch // 3, ch // 3] if kind == "local" else [w]
    out_specs = [tok(w) for w in widths]
    out_shape = [jax.ShapeDtypeStruct((b, s, w), dt) for w, dt in zip(widths, out_dtypes)]
    return pl.pallas_call(
        functools.partial(_inproj_kernel, segs=tuple(segs), rope=rope, local=local, tm=tm, seq=s),
        grid=(b, s // tm), in_specs=in_specs, out_specs=out_specs, out_shape=out_shape,
        scratch_shapes=scratch, compiler_params=_params(2), name="inproj",
    )(*args)


def _lambda(lq_ref, lam_init):
    lq = lq_ref[...]
    return (jnp.exp(jnp.sum(lq[0:1] * lq[1:2], axis=-1, keepdims=True))
            - jnp.exp(jnp.sum(lq[2:3] * lq[3:4], axis=-1, keepdims=True)) + lam_init)


def _split_halves(q):
    lane = lax.broadcasted_iota(jnp.int32, q.shape, 1)
    zero = jnp.zeros_like(q)
    return jnp.concatenate([jnp.where(lane < ATTN_HD, q, zero),
                            jnp.where(lane >= ATTN_HD, q, zero)], axis=0)


def _qk(qq, k):
    return lax.dot_general(qq, k, (((1,), (1,)), ((), ())), preferred_element_type=F32)


def _sub_ln(o, g, lam_init):
    ms = jnp.mean(o * o, axis=-1, keepdims=True)
    return o * lax.rsqrt(ms + EPS) * g * (1.0 - lam_init)


def _attn_kernel(lq_ref, q_ref, k_ref, v_ref, g_ref, o_ref, *, lam_init, tq):
    lam = _lambda(lq_ref, lam_init)
    s = _qk(_split_halves(q_ref[0]), k_ref[0])
    m = jnp.max(s, axis=-1, keepdims=True)
    p = jnp.exp2(s - m)
    l = jnp.sum(p, axis=-1, keepdims=True)
    a = (p[:tq] - p[tq:] * (lam * l[:tq] / l[tq:])).astype(BF16)
    o = _dot(a, v_ref[0]) / l[:tq]
    o_ref[0] = _sub_ln(o, g_ref[...], lam_init).astype(o_ref.dtype)


def _attn_t_kernel(lq_ref, q_ref, k_ref, vt_ref, g_ref, o_ref, s0, s1, m_scr, *, lam_init, tq, lk, kc, nq):
    j = pl.program_id(2)
    bufs = (s0, s1)

    @pl.when(j == 0)
    def _():
        s1[...] = jnp.zeros_like(s1)
        m_scr[1] = jnp.zeros(m_scr.shape[1:], F32)

    lam = _lambda(lq_ref, lam_init)
    for r in range(2):
        sig = 2 * j + r
        buf_a, buf_b = bufs[r], bufs[1 - r]
        t_a = jnp.minimum(sig, nq - 1)
        t_b = jnp.clip(sig - 1, 0, nq - 1)
        qq = _split_halves(q_ref[0, pl.ds(pl.multiple_of(t_a * tq, tq), tq), :])
        m_b = m_scr[1 - r]
        m8 = l8 = acc = None
        for c0 in range(0, lk, kc):
            rows = slice(c0, c0 + kc)
            s = _qk(k_ref[0, rows, :], qq)
            buf_a[rows, :] = s
            for i in range(0, kc, SUBLANES):
                m8 = s[i:i + SUBLANES] if m8 is None else jnp.maximum(m8, s[i:i + SUBLANES])
            p = jnp.exp2(buf_b[rows, :].reshape(kc // SUBLANES, SUBLANES, 2 * tq) - m_b)
            lc = jnp.sum(p, axis=0)
            l8 = lc if l8 is None else l8 + lc
            pv = _dot(vt_ref[0, :, rows], p.reshape(kc, 2 * tq).astype(BF16))
            acc = pv if acc is None else acc + pv
        m_scr[r] = jnp.broadcast_to(jnp.max(m8, axis=0, keepdims=True), m8.shape)
        l = jnp.sum(l8, axis=0, keepdims=True)
        ot = acc[:, :tq] / l[:, :tq] - acc[:, tq:] * (lam / l[:, tq:])
        ms = jnp.mean(ot * ot, axis=0, keepdims=True)
        ot = ot * lax.rsqrt(ms + EPS) * g_ref[...] * (1.0 - lam_init)
        o_ref[0, pl.ds(pl.multiple_of(t_b * tq, tq), tq), :] = ot.T.astype(o_ref.dtype)


def _attention_t(q, k, vt, lam_qk, subln_g, lam_init, tq=128, kc=768):
    b, s, w = q.shape
    lk = k.shape[1]
    heads = w // ATTN_VD
    nq = s // tq
    nsteps = -(-(nq + 1) // 2)
    return pl.pallas_call(
        functools.partial(_attn_t_kernel, lam_init=lam_init, tq=tq, lk=lk, kc=kc, nq=nq),
        grid=(b, heads, nsteps),
        in_specs=[pl.BlockSpec(lam_qk.shape, lambda b_, h, i: (0, 0)),
                  pl.BlockSpec((1, s, ATTN_VD), lambda b_, h, i: (b_, 0, h)),
                  pl.BlockSpec((1, lk, ATTN_VD), lambda b_, h, i: (b_, 0, h)),
                  pl.BlockSpec((1, ATTN_VD, lk), lambda b_, h, i: (b_, h, 0)),
                  pl.BlockSpec((ATTN_VD, 1), lambda b_, h, i: (0, 0))],
        out_specs=pl.BlockSpec((1, s, ATTN_VD), lambda b_, h, i: (b_, 0, h)),
        out_shape=jax.ShapeDtypeStruct((b, s, w), BF16),
        scratch_shapes=[pltpu.VMEM((lk, 2 * tq), F32)] * 2 + [pltpu.VMEM((2, SUBLANES, 2 * tq), F32)],
        compiler_params=_params(3), name="diff_attention_t",
    )(lam_qk, q, k, vt, subln_g.reshape(ATTN_VD, 1))


def _attention(q, k, v, lam_qk, subln_g, lam_init, tq=128):
    b, s, w = q.shape
    lk = k.shape[1]
    heads = w // ATTN_VD
    tq = min(tq, s)
    return pl.pallas_call(
        functools.partial(_attn_kernel, lam_init=lam_init, tq=tq),
        grid=(b, heads, s // tq),
        in_specs=[pl.BlockSpec(lam_qk.shape, lambda b_, h, i: (0, 0)),
                  pl.BlockSpec((1, tq, ATTN_VD), lambda b_, h, i: (b_, i, h)),
                  pl.BlockSpec((1, lk, ATTN_VD), lambda b_, h, i: (b_, 0, h)),
                  pl.BlockSpec((1, lk, ATTN_VD), lambda b_, h, i: (b_, 0, h)),
                  pl.BlockSpec((1, ATTN_VD), lambda b_, h, i: (0, 0))],
        out_specs=pl.BlockSpec((1, tq, ATTN_VD), lambda b_, h, i: (b_, i, h)),
        out_shape=jax.ShapeDtypeStruct((b, s, w), BF16),
        compiler_params=_params(3), name="diff_attention",
    )(lam_qk, q, k, v, subln_g.reshape(1, ATTN_VD))


def _filter_kernel(z_ref, w1_ref, b1_ref, f1_ref, w2_ref, b2_ref, f2_ref, w3_ref, dl_ref, h_ref, asum_ref, *,
                   tl, l):
    i = pl.program_id(0)
    half = tl // 2
    z = z_ref[...]
    z2 = jnp.concatenate([z[:half], z[half:]], axis=1)
    pre = jnp.dot(z2, w1_ref[...], precision=HIGHEST, preferred_element_type=F32)
    h1 = jnp.sin(f1_ref[...] * (pre + b1_ref[...]))
    h2 = jnp.sin(f2_ref[...] * (jnp.dot(h1, w2_ref[...], precision=HIGHEST,
                                        preferred_element_type=F32) + b2_ref[...]))
    hh = h2.astype(BF16)
    hl = (h2 - hh.astype(F32)).astype(BF16)
    h2s = jnp.concatenate([hh, hl, hh], axis=1)
    part = None
    for j in range(2):
        rows = slice(j * half, (j + 1) * half)
        h3 = _dot(h2s, w3_ref[0, j])
        n = i * tl + j * half + lax.broadcasted_iota(jnp.int32, (half, 1), 0)
        h = jnp.where(n == l, 0.0, h3 * jnp.exp(-z[rows, 0:1] * dl_ref[...]))
        h_ref[rows, :] = h
        pj = jnp.sum(jnp.abs(h), axis=0, keepdims=True)
        part = pj if part is None else part + pj

    @pl.when(i == 0)
    def _():
        asum_ref[...] = jnp.zeros_like(asum_ref)

    asum_ref[...] += part


def _hyena_time_kernel(l, w1, b1, f1, w2, b2, f2, w3):
    ffn = w1.shape[1]
    nc = w3.shape[1] // 2
    c = nc // 2
    tl = min(l, 512)
    w3d = jnp.swapaxes(w3.reshape(ffn, 2, nc), 0, 1)
    n = jnp.arange(2 * l, dtype=jnp.int32)
    t = jnp.where(n < l, n, 2 * l - n).astype(F32)
    bands = jnp.linspace(1e-4, HYENA_BANDS - 1, HYENA_BANDS, dtype=F32)
    ang = 2.0 * math.pi * t[:, None] * bands[None, :] / l
    emb = 2 * HYENA_BANDS + 1
    z = jnp.concatenate([(t / (l - 1))[:, None], jnp.cos(ang), jnp.sin(ang),
                         jnp.zeros((2 * l, LANES - emb), F32)], axis=1)
    w1p = jnp.zeros((LANES, ffn), F32).at[:emb].set(w1)
    zf = jnp.zeros((ffn, ffn), F32)
    w1b = jnp.concatenate([jnp.concatenate([w1p, jnp.zeros_like(w1p)], 1),
                           jnp.concatenate([jnp.zeros_like(w1p), w1p], 1)], 0)
    w2b = jnp.concatenate([jnp.concatenate([w2, zf], 1), jnp.concatenate([zf, w2], 1)], 0)
    w3hi = w3d.astype(BF16)
    w3lo = (w3d - w3hi.astype(F32)).astype(BF16)
    zw = jnp.zeros_like(w3hi)
    first = jnp.concatenate([w3hi, zw, w3hi, zw, w3lo, zw], axis=1)
    second = jnp.concatenate([zw, w3hi, zw, w3hi, zw, w3lo], axis=1)
    w3s = jnp.stack([first, second], axis=1)
    max_decay = math.log(HYENA_TARGET) / HYENA_FAST_DECAY
    min_decay = math.log(HYENA_TARGET) / HYENA_SLOW_DECAY
    deltas = jnp.abs(jnp.linspace(min_decay, max_decay, c, dtype=F32))
    dl = jnp.tile(deltas, 2).reshape(1, nc)
    row2 = lambda a: jnp.tile(a.reshape(1, -1), (1, 2))
    nfwd = l // tl
    kt, asum = pl.pallas_call(
        functools.partial(_filter_kernel, tl=tl, l=l),
        grid=(2 * nfwd,),
        in_specs=[pl.BlockSpec((tl, LANES), lambda i: (i, 0)), _const_spec(w1b.shape),
                  _const_spec((1, 2 * ffn)), _const_spec((1, 2 * ffn)),
                  _const_spec(w2b.shape), _const_spec((1, 2 * ffn)), _const_spec((1, 2 * ffn)),
                  pl.BlockSpec((1,) + w3s.shape[1:], lambda i: (i // nfwd, 0, 0, 0)), _const_spec((1, nc))],
        out_specs=[pl.BlockSpec((tl, nc), lambda i: (i, 0)), pl.BlockSpec((1, nc), lambda i: (0, 0))],
        out_shape=[jax.ShapeDtypeStruct((2 * l, nc), F32), jax.ShapeDtypeStruct((1, nc), F32)],
        compiler_params=_params(1), name="hyena_filter_mlp",
    )(z, w1b, row2(b1), row2(f1), w2b, row2(b2), row2(f2), w3s, dl)
    return kt, 1.0 / asum


def _angles(prod, n):
    th = (2.0 * math.pi / n) * (prod % n).astype(F32)
    return jnp.cos(th), jnp.sin(th)


def _dft_tables():
    r = DFT_R
    n = r * r
    idx = jnp.arange(r, dtype=jnp.int32)
    c, s = _angles(idx[:, None] * idx[None, :], r)
    ch, sh = c[:, :r // 2], s[:, :r // 2]
    w1_data = jnp.concatenate([jnp.concatenate([ch, sh], 1), jnp.concatenate([-sh, ch], 1)], 0)
    pad = jnp.zeros((HALF_ROWS - (r // 2 + 1), r), F32)
    w1_real = jnp.concatenate([c[:r // 2 + 1], pad, -s[:r // 2 + 1], pad], 0)
    ct, st = c[:r // 2], s[:r // 2]
    w2 = jnp.concatenate([jnp.concatenate([ct, -st], 1), jnp.concatenate([st, ct], 1)], 0)
    k = idx[:, None, None] + r * idx[None, :, None]
    cg, sg = _angles(k * idx[None, None, :], n)
    g = jnp.concatenate([jnp.concatenate([cg, sg], 2), jnp.concatenate([-sg, cg], 2)], 1)
    return dict(w1_data=w1_data.astype(BF16), w1_real=w1_real.astype(BF16), w2=w2.astype(BF16),
                g=g.astype(BF16))


N2_TILE = 16
K1_TILE = 8


def _pack_pair(re, im):
    rb = lax.bitcast_convert_type(re.astype(BF16).astype(F32), jnp.uint32)
    ib = lax.bitcast_convert_type(im.astype(BF16).astype(F32), jnp.uint32)
    return rb | (ib >> 16)


def _unpack_pair(w):
    re = lax.bitcast_convert_type(w & jnp.uint32(0xFFFF0000), F32)
    im = lax.bitcast_convert_type(w << 16, F32)
    return re, im


def _dft1_kernel(x_ref, w_ref, o_ref, *, parts, pack):
    r = w_ref.shape[0] // 2
    for i in range(N2_TILE):
        x = jnp.concatenate([x_ref[p, :, i, :] for p in range(parts)], axis=0)
        y = _dft_mm(w_ref[...], x)
        if pack:
            o_ref[0, :, i, :] = _pack_pair(y[:r], y[r:])
        else:
            o_ref[0, 0, :, i, :] = y[:r]
            o_ref[0, 1, :, i, :] = y[r:]


def _dft_major(x, w3, parts, pack=False):
    b, n1, r, c = x.shape
    k1 = w3.shape[0] // 2
    if pack:
        out_spec = pl.BlockSpec((1, k1, N2_TILE, c), lambda p_, j: (p_, 0, j, 0))
        out_shape = jax.ShapeDtypeStruct((b // parts, k1, r, c), jnp.uint32)
    else:
        out_spec = pl.BlockSpec((1, 2, k1, N2_TILE, c), lambda p_, j: (p_, 0, 0, j, 0))
        out_shape = jax.ShapeDtypeStruct((b // parts, 2, k1, r, c), F32)
    return pl.pallas_call(
        functools.partial(_dft1_kernel, parts=parts, pack=pack),
        grid=(b // parts, r // N2_TILE),
        in_specs=[pl.BlockSpec((parts, n1, N2_TILE, c), lambda p_, j: (p_, 0, j, 0)), _const_spec(w3.shape)],
        out_specs=out_spec, out_shape=out_shape,
        compiler_params=_params(2), name="dft_major",
    )(x, w3)


def _spectrum_kernel(lo_ref, hi_ref, g_ref, sc_ref, o_ref):
    for half, a_ref, sign in ((0, lo_ref, 1.0), (1, hi_ref, -1.0)):
        re, im = _unpack_pair(a_ref[0, 0])
        a = jnp.concatenate([re, im * sign], axis=0)
        x = _dft_mm(g_ref[half, 0], a) * sc_ref[...]
        r = x.shape[0] // 2
        o_ref[0, half, 0] = x[:r].astype(o_ref.dtype)
        o_ref[1, half, 0] = x[r:].astype(o_ref.dtype)


def _filter_spectrum(kt, scale, tabs):
    r = DFT_R
    h = r // 2
    c = kt.shape[1]
    a = _dft_major(kt.reshape(1, r, r, c), tabs["w1_real"], 1, pack=True)
    g = tabs["g"].reshape((2, h) + tabs["g"].shape[1:])
    out = pl.pallas_call(
        _spectrum_kernel,
        grid=(h,),
        in_specs=[pl.BlockSpec((1, 1, r, c), lambda k: (0, k, 0, 0)),
                  pl.BlockSpec((1, 1, r, c), lambda k: (0, h - k, 0, 0)),
                  pl.BlockSpec((2, 1) + g.shape[2:], lambda k: (0, k, 0, 0)),
                  _const_spec((1, c))],
        out_specs=pl.BlockSpec((2, 2, 1, r, c), lambda k: (0, 0, k, 0, 0)),
        out_shape=jax.ShapeDtypeStruct((2, 2, h, r, c), BF16),
        compiler_params=_params(1), name="filter_spectrum",
    )(a, a, g, scale)
    return out.reshape(2, r, r, c)


def _dft_mm_t(w, x):
    return lax.dot_general(w, x.astype(BF16), (((0,), (0,)), ((), ())), preferred_element_type=F32)


def _dft2_kernel(a_ref, g_ref, h_ref, o_ref):
    for kk in range(K1_TILE):
        hr, hi = h_ref[0, kk].astype(F32), h_ref[1, kk].astype(F32)
        for p in range(a_ref.shape[0]):
            a = jnp.concatenate(_unpack_pair(a_ref[p, kk]), axis=0)
            x = _dft_mm(g_ref[kk], a)
            r = x.shape[0] // 2
            xr, xi = x[:r], x[r:]
            y = jnp.concatenate([xr * hr - xi * hi, xr * hi + xi * hr], axis=0)
            bb = _dft_mm_t(g_ref[kk], y)
            o_ref[p, :, kk, :] = _pack_pair(bb[:r], bb[r:])


def _dft3_kernel(b_ref, w_ref, u_ref, x_ref, bias_ref, o_ref):
    bias = bias_ref[...]
    for i in range(N2_TILE):
        bb = jnp.concatenate(_unpack_pair(b_ref[0, i]), axis=0)
        y = _dft_mm(w_ref[...], bb)
        half = y.shape[0] // 2
        for h in range(2):
            o_ref[h, :, i, :] = x_ref[h, :, i, :] * (y[h * half:(h + 1) * half] + u_ref[h, :, i, :] * bias)


def _long_conv_gate(u, x, hspec, order, bias, tabs):
    b, l, c = u.shape
    r = DFT_R
    p = b // 2
    nat = (b, r // 2, r, c)
    a = _dft_major(u.reshape(nat), tabs["w1_data"], 2, pack=True)
    spec4 = pl.BlockSpec((p, K1_TILE, r, c), lambda k: (0, k, 0, 0))
    bb = pl.pallas_call(
        _dft2_kernel,
        grid=(r // K1_TILE,),
        in_specs=[spec4,
                  pl.BlockSpec((K1_TILE,) + tabs["g"].shape[1:], lambda k: (k, 0, 0)),
                  pl.BlockSpec((2, K1_TILE, r, c), lambda k: (0, k, 0, order))],
        out_specs=pl.BlockSpec((p, r, K1_TILE, c), lambda k: (0, 0, k, 0)),
        out_shape=jax.ShapeDtypeStruct((p, r, r, c), jnp.uint32),
        compiler_params=_params(1), name="dft_minor_filter",
    )(a, tabs["g"], hspec)
    blk = pl.BlockSpec((2, r // 2, N2_TILE, c), lambda p_, j: (p_, 0, j, 0))
    out = pl.pallas_call(
        _dft3_kernel,
        grid=(p, r // N2_TILE),
        in_specs=[pl.BlockSpec((1, N2_TILE, r, c), lambda p_, j: (p_, j, 0, 0)),
                  _const_spec(tabs["w2"].shape), blk, blk, _const_spec((1, c))],
        out_specs=blk,
        out_shape=jax.ShapeDtypeStruct(nat, F32),
        compiler_params=_params(2), name="idft_major_gate",
    )(bb, tabs["w2"], u.reshape(nat), x.reshape(nat), bias.reshape(1, c))
    return out.reshape(b, l, c)


def _dense_tables(l):
    n = 2 * l
    k = jnp.arange(n, dtype=jnp.int32)
    t = jnp.arange(l, dtype=jnp.int32)
    c, s = _angles(k[:, None] * t[None, :], n)
    wf = jnp.concatenate([jnp.concatenate([c, s], 1), jnp.concatenate([-s, c], 1)], 0)
    ct, st = c.T, s.T
    wi = jnp.concatenate([jnp.concatenate([ct, -st], 1), jnp.concatenate([st, ct], 1)], 0)
    ca, sa = _angles(k[:, None] * k[None, :], n)
    wk = jnp.concatenate([ca, -sa], 0)
    return dict(wf=wf.astype(BF16), wi=wi.astype(BF16), wk=wk.astype(BF16))


def _dense_spectrum_kernel(k_ref, w_ref, sc_ref, o_ref):
    o_ref[...] = _dft_mm(w_ref[...], k_ref[...]) * sc_ref[...]


def _dense_spectrum(kt, scale, wk):
    n, c = kt.shape
    return pl.pallas_call(
        _dense_spectrum_kernel,
        grid=(1,),
        in_specs=[_const_spec((n, c)), _const_spec(wk.shape), _const_spec((1, c))],
        out_specs=pl.BlockSpec((2 * n, c), lambda i: (0, 0)),
        out_shape=jax.ShapeDtypeStruct((2 * n, c), F32),
        compiler_params=_params(1), name="dense_filter_spectrum",
    )(kt, wk, scale)


def _dense_conv_kernel(u_ref, x_ref, wf_ref, wi_ref, h_ref, bias_ref, o_ref):
    u0, u1 = u_ref[0], u_ref[1]
    z = _dft_mm(wf_ref[...], jnp.concatenate([u0, u1], axis=0))
    n = z.shape[0] // 2
    zr, zi = z[:n], z[n:]
    hr, hi = h_ref[:n], h_ref[n:]
    y = _dft_mm(wi_ref[...], jnp.concatenate([zr * hr - zi * hi, zr * hi + zi * hr], axis=0))
    l = y.shape[0] // 2
    bias = bias_ref[...]
    o_ref[0] = x_ref[0] * (y[:l] + u0 * bias)
    o_ref[1] = x_ref[1] * (y[l:] + u1 * bias)


def _dense_conv_gate(u, x, hspec, order, bias, tabs):
    b, l, c = u.shape
    blk = pl.BlockSpec((2, l, c), lambda p_: (p_, 0, 0))
    return pl.pallas_call(
        _dense_conv_kernel,
        grid=(b // 2,),
        in_specs=[blk, blk, _const_spec(tabs["wf"].shape), _const_spec(tabs["wi"].shape),
                  pl.BlockSpec((4 * l, c), lambda p_: (0, order)), _const_spec((1, c))],
        out_specs=blk,
        out_shape=jax.ShapeDtypeStruct((b, l, c), F32),
        compiler_params=_params(1), name="dense_conv_gate",
    )(u, x, tabs["wf"], tabs["wi"], hspec, bias.reshape(1, c))


def _mix_kernel(h_ref, o_ref, pm_ref, y_ref, g_ref, gate_ref, wa_ref, wp_ref, wh_ref, wo_ref, out_ref):
    d = h_ref.shape[2]
    a = _dot(o_ref[0].astype(BF16), wa_ref[...])
    p = _dot(pm_ref[0].astype(BF16), wp_ref[...])
    y = _dot(y_ref[0].astype(BF16), wh_ref[...])
    gate = lambda k: g_ref[0, :, k * d:(k + 1) * d].astype(F32)
    merged = gate(0) * a + gate(1) * p + gate(2) * y
    out = _dot(merged.astype(BF16), wo_ref[...])
    out_ref[0] = h_ref[0] + gate_ref[0] * out


def _mix(h, o, pm, y, g, mod, w_attn_o, w_pool_o, w_hy_o, w_out, tm=512):
    b, s, d = h.shape
    tm = min(tm, s)

    def tok(w):
        return pl.BlockSpec((1, tm, w), lambda b_, i: (b_, i, 0))

    ws = [w.astype(BF16) for w in (w_attn_o, w_pool_o, w_hy_o, w_out)]
    return pl.pallas_call(
        _mix_kernel,
        grid=(b, s // tm),
        in_specs=[tok(d), tok(o.shape[2]), tok(pm.shape[2]), tok(y.shape[2]), tok(g.shape[2]),
                  _mod_spec(mod, d, 2)] + [_const_spec(w.shape) for w in ws],
        out_specs=tok(d),
        out_shape=jax.ShapeDtypeStruct((b, s, d), F32),
        compiler_params=_params(2), name="merge_outproj",
    )(h, o, pm, y, g, mod, *ws)


def _ffn_kernel(h_ref, sh_ref, sc_ref, gate_ref, g_ref, wi_ref, wo_ref, fg_ref, out_ref, *,
                hidden, chunk, final):
    x = h_ref[0]
    a = _norm_mod(x, g_ref[...], sh_ref[0], sc_ref[0]).astype(BF16)
    acc = jnp.zeros(x.shape, F32)
    for c0 in range(0, hidden, chunk):
        gt = _dot(a, wi_ref[:, c0:c0 + chunk])
        up = _dot(a, wi_ref[:, hidden + c0:hidden + c0 + chunk])
        act = (gt * jax.nn.sigmoid(gt) * up).astype(BF16)
        acc = acc + _dot(act, wo_ref[c0:c0 + chunk, :])
    y = x + gate_ref[0] * acc
    if final:
        ms = jnp.mean(y * y, axis=-1, keepdims=True)
        y = y * lax.rsqrt(ms + EPS) * fg_ref[...]
    out_ref[0] = y


def _ffn(h, mod, norm_g, w_in, w_out, final_g, final, tm=512):
    b, s, d = h.shape
    hidden = w_out.shape[0]
    tm = min(tm, s)
    tok = pl.BlockSpec((1, tm, d), lambda b_, i: (b_, i, 0))
    return pl.pallas_call(
        functools.partial(_ffn_kernel, hidden=hidden, chunk=hidden // 2, final=final),
        grid=(b, s // tm),
        in_specs=[tok, _mod_spec(mod, d, 3), _mod_spec(mod, d, 4), _mod_spec(mod, d, 5),
                  _const_spec((1, d)), _const_spec(w_in.shape), _const_spec(w_out.shape),
                  _const_spec((1, d))],
        out_specs=tok,
        out_shape=jax.ShapeDtypeStruct((b, s, d), F32),
        compiler_params=_params(2), name="swiglu_ffn",
    )(h, mod, mod, mod, norm_g.reshape(1, d), w_in.astype(BF16), w_out.astype(BF16),
      final_g.reshape(1, d))


def _rope_tables(seq, width):
    t = jnp.arange(seq, dtype=jnp.int32)
    row = (t // GRID_W).astype(F32)
    col = (t % GRID_W).astype(F32)
    inv = 1.0 / (ROPE_BASE ** (jnp.arange(ROPE_FREQS, dtype=F32) * 2.0 / (2 * ROPE_FREQS)))
    ar, ac = row[:, None] * inv, col[:, None] * inv
    cos = jnp.concatenate([jnp.cos(ar)] * 2 + [jnp.cos(ac)] * 2, axis=1)
    sin = jnp.concatenate([-jnp.sin(ar), jnp.sin(ar), -jnp.sin(ac), jnp.sin(ac)], axis=1)
    reps = width // cos.shape[1]
    return jnp.tile(cos, (1, reps)), jnp.tile(sin, (1, reps))


def kernel(x, c, ctx, c_ctx, w_mod, b_mod, norm1_g, norm2_g, w_in, lam_qk, subln_g, w_attn_o, w_pool, pool_scale, w_pool_o, w_short, b_short, hf_w1, hf_b1, hf_freq1, hf_w2, hf_b2, hf_freq2, hf_w3, hy_bias, w_hy_o, w_out, w_ffn_in, w_ffn_out, final_g):
    batch, seq, d = x.shape
    depth = w_mod.shape[0]
    ctx_len = ctx.shape[1]
    aw = w_attn_o.shape[1]
    pw = w_pool_o.shape[1]
    hw = w_hy_o.shape[1]
    q0, k0, v0, p0 = 0, aw, 2 * aw, 3 * aw
    h0 = p0 + pw
    g0 = h0 + 3 * hw
    assert seq * 2 == DFT_R * DFT_R and batch % 2 == 0 and batch + 1 <= SUBLANES

    cond = jnp.zeros((SUBLANES, d), F32).at[:batch].set(c).at[batch].set(c_ctx)
    mod_all = _modulation(cond, w_mod, b_mod)
    rope_tabs = _rope_tables(seq, ATTN_VD)
    tabs = _dft_tables()
    ctx_tabs = _dense_tables(ctx_len)

    h_lat, h_ctx = x, ctx
    for l in range(depth):
        last = l == depth - 1
        lam_init = LAMBDA_INIT_BASE - LAMBDA_INIT_AMP * math.exp(-LAMBDA_INIT_RATE * l)
        mod = mod_all[l, :batch].reshape(batch, 1, N_MOD * d)
        mod_c = mod_all[l, batch:batch + 1].reshape(1, 1, N_MOD * d)
        wl = w_in[l]
        fparams = (hf_w1[l], hf_b1[l], hf_freq1[l], hf_w2[l], hf_b2[l], hf_freq2[l], hf_w3[l])

        wl_bf = wl.astype(BF16)
        local_params = (w_pool[l], pool_scale[l], w_short[l], b_short[l])
        out_dt = [BF16, F32, F32, F32, BF16, BF16, BF16, BF16]
        local_seg = ("local", p0, pw + 3 * hw, 1.0)
        gate_seg = ("sigmoid", g0, 3 * d, 1.0)
        segs = [local_seg, gate_seg, ("rope", q0, aw, Q_SCALE), ("rope", k0, aw, 1.0), ("plain", v0, aw, 1.0)]
        pm, vv, x1, x2, g, q, k, v = _inproj(h_lat, mod, norm1_g[l], wl_bf, segs, out_dt, rope_tabs, local_params)

        if last:
            segs_c = [("plain", k0, aw, 1.0), ("plain", v0, aw, 1.0)]
            k_c, v_c = _inproj(h_ctx, mod_c, norm1_g[l], wl_bf, segs_c, [BF16, BF16])
        else:
            segs_c = [local_seg, gate_seg, ("plain", q0, aw, Q_SCALE), ("plain", k0, aw, 1.0), ("plain", v0, aw, 1.0)]
            pm_c, vv_c, x1_c, x2_c, g_c, q_c, k_c, v_c = _inproj(h_ctx, mod_c, norm1_g[l], wl_bf, segs_c, out_dt,
                                                                 None, local_params)
            o_c = _attention(q_c, k_c, v_c, lam_qk[l], subln_g[l], lam_init)
            kt_c, inv_norm_c = _hyena_time_kernel(ctx_len, *fparams)
            hspec_c = _dense_spectrum(kt_c, inv_norm_c / (2 * ctx_len), ctx_tabs["wk"])
            z_c = _dense_conv_gate(vv_c, x1_c, hspec_c, 0, hy_bias[l, 0], ctx_tabs)
            y_c = _dense_conv_gate(z_c, x2_c, hspec_c, 1, hy_bias[l, 1], ctx_tabs)
            h_ctx_mid = _mix(h_ctx, o_c, pm_c, y_c, g_c, mod_c, w_attn_o[l], w_pool_o[l], w_hy_o[l], w_out[l])
            h_ctx_new = _ffn(h_ctx_mid, mod_c, norm2_g[l], w_ffn_in[l], w_ffn_out[l], final_g, False)

        vt = jnp.swapaxes(jnp.concatenate([v_c, v], axis=1), 1, 2)
        o = _attention_t(q, jnp.concatenate([k_c, k], axis=1), vt, lam_qk[l], subln_g[l], lam_init)
        kt, inv_norm = _hyena_time_kernel(seq, *fparams)
        hspec = _filter_spectrum(kt, inv_norm / (2 * seq), tabs)
        z = _long_conv_gate(vv, x1, hspec, 0, hy_bias[l, 0], tabs)
        y = _long_conv_gate(z, x2, hspec, 1, hy_bias[l, 1], tabs)
        h_mid = _mix(h_lat, o, pm, y, g, mod, w_attn_o[l], w_pool_o[l], w_hy_o[l], w_out[l])
        h_lat = _ffn(h_mid, mod, norm2_g[l], w_ffn_in[l], w_ffn_out[l], final_g, last)
        if not last:
            h_ctx = h_ctx_new
    return h_lat
```

```python
import functools
import math

import jax
import jax.numpy as jnp
from jax import lax
from jax.experimental import pallas as pl
from jax.experimental.pallas import tpu as pltpu

F32 = jnp.float32
BF16 = jnp.bfloat16
HIGHEST = lax.Precision.HIGHEST

GRID_W = 64
N_MOD = 6
ATTN_HD = 64
ATTN_VD = 128
Q_SCALE = ATTN_HD ** -0.5 * math.log2(math.e)
ROPE_BASE = 10000.0
ROPE_FREQS = 16
LAMBDA_INIT_BASE = 0.8
LAMBDA_INIT_AMP = 0.6
LAMBDA_INIT_RATE = 0.3
POOL_WINDOWS = (2, 4, 8, 16)
POOL_GC = 128
LANES = 128
SUBLANES = 8
HALO = SUBLANES
PROJ_PIECE = 1024
HYENA_BANDS = 16
HYENA_FAST_DECAY = 0.3
HYENA_SLOW_DECAY = 1.5
HYENA_TARGET = 1e-2
EPS = 1e-6
DFT_R = 128
HALF_ROWS = 72

VMEM_LIMIT = 56 * 1024 * 1024


def _params(n_grid):
    return pltpu.CompilerParams(dimension_semantics=("arbitrary",) * n_grid,
                                vmem_limit_bytes=VMEM_LIMIT)


def _const_spec(shape):
    zeros = (0,) * len(shape)
    return pl.BlockSpec(shape, lambda *_: zeros, pipeline_mode=pl.Buffered(1))


def _dot(a, b):
    return jnp.dot(a, b, preferred_element_type=F32)


def _dft_mm(w, x):
    return _dot(w, x.astype(BF16))


def _mod_kernel(c_ref, w_ref, b_ref, o_ref):
    c = c_ref[...]
    act = c * jax.nn.sigmoid(c)
    o_ref[0] = jnp.dot(act, w_ref[0], precision=HIGHEST, preferred_element_type=F32) + b_ref[0]


def _modulation(cond, w_mod, b_mod):
    depth, d, n = w_mod.shape
    rows = cond.shape[0]
    tn = n // 4
    return pl.pallas_call(
        _mod_kernel,
        grid=(depth, n // tn),
        in_specs=[pl.BlockSpec((rows, d), lambda l, j: (0, 0)),
                  pl.BlockSpec((1, d, tn), lambda l, j: (l, 0, j)),
                  pl.BlockSpec((1, 1, tn), lambda l, j: (l, 0, j))],
        out_specs=pl.BlockSpec((1, rows, tn), lambda l, j: (l, 0, j)),
        out_shape=jax.ShapeDtypeStruct((depth, rows, n), F32),
        compiler_params=_params(2),
        name="modulation",
    )(cond, w_mod, b_mod.reshape(depth, 1, n))


def _mod_spec(mod, d, col):
    if mod.shape[0] == 1:
        return pl.BlockSpec((1, 1, d), lambda b, i: (0, 0, col))
    return pl.BlockSpec((1, 1, d), lambda b, i: (b, 0, col))


def _norm_mod(x, g, shift, scale):
    ms = jnp.mean(x * x, axis=-1, keepdims=True)
    a = x * lax.rsqrt(ms + EPS) * g
    return a * (1.0 + scale) + shift


def _rope(p, cos, sin):
    lanes = cos.shape[1]
    lane = lax.broadcasted_iota(jnp.int32, cos.shape, 1)
    is_a = lane % (2 * ROPE_FREQS) < ROPE_FREQS
    out = []
    for g in range(p.shape[1] // lanes):
        x = p[:, g * lanes:(g + 1) * lanes]
        partner = jnp.where(is_a, pltpu.roll(x, lanes - ROPE_FREQS, 1), pltpu.roll(x, ROPE_FREQS, 1))
        out.append(x * cos + partner * sin)
    return jnp.concatenate(out, axis=1)


def _local_steps(pbuf, hbuf, wp_ref, ps_ref, ws_ref, bs_ref, pm_ref, v_ref, x1_ref, x2_ref, i, ts, seq):
    t = i * ts + lax.broadcasted_iota(jnp.int32, (ts, 1), 0)

    def pool(g, w):
        cs = slice(g * POOL_GC, (g + 1) * POOL_GC)
        acc = pbuf[HALO - w // 2:HALO - w // 2 + ts, cs]
        for j in range(1 - w // 2, w // 2):
            acc = acc + pbuf[HALO + j:HALO + j + ts, cs]
        cnt = (jnp.minimum(t + w // 2, seq) - jnp.maximum(t - w // 2, 0)).astype(F32)
        pooled = acc / cnt - pbuf[HALO:HALO + ts, cs]
        mixed = _dot(pooled.astype(BF16), wp_ref[g])
        pm_ref[0, :, cs] = (mixed * ps_ref[:, cs]).astype(pm_ref.dtype)

    def conv(o_ref, k):
        c = o_ref.shape[2]
        cs = slice(k * c, (k + 1) * c)
        o_ref[0] = (hbuf[HALO - 1:HALO - 1 + ts, cs] * ws_ref[0:1, cs] + hbuf[HALO:HALO + ts, cs] * ws_ref[1:2, cs]
                    + hbuf[HALO + 1:HALO + 1 + ts, cs] * ws_ref[2:3, cs] + bs_ref[:, cs])

    return ([functools.partial(pool, g, w) for g, w in enumerate(POOL_WINDOWS)]
            + [functools.partial(conv, o_ref, k) for k, o_ref in enumerate((v_ref, x1_ref, x2_ref))])


def _inproj_kernel(*refs, segs, rope, local, tm, seq):
    it = iter(refs)
    h_ref = next(it)
    hp_ref, hn_ref = (next(it), next(it)) if local else (None, None)
    sh_ref, sc_ref, g_ref, w_ref = next(it), next(it), next(it), next(it)
    cos_ref, sin_ref = (next(it), next(it)) if rope else (None, None)
    loc_refs = [next(it) for _ in range(4)] if local else None
    rest = list(it)
    n_out = sum(4 if kind == "local" else 1 for kind, _, _, _ in segs)
    out_refs, scratch = rest[:n_out], rest[n_out:]

    g, sh, sc = g_ref[...], sh_ref[0], sc_ref[0]
    a32 = _norm_mod(h_ref[0], g, sh, sc)
    a = a32.astype(BF16)

    def project(kind, col, width, scale, o_ref, c0):
        p = _dot(a, w_ref[:, col + c0:col + c0 + width])
        if kind == "rope":
            p = _rope(p, cos_ref[...], sin_ref[...])
        elif kind == "sigmoid":
            p = jax.nn.sigmoid(p)
        if scale != 1.0:
            p = p * scale
        o_ref[0, :, c0:c0 + width] = p.astype(o_ref.dtype)

    mxu_work, vpu_work = [], []
    oi = 0
    for kind, col, width, scale in segs:
        if kind == "local":
            i = pl.program_id(1)
            ext = jnp.concatenate([_norm_mod(hp_ref[0], g, sh, sc), a32, _norm_mod(hn_ref[0], g, sh, sc)],
                                  axis=0).astype(BF16)
            u = _dot(ext, w_ref[:, col:col + width])
            row = lax.broadcasted_iota(jnp.int32, (tm + 2 * HALO, 1), 0)
            inside = jnp.logical_and(jnp.logical_or(i > 0, row >= HALO),
                                     jnp.logical_or(i < pl.num_programs(1) - 1, row < HALO + tm))
            u = jnp.where(inside, u, 0.0)
            pbuf, hbuf = scratch
            cp = pbuf.shape[1]
            pbuf[...] = u[:, :cp]
            hbuf[...] = u[:, cp:]
            vpu_work = _local_steps(pbuf, hbuf, *loc_refs, *out_refs[oi:oi + 4], i, tm, seq)
            oi += 4
            continue
        piece = min(width, PROJ_PIECE)
        for c0 in range(0, width, piece):
            mxu_work.append(functools.partial(project, kind, col, piece, scale, out_refs[oi], c0))
        oi += 1
    for k in range(max(len(mxu_work), len(vpu_work))):
        if k < len(vpu_work):
            vpu_work[k]()
        if k < len(mxu_work):
            mxu_work[k]()


def _inproj(h, mod, norm_g, w_cat, segs, out_dtypes, rope_tabs=None, local_params=None, tm=256):
    b, s, d = h.shape
    tm = min(tm, s)
    rope = rope_tabs is not None
    local = local_params is not None
    nh = tm // HALO
    tok = lambda w: pl.BlockSpec((1, tm, w), lambda b_, i: (b_, i, 0))
    in_specs = [tok(d)]
    args = [h]
    if local:
        in_specs += [pl.BlockSpec((1, HALO, d), lambda b_, i: (b_, jnp.maximum(i * nh - 1, 0), 0)),
                     pl.BlockSpec((1, HALO, d), lambda b_, i: (b_, jnp.minimum((i + 1) * nh, s // HALO - 1), 0))]
        args += [h, h]
    in_specs += [_mod_spec(mod, d, 0), _mod_spec(mod, d, 1), _const_spec((1, d)), _const_spec(w_cat.shape)]
    args += [mod, mod, norm_g.reshape(1, d), w_cat]
    if rope:
        wr = rope_tabs[0].shape[1]
        in_specs += [pl.BlockSpec((tm, wr), lambda b_, i: (i, 0))] * 2
        args += list(rope_tabs)
    scratch = []
    if local:
        w_pool, pool_scale, w_short, b_short = local_params
        cp, ch = pool_scale.shape[0], b_short.shape[0]
        in_specs += [_const_spec(w_pool.shape), _const_spec((1, cp)), _const_spec(w_short.shape),
                     _const_spec((1, ch))]
        args += [w_pool.astype(BF16), pool_scale.reshape(1, cp), w_short, b_short.reshape(1, ch)]
        scratch = [pltpu.VMEM((tm + 2 * HALO, cp), F32), pltpu.VMEM((tm + 2 * HALO, ch), F32)]
    widths = []
    for kind, _, w, _ in segs:
        widths += [cp, ch // 3, ch // 3, ch // 3] if kind == "local" else [w]
    out_specs = [tok(w) for w in widths]
    out_shape = [jax.ShapeDtypeStruct((b, s, w), dt) for w, dt in zip(widths, out_dtypes)]
    return pl.pallas_call(
        functools.partial(_inproj_kernel, segs=tuple(segs), rope=rope, local=local, tm=tm, seq=s),
        grid=(b, s // tm), in_specs=in_specs, out_specs=out_specs, out_shape=out_shape,
        scratch_shapes=scratch, compiler_params=_params(2), name="inproj",
    )(*args)


def _lambda(lq_ref, lam_init):
    lq = lq_ref[...]
    return (jnp.exp(jnp.sum(lq[0:1] * lq[1:2], axis=-1, keepdims=True))
            - jnp.exp(jnp.sum(lq[2:3] * lq[3:4], axis=-1, keepdims=True)) + lam_init)


def _split_halves(q):
    lane = lax.broadcasted_iota(jnp.int32, q.shape, 1)
    zero = jnp.zeros_like(q)
    return jnp.concatenate([jnp.where(lane < ATTN_HD, q, zero),
                            jnp.where(lane >= ATTN_HD, q, zero)], axis=0)


def _qk(qq, k):
    return lax.dot_general(qq, k, (((1,), (1,)), ((), ())), preferred_element_type=F32)


def _sub_ln(o, g, lam_init):
    ms = jnp.mean(o * o, axis=-1, keepdims=True)
    return o * lax.rsqrt(ms + EPS) * g * (1.0 - lam_init)


def _attn_kernel(lq_ref, q_ref, k_ref, v_ref, g_ref, o_ref, *, lam_init, tq):
    lam = _lambda(lq_ref, lam_init)
    s = _qk(_split_halves(q_ref[0]), k_ref[0])
    m = jnp.max(s, axis=-1, keepdims=True)
    p = jnp.exp2(s - m)
    l = jnp.sum(p, axis=-1, keepdims=True)
    a = (p[:tq] - p[tq:] * (lam * l[:tq] / l[tq:])).astype(BF16)
    o = _dot(a, v_ref[0]) / l[:tq]
    o_ref[0] = _sub_ln(o, g_ref[...], lam_init).astype(o_ref.dtype)


def _attn_t_kernel(lq_ref, q_ref, k_ref, vt_ref, g_ref, o_ref, s0, s1, m_scr, *, lam_init, tq, lk, kc, nq):
    j = pl.program_id(2)
    bufs = (s0, s1)

    @pl.when(j == 0)
    def _():
        s1[...] = jnp.zeros_like(s1)
        m_scr[1] = jnp.zeros(m_scr.shape[1:], F32)

    lam = _lambda(lq_ref, lam_init)
    for r in range(2):
        sig = 2 * j + r
        buf_a, buf_b = bufs[r], bufs[1 - r]
        t_a = jnp.minimum(sig, nq - 1)
        t_b = jnp.clip(sig - 1, 0, nq - 1)
        qq = _split_halves(q_ref[0, pl.ds(pl.multiple_of(t_a * tq, tq), tq), :])
        m_b = m_scr[1 - r]
        m8 = l8 = acc = None
        for c0 in range(0, lk, kc):
            rows = slice(c0, c0 + kc)
            s = _qk(k_ref[0, rows, :], qq)
            buf_a[rows, :] = s
            for i in range(0, kc, SUBLANES):
                m8 = s[i:i + SUBLANES] if m8 is None else jnp.maximum(m8, s[i:i + SUBLANES])
            p = jnp.exp2(buf_b[rows, :].reshape(kc // SUBLANES, SUBLANES, 2 * tq) - m_b)
            lc = jnp.sum(p, axis=0)
            l8 = lc if l8 is None else l8 + lc
            pv = _dot(vt_ref[0, :, rows], p.reshape(kc, 2 * tq).astype(BF16))
            acc = pv if acc is None else acc + pv
        m_scr[r] = jnp.broadcast_to(jnp.max(m8, axis=0, keepdims=True), m8.shape)
        l = jnp.sum(l8, axis=0, keepdims=True)
        ot = acc[:, :tq] / l[:, :tq] - acc[:, tq:] * (lam / l[:, tq:])
        ms = jnp.mean(ot * ot, axis=0, keepdims=True)
        ot = ot * lax.rsqrt(ms + EPS) * g_ref[...] * (1.0 - lam_init)
        o_ref[0, pl.ds(pl.multiple_of(t_b * tq, tq), tq), :] = ot.T.astype(o_ref.dtype)


def _attention_t(q, k, vt, lam_qk, subln_g, lam_init, tq=128, kc=768):
    b, s, w = q.shape
    lk = k.shape[1]
    heads = w // ATTN_VD
    nq = s // tq
    nsteps = -(-(nq + 1) // 2)
    return pl.pallas_call(
        functools.partial(_attn_t_kernel, lam_init=lam_init, tq=tq, lk=lk, kc=kc, nq=nq),
        grid=(b, heads, nsteps),
        in_specs=[pl.BlockSpec(lam_qk.shape, lambda b_, h, i: (0, 0)),
                  pl.BlockSpec((1, s, ATTN_VD), lambda b_, h, i: (b_, 0, h)),
                  pl.BlockSpec((1, lk, ATTN_VD), lambda b_, h, i: (b_, 0, h)),
                  pl.BlockSpec((1, ATTN_VD, lk), lambda b_, h, i: (b_, h, 0)),
                  pl.BlockSpec((ATTN_VD, 1), lambda b_, h, i: (0, 0))],
        out_specs=pl.BlockSpec((1, s, ATTN_VD), lambda b_, h, i: (b_, 0, h)),
        out_shape=jax.ShapeDtypeStruct((b, s, w), BF16),
        scratch_shapes=[pltpu.VMEM((lk, 2 * tq), F32)] * 2 + [pltpu.VMEM((2, SUBLANES, 2 * tq), F32)],
        compiler_params=_params(3), name="diff_attention_t",
    )(lam_qk, q, k, vt, subln_g.reshape(ATTN_VD, 1))


def _attention(q, k, v, lam_qk, subln_g, lam_init, tq=128):
    b, s, w = q.shape
    lk = k.shape[1]
    heads = w // ATTN_VD
    tq = min(tq, s)
    return pl.pallas_call(
        functools.partial(_attn_kernel, lam_init=lam_init, tq=tq),
        grid=(b, heads, s // tq),
        in_specs=[pl.BlockSpec(lam_qk.shape, lambda b_, h, i: (0, 0)),
                  pl.BlockSpec((1, tq, ATTN_VD), lambda b_, h, i: (b_, i, h)),
                  pl.BlockSpec((1, lk, ATTN_VD), lambda b_, h, i: (b_, 0, h)),
                  pl.BlockSpec((1, lk, ATTN_VD), lambda b_, h, i: (b_, 0, h)),
                  pl.BlockSpec((1, ATTN_VD), lambda b_, h, i: (0, 0))],
        out_specs=pl.BlockSpec((1, tq, ATTN_VD), lambda b_, h, i: (b_, i, h)),
        out_shape=jax.ShapeDtypeStruct((b, s, w), BF16),
        compiler_params=_params(3), name="diff_attention",
    )(lam_qk, q, k, v, subln_g.reshape(1, ATTN_VD))


def _filter_kernel(z_ref, w1_ref, b1_ref, f1_ref, w2_ref, b2_ref, f2_ref, w3_ref, dl_ref, h_ref, asum_ref, *,
                   tl, l):
    i = pl.program_id(0)
    half = tl // 2
    z = z_ref[...]
    z2 = jnp.concatenate([z[:half], z[half:]], axis=1)
    pre = jnp.dot(z2, w1_ref[...], precision=HIGHEST, preferred_element_type=F32)
    h1 = jnp.sin(f1_ref[...] * (pre + b1_ref[...]))
    h2 = jnp.sin(f2_ref[...] * (jnp.dot(h1, w2_ref[...], precision=HIGHEST,
                                        preferred_element_type=F32) + b2_ref[...]))
    hh = h2.astype(BF16)
    hl = (h2 - hh.astype(F32)).astype(BF16)
    h2s = jnp.concatenate([hh, hl, hh], axis=1)
    part = None
    for j in range(2):
        rows = slice(j * half, (j + 1) * half)
        h3 = _dot(h2s, w3_ref[0, j])
        n = i * tl + j * half + lax.broadcasted_iota(jnp.int32, (half, 1), 0)
        h = jnp.where(n == l, 0.0, h3 * jnp.exp(-z[rows, 0:1] * dl_ref[...]))
        h_ref[rows, :] = h
        pj = jnp.sum(jnp.abs(h), axis=0, keepdims=True)
        part = pj if part is None else part + pj

    @pl.when(i == 0)
    def _():
        asum_ref[...] = jnp.zeros_like(asum_ref)

    asum_ref[...] += part


def _hyena_time_kernel(l, w1, b1, f1, w2, b2, f2, w3):
    ffn = w1.shape[1]
    nc = w3.shape[1] // 2
    c = nc // 2
    tl = min(l, 512)
    w3d = jnp.swapaxes(w3.reshape(ffn, 2, nc), 0, 1)
    n = jnp.arange(2 * l, dtype=jnp.int32)
    t = jnp.where(n < l, n, 2 * l - n).astype(F32)
    bands = jnp.linspace(1e-4, HYENA_BANDS - 1, HYENA_BANDS, dtype=F32)
    ang = 2.0 * math.pi * t[:, None] * bands[None, :] / l
    emb = 2 * HYENA_BANDS + 1
    z = jnp.concatenate([(t / (l - 1))[:, None], jnp.cos(ang), jnp.sin(ang),
                         jnp.zeros((2 * l, LANES - emb), F32)], axis=1)
    w1p = jnp.zeros((LANES, ffn), F32).at[:emb].set(w1)
    zf = jnp.zeros((ffn, ffn), F32)
    w1b = jnp.concatenate([jnp.concatenate([w1p, jnp.zeros_like(w1p)], 1),
                           jnp.concatenate([jnp.zeros_like(w1p), w1p], 1)], 0)
    w2b = jnp.concatenate([jnp.concatenate([w2, zf], 1), jnp.concatenate([zf, w2], 1)], 0)
    w3hi = w3d.astype(BF16)
    w3lo = (w3d - w3hi.astype(F32)).astype(BF16)
    zw = jnp.zeros_like(w3hi)
    first = jnp.concatenate([w3hi, zw, w3hi, zw, w3lo, zw], axis=1)
    second = jnp.concatenate([zw, w3hi, zw, w3hi, zw, w3lo], axis=1)
    w3s = jnp.stack([first, second], axis=1)
    max_decay = math.log(HYENA_TARGET) / HYENA_FAST_DECAY
    min_decay = math.log(HYENA_TARGET) / HYENA_SLOW_DECAY
    deltas = jnp.abs(jnp.linspace(min_decay, max_decay, c, dtype=F32))
    dl = jnp.tile(deltas, 2).reshape(1, nc)
    row2 = lambda a: jnp.tile(a.reshape(1, -1), (1, 2))
    nfwd = l // tl
    kt, asum = pl.pallas_call(
        functools.partial(_filter_kernel, tl=tl, l=l),
        grid=(2 * nfwd,),
        in_specs=[pl.BlockSpec((tl, LANES), lambda i: (i, 0)), _const_spec(w1b.shape),
                  _const_spec((1, 2 * ffn)), _const_spec((1, 2 * ffn)),
                  _const_spec(w2b.shape), _const_spec((1, 2 * ffn)), _const_spec((1, 2 * ffn)),
                  pl.BlockSpec((1,) + w3s.shape[1:], lambda i: (i // nfwd, 0, 0, 0)), _const_spec((1, nc))],
        out_specs=[pl.BlockSpec((tl, nc), lambda i: (i, 0)), pl.BlockSpec((1, nc), lambda i: (0, 0))],
        out_shape=[jax.ShapeDtypeStruct((2 * l, nc), F32), jax.ShapeDtypeStruct((1, nc), F32)],
        compiler_params=_params(1), name="hyena_filter_mlp",
    )(z, w1b, row2(b1), row2(f1), w2b, row2(b2), row2(f2), w3s, dl)
    return kt, 1.0 / asum


def _angles(prod, n):
    th = (2.0 * math.pi / n) * (prod % n).astype(F32)
    return jnp.cos(th), jnp.sin(th)


def _dft_tables():
    r = DFT_R
    n = r * r
    idx = jnp.arange(r, dtype=jnp.int32)
    c, s = _angles(idx[:, None] * idx[None, :], r)
    ch, sh = c[:, :r // 2], s[:, :r // 2]
    w1_data = jnp.concatenate([jnp.concatenate([ch, sh], 1), jnp.concatenate([-sh, ch], 1)], 0)
    pad = jnp.zeros((HALF_ROWS - (r // 2 + 1), r), F32)
    w1_real = jnp.concatenate([c[:r // 2 + 1], pad, -s[:r // 2 + 1], pad], 0)
    ct, st = c[:r // 2], s[:r // 2]
    w2 = jnp.concatenate([jnp.concatenate([ct, -st], 1), jnp.concatenate([st, ct], 1)], 0)
    k = idx[:, None, None] + r * idx[None, :, None]
    cg, sg = _angles(k * idx[None, None, :], n)
    g = jnp.concatenate([jnp.concatenate([cg, sg], 2), jnp.concatenate([-sg, cg], 2)], 1)
    return dict(w1_data=w1_data.astype(BF16), w1_real=w1_real.astype(BF16), w2=w2.astype(BF16),
                g=g.astype(BF16))


N2_TILE = 16
K1_TILE = 8


def _pack_pair(re, im):
    rb = lax.bitcast_convert_type(re.astype(BF16).astype(F32), jnp.uint32)
    ib = lax.bitcast_convert_type(im.astype(BF16).astype(F32), jnp.uint32)
    return rb | (ib >> 16)


def _unpack_pair(w):
    re = lax.bitcast_convert_type(w & jnp.uint32(0xFFFF0000), F32)
    im = lax.bitcast_convert_type(w << 16, F32)
    return re, im


def _dft1_kernel(x_ref, w_ref, o_ref, *, parts, pack):
    r = w_ref.shape[0] // 2
    for i in range(N2_TILE):
        x = jnp.concatenate([x_ref[p, :, i, :] for p in range(parts)], axis=0)
        y = _dft_mm(w_ref[...], x)
        if pack:
            o_ref[0, :, i, :] = _pack_pair(y[:r], y[r:])
        else:
            o_ref[0, 0, :, i, :] = y[:r]
            o_ref[0, 1, :, i, :] = y[r:]


def _dft_major(x, w3, parts, pack=False):
    b, n1, r, c = x.shape
    k1 = w3.shape[0] // 2
    if pack:
        out_spec = pl.BlockSpec((1, k1, N2_TILE, c), lambda p_, j: (p_, 0, j, 0))
        out_shape = jax.ShapeDtypeStruct((b // parts, k1, r, c), jnp.uint32)
    else:
        out_spec = pl.BlockSpec((1, 2, k1, N2_TILE, c), lambda p_, j: (p_, 0, 0, j, 0))
        out_shape = jax.ShapeDtypeStruct((b // parts, 2, k1, r, c), F32)
    return pl.pallas_call(
        functools.partial(_dft1_kernel, parts=parts, pack=pack),
        grid=(b // parts, r // N2_TILE),
        in_specs=[pl.BlockSpec((parts, n1, N2_TILE, c), lambda p_, j: (p_, 0, j, 0)), _const_spec(w3.shape)],
        out_specs=out_spec, out_shape=out_shape,
        compiler_params=_params(2), name="dft_major",
    )(x, w3)


def _spectrum_kernel(lo_ref, hi_ref, g_ref, sc_ref, o_ref):
    for half, a_ref, sign in ((0, lo_ref, 1.0), (1, hi_ref, -1.0)):
        re, im = _unpack_pair(a_ref[0, 0])
        a = jnp.concatenate([re, im * sign], axis=0)
        x = _dft_mm(g_ref[half, 0], a) * sc_ref[...]
        r = x.shape[0] // 2
        o_ref[0, half, 0] = x[:r].astype(o_ref.dtype)
        o_ref[1, half, 0] = x[r:].astype(o_ref.dtype)


def _filter_spectrum(kt, scale, tabs):
    r = DFT_R
    h = r // 2
    c = kt.shape[1]
    a = _dft_major(kt.reshape(1, r, r, c), tabs["w1_real"], 1, pack=True)
    g = tabs["g"].reshape((2, h) + tabs["g"].shape[1:])
    out = pl.pallas_call(
        _spectrum_kernel,
        grid=(h,),
        in_specs=[pl.BlockSpec((1, 1, r, c), lambda k: (0, k, 0, 0)),
                  pl.BlockSpec((1, 1, r, c), lambda k: (0, h - k, 0, 0)),
                  pl.BlockSpec((2, 1) + g.shape[2:], lambda k: (0, k, 0, 0)),
                  _const_spec((1, c))],
        out_specs=pl.BlockSpec((2, 2, 1, r, c), lambda k: (0, 0, k, 0, 0)),
        out_shape=jax.ShapeDtypeStruct((2, 2, h, r, c), BF16),
        compiler_params=_params(1), name="filter_spectrum",
    )(a, a, g, scale)
    return out.reshape(2, r, r, c)


def _dft_mm_t(w, x):
    return lax.dot_general(w, x.astype(BF16), (((0,), (0,)), ((), ())), preferred_element_type=F32)


def _dft2_kernel(a_ref, g_ref, h_ref, o_ref):
    for kk in range(K1_TILE):
        hr, hi = h_ref[0, kk].astype(F32), h_ref[1, kk].astype(F32)
        for p in range(a_ref.shape[0]):
            a = jnp.concatenate(_unpack_pair(a_ref[p, kk]), axis=0)
            x = _dft_mm(g_ref[kk], a)
            r = x.shape[0] // 2
            xr, xi = x[:r], x[r:]
            y = jnp.concatenate([xr * hr - xi * hi, xr * hi + xi * hr], axis=0)
            bb = _dft_mm_t(g_ref[kk], y)
            o_ref[p, :, kk, :] = _pack_pair(bb[:r], bb[r:])


def _dft3_kernel(b_ref, w_ref, u_ref, x_ref, bias_ref, *rest, chain):
    w1_ref, o_ref, a_ref = rest if chain else (None, rest[0], None)
    bias = bias_ref[...]
    for i in range(N2_TILE):
        bb = jnp.concatenate(_unpack_pair(b_ref[0, i]), axis=0)
        y = _dft_mm(w_ref[...], bb)
        half = y.shape[0] // 2
        z = [x_ref[h, :, i, :] * (y[h * half:(h + 1) * half] + u_ref[h, :, i, :] * bias) for h in range(2)]
        for h in range(2):
            o_ref[h, :, i, :] = z[h]
        if chain:
            nxt = _dft_mm(w1_ref[...], jnp.concatenate(z, axis=0))
            r = nxt.shape[0] // 2
            a_ref[0, :, i, :] = _pack_pair(nxt[:r], nxt[r:])


def _long_conv_gate(u, x, hspec, order, bias, tabs, a=None, chain=False):
    b, l, c = u.shape
    r = DFT_R
    p = b // 2
    nat = (b, r // 2, r, c)
    if a is None:
        a = _dft_major(u.reshape(nat), tabs["w1_data"], 2, pack=True)
    spec4 = pl.BlockSpec((p, K1_TILE, r, c), lambda k: (0, k, 0, 0))
    bb = pl.pallas_call(
        _dft2_kernel,
        grid=(r // K1_TILE,),
        in_specs=[spec4,
                  pl.BlockSpec((K1_TILE,) + tabs["g"].shape[1:], lambda k: (k, 0, 0)),
                  pl.BlockSpec((2, K1_TILE, r, c), lambda k: (0, k, 0, order))],
        out_specs=pl.BlockSpec((p, r, K1_TILE, c), lambda k: (0, 0, k, 0)),
        out_shape=jax.ShapeDtypeStruct((p, r, r, c), jnp.uint32),
        compiler_params=_params(1), name="dft_minor_filter",
    )(a, tabs["g"], hspec)
    blk = pl.BlockSpec((2, r // 2, N2_TILE, c), lambda p_, j: (p_, 0, j, 0))
    in_specs = [pl.BlockSpec((1, N2_TILE, r, c), lambda p_, j: (p_, j, 0, 0)),
                _const_spec(tabs["w2"].shape), blk, blk, _const_spec((1, c))]
    args = [bb, tabs["w2"], u.reshape(nat), x.reshape(nat), bias.reshape(1, c)]
    out_specs, out_shape = [blk], [jax.ShapeDtypeStruct(nat, F32)]
    if chain:
        in_specs.append(_const_spec(tabs["w1_data"].shape))
        args.append(tabs["w1_data"])
        out_specs.append(pl.BlockSpec((1, r, N2_TILE, c), lambda p_, j: (p_, 0, j, 0)))
        out_shape.append(jax.ShapeDtypeStruct((p, r, r, c), jnp.uint32))
    res = pl.pallas_call(
        functools.partial(_dft3_kernel, chain=chain),
        grid=(p, r // N2_TILE),
        in_specs=in_specs, out_specs=out_specs, out_shape=out_shape,
        compiler_params=_params(2), name="idft_major_gate",
    )(*args)
    if chain:
        return res[0].reshape(b, l, c), res[1]
    return res[0].reshape(b, l, c)


def _dense_tables(l):
    n = 2 * l
    k = jnp.arange(n, dtype=jnp.int32)
    t = jnp.arange(l, dtype=jnp.int32)
    c, s = _angles(k[:, None] * t[None, :], n)
    wf = jnp.concatenate([jnp.concatenate([c, s], 1), jnp.concatenate([-s, c], 1)], 0)
    ct, st = c.T, s.T
    wi = jnp.concatenate([jnp.concatenate([ct, -st], 1), jnp.concatenate([st, ct], 1)], 0)
    ca, sa = _angles(k[:, None] * k[None, :], n)
    wk = jnp.concatenate([ca, -sa], 0)
    return dict(wf=wf.astype(BF16), wi=wi.astype(BF16), wk=wk.astype(BF16))


def _dense_spectrum_kernel(k_ref, w_ref, sc_ref, o_ref):
    o_ref[...] = _dft_mm(w_ref[...], k_ref[...]) * sc_ref[...]


def _dense_spectrum(kt, scale, wk):
    n, c = kt.shape
    return pl.pallas_call(
        _dense_spectrum_kernel,
        grid=(1,),
        in_specs=[_const_spec((n, c)), _const_spec(wk.shape), _const_spec((1, c))],
        out_specs=pl.BlockSpec((2 * n, c), lambda i: (0, 0)),
        out_shape=jax.ShapeDtypeStruct((2 * n, c), F32),
        compiler_params=_params(1), name="dense_filter_spectrum",
    )(kt, wk, scale)


def _dense_conv_kernel(u_ref, x_ref, wf_ref, wi_ref, h_ref, bias_ref, o_ref):
    u0, u1 = u_ref[0], u_ref[1]
    z = _dft_mm(wf_ref[...], jnp.concatenate([u0, u1], axis=0))
    n = z.shape[0] // 2
    zr, zi = z[:n], z[n:]
    hr, hi = h_ref[:n], h_ref[n:]
    y = _dft_mm(wi_ref[...], jnp.concatenate([zr * hr - zi * hi, zr * hi + zi * hr], axis=0))
    l = y.shape[0] // 2
    bias = bias_ref[...]
    o_ref[0] = x_ref[0] * (y[:l] + u0 * bias)
    o_ref[1] = x_ref[1] * (y[l:] + u1 * bias)


def _dense_conv_gate(u, x, hspec, order, bias, tabs):
    b, l, c = u.shape
    blk = pl.BlockSpec((2, l, c), lambda p_: (p_, 0, 0))
    return pl.pallas_call(
        _dense_conv_kernel,
        grid=(b // 2,),
        in_specs=[blk, blk, _const_spec(tabs["wf"].shape), _const_spec(tabs["wi"].shape),
                  pl.BlockSpec((4 * l, c), lambda p_: (0, order)), _const_spec((1, c))],
        out_specs=blk,
        out_shape=jax.ShapeDtypeStruct((b, l, c), F32),
        compiler_params=_params(1), name="dense_conv_gate",
    )(u, x, tabs["wf"], tabs["wi"], hspec, bias.reshape(1, c))


def _mix_kernel(h_ref, o_ref, pm_ref, y_ref, g_ref, gate_ref, wa_ref, wp_ref, wh_ref, wo_ref, out_ref):
    d = h_ref.shape[2]
    a = _dot(o_ref[0].astype(BF16), wa_ref[...])
    p = _dot(pm_ref[0].astype(BF16), wp_ref[...])
    y = _dot(y_ref[0].astype(BF16), wh_ref[...])
    gate = lambda k: g_ref[0, :, k * d:(k + 1) * d].astype(F32)
    merged = gate(0) * a + gate(1) * p + gate(2) * y
    out = _dot(merged.astype(BF16), wo_ref[...])
    out_ref[0] = h_ref[0] + gate_ref[0] * out


def _mix(h, o, pm, y, g, mod, w_attn_o, w_pool_o, w_hy_o, w_out, tm=512):
    b, s, d = h.shape
    tm = min(tm, s)

    def tok(w):
        return pl.BlockSpec((1, tm, w), lambda b_, i: (b_, i, 0))

    ws = [w.astype(BF16) for w in (w_attn_o, w_pool_o, w_hy_o, w_out)]
    return pl.pallas_call(
        _mix_kernel,
        grid=(b, s // tm),
        in_specs=[tok(d), tok(o.shape[2]), tok(pm.shape[2]), tok(y.shape[2]), tok(g.shape[2]),
                  _mod_spec(mod, d, 2)] + [_const_spec(w.shape) for w in ws],
        out_specs=tok(d),
        out_shape=jax.ShapeDtypeStruct((b, s, d), F32),
        compiler_params=_params(2), name="merge_outproj",
    )(h, o, pm, y, g, mod, *ws)


def _ffn_kernel(h_ref, sh_ref, sc_ref, gate_ref, g_ref, wi_ref, wo_ref, fg_ref, out_ref, *,
                hidden, chunk, final):
    x = h_ref[0]
    a = _norm_mod(x, g_ref[...], sh_ref[0], sc_ref[0]).astype(BF16)
    acc = jnp.zeros(x.shape, F32)
    for c0 in range(0, hidden, chunk):
        gt = _dot(a, wi_ref[:, c0:c0 + chunk])
        up = _dot(a, wi_ref[:, hidden + c0:hidden + c0 + chunk])
        act = (gt * jax.nn.sigmoid(gt) * up).astype(BF16)
        acc = acc + _dot(act, wo_ref[c0:c0 + chunk, :])
    y = x + gate_ref[0] * acc
    if final:
        ms = jnp.mean(y * y, axis=-1, keepdims=True)
        y = y * lax.rsqrt(ms + EPS) * fg_ref[...]
    out_ref[0] = y


def _ffn(h, mod, norm_g, w_in, w_out, final_g, final, tm=512):
    b, s, d = h.shape
    hidden = w_out.shape[0]
    tm = min(tm, s)
    tok = pl.BlockSpec((1, tm, d), lambda b_, i: (b_, i, 0))
    return pl.pallas_call(
        functools.partial(_ffn_kernel, hidden=hidden, chunk=hidden // 2, final=final),
        grid=(b, s // tm),
        in_specs=[tok, _mod_spec(mod, d, 3), _mod_spec(mod, d, 4), _mod_spec(mod, d, 5),
                  _const_spec((1, d)), _const_spec(w_in.shape), _const_spec(w_out.shape),
                  _const_spec((1, d))],
        out_specs=tok,
        out_shape=jax.ShapeDtypeStruct((b, s, d), F32),
        compiler_params=_params(2), name="swiglu_ffn",
    )(h, mod, mod, mod, norm_g.reshape(1, d), w_in.astype(BF16), w_out.astype(BF16),
      final_g.reshape(1, d))


def _rope_tables(seq, width):
    t = jnp.arange(seq, dtype=jnp.int32)
    row = (t // GRID_W).astype(F32)
    col = (t % GRID_W).astype(F32)
    inv = 1.0 / (ROPE_BASE ** (jnp.arange(ROPE_FREQS, dtype=F32) * 2.0 / (2 * ROPE_FREQS)))
    ar, ac = row[:, None] * inv, col[:, None] * inv
    cos = jnp.concatenate([jnp.cos(ar)] * 2 + [jnp.cos(ac)] * 2, axis=1)
    sin = jnp.concatenate([-jnp.sin(ar), jnp.sin(ar), -jnp.sin(ac), jnp.sin(ac)], axis=1)
    reps = width // cos.shape[1]
    return jnp.tile(cos, (1, reps)), jnp.tile(sin, (1, reps))


def kernel(x, c, ctx, c_ctx, w_mod, b_mod, norm1_g, norm2_g, w_in, lam_qk, subln_g, w_attn_o, w_pool, pool_scale, w_pool_o, w_short, b_short, hf_w1, hf_b1, hf_freq1, hf_w2, hf_b2, hf_freq2, hf_w3, hy_bias, w_hy_o, w_out, w_ffn_in, w_ffn_out, final_g):
    batch, seq, d = x.shape
    depth = w_mod.shape[0]
    ctx_len = ctx.shape[1]
    aw = w_attn_o.shape[1]
    pw = w_pool_o.shape[1]
    hw = w_hy_o.shape[1]
    q0, k0, v0, p0 = 0, aw, 2 * aw, 3 * aw
    h0 = p0 + pw
    g0 = h0 + 3 * hw
    assert seq * 2 == DFT_R * DFT_R and batch % 2 == 0 and batch + 1 <= SUBLANES

    cond = jnp.zeros((SUBLANES, d), F32).at[:batch].set(c).at[batch].set(c_ctx)
    mod_all = _modulation(cond, w_mod, b_mod)
    rope_tabs = _rope_tables(seq, ATTN_VD)
    tabs = _dft_tables()
    ctx_tabs = _dense_tables(ctx_len)

    h_lat, h_ctx = x, ctx
    for l in range(depth):
        last = l == depth - 1
        lam_init = LAMBDA_INIT_BASE - LAMBDA_INIT_AMP * math.exp(-LAMBDA_INIT_RATE * l)
        mod = mod_all[l, :batch].reshape(batch, 1, N_MOD * d)
        mod_c = mod_all[l, batch:batch + 1].reshape(1, 1, N_MOD * d)
        wl = w_in[l]
        fparams = (hf_w1[l], hf_b1[l], hf_freq1[l], hf_w2[l], hf_b2[l], hf_freq2[l], hf_w3[l])

        wl_bf = wl.astype(BF16)
        local_params = (w_pool[l], pool_scale[l], w_short[l], b_short[l])
        out_dt = [BF16, F32, F32, F32, BF16, BF16, BF16, BF16]
        local_seg = ("local", p0, pw + 3 * hw, 1.0)
        gate_seg = ("sigmoid", g0, 3 * d, 1.0)
        segs = [local_seg, gate_seg, ("rope", q0, aw, Q_SCALE), ("rope", k0, aw, 1.0), ("plain", v0, aw, 1.0)]
        pm, vv, x1, x2, g, q, k, v = _inproj(h_lat, mod, norm1_g[l], wl_bf, segs, out_dt, rope_tabs, local_params)

        if last:
            segs_c = [("plain", k0, aw, 1.0), ("plain", v0, aw, 1.0)]
            k_c, v_c = _inproj(h_ctx, mod_c, norm1_g[l], wl_bf, segs_c, [BF16, BF16])
        else:
            segs_c = [local_seg, gate_seg, ("plain", q0, aw, Q_SCALE), ("plain", k0, aw, 1.0), ("plain", v0, aw, 1.0)]
            pm_c, vv_c, x1_c, x2_c, g_c, q_c, k_c, v_c = _inproj(h_ctx, mod_c, norm1_g[l], wl_bf, segs_c, out_dt,
                                                                 None, local_params)
            o_c = _attention(q_c, k_c, v_c, lam_qk[l], subln_g[l], lam_init)
            kt_c, inv_norm_c = _hyena_time_kernel(ctx_len, *fparams)
            hspec_c = _dense_spectrum(kt_c, inv_norm_c / (2 * ctx_len), ctx_tabs["wk"])
            z_c = _dense_conv_gate(vv_c, x1_c, hspec_c, 0, hy_bias[l, 0], ctx_tabs)
            y_c = _dense_conv_gate(z_c, x2_c, hspec_c, 1, hy_bias[l, 1], ctx_tabs)
            h_ctx_mid = _mix(h_ctx, o_c, pm_c, y_c, g_c, mod_c, w_attn_o[l], w_pool_o[l], w_hy_o[l], w_out[l])
            h_ctx_new = _ffn(h_ctx_mid, mod_c, norm2_g[l], w_ffn_in[l], w_ffn_out[l], final_g, False)

        vt = jnp.swapaxes(jnp.concatenate([v_c, v], axis=1), 1, 2)
        o = _attention_t(q, jnp.concatenate([k_c, k], axis=1), vt, lam_qk[l], subln_g[l], lam_init)
        kt, inv_norm = _hyena_time_kernel(seq, *fparams)
        hspec = _filter_spectrum(kt, inv_norm / (2 * seq), tabs)
        z, a_z = _long_conv_gate(vv, x1, hspec, 0, hy_bias[l, 0], tabs, chain=True)
        y = _long_conv_gate(z, x2, hspec, 1, hy_bias[l, 1], tabs, a=a_z)
        h_mid = _mix(h_lat, o, pm, y, g, mod, w_attn_o[l], w_pool_o[l], w_hy_o[l], w_out[l])
        h_lat = _ffn(h_mid, mod, norm2_g[l], w_ffn_in[l], w_ffn_out[l], final_g, last)
        if not last:
            h_ctx = h_ctx_new
    return h_lat
```
